```python
import math
import jax, jax.numpy as jnp
from jax import lax
import numpy as np

D_MODEL = 1024
BATCH = 8
SEQ = 2048
DEPTH = 1

CHUNK = 64
Q_BLOCK = 128
ROPE_THETA = 10000.0
LN_EPS = 1e-5

DA_HEADS = 4
DA_V_DIM = D_MODEL // (2 * DA_HEADS)
DA_HEAD_DIM = DA_V_DIM // 2
GLA_HEADS = 4
GLA_V_DIM = D_MODEL // (2 * GLA_HEADS)
GLA_KEY_DIM = GLA_V_DIM // 2
GLA_GATE_RANK = 16
GLA_GATE_NORMALIZER = 16.0

DA_Q = DA_HEADS * 2 * DA_HEAD_DIM
DA_K = DA_HEADS * 2 * DA_HEAD_DIM
DA_V = DA_HEADS * DA_V_DIM
GLA_Q = GLA_HEADS * GLA_KEY_DIM
GLA_K = GLA_HEADS * GLA_KEY_DIM
GLA_V = GLA_HEADS * GLA_V_DIM
GLA_OG = GLA_HEADS * GLA_V_DIM
PROJ_SIZES = (DA_Q, DA_K, DA_V, GLA_Q, GLA_K, GLA_V, GLA_OG, GLA_GATE_RANK)
D_IN = DA_Q + DA_K + DA_V + GLA_Q + GLA_K + GLA_V + GLA_OG + GLA_GATE_RANK
MIX_WIDTH = DA_HEADS * DA_V_DIM + GLA_HEADS * GLA_V_DIM

N_GROUPS = 4
EXPERTS_PER_GROUP = 8
N_EXPERTS = N_GROUPS * EXPERTS_PER_GROUP
TOP_K = 2
D_EXPERT = D_MODEL // 2
MOE_BLOCK = 128

DEEPNORM_ALPHA = (2 * DEPTH) ** 0.25
DEEPNORM_BETA = (8 * DEPTH) ** -0.25

kernel_name = "hybrid_diffattn_gla_hiermoe_deepnorm"


def layer_norm(x, g, b):
    xf = x.astype(jnp.float32)
    mu = jnp.mean(xf, -1, keepdims=True)
    var = jnp.mean(jnp.square(xf - mu), -1, keepdims=True)
    return ((xf - mu) * lax.rsqrt(var + LN_EPS) * g + b).astype(x.dtype)


def rms_norm(x, g):
    xf = x.astype(jnp.float32)
    return (xf * lax.rsqrt(jnp.mean(jnp.square(xf), -1, keepdims=True) + LN_EPS) * g).astype(x.dtype)


def rope_tables(positions, dim):
    inv_freq = ROPE_THETA ** (-jnp.arange(0, dim, 2, dtype=jnp.float32) / dim)
    ang = positions.astype(jnp.float32)[..., None] * inv_freq
    return jnp.cos(ang)[:, :, None, :], jnp.sin(ang)[:, :, None, :]


def apply_rope(t, cos, sin):
    t2 = t.reshape(*t.shape[:-1], -1, 2)
    te, to = t2[..., 0], t2[..., 1]
    c, s = cos.astype(t.dtype), sin.astype(t.dtype)
    return jnp.stack([te * c - to * s, te * s + to * c], -1).reshape(t.shape)


def diff_attention(q, k, v, lam):
    B, H2, S, dh = q.shape
    H = H2 // 2
    n_blk = S // Q_BLOCK
    qb = q.reshape(B, H2, n_blk, Q_BLOCK, dh).transpose(2, 0, 1, 3, 4)
    key_chunk = jnp.arange(S) // CHUNK
    scale = dh ** -0.5

    def one_block(args):
        qi, bi = args
        s = jnp.einsum('bhqd,bhkd->bhqk', qi, k).astype(jnp.float32) * scale
        q_chunk = (bi * Q_BLOCK + jnp.arange(Q_BLOCK)) // CHUNK
        mask = key_chunk[None, :] <= q_chunk[:, None]
        p = jax.nn.softmax(jnp.where(mask, s, -jnp.inf), -1).reshape(B, H, 2, Q_BLOCK, S)
        a = p[:, :, 0] - lam * p[:, :, 1]
        return jnp.einsum('bhqk,bhkd->bhqd', a.astype(v.dtype), v)

    o = lax.map(one_block, (qb, jnp.arange(n_blk)))
    return o.transpose(1, 2, 0, 3, 4).reshape(B, H, S, v.shape[-1])


def gla_chunked(q, k, v, log_a):
    B, S, H, dk = q.shape
    dv = v.shape[-1]
    N = S // CHUNK

    def chunks(t):
        return t.astype(jnp.float32).reshape(B, N, CHUNK, H, t.shape[-1]).transpose(0, 3, 1, 2, 4)

    qc = chunks(q) * dk ** -0.5
    kc, vc, gc = chunks(k), chunks(v), chunks(log_a)
    bcum = jnp.cumsum(gc, axis=3)
    b_last = bcum[:, :, :, -1:, :]
    q_t = qc * jnp.exp(bcum)
    k_t = kc * jnp.exp(-bcum)
    k_end = kc * jnp.exp(b_last - bcum)
    causal = jnp.tril(jnp.ones((CHUNK, CHUNK), dtype=bool))
    att = jnp.where(causal, jnp.einsum('bhncd,bhnjd->bhncj', q_t, k_t), 0.0)
    o_intra = jnp.einsum('bhncj,bhnje->bhnce', att, vc)
    d_state = jnp.einsum('bhncd,bhnce->bhnde', k_end, vc)
    decay = jnp.exp(b_last[:, :, :, 0, :])

    def step(state, inp):
        d_n, ds_n = inp
        return d_n[..., None] * state + ds_n, state

    s0 = jnp.zeros((B, H, dk, dv), jnp.float32)
    _, s_prev = lax.scan(step, s0, (decay.transpose(2, 0, 1, 3), d_state.transpose(2, 0, 1, 3, 4)))
    o_inter = jnp.einsum('bhncd,nbhde->bhnce', q_t, s_prev)
    o = (o_intra + o_inter).transpose(0, 2, 3, 1, 4).reshape(B, S, H, dv)
    return o.astype(v.dtype)


def token_mixer(xn, cos, sin, layer_idx, w_in, lam_q1, lam_k1, lam_q2, lam_k2, da_subln_g,
                gla_w_gate2, gla_b_gate2, gla_norm_g, w_o):
    B, S, _ = xn.shape
    proj = xn @ w_in
    splits = [int(i) for i in np.cumsum(PROJ_SIZES)[:-1]]
    dq, dk_, dv, gq, gk, gv, g_out, g_low = jnp.split(proj, splits, axis=-1)

    dq = apply_rope(dq.reshape(B, S, 2 * DA_HEADS, DA_HEAD_DIM), cos, sin).transpose(0, 2, 1, 3)
    dk_ = apply_rope(dk_.reshape(B, S, 2 * DA_HEADS, DA_HEAD_DIM), cos, sin).transpose(0, 2, 1, 3)
    dv = dv.reshape(B, S, DA_HEADS, DA_V_DIM).transpose(0, 2, 1, 3)
    lam_init = 0.8 - 0.6 * math.exp(-0.3 * layer_idx)
    lam = (jnp.exp(jnp.sum(lam_q1.astype(jnp.float32) * lam_k1))
           - jnp.exp(jnp.sum(lam_q2.astype(jnp.float32) * lam_k2)) + lam_init)
    da = diff_attention(dq, dk_, dv, lam).transpose(0, 2, 1, 3)
    da = (rms_norm(da, da_subln_g) * (1.0 - lam_init)).reshape(B, S, DA_HEADS * DA_V_DIM)

    log_a = jax.nn.log_sigmoid((g_low @ gla_w_gate2 + gla_b_gate2).astype(jnp.float32)) / GLA_GATE_NORMALIZER
    o = gla_chunked(gq.reshape(B, S, GLA_HEADS, GLA_KEY_DIM),
                    gk.reshape(B, S, GLA_HEADS, GLA_KEY_DIM),
                    gv.reshape(B, S, GLA_HEADS, GLA_V_DIM),
                    log_a.reshape(B, S, GLA_HEADS, GLA_KEY_DIM))
    o = rms_norm(o, gla_norm_g).reshape(B, S, GLA_HEADS * GLA_V_DIM) * jax.nn.silu(g_out)

    return jnp.concatenate([da, o], axis=-1) @ w_o


def hier_moe(h, w_rg, b_rg, w_re, b_re, w_gate, w_up, w_down):
    B, S, D = h.shape
    T = B * S
    xt = h.reshape(T, D)
    tok = jnp.arange(T)
    g_logits = (xt @ w_rg).astype(jnp.float32) + b_rg
    g_prob = jax.nn.softmax(g_logits, -1)
    g_top = jnp.argmax(g_logits, -1)
    p_g = g_prob[tok, g_top]
    e_logits = ((xt @ w_re).astype(jnp.float32) + b_re).reshape(T, N_GROUPS, EXPERTS_PER_GROUP)
    e_in = e_logits[tok, g_top]
    top_v, top_i = lax.top_k(e_in, TOP_K)
    gate = jax.nn.softmax(top_v, -1) * p_g[:, None]
    eid = g_top[:, None] * EXPERTS_PER_GROUP + top_i

    flat_e = eid.reshape(-1)
    flat_w = gate.reshape(-1)
    flat_tok = jnp.repeat(tok, TOP_K)
    order = jnp.argsort(flat_e)
    s_e, s_tok, s_w = flat_e[order], flat_tok[order], flat_w[order]
    counts = jnp.bincount(flat_e, length=N_EXPERTS)
    starts = jnp.cumsum(counts) - counts
    padded = ((counts + MOE_BLOCK - 1) // MOE_BLOCK) * MOE_BLOCK
    pad_ends = jnp.cumsum(padded)
    pad_starts = pad_ends - padded
    dest = pad_starts[s_e] + (jnp.arange(T * TOP_K) - starts[s_e])
    n_rows = T * TOP_K + N_EXPERTS * MOE_BLOCK
    n_blocks = n_rows // MOE_BLOCK
    row_tok = jnp.full((n_rows,), T, jnp.int32).at[dest].set(s_tok.astype(jnp.int32))
    x_ext = jnp.concatenate([xt, jnp.zeros((1, D), xt.dtype)], axis=0)
    x_rows = x_ext[row_tok].reshape(n_blocks, MOE_BLOCK, D)
    blk_e = jnp.minimum(jnp.searchsorted(pad_ends, jnp.arange(n_blocks) * MOE_BLOCK, side='right'),
                        N_EXPERTS - 1)

    def expert_block(args):
        xb, e = args
        return (jax.nn.silu(xb @ w_gate[e]) * (xb @ w_up[e])) @ w_down[e]

    y_rows = lax.map(expert_block, (x_rows, blk_e)).reshape(n_rows, D)
    y = jnp.zeros((T, D), xt.dtype).at[s_tok].add(y_rows[dest] * s_w[:, None].astype(xt.dtype))
    return y.reshape(B, S, D)


def setup_inputs(seed: int = 0) -> dict:
    key = jax.random.key(seed)
    ks = jax.random.split(key, 26)
    f32 = jnp.float32
    nrm = lambda k, shape, s: jax.random.normal(k, shape, f32) * s
    x = jax.random.normal(ks[0], (BATCH, SEQ, D_MODEL), f32)
    offsets = jax.random.randint(ks[1], (BATCH, 1), 0, 256) * CHUNK
    positions = (offsets + jnp.arange(SEQ)[None, :]).astype(jnp.int32)
    col_scale = jnp.concatenate([
        jnp.full((n,), s, f32) for n, s in zip(
            PROJ_SIZES, (1.0, 1.0, DEEPNORM_BETA, 1.0, 1.0, DEEPNORM_BETA, 1.0, 1.0))])
    w_in = nrm(ks[2], (DEPTH, D_MODEL, D_IN), D_MODEL ** -0.5) * col_scale
    return {
        "x": x,
        "positions": positions,
        "ln_in_g": 1.0 + nrm(ks[3], (D_MODEL,), 0.02),
        "ln_in_b": nrm(ks[4], (D_MODEL,), 0.02),
        "w_in": w_in,
        "lam_q1": nrm(ks[5], (DEPTH, DA_HEAD_DIM), 0.1),
        "lam_k1": nrm(ks[6], (DEPTH, DA_HEAD_DIM), 0.1),
        "lam_q2": nrm(ks[7], (DEPTH, DA_HEAD_DIM), 0.1),
        "lam_k2": nrm(ks[8], (DEPTH, DA_HEAD_DIM), 0.1),
        "da_subln_g": 1.0 + nrm(ks[9], (DEPTH, DA_V_DIM), 0.02),
        "gla_w_gate2": nrm(ks[10], (DEPTH, GLA_GATE_RANK, GLA_HEADS * GLA_KEY_DIM), GLA_GATE_RANK ** -0.5),
        "gla_b_gate2": nrm(ks[11], (DEPTH, GLA_HEADS * GLA_KEY_DIM), 0.1),
        "gla_norm_g": 1.0 + nrm(ks[12], (DEPTH, GLA_V_DIM), 0.02),
        "w_o": nrm(ks[13], (DEPTH, MIX_WIDTH, D_MODEL), MIX_WIDTH ** -0.5 * DEEPNORM_BETA),
        "ln1_g": 1.0 + nrm(ks[14], (DEPTH, D_MODEL), 0.02),
        "ln1_b": nrm(ks[15], (DEPTH, D_MODEL), 0.02),
        "router_w_group": nrm(ks[16], (DEPTH, D_MODEL, N_GROUPS), D_MODEL ** -0.5),
        "router_b_group": nrm(ks[17], (DEPTH, N_GROUPS), 0.01),
        "router_w_expert": nrm(ks[18], (DEPTH, D_MODEL, N_EXPERTS), D_MODEL ** -0.5),
        "router_b_expert": nrm(ks[19], (DEPTH, N_EXPERTS), 0.01),
        "w_gate": nrm(ks[20], (DEPTH, N_EXPERTS, D_MODEL, D_EXPERT), D_MODEL ** -0.5),
        "w_up": nrm(ks[21], (DEPTH, N_EXPERTS, D_MODEL, D_EXPERT), D_MODEL ** -0.5),
        "w_down": nrm(ks[22], (DEPTH, N_EXPERTS, D_EXPERT, D_MODEL), D_EXPERT ** -0.5 * DEEPNORM_BETA),
        "ln2_g": 1.0 + nrm(ks[23], (DEPTH, D_MODEL), 0.02),
        "ln2_b": nrm(ks[24], (DEPTH, D_MODEL), 0.02),
    }


def reference(x, positions, ln_in_g, ln_in_b, w_in, lam_q1, lam_k1, lam_q2, lam_k2, da_subln_g,
              gla_w_gate2, gla_b_gate2, gla_norm_g, w_o, ln1_g, ln1_b, router_w_group,
              router_b_group, router_w_expert, router_b_expert, w_gate, w_up, w_down,
              ln2_g, ln2_b):
    cos, sin = rope_tables(positions, DA_HEAD_DIM)
    x = layer_norm(x, ln_in_g, ln_in_b)
    for l in range(DEPTH):
        mix = token_mixer(x, cos, sin, l, w_in[l], lam_q1[l], lam_k1[l], lam_q2[l], lam_k2[l],
                          da_subln_g[l], gla_w_gate2[l], gla_b_gate2[l], gla_norm_g[l], w_o[l])
        h = layer_norm(DEEPNORM_ALPHA * x + mix, ln1_g[l], ln1_b[l])
        ffn = hier_moe(h, router_w_group[l], router_b_group[l], router_w_expert[l],
                       router_b_expert[l], w_gate[l], w_up[l], w_down[l])
        x = layer_norm(DEEPNORM_ALPHA * h + ffn, ln2_g[l], ln2_b[l])
    return x
```

```python
import functools
import math

import jax
import jax.numpy as jnp
import numpy as np
from jax import lax
from jax.experimental import pallas as pl
from jax.experimental.pallas import tpu as pltpu

F32 = jnp.float32
BF16 = jnp.bfloat16

D_MODEL = 1024
CHUNK = 64
ROPE_THETA = 10000.0
LN_EPS = 1e-5

DA_HEADS = 4
DA_V_DIM = D_MODEL // (2 * DA_HEADS)
DA_HEAD_DIM = DA_V_DIM // 2
GLA_HEADS = 4
GLA_V_DIM = D_MODEL // (2 * GLA_HEADS)
GLA_KEY_DIM = GLA_V_DIM // 2
GLA_GATE_RANK = 16
GLA_GATE_NORMALIZER = 16.0

DA_Q = DA_HEADS * 2 * DA_HEAD_DIM
DA_K = DA_Q
DA_V = DA_HEADS * DA_V_DIM
GLA_Q = GLA_HEADS * GLA_KEY_DIM
GLA_K = GLA_Q
GLA_V = GLA_HEADS * GLA_V_DIM
GLA_OG = GLA_V
D_MAIN = DA_Q + DA_K + DA_V + GLA_Q + GLA_K + GLA_V + GLA_OG

N_GROUPS = 4
EXPERTS_PER_GROUP = 8
N_EXPERTS = N_GROUPS * EXPERTS_PER_GROUP
TOP_K = 2
D_EXPERT = D_MODEL // 2

LANES = 128
VMEM_LIMIT = 48 * 1024 * 1024

R_E0, R_E1, R_G0, R_G1, R_RANK0, R_RANK1 = 0, 1, 2, 3, 4, 5
EXPERT_LANE0 = 32


def _layer_norm(x, g, b):
    mu = jnp.mean(x, axis=-1, keepdims=True)
    xc = x - mu
    var = jnp.mean(xc * xc, axis=-1, keepdims=True)
    return xc * lax.rsqrt(var + LN_EPS) * g + b


def _dot(a, b):
    return jnp.dot(a, b, preferred_element_type=F32)


def _dot_nt(a, b):
    return lax.dot_general(a, b, (((1,), (1,)), ((), ())), preferred_element_type=F32)


def _dot_tn(a, b):
    return lax.dot_general(a, b, (((0,), (0,)), ((), ())), preferred_element_type=F32)


def _in_proj_kernel(x_ref, pos_ref, g_ref, b_ref, invf_ref, w_ref, wgl_ref, wg2_ref, bg2_ref,
                    q_ref, k_ref, v_ref, gq_ref, gk_ref, gv_ref, go_ref, la_ref):
    tm = x_ref.shape[0]
    xn = _layer_norm(x_ref[...], g_ref[...], b_ref[...])
    xb = xn.astype(BF16)
    proj = _dot(xb, w_ref[...])

    ang = pos_ref[...].astype(F32) * invf_ref[...]
    c = jnp.cos(ang)
    s = jnp.sin(ang)
    lane = lax.broadcasted_iota(jnp.int32, (tm, LANES), 1)
    first = (lane & (DA_HEAD_DIM // 2)) == 0
    s_lo = jnp.where(first, -s, 0.0)
    s_hi = jnp.where(first, 0.0, s)
    reps = DA_Q // LANES
    c4 = jnp.concatenate([c] * reps, axis=1)
    s_lo4 = jnp.concatenate([s_lo] * reps, axis=1)
    s_hi4 = jnp.concatenate([s_hi] * reps, axis=1)
    half = DA_HEAD_DIM // 2

    def rope(t):
        up = pltpu.roll(t, DA_Q - half, 1)
        dn = pltpu.roll(t, half, 1)
        return t * c4 + up * s_lo4 + dn * s_hi4

    o = 0
    q = rope(proj[:, o:o + DA_Q]) * (DA_HEAD_DIM ** -0.5)
    o += DA_Q
    k = rope(proj[:, o:o + DA_K])
    o += DA_K
    q_ref[...] = q.astype(BF16)
    k_ref[...] = k.astype(BF16)
    v_ref[...] = proj[:, o:o + DA_V].astype(BF16)
    o += DA_V
    gq_ref[...] = proj[:, o:o + GLA_Q].astype(BF16)
    o += GLA_Q
    gk_ref[...] = proj[:, o:o + GLA_K].astype(BF16)
    o += GLA_K
    gv_ref[...] = proj[:, o:o + GLA_V].astype(BF16)
    o += GLA_V
    go_ref[...] = proj[:, o:o + GLA_OG].astype(BF16)

    g_low = _dot(xb, wgl_ref[...])
    z = _dot(g_low.astype(BF16), wg2_ref[...]) + bg2_ref[...]
    log_sig = jnp.minimum(z, 0.0) - jnp.log1p(jnp.exp(-jnp.abs(z)))
    la_ref[...] = log_sig / GLA_GATE_NORMALIZER


def _in_proj(x2, pos2, ln_g, ln_b, inv_freq, w_main, w_glow, w_gate2, b_gate2, tm):
    T = x2.shape[0]
    row = lambda n: pl.BlockSpec((tm, n), lambda i: (i, 0))
    full = lambda a: pl.BlockSpec(a.shape, lambda i: (0,) * a.ndim)
    out_shape = [jax.ShapeDtypeStruct((T, n), dt) for n, dt in (
        (DA_Q, BF16), (DA_K, BF16), (DA_V, BF16), (GLA_Q, BF16), (GLA_K, BF16),
        (GLA_V, BF16), (GLA_OG, BF16), (GLA_K, F32))]
    return pl.pallas_call(
        _in_proj_kernel,
        grid=(T // tm,),
        in_specs=[row(D_MODEL), row(1), full(ln_g), full(ln_b), full(inv_freq), full(w_main),
                  full(w_glow), full(w_gate2), full(b_gate2)],
        out_specs=[row(s.shape[1]) for s in out_shape],
        out_shape=out_shape,
        compiler_params=pltpu.CompilerParams(dimension_semantics=("arbitrary",),
                                             vmem_limit_bytes=VMEM_LIMIT),
        name="in_proj",
    )(x2, pos2, ln_g, ln_b, inv_freq, w_main, w_glow, w_gate2, b_gate2)


def _diff_attn_kernel(lam_init, lq1_ref, lk1_ref, lq2_ref, lk2_ref, g_ref, q_ref, k_ref, v_ref, o_ref):
    tq = q_ref.shape[1]
    tk = tq
    qi = pl.program_id(2)
    lam = (jnp.exp(jnp.sum(lq1_ref[...] * lk1_ref[...], axis=-1, keepdims=True))
           - jnp.exp(jnp.sum(lq2_ref[...] * lk2_ref[...], axis=-1, keepdims=True)) + lam_init)

    q = q_ref[0]
    lane = lax.broadcasted_iota(jnp.int32, q.shape, 1)
    zero = jnp.zeros_like(q)
    qs = (jnp.where(lane < DA_HEAD_DIM, q, zero), jnp.where(lane >= DA_HEAD_DIM, q, zero))

    def step(kb, vb, carry, mask):
        new = []
        for mi in range(2):
            m, l, a = carry[3 * mi:3 * mi + 3]
            s = _dot_nt(qs[mi], kb)
            if mask is not None:
                s = jnp.where(mask, s, -jnp.inf)
            m_new = jnp.maximum(m, jnp.max(s, axis=-1, keepdims=True))
            alpha = jnp.exp(m - m_new)
            p = jnp.exp(s - m_new)
            l = alpha * l + jnp.sum(p, axis=-1, keepdims=True)
            a = alpha * a + _dot(p.astype(BF16), vb)
            new += [m_new, l, a]
        return tuple(new)

    def body(j, carry):
        r0 = pl.multiple_of(j * tk, tk)
        return step(k_ref[0, pl.ds(r0, tk), :], v_ref[0, pl.ds(r0, tk), :], carry, None)

    init = (jnp.full((tq, 1), -jnp.inf, F32), jnp.zeros((tq, 1), F32), jnp.zeros((tq, DA_V_DIM), F32)) * 2
    carry = lax.fori_loop(0, qi, body, init)

    r0 = pl.multiple_of(qi * tk, tk)
    rq = lax.broadcasted_iota(jnp.int32, (tq, tk), 0) // CHUNK
    ck = lax.broadcasted_iota(jnp.int32, (tq, tk), 1) // CHUNK
    m1, l1, a1, m2, l2, a2 = step(k_ref[0, pl.ds(r0, tk), :], v_ref[0, pl.ds(r0, tk), :], carry, ck <= rq)

    o = a1 / l1 - lam * (a2 / l2)
    o = o * lax.rsqrt(jnp.mean(o * o, axis=-1, keepdims=True) + LN_EPS) * g_ref[...]
    o_ref[0] = (o * (1.0 - lam_init)).astype(o_ref.dtype)


def _diff_attn(q, k, v, lam_q1, lam_k1, lam_q2, lam_k2, subln_g, lam_init, tq):
    B, S, _ = q.shape
    vec = pl.BlockSpec((1, DA_HEAD_DIM), lambda b, h, i: (0, 0))
    return pl.pallas_call(
        functools.partial(_diff_attn_kernel, lam_init),
        grid=(B, DA_HEADS, S // tq),
        in_specs=[vec, vec, vec, vec,
                  pl.BlockSpec((1, DA_V_DIM), lambda b, h, i: (0, 0)),
                  pl.BlockSpec((1, tq, LANES), lambda b, h, i: (b, i, h)),
                  pl.BlockSpec((1, S, LANES), lambda b, h, i: (b, 0, h)),
                  pl.BlockSpec((1, S, LANES), lambda b, h, i: (b, 0, h))],
        out_specs=pl.BlockSpec((1, tq, LANES), lambda b, h, i: (b, i, h)),
        out_shape=jax.ShapeDtypeStruct((B, S, DA_V), BF16),
        compiler_params=pltpu.CompilerParams(dimension_semantics=("arbitrary",) * 3,
                                             vmem_limit_bytes=VMEM_LIMIT),
        name="diff_attn",
    )(lam_q1, lam_k1, lam_q2, lam_k2, subln_g, q, k, v)


def _gla_kernel(q_ref, k_ref, la_ref, v_ref, go_ref, ng_ref, o_ref, *, unroll):
    S = q_ref.shape[1]
    C = CHUNK
    row = lax.broadcasted_iota(jnp.int32, (C, C), 0)
    col = lax.broadcasted_iota(jnp.int32, (C, C), 1)
    causal = col <= row
    tri = jnp.where(causal, 1.0, 0.0).astype(BF16)
    lane = lax.broadcasted_iota(jnp.int32, (C, LANES), 1)
    lane_sq = lax.broadcasted_iota(jnp.int32, (LANES, LANES), 1)
    head_lanes = (lane < GLA_KEY_DIM, lane >= GLA_KEY_DIM)
    head_lanes_sq = (lane_sq < GLA_KEY_DIM, lane_sq >= GLA_KEY_DIM)

    def one_chunk(r0, states):
        g = la_ref[0, pl.ds(r0, C), :]
        g1 = g.astype(BF16)
        e1 = g - g1.astype(F32)
        g2 = e1.astype(BF16)
        g3 = (e1 - g2.astype(F32)).astype(BF16)
        bcum = _dot(tri, g1) + _dot(tri, g2) + _dot(tri, g3)
        b_last = bcum[C - 1:C, :]
        qf = q_ref[0, pl.ds(r0, C), :].astype(F32) * (GLA_KEY_DIM ** -0.5)
        kf = k_ref[0, pl.ds(r0, C), :].astype(F32)
        q_t = (qf * jnp.exp(bcum)).astype(BF16)
        k_t = (kf * jnp.exp(-bcum)).astype(BF16)
        k_end = (kf * jnp.exp(b_last - bcum)).astype(BF16)
        decay = jnp.exp(b_last)
        new_states = []
        for hh in range(2):
            qm = jnp.where(head_lanes[hh], q_t, jnp.zeros_like(q_t))
            att = jnp.where(causal, _dot_nt(qm, k_t), 0.0).astype(BF16)
            vh = v_ref[0, pl.ds(r0, C), hh * GLA_V_DIM:(hh + 1) * GLA_V_DIM]
            st = states[hh]
            o = _dot(att, vh) + _dot_nt(qm, st.astype(BF16))
            ds = jnp.where(head_lanes_sq[hh], _dot_tn(vh, k_end), 0.0)
            new_states.append(st * decay + ds)
            o = o * lax.rsqrt(jnp.mean(o * o, axis=-1, keepdims=True) + LN_EPS) * ng_ref[...]
            gate = go_ref[0, pl.ds(r0, C), hh * GLA_V_DIM:(hh + 1) * GLA_V_DIM].astype(F32)
            o = o * (gate * jax.nn.sigmoid(gate))
            o_ref[0, pl.ds(r0, C), hh * GLA_V_DIM:(hh + 1) * GLA_V_DIM] = o.astype(o_ref.dtype)
        return tuple(new_states)

    def body(i, states):
        for u in range(unroll):
            r0 = pl.multiple_of((i * unroll + u) * C, C)
            states = one_chunk(r0, states)
        return states

    init = (jnp.zeros((GLA_V_DIM, LANES), F32),) * 2
    lax.fori_loop(0, S // (C * unroll), body, init)


def _gla(gq, gk, la, gv, go, norm_g, unroll=2):
    B, S, _ = gq.shape
    pairs = GLA_HEADS // 2
    narrow = pl.BlockSpec((1, S, LANES), lambda b, p: (b, 0, p))
    wide = pl.BlockSpec((1, S, 2 * GLA_V_DIM), lambda b, p: (b, 0, p))
    return pl.pallas_call(
        functools.partial(_gla_kernel, unroll=unroll),
        grid=(B, pairs),
        in_specs=[narrow, narrow, narrow, wide, wide,
                  pl.BlockSpec((1, GLA_V_DIM), lambda b, p: (0, 0))],
        out_specs=wide,
        out_shape=jax.ShapeDtypeStruct((B, S, GLA_V), BF16),
        compiler_params=pltpu.CompilerParams(dimension_semantics=("arbitrary",) * 2,
                                             vmem_limit_bytes=VMEM_LIMIT),
        name="gla",
    )(gq, gk, la, gv, go, norm_g)


def _split3(a):
    hi = a.astype(BF16)
    lo = (a - hi.astype(F32)).astype(BF16)
    return hi, lo


def _mix_out_kernel(alpha, x_ref, da_ref, gl_ref, lng_ref, lnb_ref, wo_ref, g1_ref, b1_ref,
                    wr_hi_ref, wr_lo_ref, br_ref, h_ref, route_ref, cnt_ref):
    tm = x_ref.shape[0]
    i = pl.program_id(0)

    @pl.when(i == 0)
    def _():
        cnt_ref[...] = jnp.zeros_like(cnt_ref)

    xn = _layer_norm(x_ref[...], lng_ref[...], lnb_ref[...])
    mix = _dot(da_ref[...], wo_ref[0:DA_V, :]) + _dot(gl_ref[...], wo_ref[DA_V:, :])
    h = _layer_norm(alpha * xn + mix, g1_ref[...], b1_ref[...])
    h_ref[...] = h

    h_hi, h_lo = _split3(h)
    logits = (_dot(h_hi, wr_hi_ref[...]) + _dot(h_hi, wr_lo_ref[...]) + _dot(h_lo, wr_hi_ref[...])
              + br_ref[...])
    lane = lax.broadcasted_iota(jnp.int32, (tm, LANES), 1)
    neg = -jnp.inf
    big = jnp.int32(LANES)

    def first_argmax(vals, valid):
        v = jnp.where(valid, vals, neg)
        mx = jnp.max(v, axis=-1, keepdims=True)
        idx = jnp.min(jnp.where(valid & (v == mx), lane, big), axis=-1, keepdims=True)
        return mx, idx

    is_group = lane < N_GROUPS
    g_max, g_top = first_argmax(logits, is_group)
    p_g = 1.0 / jnp.sum(jnp.where(is_group, jnp.exp(logits - g_max), 0.0), axis=-1, keepdims=True)

    e_lo = EXPERT_LANE0 + g_top * EXPERTS_PER_GROUP
    in_group = (lane >= e_lo) & (lane < e_lo + EXPERTS_PER_GROUP)
    v0, i0 = first_argmax(logits, in_group)
    v1, i1 = first_argmax(logits, in_group & (lane != i0))
    w1 = jnp.exp(v1 - v0)
    gate0 = p_g / (1.0 + w1)
    gate1 = p_g * w1 / (1.0 + w1)
    e0 = i0 - EXPERT_LANE0
    e1 = i1 - EXPERT_LANE0

    oh0 = jnp.where(lane == e0, 1.0, 0.0)
    oh1 = jnp.where(lane == e1, 1.0, 0.0)
    oh = oh0 + oh1
    r = lax.broadcasted_iota(jnp.int32, (tm, tm), 0)
    cidx = lax.broadcasted_iota(jnp.int32, (tm, tm), 1)
    strict_lower = jnp.where(cidx < r, 1.0, 0.0).astype(BF16)
    before = _dot(strict_lower, oh.astype(BF16)) + cnt_ref[0:1, :]
    rank0 = jnp.sum(oh0 * before, axis=-1, keepdims=True)
    rank1 = jnp.sum(oh1 * before, axis=-1, keepdims=True)
    cnt_ref[...] = cnt_ref[...] + jnp.sum(oh, axis=0, keepdims=True)

    rec = jnp.zeros((tm, LANES), F32)
    for ln, val in ((R_E0, e0.astype(F32)), (R_E1, e1.astype(F32)), (R_G0, gate0), (R_G1, gate1),
                    (R_RANK0, rank0), (R_RANK1, rank1)):
        rec = jnp.where(lane == ln, val, rec)
    route_ref[...] = rec


def _mix_out(x2, da2, gl2, ln_g, ln_b, w_o, ln1_g, ln1_b, wr_hi, wr_lo, b_r, alpha, tm):
    T = x2.shape[0]
    row = lambda n: pl.BlockSpec((tm, n), lambda i: (i, 0))
    full = lambda a: pl.BlockSpec(a.shape, lambda i: (0,) * a.ndim)
    return pl.pallas_call(
        functools.partial(_mix_out_kernel, alpha),
        grid=(T // tm,),
        in_specs=[row(D_MODEL), row(DA_V), row(GLA_V), full(ln_g), full(ln_b), full(w_o),
                  full(ln1_g), full(ln1_b), full(wr_hi), full(wr_lo), full(b_r)],
        out_specs=[row(D_MODEL), row(LANES), pl.BlockSpec((8, LANES), lambda i: (0, 0))],
        out_shape=[jax.ShapeDtypeStruct((T, D_MODEL), F32), jax.ShapeDtypeStruct((T, LANES), F32),
                   jax.ShapeDtypeStruct((8, LANES), F32)],
        compiler_params=pltpu.CompilerParams(dimension_semantics=("arbitrary",),
                                             vmem_limit_bytes=VMEM_LIMIT),
        name="mix_out",
    )(x2, da2, gl2, ln_g, ln_b, w_o, ln1_g, ln1_b, wr_hi, wr_lo, b_r)


def _dispatch_kernel(dest_ref, h_ref, xs_ref, sem):
    tm = h_ref.shape[0]
    base = pl.program_id(0) * (tm * TOP_K)

    def row_copy(j, slot):
        t = j // TOP_K
        return pltpu.make_async_copy(h_ref.at[pl.ds(t, 1), :], xs_ref.at[pl.ds(slot, 1), :], sem)

    def start(j, c):
        row_copy(j, dest_ref[base + j]).start()
        return c

    lax.fori_loop(0, tm * TOP_K, start, 0)

    def wait(j, c):
        row_copy(j, 0).wait()
        return c

    lax.fori_loop(0, tm * TOP_K, wait, 0)


def _dispatch(dest_flat, h, tm):
    T = h.shape[0]
    return pl.pallas_call(
        _dispatch_kernel,
        grid_spec=pltpu.PrefetchScalarGridSpec(
            num_scalar_prefetch=1,
            grid=(T // tm,),
            in_specs=[pl.BlockSpec((tm, D_MODEL), lambda i, d: (i, 0))],
            out_specs=pl.BlockSpec(memory_space=pl.ANY),
            scratch_shapes=[pltpu.SemaphoreType.DMA(())]),
        out_shape=jax.ShapeDtypeStruct((T * TOP_K, D_MODEL), F32),
        compiler_params=pltpu.CompilerParams(dimension_semantics=("arbitrary",),
                                             vmem_limit_bytes=VMEM_LIMIT),
        name="dispatch",
    )(dest_flat, h)


def _experts_kernel(tile_ref, exp_ref, lo_ref, hi_ref, x_ref, wg_ref, wu_ref, wd_ref, y_ref):
    bm = x_ref.shape[0]
    v = pl.program_id(0)
    lo = lo_ref[v]
    hi = hi_ref[v]

    @pl.when(hi > lo)
    def _():
        xb = x_ref[...].astype(BF16)
        g = _dot(xb, wg_ref[0].astype(BF16))
        u = _dot(xb, wu_ref[0].astype(BF16))
        mid = (g * jax.nn.sigmoid(g) * u).astype(BF16)
        y = _dot(mid, wd_ref[0].astype(BF16))
        rows = tile_ref[v] * bm + lax.broadcasted_iota(jnp.int32, (bm, 1), 0)
        mine = (rows >= lo) & (rows < hi)
        first_of_tile = rows == lo
        starts_tile = lo == tile_ref[v] * bm

        @pl.when(starts_tile)
        def _():
            y_ref[...] = jnp.where(mine, y, 0.0)

        @pl.when(jnp.logical_not(starts_tile))
        def _():
            y_ref[...] = jnp.where(mine, y, y_ref[...])


def _experts(meta, xs, w_gate, w_up, w_down, bm):
    n_rows = xs.shape[0]
    n_visits = meta[0].shape[0]
    return pl.pallas_call(
        _experts_kernel,
        grid_spec=pltpu.PrefetchScalarGridSpec(
            num_scalar_prefetch=4,
            grid=(n_visits,),
            in_specs=[pl.BlockSpec((bm, D_MODEL), lambda v, t, e, lo, hi: (t[v], 0)),
                      pl.BlockSpec((1, D_MODEL, D_EXPERT), lambda v, t, e, lo, hi: (e[v], 0, 0)),
                      pl.BlockSpec((1, D_MODEL, D_EXPERT), lambda v, t, e, lo, hi: (e[v], 0, 0)),
                      pl.BlockSpec((1, D_EXPERT, D_MODEL), lambda v, t, e, lo, hi: (e[v], 0, 0))],
            out_specs=pl.BlockSpec((bm, D_MODEL), lambda v, t, e, lo, hi: (t[v], 0))),
        out_shape=jax.ShapeDtypeStruct((n_rows, D_MODEL), F32),
        compiler_params=pltpu.CompilerParams(dimension_semantics=("arbitrary",),
                                             vmem_limit_bytes=VMEM_LIMIT),
        name="experts",
    )(*meta, xs, w_gate, w_up, w_down)


def _visit_metadata(counts, n_rows, bm):
    n_tiles = n_rows // bm
    n_visits = n_tiles + N_EXPERTS - 1
    ends = jnp.cumsum(counts)
    starts = ends - counts
    first_tile = starts // bm
    last_tile = jnp.maximum(ends - 1, 0) // bm
    n_vis = jnp.where(counts > 0, last_tile - first_tile + 1, 0)
    vis_end = jnp.cumsum(n_vis)
    vis_start = vis_end - n_vis
    v = jnp.arange(n_visits, dtype=jnp.int32)
    total = vis_end[-1]
    vc = jnp.minimum(v, total - 1)
    e = jnp.sum((vis_end[None, :] <= vc[:, None]).astype(jnp.int32), axis=1)
    tile = first_tile[e] + (vc - vis_start[e])
    lo = jnp.maximum(starts[e], tile * bm)
    hi = jnp.minimum(ends[e], (tile + 1) * bm)
    hi = jnp.where(v < total, hi, lo)
    i32 = lambda a: a.astype(jnp.int32)
    return i32(tile), i32(e), i32(lo), i32(hi)


def _combine_kernel(alpha, dest_ref, h_ref, route_ref, g_ref, b_ref, y_ref, o_ref, buf, sem):
    tm = h_ref.shape[0]
    base = pl.program_id(0) * (tm * TOP_K)

    def row_copy(j, src):
        slot = (j % TOP_K) * tm + j // TOP_K
        return pltpu.make_async_copy(y_ref.at[pl.ds(src, 1), :], buf.at[pl.ds(slot, 1), :], sem)

    def start(j, c):
        row_copy(j, dest_ref[base + j]).start()
        return c

    lax.fori_loop(0, tm * TOP_K, start, 0)

    def wait(j, c):
        row_copy(j, 0).wait()
        return c

    lax.fori_loop(0, tm * TOP_K, wait, 0)

    rec = route_ref[...]
    ffn = rec[:, R_G0:R_G0 + 1] * buf[0:tm, :] + rec[:, R_G1:R_G1 + 1] * buf[tm:2 * tm, :]
    o_ref[...] = _layer_norm(alpha * h_ref[...] + ffn, g_ref[...], b_ref[...])


def _combine(dest_flat, h, route, ln2_g, ln2_b, y_sorted, alpha, tm):
    T = h.shape[0]
    return pl.pallas_call(
        functools.partial(_combine_kernel, alpha),
        grid_spec=pltpu.PrefetchScalarGridSpec(
            num_scalar_prefetch=1,
            grid=(T // tm,),
            in_specs=[pl.BlockSpec((tm, D_MODEL), lambda i, d: (i, 0)),
                      pl.BlockSpec((tm, LANES), lambda i, d: (i, 0)),
                      pl.BlockSpec((1, D_MODEL), lambda i, d: (0, 0)),
                      pl.BlockSpec((1, D_MODEL), lambda i, d: (0, 0)),
                      pl.BlockSpec(memory_space=pl.ANY)],
            out_specs=pl.BlockSpec((tm, D_MODEL), lambda i, d: (i, 0)),
            scratch_shapes=[pltpu.VMEM((TOP_K * tm, D_MODEL), F32), pltpu.SemaphoreType.DMA(())]),
        out_shape=jax.ShapeDtypeStruct((T, D_MODEL), F32),
        compiler_params=pltpu.CompilerParams(dimension_semantics=("arbitrary",),
                                             vmem_limit_bytes=VMEM_LIMIT),
        name="combine",
    )(dest_flat, h, route, ln2_g, ln2_b, y_sorted)


def _rope_column_order():
    within = np.concatenate([np.arange(0, DA_HEAD_DIM, 2), np.arange(1, DA_HEAD_DIM, 2)])
    return np.concatenate([m * DA_HEAD_DIM + within for m in range(2 * DA_HEADS)])


def kernel(x, positions, ln_in_g, ln_in_b, w_in, lam_q1, lam_k1, lam_q2, lam_k2, da_subln_g, gla_w_gate2, gla_b_gate2, gla_norm_g, w_o, ln1_g, ln1_b, router_w_group, router_b_group, router_w_expert, router_b_expert, w_gate, w_up, w_down, ln2_g, ln2_b):
    B, S, D = x.shape
    T = B * S
    depth = w_in.shape[0]
    alpha = (2 * depth) ** 0.25
    row2 = lambda a: a.reshape(1, -1)

    inv_freq = ROPE_THETA ** (-jnp.arange(0, DA_HEAD_DIM, 2, dtype=F32) / DA_HEAD_DIM)
    inv_freq = jnp.tile(inv_freq, LANES // (DA_HEAD_DIM // 2)).reshape(1, LANES)
    perm = _rope_column_order()
    pos2 = positions.reshape(T, 1)

    cur = x.reshape(T, D)
    cur_g, cur_b = row2(ln_in_g), row2(ln_in_b)
    for l in range(depth):
        w = w_in[l]
        w_q = w[:, :DA_Q][:, perm]
        w_k = w[:, DA_Q:DA_Q + DA_K][:, perm]
        w_main = jnp.concatenate([w_q, w_k, w[:, DA_Q + DA_K:D_MAIN]], axis=1).astype(BF16)
        w_glow = jnp.pad(w[:, D_MAIN:], ((0, 0), (0, LANES - GLA_GATE_RANK))).astype(BF16)
        w_gate2 = jnp.pad(gla_w_gate2[l], ((0, LANES - GLA_GATE_RANK), (0, 0))).astype(BF16)

        q, k, v, gq, gk, gv, go, la = _in_proj(cur, pos2, cur_g, cur_b, inv_freq, w_main, w_glow,
                                               w_gate2, row2(gla_b_gate2[l]), tm=512)
        lam_init = 0.8 - 0.6 * math.exp(-0.3 * l)
        sh = lambda a: a.reshape(B, S, a.shape[-1])
        da = _diff_attn(sh(q), sh(k), sh(v), row2(lam_q1[l]), row2(lam_k1[l]), row2(lam_q2[l]),
                        row2(lam_k2[l]), row2(da_subln_g[l]), lam_init, tq=256)
        gl = _gla(sh(gq), sh(gk), sh(la), sh(gv), sh(go), row2(gla_norm_g[l]))

        w_r = jnp.zeros((D, LANES), F32)
        w_r = w_r.at[:, :N_GROUPS].set(router_w_group[l])
        w_r = w_r.at[:, EXPERT_LANE0:EXPERT_LANE0 + N_EXPERTS].set(router_w_expert[l])
        b_r = jnp.zeros((1, LANES), F32)
        b_r = b_r.at[0, :N_GROUPS].set(router_b_group[l])
        b_r = b_r.at[0, EXPERT_LANE0:EXPERT_LANE0 + N_EXPERTS].set(router_b_expert[l])
        wr_hi = w_r.astype(BF16)
        wr_lo = (w_r - wr_hi.astype(F32)).astype(BF16)

        h, route, cnt = _mix_out(cur, da.reshape(T, DA_V), gl.reshape(T, GLA_V), cur_g, cur_b,
                                 w_o[l].astype(BF16), row2(ln1_g[l]), row2(ln1_b[l]), wr_hi, wr_lo, b_r,
                                 alpha, tm=512)

        counts = cnt[0, :N_EXPERTS].astype(jnp.int32)
        seg_start = jnp.cumsum(counts) - counts
        eid = route[:, R_E0:R_E1 + 1].astype(jnp.int32)
        rank = route[:, R_RANK0:R_RANK1 + 1].astype(jnp.int32)
        onehot = eid[..., None] == jnp.arange(N_EXPERTS, dtype=jnp.int32)
        dest = jnp.sum(jnp.where(onehot, seg_start, 0), axis=-1) + rank
        dest_flat = dest.reshape(T * TOP_K)

        xs = _dispatch(dest_flat, h, tm=256)
        meta = _visit_metadata(counts, T * TOP_K, bm=256)
        ys = _experts(meta, xs, w_gate[l], w_up[l], w_down[l], bm=256)
        cur = _combine(dest_flat, h, route, row2(ln2_g[l]), row2(ln2_b[l]), ys, alpha, tm=256)
        cur_g, cur_b = None, None
        if l + 1 < depth:
            raise NotImplementedError("only DEPTH == 1 is supported")
    return cur.reshape(B, S, D)
```

```python
import functools
import math

import jax
import jax.numpy as jnp
import numpy as np
from jax import lax
from jax.experimental import pallas as pl
from jax.experimental.pallas import tpu as pltpu

F32 = jnp.float32
BF16 = jnp.bfloat16

D_MODEL = 1024
CHUNK = 64
ROPE_THETA = 10000.0
LN_EPS = 1e-5

DA_HEADS = 4
DA_V_DIM = D_MODEL // (2 * DA_HEADS)
DA_HEAD_DIM = DA_V_DIM // 2
GLA_HEADS = 4
GLA_V_DIM = D_MODEL // (2 * GLA_HEADS)
GLA_KEY_DIM = GLA_V_DIM // 2
GLA_GATE_RANK = 16
GLA_GATE_NORMALIZER = 16.0

DA_Q = DA_HEADS * 2 * DA_HEAD_DIM
DA_K = DA_Q
DA_V = DA_HEADS * DA_V_DIM
GLA_Q = GLA_HEADS * GLA_KEY_DIM
GLA_K = GLA_Q
GLA_V = GLA_HEADS * GLA_V_DIM
GLA_OG = GLA_V
D_MAIN = DA_Q + DA_K + DA_V + GLA_Q + GLA_K + GLA_V + GLA_OG

N_GROUPS = 4
EXPERTS_PER_GROUP = 8
N_EXPERTS = N_GROUPS * EXPERTS_PER_GROUP
TOP_K = 2
D_EXPERT = D_MODEL // 2

LANES = 128
ROW_SUB = D_MODEL // LANES
DMA_UNROLL = 8
VMEM_LIMIT = 48 * 1024 * 1024

R_E0, R_E1, R_G0, R_G1, R_RANK0, R_RANK1 = 0, 1, 2, 3, 4, 5
EXPERT_LANE0 = 32


def _layer_norm(x, g, b):
    mu = jnp.mean(x, axis=-1, keepdims=True)
    xc = x - mu
    var = jnp.mean(xc * xc, axis=-1, keepdims=True)
    return xc * lax.rsqrt(var + LN_EPS) * g + b


def _dot(a, b):
    return jnp.dot(a, b, preferred_element_type=F32)


def _dot_nt(a, b):
    return lax.dot_general(a, b, (((1,), (1,)), ((), ())), preferred_element_type=F32)


def _dot_tn(a, b):
    return lax.dot_general(a, b, (((0,), (0,)), ((), ())), preferred_element_type=F32)


def _in_proj_kernel(x_ref, pos_ref, g_ref, b_ref, invf_ref, w_ref, wgl_ref, wg2_ref, bg2_ref,
                    q_ref, k_ref, v_ref, gq_ref, gk_ref, gv_ref, go_ref, la_ref):
    tm = x_ref.shape[0]
    xn = _layer_norm(x_ref[...], g_ref[...], b_ref[...])
    xb = xn.astype(BF16)
    proj = _dot(xb, w_ref[...])

    ang = pos_ref[...].astype(F32) * invf_ref[...]
    c = jnp.cos(ang)
    s = jnp.sin(ang)
    lane = lax.broadcasted_iota(jnp.int32, (tm, LANES), 1)
    first = (lane & (DA_HEAD_DIM // 2)) == 0
    s_lo = jnp.where(first, -s, 0.0)
    s_hi = jnp.where(first, 0.0, s)
    reps = DA_Q // LANES
    c4 = jnp.concatenate([c] * reps, axis=1)
    s_lo4 = jnp.concatenate([s_lo] * reps, axis=1)
    s_hi4 = jnp.concatenate([s_hi] * reps, axis=1)
    half = DA_HEAD_DIM // 2

    def rope(t):
        up = pltpu.roll(t, DA_Q - half, 1)
        dn = pltpu.roll(t, half, 1)
        return t * c4 + up * s_lo4 + dn * s_hi4

    o = 0
    q = rope(proj[:, o:o + DA_Q]) * (DA_HEAD_DIM ** -0.5)
    o += DA_Q
    k = rope(proj[:, o:o + DA_K])
    o += DA_K
    q_ref[...] = q.astype(BF16)
    k_ref[...] = k.astype(BF16)
    v_ref[...] = proj[:, o:o + DA_V].astype(BF16)
    o += DA_V
    gq_ref[...] = proj[:, o:o + GLA_Q].astype(BF16)
    o += GLA_Q
    gk_ref[...] = proj[:, o:o + GLA_K].astype(BF16)
    o += GLA_K
    gv_ref[...] = proj[:, o:o + GLA_V].astype(BF16)
    o += GLA_V
    go_ref[...] = proj[:, o:o + GLA_OG].astype(BF16)

    g_low = _dot(xb, wgl_ref[...])
    z = _dot(g_low.astype(BF16), wg2_ref[...]) + bg2_ref[...]
    log_sig = jnp.minimum(z, 0.0) - jnp.log1p(jnp.exp(-jnp.abs(z)))
    la_ref[...] = log_sig / GLA_GATE_NORMALIZER


def _in_proj(x2, pos2, ln_g, ln_b, inv_freq, w_main, w_glow, w_gate2, b_gate2, tm):
    T = x2.shape[0]
    row = lambda n: pl.BlockSpec((tm, n), lambda i: (i, 0))
    full = lambda a: pl.BlockSpec(a.shape, lambda i: (0,) * a.ndim)
    out_shape = [jax.ShapeDtypeStruct((T, n), dt) for n, dt in (
        (DA_Q, BF16), (DA_K, BF16), (DA_V, BF16), (GLA_Q, BF16), (GLA_K, BF16),
        (GLA_V, BF16), (GLA_OG, BF16), (GLA_K, F32))]
    return pl.pallas_call(
        _in_proj_kernel,
        grid=(T // tm,),
        in_specs=[row(D_MODEL), row(1), full(ln_g), full(ln_b), full(inv_freq), full(w_main),
                  full(w_glow), full(w_gate2), full(b_gate2)],
        out_specs=[row(s.shape[1]) for s in out_shape],
        out_shape=out_shape,
        compiler_params=pltpu.CompilerParams(dimension_semantics=("arbitrary",),
                                             vmem_limit_bytes=VMEM_LIMIT),
        name="in_proj",
    )(x2, pos2, ln_g, ln_b, inv_freq, w_main, w_glow, w_gate2, b_gate2)


def _diff_attn_kernel(lam_init, lq1_ref, lk1_ref, lq2_ref, lk2_ref, g_ref, q_ref, k_ref, v_ref, o_ref):
    tq = q_ref.shape[1]
    tk = tq
    qi = pl.program_id(2)
    lam = (jnp.exp(jnp.sum(lq1_ref[...] * lk1_ref[...], axis=-1, keepdims=True))
           - jnp.exp(jnp.sum(lq2_ref[...] * lk2_ref[...], axis=-1, keepdims=True)) + lam_init)

    q = q_ref[0]
    lane = lax.broadcasted_iota(jnp.int32, q.shape, 1)
    zero = jnp.zeros_like(q)
    qs = (jnp.where(lane < DA_HEAD_DIM, q, zero), jnp.where(lane >= DA_HEAD_DIM, q, zero))

    def step(kb, vb, carry, mask):
        new = []
        for mi in range(2):
            m, l, a = carry[3 * mi:3 * mi + 3]
            s = _dot_nt(qs[mi], kb)
            if mask is not None:
                s = jnp.where(mask, s, -jnp.inf)
            m_new = jnp.maximum(m, jnp.max(s, axis=-1, keepdims=True))
            alpha = jnp.exp(m - m_new)
            p = jnp.exp(s - m_new)
            l = alpha * l + jnp.sum(p, axis=-1, keepdims=True)
            a = alpha * a + _dot(p.astype(BF16), vb)
            new += [m_new, l, a]
        return tuple(new)

    def body(j, carry):
        r0 = pl.multiple_of(j * tk, tk)
        return step(k_ref[0, pl.ds(r0, tk), :], v_ref[0, pl.ds(r0, tk), :], carry, None)

    init = (jnp.full((tq, 1), -jnp.inf, F32), jnp.zeros((tq, 1), F32), jnp.zeros((tq, DA_V_DIM), F32)) * 2
    carry = lax.fori_loop(0, qi, body, init)

    r0 = pl.multiple_of(qi * tk, tk)
    rq = lax.broadcasted_iota(jnp.int32, (tq, tk), 0) // CHUNK
    ck = lax.broadcasted_iota(jnp.int32, (tq, tk), 1) // CHUNK
    m1, l1, a1, m2, l2, a2 = step(k_ref[0, pl.ds(r0, tk), :], v_ref[0, pl.ds(r0, tk), :], carry, ck <= rq)

    o = a1 / l1 - lam * (a2 / l2)
    o = o * lax.rsqrt(jnp.mean(o * o, axis=-1, keepdims=True) + LN_EPS) * g_ref[...]
    o_ref[0] = (o * (1.0 - lam_init)).astype(o_ref.dtype)


def _diff_attn(q, k, v, lam_q1, lam_k1, lam_q2, lam_k2, subln_g, lam_init, tq):
    B, S, _ = q.shape
    vec = pl.BlockSpec((1, DA_HEAD_DIM), lambda b, h, i: (0, 0))
    return pl.pallas_call(
        functools.partial(_diff_attn_kernel, lam_init),
        grid=(B, DA_HEADS, S // tq),
        in_specs=[vec, vec, vec, vec,
                  pl.BlockSpec((1, DA_V_DIM), lambda b, h, i: (0, 0)),
                  pl.BlockSpec((1, tq, LANES), lambda b, h, i: (b, i, h)),
                  pl.BlockSpec((1, S, LANES), lambda b, h, i: (b, 0, h)),
                  pl.BlockSpec((1, S, LANES), lambda b, h, i: (b, 0, h))],
        out_specs=pl.BlockSpec((1, tq, LANES), lambda b, h, i: (b, i, h)),
        out_shape=jax.ShapeDtypeStruct((B, S, DA_V), BF16),
        compiler_params=pltpu.CompilerParams(dimension_semantics=("arbitrary",) * 3,
                                             vmem_limit_bytes=VMEM_LIMIT),
        name="diff_attn",
    )(lam_q1, lam_k1, lam_q2, lam_k2, subln_g, q, k, v)


def _gla_kernel(q_ref, k_ref, la_ref, v_ref, go_ref, ng_ref, o_ref, *, unroll):
    S = q_ref.shape[1]
    C = CHUNK
    row = lax.broadcasted_iota(jnp.int32, (C, C), 0)
    col = lax.broadcasted_iota(jnp.int32, (C, C), 1)
    causal = col <= row
    tri = jnp.where(causal, 1.0, 0.0).astype(BF16)
    lane = lax.broadcasted_iota(jnp.int32, (C, LANES), 1)
    lane_sq = lax.broadcasted_iota(jnp.int32, (LANES, LANES), 1)
    head_lanes = (lane < GLA_KEY_DIM, lane >= GLA_KEY_DIM)
    head_lanes_sq = (lane_sq < GLA_KEY_DIM, lane_sq >= GLA_KEY_DIM)

    def one_chunk(r0, states):
        g = la_ref[0, pl.ds(r0, C), :]
        g1 = g.astype(BF16)
        e1 = g - g1.astype(F32)
        g2 = e1.astype(BF16)
        g3 = (e1 - g2.astype(F32)).astype(BF16)
        bcum = _dot(tri, g1) + _dot(tri, g2) + _dot(tri, g3)
        b_last = bcum[C - 1:C, :]
        qf = q_ref[0, pl.ds(r0, C), :].astype(F32) * (GLA_KEY_DIM ** -0.5)
        kf = k_ref[0, pl.ds(r0, C), :].astype(F32)
        q_t = (qf * jnp.exp(bcum)).astype(BF16)
        k_t = (kf * jnp.exp(-bcum)).astype(BF16)
        k_end = (kf * jnp.exp(b_last - bcum)).astype(BF16)
        decay = jnp.exp(b_last)
        new_states = []
        for hh in range(2):
            qm = jnp.where(head_lanes[hh], q_t, jnp.zeros_like(q_t))
            att = jnp.where(causal, _dot_nt(qm, k_t), 0.0).astype(BF16)
            vh = v_ref[0, pl.ds(r0, C), hh * GLA_V_DIM:(hh + 1) * GLA_V_DIM]
            st = states[hh]
            o = _dot(att, vh) + _dot_nt(qm, st.astype(BF16))
            ds = jnp.where(head_lanes_sq[hh], _dot_tn(vh, k_end), 0.0)
            new_states.append(st * decay + ds)
            o = o * lax.rsqrt(jnp.mean(o * o, axis=-1, keepdims=True) + LN_EPS) * ng_ref[...]
            gate = go_ref[0, pl.ds(r0, C), hh * GLA_V_DIM:(hh + 1) * GLA_V_DIM].astype(F32)
            o = o * (gate * jax.nn.sigmoid(gate))
            o_ref[0, pl.ds(r0, C), hh * GLA_V_DIM:(hh + 1) * GLA_V_DIM] = o.astype(o_ref.dtype)
        return tuple(new_states)

    def body(i, states):
        for u in range(unroll):
            r0 = pl.multiple_of((i * unroll + u) * C, C)
            states = one_chunk(r0, states)
        return states

    init = (jnp.zeros((GLA_V_DIM, LANES), F32),) * 2
    lax.fori_loop(0, S // (C * unroll), body, init)


def _gla(gq, gk, la, gv, go, norm_g, unroll=2):
    B, S, _ = gq.shape
    pairs = GLA_HEADS // 2
    narrow = pl.BlockSpec((1, S, LANES), lambda b, p: (b, 0, p))
    wide = pl.BlockSpec((1, S, 2 * GLA_V_DIM), lambda b, p: (b, 0, p))
    return pl.pallas_call(
        functools.partial(_gla_kernel, unroll=unroll),
        grid=(B, pairs),
        in_specs=[narrow, narrow, narrow, wide, wide,
                  pl.BlockSpec((1, GLA_V_DIM), lambda b, p: (0, 0))],
        out_specs=wide,
        out_shape=jax.ShapeDtypeStruct((B, S, GLA_V), BF16),
        compiler_params=pltpu.CompilerParams(dimension_semantics=("arbitrary",) * 2,
                                             vmem_limit_bytes=VMEM_LIMIT),
        name="gla",
    )(gq, gk, la, gv, go, norm_g)


def _split3(a):
    hi = a.astype(BF16)
    lo = (a - hi.astype(F32)).astype(BF16)
    return hi, lo


def _mix_out_kernel(alpha, x_ref, da_ref, gl_ref, lng_ref, lnb_ref, wo_ref, g1_ref, b1_ref,
                    wr_hi_ref, wr_lo_ref, br_ref, h_ref, route_ref, cnt_ref):
    tm = x_ref.shape[0]
    i = pl.program_id(0)

    @pl.when(i == 0)
    def _():
        cnt_ref[...] = jnp.zeros_like(cnt_ref)

    xn = _layer_norm(x_ref[...], lng_ref[...], lnb_ref[...])
    mix = _dot(da_ref[...], wo_ref[0:DA_V, :]) + _dot(gl_ref[...], wo_ref[DA_V:, :])
    h = _layer_norm(alpha * xn + mix, g1_ref[...], b1_ref[...])
    h_ref[...] = h

    h_hi, h_lo = _split3(h)
    logits = (_dot(h_hi, wr_hi_ref[...]) + _dot(h_hi, wr_lo_ref[...]) + _dot(h_lo, wr_hi_ref[...])
              + br_ref[...])
    lane = lax.broadcasted_iota(jnp.int32, (tm, LANES), 1)
    neg = -jnp.inf
    big = jnp.int32(LANES)

    def first_argmax(vals, valid):
        v = jnp.where(valid, vals, neg)
        mx = jnp.max(v, axis=-1, keepdims=True)
        idx = jnp.min(jnp.where(valid & (v == mx), lane, big), axis=-1, keepdims=True)
        return mx, idx

    is_group = lane < N_GROUPS
    g_max, g_top = first_argmax(logits, is_group)
    p_g = 1.0 / jnp.sum(jnp.where(is_group, jnp.exp(logits - g_max), 0.0), axis=-1, keepdims=True)

    e_lo = EXPERT_LANE0 + g_top * EXPERTS_PER_GROUP
    in_group = (lane >= e_lo) & (lane < e_lo + EXPERTS_PER_GROUP)
    v0, i0 = first_argmax(logits, in_group)
    v1, i1 = first_argmax(logits, in_group & (lane != i0))
    w1 = jnp.exp(v1 - v0)
    gate0 = p_g / (1.0 + w1)
    gate1 = p_g * w1 / (1.0 + w1)
    e0 = i0 - EXPERT_LANE0
    e1 = i1 - EXPERT_LANE0

    oh0 = jnp.where(lane == e0, 1.0, 0.0)
    oh1 = jnp.where(lane == e1, 1.0, 0.0)
    oh = oh0 + oh1
    r = lax.broadcasted_iota(jnp.int32, (tm, tm), 0)
    cidx = lax.broadcasted_iota(jnp.int32, (tm, tm), 1)
    strict_lower = jnp.where(cidx < r, 1.0, 0.0).astype(BF16)
    before = _dot(strict_lower, oh.astype(BF16)) + cnt_ref[0:1, :]
    rank0 = jnp.sum(oh0 * before, axis=-1, keepdims=True)
    rank1 = jnp.sum(oh1 * before, axis=-1, keepdims=True)
    cnt_ref[...] = cnt_ref[...] + jnp.sum(oh, axis=0, keepdims=True)

    rec = jnp.zeros((tm, LANES), F32)
    for ln, val in ((R_E0, e0.astype(F32)), (R_E1, e1.astype(F32)), (R_G0, gate0), (R_G1, gate1),
                    (R_RANK0, rank0), (R_RANK1, rank1)):
        rec = jnp.where(lane == ln, val, rec)
    route_ref[...] = rec


def _mix_out(x2, da2, gl2, ln_g, ln_b, w_o, ln1_g, ln1_b, wr_hi, wr_lo, b_r, alpha, tm):
    T = x2.shape[0]
    row = lambda n: pl.BlockSpec((tm, n), lambda i: (i, 0))
    full = lambda a: pl.BlockSpec(a.shape, lambda i: (0,) * a.ndim)
    return pl.pallas_call(
        functools.partial(_mix_out_kernel, alpha),
        grid=(T // tm,),
        in_specs=[row(D_MODEL), row(DA_V), row(GLA_V), full(ln_g), full(ln_b), full(w_o),
                  full(ln1_g), full(ln1_b), full(wr_hi), full(wr_lo), full(b_r)],
        out_specs=[row(D_MODEL), row(LANES), pl.BlockSpec((8, LANES), lambda i: (0, 0))],
        out_shape=[jax.ShapeDtypeStruct((T, D_MODEL), F32), jax.ShapeDtypeStruct((T, LANES), F32),
                   jax.ShapeDtypeStruct((8, LANES), F32)],
        compiler_params=pltpu.CompilerParams(dimension_semantics=("arbitrary",),
                                             vmem_limit_bytes=VMEM_LIMIT),
        name="mix_out",
    )(x2, da2, gl2, ln_g, ln_b, w_o, ln1_g, ln1_b, wr_hi, wr_lo, b_r)


def _rows_to_tiles(dst_ref, val):
    for s in range(ROW_SUB):
        dst_ref[:, s, :] = val[:, s * LANES:(s + 1) * LANES]


def _tiles_to_rows(src_ref, r0, n):
    return jnp.concatenate([src_ref[pl.ds(r0, n), s, :] for s in range(ROW_SUB)], axis=1)


def _dispatch_kernel(dest_ref, h_ref, xs_ref, stage, sem):
    tm = h_ref.shape[0]
    base = pl.program_id(0) * (tm * TOP_K)
    _rows_to_tiles(stage, h_ref[...])

    def row_copy(t, slot):
        return pltpu.make_async_copy(stage.at[t], xs_ref.at[slot], sem)

    def start(i, c):
        for u in range(DMA_UNROLL):
            j = i * DMA_UNROLL + u
            row_copy(j // TOP_K, dest_ref[base + j]).start(priority=u % 2)
        return c

    lax.fori_loop(0, tm * TOP_K // DMA_UNROLL, start, 0)

    def wait(i, c):
        for u in range(DMA_UNROLL):
            row_copy(0, 0).wait()
        return c

    lax.fori_loop(0, tm * TOP_K // DMA_UNROLL, wait, 0)


def _dispatch(dest_flat, h, tm):
    T = h.shape[0]
    return pl.pallas_call(
        _dispatch_kernel,
        grid_spec=pltpu.PrefetchScalarGridSpec(
            num_scalar_prefetch=1,
            grid=(T // tm,),
            in_specs=[pl.BlockSpec((tm, D_MODEL), lambda i, d: (i, 0))],
            out_specs=pl.BlockSpec(memory_space=pl.ANY),
            scratch_shapes=[pltpu.VMEM((tm, ROW_SUB, LANES), F32), pltpu.SemaphoreType.DMA(())]),
        out_shape=jax.ShapeDtypeStruct((T * TOP_K, ROW_SUB, LANES), F32),
        compiler_params=pltpu.CompilerParams(dimension_semantics=("arbitrary",),
                                             vmem_limit_bytes=VMEM_LIMIT),
        name="dispatch",
    )(dest_flat, h)


def _experts_kernel(tile_ref, exp_ref, lo_ref, hi_ref, x_ref, wg_ref, wu_ref, wd_ref, y_ref):
    bm = x_ref.shape[0]
    v = pl.program_id(0)
    lo = lo_ref[v]
    hi = hi_ref[v]

    @pl.when(hi > lo)
    def _():
        xb = _tiles_to_rows(x_ref, 0, bm).astype(BF16)
        g = _dot(xb, wg_ref[0].astype(BF16))
        u = _dot(xb, wu_ref[0].astype(BF16))
        mid = (g * jax.nn.sigmoid(g) * u).astype(BF16)
        y = _dot(mid, wd_ref[0].astype(BF16))
        rows = tile_ref[v] * bm + lax.broadcasted_iota(jnp.int32, (bm, 1), 0)
        mine = (rows >= lo) & (rows < hi)
        starts_tile = lo == tile_ref[v] * bm

        @pl.when(starts_tile)
        def _():
            _rows_to_tiles(y_ref, jnp.where(mine, y, 0.0))

        @pl.when(jnp.logical_not(starts_tile))
        def _():
            _rows_to_tiles(y_ref, jnp.where(mine, y, _tiles_to_rows(y_ref, 0, bm)))


def _experts(meta, xs, w_gate, w_up, w_down, bm):
    n_rows = xs.shape[0]
    n_visits = meta[0].shape[0]
    rows = pl.BlockSpec((bm, ROW_SUB, LANES), lambda v, t, e, lo, hi: (t[v], 0, 0))
    return pl.pallas_call(
        _experts_kernel,
        grid_spec=pltpu.PrefetchScalarGridSpec(
            num_scalar_prefetch=4,
            grid=(n_visits,),
            in_specs=[rows,
                      pl.BlockSpec((1, D_MODEL, D_EXPERT), lambda v, t, e, lo, hi: (e[v], 0, 0)),
                      pl.BlockSpec((1, D_MODEL, D_EXPERT), lambda v, t, e, lo, hi: (e[v], 0, 0)),
                      pl.BlockSpec((1, D_EXPERT, D_MODEL), lambda v, t, e, lo, hi: (e[v], 0, 0))],
            out_specs=rows),
        out_shape=jax.ShapeDtypeStruct((n_rows, ROW_SUB, LANES), F32),
        compiler_params=pltpu.CompilerParams(dimension_semantics=("arbitrary",),
                                             vmem_limit_bytes=VMEM_LIMIT),
        name="experts",
    )(*meta, xs, w_gate, w_up, w_down)


def _visit_metadata(counts, n_rows, bm):
    n_tiles = n_rows // bm
    n_visits = n_tiles + N_EXPERTS - 1
    ends = jnp.cumsum(counts)
    starts = ends - counts
    first_tile = starts // bm
    last_tile = jnp.maximum(ends - 1, 0) // bm
    n_vis = jnp.where(counts > 0, last_tile - first_tile + 1, 0)
    vis_end = jnp.cumsum(n_vis)
    vis_start = vis_end - n_vis
    v = jnp.arange(n_visits, dtype=jnp.int32)
    total = vis_end[-1]
    vc = jnp.minimum(v, total - 1)
    e = jnp.sum((vis_end[None, :] <= vc[:, None]).astype(jnp.int32), axis=1)
    tile = first_tile[e] + (vc - vis_start[e])
    lo = jnp.maximum(starts[e], tile * bm)
    hi = jnp.minimum(ends[e], (tile + 1) * bm)
    hi = jnp.where(v < total, hi, lo)
    i32 = lambda a: a.astype(jnp.int32)
    return i32(tile), i32(e), i32(lo), i32(hi)


def _combine_kernel(alpha, dest_ref, h_ref, route_ref, g_ref, b_ref, y_ref, o_ref, buf, sem):
    tm = h_ref.shape[0]
    base = pl.program_id(0) * (tm * TOP_K)

    def row_copy(src, slot):
        return pltpu.make_async_copy(y_ref.at[src], buf.at[slot], sem)

    def start(i, c):
        for u in range(DMA_UNROLL):
            j = i * DMA_UNROLL + u
            slot = (u % TOP_K) * tm + i * (DMA_UNROLL // TOP_K) + u // TOP_K
            row_copy(dest_ref[base + j], slot).start(priority=u % 2)
        return c

    lax.fori_loop(0, tm * TOP_K // DMA_UNROLL, start, 0)

    def wait(i, c):
        for u in range(DMA_UNROLL):
            row_copy(0, 0).wait()
        return c

    lax.fori_loop(0, tm * TOP_K // DMA_UNROLL, wait, 0)

    rec = route_ref[...]
    ffn = (rec[:, R_G0:R_G0 + 1] * _tiles_to_rows(buf, 0, tm)
           + rec[:, R_G1:R_G1 + 1] * _tiles_to_rows(buf, tm, tm))
    o_ref[...] = _layer_norm(alpha * h_ref[...] + ffn, g_ref[...], b_ref[...])


def _combine(dest_flat, h, route, ln2_g, ln2_b, y_sorted, alpha, tm):
    T = h.shape[0]
    return pl.pallas_call(
        functools.partial(_combine_kernel, alpha),
        grid_spec=pltpu.PrefetchScalarGridSpec(
            num_scalar_prefetch=1,
            grid=(T // tm,),
            in_specs=[pl.BlockSpec((tm, D_MODEL), lambda i, d: (i, 0)),
                      pl.BlockSpec((tm, LANES), lambda i, d: (i, 0)),
                      pl.BlockSpec((1, D_MODEL), lambda i, d: (0, 0)),
                      pl.BlockSpec((1, D_MODEL), lambda i, d: (0, 0)),
                      pl.BlockSpec(memory_space=pl.ANY)],
            out_specs=pl.BlockSpec((tm, D_MODEL), lambda i, d: (i, 0)),
            scratch_shapes=[pltpu.VMEM((TOP_K * tm, ROW_SUB, LANES), F32), pltpu.SemaphoreType.DMA(())]),
        out_shape=jax.ShapeDtypeStruct((T, D_MODEL), F32),
        compiler_params=pltpu.CompilerParams(dimension_semantics=("arbitrary",),
                                             vmem_limit_bytes=VMEM_LIMIT),
        name="combine",
    )(dest_flat, h, route, ln2_g, ln2_b, y_sorted)


def _rope_column_order():
    within = np.concatenate([np.arange(0, DA_HEAD_DIM, 2), np.arange(1, DA_HEAD_DIM, 2)])
    return np.concatenate([m * DA_HEAD_DIM + within for m in range(2 * DA_HEADS)])


def kernel(x, positions, ln_in_g, ln_in_b, w_in, lam_q1, lam_k1, lam_q2, lam_k2, da_subln_g, gla_w_gate2, gla_b_gate2, gla_norm_g, w_o, ln1_g, ln1_b, router_w_group, router_b_group, router_w_expert, router_b_expert, w_gate, w_up, w_down, ln2_g, ln2_b):
    B, S, D = x.shape
    T = B * S
    depth = w_in.shape[0]
    assert depth == 1, "only a single layer is supported"
    alpha = (2 * depth) ** 0.25
    row2 = lambda a: a.reshape(1, -1)

    inv_freq = ROPE_THETA ** (-jnp.arange(0, DA_HEAD_DIM, 2, dtype=F32) / DA_HEAD_DIM)
    inv_freq = jnp.tile(inv_freq, LANES // (DA_HEAD_DIM // 2)).reshape(1, LANES)
    perm = _rope_column_order()
    pos2 = positions.reshape(T, 1)

    cur = x.reshape(T, D)
    cur_g, cur_b = row2(ln_in_g), row2(ln_in_b)
    for l in range(depth):
        w = w_in[l]
        w_q = w[:, :DA_Q][:, perm]
        w_k = w[:, DA_Q:DA_Q + DA_K][:, perm]
        w_main = jnp.concatenate([w_q, w_k, w[:, DA_Q + DA_K:D_MAIN]], axis=1).astype(BF16)
        w_glow = jnp.pad(w[:, D_MAIN:], ((0, 0), (0, LANES - GLA_GATE_RANK))).astype(BF16)
        w_gate2 = jnp.pad(gla_w_gate2[l], ((0, LANES - GLA_GATE_RANK), (0, 0))).astype(BF16)

        q, k, v, gq, gk, gv, go, la = _in_proj(cur, pos2, cur_g, cur_b, inv_freq, w_main, w_glow,
                                               w_gate2, row2(gla_b_gate2[l]), tm=512)
        lam_init = 0.8 - 0.6 * math.exp(-0.3 * l)
        sh = lambda a: a.reshape(B, S, a.shape[-1])
        da = _diff_attn(sh(q), sh(k), sh(v), row2(lam_q1[l]), row2(lam_k1[l]), row2(lam_q2[l]),
                        row2(lam_k2[l]), row2(da_subln_g[l]), lam_init, tq=256)
        gl = _gla(sh(gq), sh(gk), sh(la), sh(gv), sh(go), row2(gla_norm_g[l]))

        w_r = jnp.zeros((D, LANES), F32)
        w_r = w_r.at[:, :N_GROUPS].set(router_w_group[l])
        w_r = w_r.at[:, EXPERT_LANE0:EXPERT_LANE0 + N_EXPERTS].set(router_w_expert[l])
        b_r = jnp.zeros((1, LANES), F32)
        b_r = b_r.at[0, :N_GROUPS].set(router_b_group[l])
        b_r = b_r.at[0, EXPERT_LANE0:EXPERT_LANE0 + N_EXPERTS].set(router_b_expert[l])
        wr_hi = w_r.astype(BF16)
        wr_lo = (w_r - wr_hi.astype(F32)).astype(BF16)

        h, route, cnt = _mix_out(cur, da.reshape(T, DA_V), gl.reshape(T, GLA_V), cur_g, cur_b,
                                 w_o[l].astype(BF16), row2(ln1_g[l]), row2(ln1_b[l]), wr_hi, wr_lo, b_r,
                                 alpha, tm=512)

        counts = cnt[0, :N_EXPERTS].astype(jnp.int32)
        seg_start = jnp.cumsum(counts) - counts
        eid = route[:, R_E0:R_E1 + 1].astype(jnp.int32)
        rank = route[:, R_RANK0:R_RANK1 + 1].astype(jnp.int32)
        onehot = eid[..., None] == jnp.arange(N_EXPERTS, dtype=jnp.int32)
        dest = jnp.sum(jnp.where(onehot, seg_start, 0), axis=-1) + rank
        dest_flat = dest.reshape(T * TOP_K)

        xs = _dispatch(dest_flat, h, tm=256)
        meta = _visit_metadata(counts, T * TOP_K, bm=256)
        ys = _experts(meta, xs, w_gate[l], w_up[l], w_down[l], bm=256)
        cur = _combine(dest_flat, h, route, row2(ln2_g[l]), row2(ln2_b[l]), ys, alpha, tm=256)
    return cur.reshape(B, S, D)
```

```python
import functools
import math

import jax
import jax.numpy as jnp
import numpy as np
from jax import lax
from jax.experimental import pallas as pl
from jax.experimental.pallas import tpu as pltpu

F32 = jnp.float32
BF16 = jnp.bfloat16

D_MODEL = 1024
CHUNK = 64
ROPE_THETA = 10000.0
LN_EPS = 1e-5

DA_HEADS = 4
DA_V_DIM = D_MODEL // (2 * DA_HEADS)
DA_HEAD_DIM = DA_V_DIM // 2
GLA_HEADS = 4
GLA_V_DIM = D_MODEL // (2 * GLA_HEADS)
GLA_KEY_DIM = GLA_V_DIM // 2
GLA_GATE_RANK = 16
GLA_GATE_NORMALIZER = 16.0

DA_Q = DA_HEADS * 2 * DA_HEAD_DIM
DA_K = DA_Q
DA_V = DA_HEADS * DA_V_DIM
GLA_Q = GLA_HEADS * GLA_KEY_DIM
GLA_K = GLA_Q
GLA_V = GLA_HEADS * GLA_V_DIM
GLA_OG = GLA_V
D_MAIN = DA_Q + DA_K + DA_V + GLA_Q + GLA_K + GLA_V + GLA_OG

N_GROUPS = 4
EXPERTS_PER_GROUP = 8
N_EXPERTS = N_GROUPS * EXPERTS_PER_GROUP
TOP_K = 2
D_EXPERT = D_MODEL // 2

LANES = 128
ROW_SUB = D_MODEL // LANES
ATTN_BLOCK = 256
DMA_UNROLL = 8
VMEM_LIMIT = 48 * 1024 * 1024

R_E0, R_E1, R_G0, R_G1, R_RANK0, R_RANK1 = 0, 1, 2, 3, 4, 5
EXPERT_LANE0 = 32


def _layer_norm(x, g, b):
    mu = jnp.mean(x, axis=-1, keepdims=True)
    xc = x - mu
    var = jnp.mean(xc * xc, axis=-1, keepdims=True)
    return xc * lax.rsqrt(var + LN_EPS) * g + b


def _dot(a, b):
    return jnp.dot(a, b, preferred_element_type=F32)


def _dot_nt(a, b):
    return lax.dot_general(a, b, (((1,), (1,)), ((), ())), preferred_element_type=F32)


def _dot_tn(a, b):
    return lax.dot_general(a, b, (((0,), (0,)), ((), ())), preferred_element_type=F32)


def _in_proj_kernel(x_ref, pos_ref, g_ref, b_ref, invf_ref, w_ref, wgl_ref, wg2_ref, bg2_ref,
                    q_ref, k_ref, v_ref, gq_ref, gk_ref, gv_ref, go_ref, la_ref):
    tm = x_ref.shape[0]
    xn = _layer_norm(x_ref[...], g_ref[...], b_ref[...])
    xb = xn.astype(BF16)
    proj = _dot(xb, w_ref[...])

    ang = pos_ref[...].astype(F32) * invf_ref[...]
    c = jnp.cos(ang)
    s = jnp.sin(ang)
    lane = lax.broadcasted_iota(jnp.int32, (tm, LANES), 1)
    first = (lane & (DA_HEAD_DIM // 2)) == 0
    s_lo = jnp.where(first, -s, 0.0)
    s_hi = jnp.where(first, 0.0, s)
    reps = DA_Q // LANES
    c4 = jnp.concatenate([c] * reps, axis=1)
    s_lo4 = jnp.concatenate([s_lo] * reps, axis=1)
    s_hi4 = jnp.concatenate([s_hi] * reps, axis=1)
    half = DA_HEAD_DIM // 2

    def rope(t):
        up = pltpu.roll(t, DA_Q - half, 1)
        dn = pltpu.roll(t, half, 1)
        return t * c4 + up * s_lo4 + dn * s_hi4

    o = 0
    q = rope(proj[:, o:o + DA_Q]) * (DA_HEAD_DIM ** -0.5)
    o += DA_Q
    k = rope(proj[:, o:o + DA_K])
    o += DA_K
    q_ref[...] = q.astype(BF16)
    k_ref[...] = k.astype(BF16)
    v_ref[...] = proj[:, o:o + DA_V].astype(BF16)
    o += DA_V
    gq_ref[...] = proj[:, o:o + GLA_Q].astype(BF16)
    o += GLA_Q
    gk_ref[...] = proj[:, o:o + GLA_K].astype(BF16)
    o += GLA_K
    gv_ref[...] = proj[:, o:o + GLA_V].astype(BF16)
    o += GLA_V
    go_ref[...] = proj[:, o:o + GLA_OG].astype(BF16)

    g_low = _dot(xb, wgl_ref[...])
    z = _dot(g_low.astype(BF16), wg2_ref[...]) + bg2_ref[...]
    log_sig = jnp.minimum(z, 0.0) - jnp.log1p(jnp.exp(-jnp.abs(z)))
    la_ref[...] = log_sig / GLA_GATE_NORMALIZER


def _in_proj(x2, pos2, ln_g, ln_b, inv_freq, w_main, w_glow, w_gate2, b_gate2, tm):
    T = x2.shape[0]
    row = lambda n: pl.BlockSpec((tm, n), lambda i: (i, 0))
    full = lambda a: pl.BlockSpec(a.shape, lambda i: (0,) * a.ndim)
    out_shape = [jax.ShapeDtypeStruct((T, n), dt) for n, dt in (
        (DA_Q, BF16), (DA_K, BF16), (DA_V, BF16), (GLA_Q, BF16), (GLA_K, BF16),
        (GLA_V, BF16), (GLA_OG, BF16), (GLA_K, F32))]
    return pl.pallas_call(
        _in_proj_kernel,
        grid=(T // tm,),
        in_specs=[row(D_MODEL), row(1), full(ln_g), full(ln_b), full(inv_freq), full(w_main),
                  full(w_glow), full(w_gate2), full(b_gate2)],
        out_specs=[row(s.shape[1]) for s in out_shape],
        out_shape=out_shape,
        compiler_params=pltpu.CompilerParams(dimension_semantics=("arbitrary",),
                                             vmem_limit_bytes=VMEM_LIMIT),
        name="in_proj",
    )(x2, pos2, ln_g, ln_b, inv_freq, w_main, w_glow, w_gate2, b_gate2)


def _diff_attn_kernel(lam_init, lq1_ref, lk1_ref, lq2_ref, lk2_ref, g_ref, q_ref, k_ref, v_ref, o_ref,
                      s_scr, p_scr):
    S = q_ref.shape[1]
    tq = ATTN_BLOCK
    lam = (jnp.exp(jnp.sum(lq1_ref[...] * lk1_ref[...], axis=-1, keepdims=True))
           - jnp.exp(jnp.sum(lq2_ref[...] * lk2_ref[...], axis=-1, keepdims=True)) + lam_init)
    lane = lax.broadcasted_iota(jnp.int32, (tq, LANES), 1)
    rq = lax.broadcasted_iota(jnp.int32, (2 * tq, tq), 0) % tq // CHUNK
    ck = lax.broadcasted_iota(jnp.int32, (2 * tq, tq), 1) // CHUNK
    diag_mask = ck <= rq

    for qi in range(S // tq):
        q = q_ref[0, qi * tq:(qi + 1) * tq, :]
        zero = jnp.zeros_like(q)
        qq = jnp.concatenate([jnp.where(lane < DA_HEAD_DIM, q, zero),
                              jnp.where(lane >= DA_HEAD_DIM, q, zero)], axis=0)
        m = None
        for j in range(qi + 1):
            s = _dot_nt(qq, k_ref[0, j * tq:(j + 1) * tq, :])
            if j == qi:
                s = jnp.where(diag_mask, s, -jnp.inf)
            s_scr[:, j * tq:(j + 1) * tq] = s
            mj = jnp.max(s, axis=-1, keepdims=True)
            m = mj if m is None else jnp.maximum(m, mj)
        l = None
        for j in range(qi + 1):
            p = jnp.exp(s_scr[:, j * tq:(j + 1) * tq] - m)
            p_scr[:, j * tq:(j + 1) * tq] = p.astype(BF16)
            lj = jnp.sum(p, axis=-1, keepdims=True)
            l = lj if l is None else l + lj
        nk = (qi + 1) * tq
        a = _dot(p_scr[:, 0:nk], v_ref[0, 0:nk, :]) / l
        o = a[0:tq] - lam * a[tq:2 * tq]
        o = o * lax.rsqrt(jnp.mean(o * o, axis=-1, keepdims=True) + LN_EPS) * g_ref[...]
        o_ref[0, qi * tq:(qi + 1) * tq, :] = (o * (1.0 - lam_init)).astype(o_ref.dtype)


def _diff_attn(q, k, v, lam_q1, lam_k1, lam_q2, lam_k2, subln_g, lam_init):
    B, S, _ = q.shape
    vec = pl.BlockSpec((1, DA_HEAD_DIM), lambda b, h: (0, 0))
    seq = pl.BlockSpec((1, S, LANES), lambda b, h: (b, 0, h))
    return pl.pallas_call(
        functools.partial(_diff_attn_kernel, lam_init),
        grid=(B, DA_HEADS),
        in_specs=[vec, vec, vec, vec, pl.BlockSpec((1, DA_V_DIM), lambda b, h: (0, 0)), seq, seq, seq],
        out_specs=seq,
        out_shape=jax.ShapeDtypeStruct((B, S, DA_V), BF16),
        scratch_shapes=[pltpu.VMEM((2 * ATTN_BLOCK, S), F32), pltpu.VMEM((2 * ATTN_BLOCK, S), BF16)],
        compiler_params=pltpu.CompilerParams(dimension_semantics=("arbitrary",) * 2,
                                             vmem_limit_bytes=VMEM_LIMIT),
        name="diff_attn",
    )(lam_q1, lam_k1, lam_q2, lam_k2, subln_g, q, k, v)


def _gla_kernel(q_ref, k_ref, la_ref, v_ref, go_ref, ng_ref, o_ref, *, unroll):
    S = q_ref.shape[1]
    C = CHUNK
    row = lax.broadcasted_iota(jnp.int32, (C, C), 0)
    col = lax.broadcasted_iota(jnp.int32, (C, C), 1)
    causal = col <= row
    tri = jnp.where(causal, 1.0, 0.0).astype(BF16)
    lane = lax.broadcasted_iota(jnp.int32, (C, LANES), 1)
    lane_sq = lax.broadcasted_iota(jnp.int32, (LANES, LANES), 1)
    head_lanes = (lane < GLA_KEY_DIM, lane >= GLA_KEY_DIM)
    head_lanes_sq = (lane_sq < GLA_KEY_DIM, lane_sq >= GLA_KEY_DIM)

    def one_chunk(r0, states):
        g = la_ref[0, pl.ds(r0, C), :]
        g1 = g.astype(BF16)
        e1 = g - g1.astype(F32)
        g2 = e1.astype(BF16)
        g3 = (e1 - g2.astype(F32)).astype(BF16)
        bcum = _dot(tri, g1) + _dot(tri, g2) + _dot(tri, g3)
        b_last = bcum[C - 1:C, :]
        qf = q_ref[0, pl.ds(r0, C), :].astype(F32) * (GLA_KEY_DIM ** -0.5)
        kf = k_ref[0, pl.ds(r0, C), :].astype(F32)
        q_t = (qf * jnp.exp(bcum)).astype(BF16)
        k_t = (kf * jnp.exp(-bcum)).astype(BF16)
        k_end = (kf * jnp.exp(b_last - bcum)).astype(BF16)
        decay = jnp.exp(b_last)
        new_states = []
        for hh in range(2):
            qm = jnp.where(head_lanes[hh], q_t, jnp.zeros_like(q_t))
            att = jnp.where(causal, _dot_nt(qm, k_t), 0.0).astype(BF16)
            vh = v_ref[0, pl.ds(r0, C), hh * GLA_V_DIM:(hh + 1) * GLA_V_DIM]
            st = states[hh]
            o = _dot(att, vh) + _dot_nt(qm, st.astype(BF16))
            ds = jnp.where(head_lanes_sq[hh], _dot_tn(vh, k_end), 0.0)
            new_states.append(st * decay + ds)
            o = o * lax.rsqrt(jnp.mean(o * o, axis=-1, keepdims=True) + LN_EPS) * ng_ref[...]
            gate = go_ref[0, pl.ds(r0, C), hh * GLA_V_DIM:(hh + 1) * GLA_V_DIM].astype(F32)
            o = o * (gate * jax.nn.sigmoid(gate))
            o_ref[0, pl.ds(r0, C), hh * GLA_V_DIM:(hh + 1) * GLA_V_DIM] = o.astype(o_ref.dtype)
        return tuple(new_states)

    def body(i, states):
        for u in range(unroll):
            r0 = pl.multiple_of((i * unroll + u) * C, C)
            states = one_chunk(r0, states)
        return states

    init = (jnp.zeros((GLA_V_DIM, LANES), F32),) * 2
    lax.fori_loop(0, S // (C * unroll), body, init)


def _gla(gq, gk, la, gv, go, norm_g, unroll=2):
    B, S, _ = gq.shape
    pairs = GLA_HEADS // 2
    narrow = pl.BlockSpec((1, S, LANES), lambda b, p: (b, 0, p))
    wide = pl.BlockSpec((1, S, 2 * GLA_V_DIM), lambda b, p: (b, 0, p))
    return pl.pallas_call(
        functools.partial(_gla_kernel, unroll=unroll),
        grid=(B, pairs),
        in_specs=[narrow, narrow, narrow, wide, wide,
                  pl.BlockSpec((1, GLA_V_DIM), lambda b, p: (0, 0))],
        out_specs=wide,
        out_shape=jax.ShapeDtypeStruct((B, S, GLA_V), BF16),
        compiler_params=pltpu.CompilerParams(dimension_semantics=("arbitrary",) * 2,
                                             vmem_limit_bytes=VMEM_LIMIT),
        name="gla",
    )(gq, gk, la, gv, go, norm_g)


def _split3(a):
    hi = a.astype(BF16)
    lo = (a - hi.astype(F32)).astype(BF16)
    return hi, lo


def _mix_out_kernel(alpha, x_ref, da_ref, gl_ref, lng_ref, lnb_ref, wo_ref, g1_ref, b1_ref,
                    wr_hi_ref, wr_lo_ref, br_ref, h_ref, route_ref, cnt_ref):
    tm = x_ref.shape[0]
    i = pl.program_id(0)

    @pl.when(i == 0)
    def _():
        cnt_ref[...] = jnp.zeros_like(cnt_ref)

    xn = _layer_norm(x_ref[...], lng_ref[...], lnb_ref[...])
    mix = _dot(da_ref[...], wo_ref[0:DA_V, :]) + _dot(gl_ref[...], wo_ref[DA_V:, :])
    h = _layer_norm(alpha * xn + mix, g1_ref[...], b1_ref[...])
    h_ref[...] = h

    h_hi, h_lo = _split3(h)
    logits = (_dot(h_hi, wr_hi_ref[...]) + _dot(h_hi, wr_lo_ref[...]) + _dot(h_lo, wr_hi_ref[...])
              + br_ref[...])
    lane = lax.broadcasted_iota(jnp.int32, (tm, LANES), 1)
    neg = -jnp.inf
    big = jnp.int32(LANES)

    def first_argmax(vals, valid):
        v = jnp.where(valid, vals, neg)
        mx = jnp.max(v, axis=-1, keepdims=True)
        idx = jnp.min(jnp.where(valid & (v == mx), lane, big), axis=-1, keepdims=True)
        return mx, idx

    is_group = lane < N_GROUPS
    g_max, g_top = first_argmax(logits, is_group)
    p_g = 1.0 / jnp.sum(jnp.where(is_group, jnp.exp(logits - g_max), 0.0), axis=-1, keepdims=True)

    e_lo = EXPERT_LANE0 + g_top * EXPERTS_PER_GROUP
    in_group = (lane >= e_lo) & (lane < e_lo + EXPERTS_PER_GROUP)
    v0, i0 = first_argmax(logits, in_group)
    v1, i1 = first_argmax(logits, in_group & (lane != i0))
    w1 = jnp.exp(v1 - v0)
    gate0 = p_g / (1.0 + w1)
    gate1 = p_g * w1 / (1.0 + w1)
    e0 = i0 - EXPERT_LANE0
    e1 = i1 - EXPERT_LANE0

    oh0 = jnp.where(lane == e0, 1.0, 0.0)
    oh1 = jnp.where(lane == e1, 1.0, 0.0)
    oh = oh0 + oh1
    r = lax.broadcasted_iota(jnp.int32, (tm, tm), 0)
    cidx = lax.broadcasted_iota(jnp.int32, (tm, tm), 1)
    strict_lower = jnp.where(cidx < r, 1.0, 0.0).astype(BF16)
    before = _dot(strict_lower, oh.astype(BF16)) + cnt_ref[0:1, :]
    rank0 = jnp.sum(oh0 * before, axis=-1, keepdims=True)
    rank1 = jnp.sum(oh1 * before, axis=-1, keepdims=True)
    cnt_ref[...] = cnt_ref[...] + jnp.sum(oh, axis=0, keepdims=True)

    rec = jnp.zeros((tm, LANES), F32)
    for ln, val in ((R_E0, e0.astype(F32)), (R_E1, e1.astype(F32)), (R_G0, gate0), (R_G1, gate1),
                    (R_RANK0, rank0), (R_RANK1, rank1)):
        rec = jnp.where(lane == ln, val, rec)
    route_ref[...] = rec


def _mix_out(x2, da2, gl2, ln_g, ln_b, w_o, ln1_g, ln1_b, wr_hi, wr_lo, b_r, alpha, tm):
    T = x2.shape[0]
    row = lambda n: pl.BlockSpec((tm, n), lambda i: (i, 0))
    full = lambda a: pl.BlockSpec(a.shape, lambda i: (0,) * a.ndim)
    return pl.pallas_call(
        functools.partial(_mix_out_kernel, alpha),
        grid=(T // tm,),
        in_specs=[row(D_MODEL), row(DA_V), row(GLA_V), full(ln_g), full(ln_b), full(w_o),
                  full(ln1_g), full(ln1_b), full(wr_hi), full(wr_lo), full(b_r)],
        out_specs=[row(D_MODEL), row(LANES), pl.BlockSpec((8, LANES), lambda i: (0, 0))],
        out_shape=[jax.ShapeDtypeStruct((T, D_MODEL), F32), jax.ShapeDtypeStruct((T, LANES), F32),
                   jax.ShapeDtypeStruct((8, LANES), F32)],
        compiler_params=pltpu.CompilerParams(dimension_semantics=("arbitrary",),
                                             vmem_limit_bytes=VMEM_LIMIT),
        name="mix_out",
    )(x2, da2, gl2, ln_g, ln_b, w_o, ln1_g, ln1_b, wr_hi, wr_lo, b_r)


def _rows_to_tiles(dst_ref, val):
    for s in range(ROW_SUB):
        dst_ref[:, s, :] = val[:, s * LANES:(s + 1) * LANES]


def _tiles_to_rows(src_ref, r0, n):
    return jnp.concatenate([src_ref[pl.ds(r0, n), s, :] for s in range(ROW_SUB)], axis=1)


def _dispatch_kernel(dest_ref, h_ref, xs_ref, stage, sem):
    tm = h_ref.shape[0]
    base = pl.program_id(0) * (tm * TOP_K)
    _rows_to_tiles(stage, h_ref[...])

    def row_copy(t, slot):
        return pltpu.make_async_copy(stage.at[t], xs_ref.at[slot], sem)

    def start(i, c):
        for u in range(DMA_UNROLL):
            j = i * DMA_UNROLL + u
            row_copy(j // TOP_K, dest_ref[base + j]).start(priority=u % 2)
        return c

    lax.fori_loop(0, tm * TOP_K // DMA_UNROLL, start, 0)

    def wait(i, c):
        for u in range(DMA_UNROLL):
            row_copy(0, 0).wait()
        return c

    lax.fori_loop(0, tm * TOP_K // DMA_UNROLL, wait, 0)


def _dispatch(dest_flat, h, tm):
    T = h.shape[0]
    return pl.pallas_call(
        _dispatch_kernel,
        grid_spec=pltpu.PrefetchScalarGridSpec(
            num_scalar_prefetch=1,
            grid=(T // tm,),
            in_specs=[pl.BlockSpec((tm, D_MODEL), lambda i, d: (i, 0))],
            out_specs=pl.BlockSpec(memory_space=pl.ANY),
            scratch_shapes=[pltpu.VMEM((tm, ROW_SUB, LANES), F32), pltpu.SemaphoreType.DMA(())]),
        out_shape=jax.ShapeDtypeStruct((T * TOP_K, ROW_SUB, LANES), F32),
        compiler_params=pltpu.CompilerParams(dimension_semantics=("arbitrary",),
                                             vmem_limit_bytes=VMEM_LIMIT),
        name="dispatch",
    )(dest_flat, h)


def _experts_kernel(tile_ref, exp_ref, lo_ref, hi_ref, x_ref, wg_ref, wu_ref, wd_ref, y_ref):
    bm = x_ref.shape[0]
    v = pl.program_id(0)
    lo = lo_ref[v]
    hi = hi_ref[v]

    @pl.when(hi > lo)
    def _():
        xb = _tiles_to_rows(x_ref, 0, bm).astype(BF16)
        g = _dot(xb, wg_ref[0].astype(BF16))
        u = _dot(xb, wu_ref[0].astype(BF16))
        mid = (g * jax.nn.sigmoid(g) * u).astype(BF16)
        y = _dot(mid, wd_ref[0].astype(BF16))
        rows = tile_ref[v] * bm + lax.broadcasted_iota(jnp.int32, (bm, 1), 0)
        mine = (rows >= lo) & (rows < hi)
        starts_tile = lo == tile_ref[v] * bm

        @pl.when(starts_tile)
        def _():
            _rows_to_tiles(y_ref, jnp.where(mine, y, 0.0))

        @pl.when(jnp.logical_not(starts_tile))
        def _():
            _rows_to_tiles(y_ref, jnp.where(mine, y, _tiles_to_rows(y_ref, 0, bm)))


def _experts(meta, xs, w_gate, w_up, w_down, bm):
    n_rows = xs.shape[0]
    n_visits = meta[0].shape[0]
    rows = pl.BlockSpec((bm, ROW_SUB, LANES), lambda v, t, e, lo, hi: (t[v], 0, 0))
    return pl.pallas_call(
        _experts_kernel,
        grid_spec=pltpu.PrefetchScalarGridSpec(
            num_scalar_prefetch=4,
            grid=(n_visits,),
            in_specs=[rows,
                      pl.BlockSpec((1, D_MODEL, D_EXPERT), lambda v, t, e, lo, hi: (e[v], 0, 0)),
                      pl.BlockSpec((1, D_MODEL, D_EXPERT), lambda v, t, e, lo, hi: (e[v], 0, 0)),
                      pl.BlockSpec((1, D_EXPERT, D_MODEL), lambda v, t, e, lo, hi: (e[v], 0, 0))],
            out_specs=rows),
        out_shape=jax.ShapeDtypeStruct((n_rows, ROW_SUB, LANES), F32),
        compiler_params=pltpu.CompilerParams(dimension_semantics=("arbitrary",),
                                             vmem_limit_bytes=VMEM_LIMIT),
        name="experts",
    )(*meta, xs, w_gate, w_up, w_down)


def _visit_metadata(counts, n_rows, bm):
    n_tiles = n_rows // bm
    n_visits = n_tiles + N_EXPERTS - 1
    ends = jnp.cumsum(counts)
    starts = ends - counts
    first_tile = starts // bm
    last_tile = jnp.maximum(ends - 1, 0) // bm
    n_vis = jnp.where(counts > 0, last_tile - first_tile + 1, 0)
    vis_end = jnp.cumsum(n_vis)
    vis_start = vis_end - n_vis
    v = jnp.arange(n_visits, dtype=jnp.int32)
    total = vis_end[-1]
    vc = jnp.minimum(v, total - 1)
    e = jnp.sum((vis_end[None, :] <= vc[:, None]).astype(jnp.int32), axis=1)
    tile = first_tile[e] + (vc - vis_start[e])
    lo = jnp.maximum(starts[e], tile * bm)
    hi = jnp.minimum(ends[e], (tile + 1) * bm)
    hi = jnp.where(v < total, hi, lo)
    i32 = lambda a: a.astype(jnp.int32)
    return i32(tile), i32(e), i32(lo), i32(hi)


def _combine_kernel(alpha, dest_ref, h_ref, route_ref, g_ref, b_ref, y_ref, o_ref, buf, sem):
    tm = h_ref.shape[0]
    base = pl.program_id(0) * (tm * TOP_K)

    def row_copy(src, slot):
        return pltpu.make_async_copy(y_ref.at[src], buf.at[slot], sem)

    def start(i, c):
        for u in range(DMA_UNROLL):
            j = i * DMA_UNROLL + u
            slot = (u % TOP_K) * tm + i * (DMA_UNROLL // TOP_K) + u // TOP_K
            row_copy(dest_ref[base + j], slot).start(priority=u % 2)
        return c

    lax.fori_loop(0, tm * TOP_K // DMA_UNROLL, start, 0)

    def wait(i, c):
        for u in range(DMA_UNROLL):
            row_copy(0, 0).wait()
        return c

    lax.fori_loop(0, tm * TOP_K // DMA_UNROLL, wait, 0)

    rec = route_ref[...]
    ffn = (rec[:, R_G0:R_G0 + 1] * _tiles_to_rows(buf, 0, tm)
           + rec[:, R_G1:R_G1 + 1] * _tiles_to_rows(buf, tm, tm))
    o_ref[...] = _layer_norm(alpha * h_ref[...] + ffn, g_ref[...], b_ref[...])


def _combine(dest_flat, h, route, ln2_g, ln2_b, y_sorted, alpha, tm):
    T = h.shape[0]
    return pl.pallas_call(
        functools.partial(_combine_kernel, alpha),
        grid_spec=pltpu.PrefetchScalarGridSpec(
            num_scalar_prefetch=1,
            grid=(T // tm,),
            in_specs=[pl.BlockSpec((tm, D_MODEL), lambda i, d: (i, 0)),
                      pl.BlockSpec((tm, LANES), lambda i, d: (i, 0)),
                      pl.BlockSpec((1, D_MODEL), lambda i, d: (0, 0)),
                      pl.BlockSpec((1, D_MODEL), lambda i, d: (0, 0)),
                      pl.BlockSpec(memory_space=pl.ANY)],
            out_specs=pl.BlockSpec((tm, D_MODEL), lambda i, d: (i, 0)),
            scratch_shapes=[pltpu.VMEM((TOP_K * tm, ROW_SUB, LANES), F32), pltpu.SemaphoreType.DMA(())]),
        out_shape=jax.ShapeDtypeStruct((T, D_MODEL), F32),
        compiler_params=pltpu.CompilerParams(dimension_semantics=("arbitrary",),
                                             vmem_limit_bytes=VMEM_LIMIT),
        name="combine",
    )(dest_flat, h, route, ln2_g, ln2_b, y_sorted)


def _rope_column_order():
    within = np.concatenate([np.arange(0, DA_HEAD_DIM, 2), np.arange(1, DA_HEAD_DIM, 2)])
    return np.concatenate([m * DA_HEAD_DIM + within for m in range(2 * DA_HEADS)])


def kernel(x, positions, ln_in_g, ln_in_b, w_in, lam_q1, lam_k1, lam_q2, lam_k2, da_subln_g, gla_w_gate2, gla_b_gate2, gla_norm_g, w_o, ln1_g, ln1_b, router_w_group, router_b_group, router_w_expert, router_b_expert, w_gate, w_up, w_down, ln2_g, ln2_b):
    B, S, D = x.shape
    T = B * S
    depth = w_in.shape[0]
    assert depth == 1, "only a single layer is supported"
    alpha = (2 * depth) ** 0.25
    row2 = lambda a: a.reshape(1, -1)

    inv_freq = ROPE_THETA ** (-jnp.arange(0, DA_HEAD_DIM, 2, dtype=F32) / DA_HEAD_DIM)
    inv_freq = jnp.tile(inv_freq, LANES // (DA_HEAD_DIM // 2)).reshape(1, LANES)
    perm = _rope_column_order()
    pos2 = positions.reshape(T, 1)

    cur = x.reshape(T, D)
    cur_g, cur_b = row2(ln_in_g), row2(ln_in_b)
    for l in range(depth):
        w = w_in[l]
        w_q = w[:, :DA_Q][:, perm]
        w_k = w[:, DA_Q:DA_Q + DA_K][:, perm]
        w_main = jnp.concatenate([w_q, w_k, w[:, DA_Q + DA_K:D_MAIN]], axis=1).astype(BF16)
        w_glow = jnp.pad(w[:, D_MAIN:], ((0, 0), (0, LANES - GLA_GATE_RANK))).astype(BF16)
        w_gate2 = jnp.pad(gla_w_gate2[l], ((0, LANES - GLA_GATE_RANK), (0, 0))).astype(BF16)

        q, k, v, gq, gk, gv, go, la = _in_proj(cur, pos2, cur_g, cur_b, inv_freq, w_main, w_glow,
                                               w_gate2, row2(gla_b_gate2[l]), tm=512)
        lam_init = 0.8 - 0.6 * math.exp(-0.3 * l)
        sh = lambda a: a.reshape(B, S, a.shape[-1])
        da = _diff_attn(sh(q), sh(k), sh(v), row2(lam_q1[l]), row2(lam_k1[l]), row2(lam_q2[l]),
                        row2(lam_k2[l]), row2(da_subln_g[l]), lam_init)
        gl = _gla(sh(gq), sh(gk), sh(la), sh(gv), sh(go), row2(gla_norm_g[l]))

        w_r = jnp.zeros((D, LANES), F32)
        w_r = w_r.at[:, :N_GROUPS].set(router_w_group[l])
        w_r = w_r.at[:, EXPERT_LANE0:EXPERT_LANE0 + N_EXPERTS].set(router_w_expert[l])
        b_r = jnp.zeros((1, LANES), F32)
        b_r = b_r.at[0, :N_GROUPS].set(router_b_group[l])
        b_r = b_r.at[0, EXPERT_LANE0:EXPERT_LANE0 + N_EXPERTS].set(router_b_expert[l])
        wr_hi = w_r.astype(BF16)
        wr_lo = (w_r - wr_hi.astype(F32)).astype(BF16)

        h, route, cnt = _mix_out(cur, da.reshape(T, DA_V), gl.reshape(T, GLA_V), cur_g, cur_b,
                                 w_o[l].astype(BF16), row2(ln1_g[l]), row2(ln1_b[l]), wr_hi, wr_lo, b_r,
                                 alpha, tm=512)

        counts = cnt[0, :N_EXPERTS].astype(jnp.int32)
        seg_start = jnp.cumsum(counts) - counts
        eid = route[:, R_E0:R_E1 + 1].astype(jnp.int32)
        rank = route[:, R_RANK0:R_RANK1 + 1].astype(jnp.int32)
        onehot = eid[..., None] == jnp.arange(N_EXPERTS, dtype=jnp.int32)
        dest = jnp.sum(jnp.where(onehot, seg_start, 0), axis=-1) + rank
        dest_flat = dest.reshape(T * TOP_K)

        xs = _dispatch(dest_flat, h, tm=256)
        meta = _visit_metadata(counts, T * TOP_K, bm=256)
        ys = _experts(meta, xs, w_gate[l], w_up[l], w_down[l], bm=256)
        cur = _combine(dest_flat, h, route, row2(ln2_g[l]), row2(ln2_b[l]), ys, alpha, tm=256)
    return cur.reshape(B, S, D)
```

```python
import functools
import math

import jax
import jax.numpy as jnp
import numpy as np
from jax import lax
from jax.experimental import pallas as pl
from jax.experimental.pallas import tpu as pltpu

F32 = jnp.float32
BF16 = jnp.bfloat16

D_MODEL = 1024
CHUNK = 64
ROPE_THETA = 10000.0
LN_EPS = 1e-5

DA_HEADS = 4
DA_V_DIM = D_MODEL // (2 * DA_HEADS)
DA_HEAD_DIM = DA_V_DIM // 2
GLA_HEADS = 4
GLA_V_DIM = D_MODEL // (2 * GLA_HEADS)
GLA_KEY_DIM = GLA_V_DIM // 2
GLA_GATE_RANK = 16
GLA_GATE_NORMALIZER = 16.0

DA_Q = DA_HEADS * 2 * DA_HEAD_DIM
DA_K = DA_Q
DA_V = DA_HEADS * DA_V_DIM
GLA_Q = GLA_HEADS * GLA_KEY_DIM
GLA_K = GLA_Q
GLA_V = GLA_HEADS * GLA_V_DIM
GLA_OG = GLA_V
D_MAIN = DA_Q + DA_K + DA_V + GLA_Q + GLA_K + GLA_V + GLA_OG

N_GROUPS = 4
EXPERTS_PER_GROUP = 8
N_EXPERTS = N_GROUPS * EXPERTS_PER_GROUP
TOP_K = 2
D_EXPERT = D_MODEL // 2

LANES = 128
ROW_SUB = D_MODEL // LANES
ATTN_BLOCK = 256
DMA_UNROLL = 8
VMEM_LIMIT = 48 * 1024 * 1024

R_E0, R_E1, R_G0, R_G1, R_RANK0, R_RANK1 = 0, 1, 2, 3, 4, 5
EXPERT_LANE0 = 32


def _layer_norm(x, g, b):
    mu = jnp.mean(x, axis=-1, keepdims=True)
    xc = x - mu
    var = jnp.mean(xc * xc, axis=-1, keepdims=True)
    return xc * lax.rsqrt(var + LN_EPS) * g + b


def _dot(a, b):
    return jnp.dot(a, b, preferred_element_type=F32)


def _dot_nt(a, b):
    return lax.dot_general(a, b, (((1,), (1,)), ((), ())), preferred_element_type=F32)


def _dot_tn(a, b):
    return lax.dot_general(a, b, (((0,), (0,)), ((), ())), preferred_element_type=F32)


def _in_proj_kernel(x_ref, pos_ref, g_ref, b_ref, invf_ref, w_ref, wgl_ref, wg2_ref, bg2_ref,
                    q_ref, k_ref, v_ref, gq_ref, gk_ref, gv_ref, go_ref, la_ref):
    tm = x_ref.shape[0]
    xn = _layer_norm(x_ref[...], g_ref[...], b_ref[...])
    xb = xn.astype(BF16)
    proj = _dot(xb, w_ref[...])

    ang = pos_ref[...].astype(F32) * invf_ref[...]
    c = jnp.cos(ang)
    s = jnp.sin(ang)
    lane = lax.broadcasted_iota(jnp.int32, (tm, LANES), 1)
    first = (lane & (DA_HEAD_DIM // 2)) == 0
    s_lo = jnp.where(first, -s, 0.0)
    s_hi = jnp.where(first, 0.0, s)
    reps = DA_Q // LANES
    c4 = jnp.concatenate([c] * reps, axis=1)
    s_lo4 = jnp.concatenate([s_lo] * reps, axis=1)
    s_hi4 = jnp.concatenate([s_hi] * reps, axis=1)
    half = DA_HEAD_DIM // 2

    def rope(t):
        up = pltpu.roll(t, DA_Q - half, 1)
        dn = pltpu.roll(t, half, 1)
        return t * c4 + up * s_lo4 + dn * s_hi4

    o = 0
    q = rope(proj[:, o:o + DA_Q]) * (DA_HEAD_DIM ** -0.5)
    o += DA_Q
    k = rope(proj[:, o:o + DA_K])
    o += DA_K
    q_ref[...] = q.astype(BF16)
    k_ref[...] = k.astype(BF16)
    v_ref[...] = proj[:, o:o + DA_V].astype(BF16)
    o += DA_V
    gq_ref[...] = proj[:, o:o + GLA_Q].astype(BF16)
    o += GLA_Q
    gk_ref[...] = proj[:, o:o + GLA_K].astype(BF16)
    o += GLA_K
    gv_ref[...] = proj[:, o:o + GLA_V].astype(BF16)
    o += GLA_V
    go_ref[...] = proj[:, o:o + GLA_OG].astype(BF16)

    g_low = _dot(xb, wgl_ref[...])
    z = _dot(g_low.astype(BF16), wg2_ref[...]) + bg2_ref[...]
    log_sig = jnp.minimum(z, 0.0) - jnp.log1p(jnp.exp(-jnp.abs(z)))
    la_ref[...] = log_sig / GLA_GATE_NORMALIZER


def _in_proj(x2, pos2, ln_g, ln_b, inv_freq, w_main, w_glow, w_gate2, b_gate2, tm):
    T = x2.shape[0]
    row = lambda n: pl.BlockSpec((tm, n), lambda i: (i, 0))
    full = lambda a: pl.BlockSpec(a.shape, lambda i: (0,) * a.ndim)
    out_shape = [jax.ShapeDtypeStruct((T, n), dt) for n, dt in (
        (DA_Q, BF16), (DA_K, BF16), (DA_V, BF16), (GLA_Q, BF16), (GLA_K, BF16),
        (GLA_V, BF16), (GLA_OG, BF16), (GLA_K, F32))]
    return pl.pallas_call(
        _in_proj_kernel,
        grid=(T // tm,),
        in_specs=[row(D_MODEL), row(1), full(ln_g), full(ln_b), full(inv_freq), full(w_main),
                  full(w_glow), full(w_gate2), full(b_gate2)],
        out_specs=[row(s.shape[1]) for s in out_shape],
        out_shape=out_shape,
        compiler_params=pltpu.CompilerParams(dimension_semantics=("arbitrary",),
                                             vmem_limit_bytes=VMEM_LIMIT),
        name="in_proj",
    )(x2, pos2, ln_g, ln_b, inv_freq, w_main, w_glow, w_gate2, b_gate2)


def _diff_attn_kernel(lam_init, lq1_ref, lk1_ref, lq2_ref, lk2_ref, g_ref, q_ref, k_ref, v_ref, o_ref,
                      s_scr, p_scr):
    S = q_ref.shape[1]
    tq = ATTN_BLOCK
    lam = (jnp.exp(jnp.sum(lq1_ref[...] * lk1_ref[...], axis=-1, keepdims=True))
           - jnp.exp(jnp.sum(lq2_ref[...] * lk2_ref[...], axis=-1, keepdims=True)) + lam_init)
    lane = lax.broadcasted_iota(jnp.int32, (tq, LANES), 1)
    rq = lax.broadcasted_iota(jnp.int32, (2 * tq, tq), 0) % tq // CHUNK
    ck = lax.broadcasted_iota(jnp.int32, (2 * tq, tq), 1) // CHUNK
    diag_mask = ck <= rq

    for qi in range(S // tq):
        q = q_ref[0, qi * tq:(qi + 1) * tq, :]
        zero = jnp.zeros_like(q)
        qq = jnp.concatenate([jnp.where(lane < DA_HEAD_DIM, q, zero),
                              jnp.where(lane >= DA_HEAD_DIM, q, zero)], axis=0)
        m = None
        for j in range(qi + 1):
            s = _dot_nt(qq, k_ref[0, j * tq:(j + 1) * tq, :])
            if j == qi:
                s = jnp.where(diag_mask, s, -jnp.inf)
            s_scr[:, j * tq:(j + 1) * tq] = s
            mj = jnp.max(s, axis=-1, keepdims=True)
            m = mj if m is None else jnp.maximum(m, mj)
        l = None
        for j in range(qi + 1):
            p = jnp.exp(s_scr[:, j * tq:(j + 1) * tq] - m)
            p_scr[:, j * tq:(j + 1) * tq] = p.astype(BF16)
            lj = jnp.sum(p, axis=-1, keepdims=True)
            l = lj if l is None else l + lj
        nk = (qi + 1) * tq
        a = _dot(p_scr[:, 0:nk], v_ref[0, 0:nk, :]) / l
        o = a[0:tq] - lam * a[tq:2 * tq]
        o = o * lax.rsqrt(jnp.mean(o * o, axis=-1, keepdims=True) + LN_EPS) * g_ref[...]
        o_ref[0, qi * tq:(qi + 1) * tq, :] = (o * (1.0 - lam_init)).astype(o_ref.dtype)


def _diff_attn(q, k, v, lam_q1, lam_k1, lam_q2, lam_k2, subln_g, lam_init):
    B, S, _ = q.shape
    vec = pl.BlockSpec((1, DA_HEAD_DIM), lambda b, h: (0, 0))
    seq = pl.BlockSpec((1, S, LANES), lambda b, h: (b, 0, h))
    return pl.pallas_call(
        functools.partial(_diff_attn_kernel, lam_init),
        grid=(B, DA_HEADS),
        in_specs=[vec, vec, vec, vec, pl.BlockSpec((1, DA_V_DIM), lambda b, h: (0, 0)), seq, seq, seq],
        out_specs=seq,
        out_shape=jax.ShapeDtypeStruct((B, S, DA_V), BF16),
        scratch_shapes=[pltpu.VMEM((2 * ATTN_BLOCK, S), F32), pltpu.VMEM((2 * ATTN_BLOCK, S), BF16)],
        compiler_params=pltpu.CompilerParams(dimension_semantics=("arbitrary",) * 2,
                                             vmem_limit_bytes=VMEM_LIMIT),
        name="diff_attn",
    )(lam_q1, lam_k1, lam_q2, lam_k2, subln_g, q, k, v)


def _gla_kernel(q_ref, k_ref, la_ref, v_ref, go_ref, ng_ref, o_ref, *, unroll):
    S = q_ref.shape[1]
    C = CHUNK
    row = lax.broadcasted_iota(jnp.int32, (C, C), 0)
    col = lax.broadcasted_iota(jnp.int32, (C, C), 1)
    causal = col <= row
    tri = jnp.where(causal, 1.0, 0.0).astype(BF16)
    lane = lax.broadcasted_iota(jnp.int32, (C, LANES), 1)
    lane_sq = lax.broadcasted_iota(jnp.int32, (LANES, LANES), 1)
    head_lanes = (lane < GLA_KEY_DIM, lane >= GLA_KEY_DIM)
    head_lanes_sq = (lane_sq < GLA_KEY_DIM, lane_sq >= GLA_KEY_DIM)

    def one_chunk(r0, states):
        g = la_ref[0, pl.ds(r0, C), :]
        g1 = g.astype(BF16)
        e1 = g - g1.astype(F32)
        g2 = e1.astype(BF16)
        g3 = (e1 - g2.astype(F32)).astype(BF16)
        bcum = _dot(tri, g1) + _dot(tri, g2) + _dot(tri, g3)
        b_last = bcum[C - 1:C, :]
        qf = q_ref[0, pl.ds(r0, C), :].astype(F32) * (GLA_KEY_DIM ** -0.5)
        kf = k_ref[0, pl.ds(r0, C), :].astype(F32)
        q_t = (qf * jnp.exp(bcum)).astype(BF16)
        k_t = (kf * jnp.exp(-bcum)).astype(BF16)
        k_end = (kf * jnp.exp(b_last - bcum)).astype(BF16)
        decay = jnp.exp(b_last)
        new_states = []
        for hh in range(2):
            qm = jnp.where(head_lanes[hh], q_t, jnp.zeros_like(q_t))
            att = jnp.where(causal, _dot_nt(qm, k_t), 0.0).astype(BF16)
            vh = v_ref[0, pl.ds(r0, C), hh * GLA_V_DIM:(hh + 1) * GLA_V_DIM]
            st = states[hh]
            o = _dot(att, vh) + _dot_nt(qm, st.astype(BF16))
            ds = jnp.where(head_lanes_sq[hh], _dot_tn(vh, k_end), 0.0)
            new_states.append(st * decay + ds)
            o = o * lax.rsqrt(jnp.mean(o * o, axis=-1, keepdims=True) + LN_EPS) * ng_ref[...]
            gate = go_ref[0, pl.ds(r0, C), hh * GLA_V_DIM:(hh + 1) * GLA_V_DIM].astype(F32)
            o = o * (gate * jax.nn.sigmoid(gate))
            o_ref[0, pl.ds(r0, C), hh * GLA_V_DIM:(hh + 1) * GLA_V_DIM] = o.astype(o_ref.dtype)
        return tuple(new_states)

    def body(i, states):
        for u in range(unroll):
            r0 = pl.multiple_of((i * unroll + u) * C, C)
            states = one_chunk(r0, states)
        return states

    init = (jnp.zeros((GLA_V_DIM, LANES), F32),) * 2
    lax.fori_loop(0, S // (C * unroll), body, init)


def _gla(gq, gk, la, gv, go, norm_g, unroll=2):
    B, S, _ = gq.shape
    pairs = GLA_HEADS // 2
    narrow = pl.BlockSpec((1, S, LANES), lambda b, p: (b, 0, p))
    wide = pl.BlockSpec((1, S, 2 * GLA_V_DIM), lambda b, p: (b, 0, p))
    return pl.pallas_call(
        functools.partial(_gla_kernel, unroll=unroll),
        grid=(B, pairs),
        in_specs=[narrow, narrow, narrow, wide, wide,
                  pl.BlockSpec((1, GLA_V_DIM), lambda b, p: (0, 0))],
        out_specs=wide,
        out_shape=jax.ShapeDtypeStruct((B, S, GLA_V), BF16),
        compiler_params=pltpu.CompilerParams(dimension_semantics=("arbitrary",) * 2,
                                             vmem_limit_bytes=VMEM_LIMIT),
        name="gla",
    )(gq, gk, la, gv, go, norm_g)


def _split3(a):
    hi = a.astype(BF16)
    lo = (a - hi.astype(F32)).astype(BF16)
    return hi, lo


def _mix_out_kernel(alpha, x_ref, da_ref, gl_ref, lng_ref, lnb_ref, wo_ref, g1_ref, b1_ref,
                    wr_hi_ref, wr_lo_ref, br_ref, h_ref, route_ref, cnt_ref):
    tm = x_ref.shape[0]
    i = pl.program_id(0)

    @pl.when(i == 0)
    def _():
        cnt_ref[...] = jnp.zeros_like(cnt_ref)

    xn = _layer_norm(x_ref[...], lng_ref[...], lnb_ref[...])
    mix = _dot(da_ref[...], wo_ref[0:DA_V, :]) + _dot(gl_ref[...], wo_ref[DA_V:, :])
    h = _layer_norm(alpha * xn + mix, g1_ref[...], b1_ref[...])
    h_ref[...] = h

    h_hi, h_lo = _split3(h)
    logits = (_dot(h_hi, wr_hi_ref[...]) + _dot(h_hi, wr_lo_ref[...]) + _dot(h_lo, wr_hi_ref[...])
              + br_ref[...])
    lane = lax.broadcasted_iota(jnp.int32, (tm, LANES), 1)
    neg = -jnp.inf
    big = jnp.int32(LANES)

    def first_argmax(vals, valid):
        v = jnp.where(valid, vals, neg)
        mx = jnp.max(v, axis=-1, keepdims=True)
        idx = jnp.min(jnp.where(valid & (v == mx), lane, big), axis=-1, keepdims=True)
        return mx, idx

    is_group = lane < N_GROUPS
    g_max, g_top = first_argmax(logits, is_group)
    p_g = 1.0 / jnp.sum(jnp.where(is_group, jnp.exp(logits - g_max), 0.0), axis=-1, keepdims=True)

    e_lo = EXPERT_LANE0 + g_top * EXPERTS_PER_GROUP
    in_group = (lane >= e_lo) & (lane < e_lo + EXPERTS_PER_GROUP)
    v0, i0 = first_argmax(logits, in_group)
    v1, i1 = first_argmax(logits, in_group & (lane != i0))
    w1 = jnp.exp(v1 - v0)
    gate0 = p_g / (1.0 + w1)
    gate1 = p_g * w1 / (1.0 + w1)
    e0 = i0 - EXPERT_LANE0
    e1 = i1 - EXPERT_LANE0

    oh0 = jnp.where(lane == e0, 1.0, 0.0)
    oh1 = jnp.where(lane == e1, 1.0, 0.0)
    oh = oh0 + oh1
    r = lax.broadcasted_iota(jnp.int32, (tm, tm), 0)
    cidx = lax.broadcasted_iota(jnp.int32, (tm, tm), 1)
    strict_lower = jnp.where(cidx < r, 1.0, 0.0).astype(BF16)
    before = _dot(strict_lower, oh.astype(BF16)) + cnt_ref[0:1, :]
    rank0 = jnp.sum(oh0 * before, axis=-1, keepdims=True)
    rank1 = jnp.sum(oh1 * before, axis=-1, keepdims=True)
    cnt_ref[...] = cnt_ref[...] + jnp.sum(oh, axis=0, keepdims=True)

    rec = jnp.zeros((tm, LANES), F32)
    for ln, val in ((R_E0, e0.astype(F32)), (R_E1, e1.astype(F32)), (R_G0, gate0), (R_G1, gate1),
                    (R_RANK0, rank0), (R_RANK1, rank1)):
        rec = jnp.where(lane == ln, val, rec)
    route_ref[...] = rec


def _mix_out(x2, da2, gl2, ln_g, ln_b, w_o, ln1_g, ln1_b, wr_hi, wr_lo, b_r, alpha, tm):
    T = x2.shape[0]
    row = lambda n: pl.BlockSpec((tm, n), lambda i: (i, 0))
    full = lambda a: pl.BlockSpec(a.shape, lambda i: (0,) * a.ndim)
    return pl.pallas_call(
        functools.partial(_mix_out_kernel, alpha),
        grid=(T // tm,),
        in_specs=[row(D_MODEL), row(DA_V), row(GLA_V), full(ln_g), full(ln_b), full(w_o),
                  full(ln1_g), full(ln1_b), full(wr_hi), full(wr_lo), full(b_r)],
        out_specs=[row(D_MODEL), row(LANES), pl.BlockSpec((8, LANES), lambda i: (0, 0))],
        out_shape=[jax.ShapeDtypeStruct((T, D_MODEL), F32), jax.ShapeDtypeStruct((T, LANES), F32),
                   jax.ShapeDtypeStruct((8, LANES), F32)],
        compiler_params=pltpu.CompilerParams(dimension_semantics=("arbitrary",),
                                             vmem_limit_bytes=VMEM_LIMIT),
        name="mix_out",
    )(x2, da2, gl2, ln_g, ln_b, w_o, ln1_g, ln1_b, wr_hi, wr_lo, b_r)


def _rows_to_tiles(dst_ref, val):
    n = val.shape[0]
    for s in range(ROW_SUB):
        dst_ref[pl.ds(s, n, stride=ROW_SUB), :] = val[:, s * LANES:(s + 1) * LANES]


def _tiles_to_rows(src_ref, r0, n):
    return jnp.concatenate([src_ref[pl.ds(r0 * ROW_SUB + s, n, stride=ROW_SUB), :] for s in range(ROW_SUB)],
                           axis=1)


def _row_tile(ref, r):
    return ref.at[pl.ds(pl.multiple_of(r * ROW_SUB, ROW_SUB), ROW_SUB), :]


def _dispatch_kernel(dest_ref, h_ref, xs_ref, stage, sem):
    tm = h_ref.shape[0]
    base = pl.program_id(0) * (tm * TOP_K)
    _rows_to_tiles(stage, h_ref[...])

    def row_copy(t, slot):
        return pltpu.make_async_copy(_row_tile(stage, t), _row_tile(xs_ref, slot), sem)

    def start(i, c):
        for u in range(DMA_UNROLL):
            j = i * DMA_UNROLL + u
            row_copy(j // TOP_K, dest_ref[base + j]).start(priority=u % 2)
        return c

    lax.fori_loop(0, tm * TOP_K // DMA_UNROLL, start, 0)

    def wait(i, c):
        for u in range(DMA_UNROLL):
            row_copy(0, 0).wait()
        return c

    lax.fori_loop(0, tm * TOP_K // DMA_UNROLL, wait, 0)


def _dispatch(dest_flat, h, tm):
    T = h.shape[0]
    return pl.pallas_call(
        _dispatch_kernel,
        grid_spec=pltpu.PrefetchScalarGridSpec(
            num_scalar_prefetch=1,
            grid=(T // tm,),
            in_specs=[pl.BlockSpec((tm, D_MODEL), lambda i, d: (i, 0))],
            out_specs=pl.BlockSpec(memory_space=pl.ANY),
            scratch_shapes=[pltpu.VMEM((tm * ROW_SUB, LANES), F32), pltpu.SemaphoreType.DMA(())]),
        out_shape=jax.ShapeDtypeStruct((T * TOP_K * ROW_SUB, LANES), F32),
        compiler_params=pltpu.CompilerParams(dimension_semantics=("arbitrary",),
                                             vmem_limit_bytes=VMEM_LIMIT),
        name="dispatch",
    )(dest_flat, h)


def _experts_kernel(tile_ref, exp_ref, lo_ref, hi_ref, x_ref, wg_ref, wu_ref, wd_ref, y_ref):
    bm = x_ref.shape[0] // ROW_SUB
    v = pl.program_id(0)
    lo = lo_ref[v]
    hi = hi_ref[v]

    @pl.when(hi > lo)
    def _():
        xb = _tiles_to_rows(x_ref, 0, bm).astype(BF16)
        g = _dot(xb, wg_ref[0].astype(BF16))
        u = _dot(xb, wu_ref[0].astype(BF16))
        mid = (g * jax.nn.sigmoid(g) * u).astype(BF16)
        y = _dot(mid, wd_ref[0].astype(BF16))
        rows = tile_ref[v] * bm + lax.broadcasted_iota(jnp.int32, (bm, 1), 0)
        mine = (rows >= lo) & (rows < hi)
        starts_tile = lo == tile_ref[v] * bm

        @pl.when(starts_tile)
        def _():
            _rows_to_tiles(y_ref, jnp.where(mine, y, 0.0))

        @pl.when(jnp.logical_not(starts_tile))
        def _():
            _rows_to_tiles(y_ref, jnp.where(mine, y, _tiles_to_rows(y_ref, 0, bm)))


def _experts(meta, xs, w_gate, w_up, w_down, bm):
    n_rows = xs.shape[0] // ROW_SUB
    n_visits = meta[0].shape[0]
    rows = pl.BlockSpec((bm * ROW_SUB, LANES), lambda v, t, e, lo, hi: (t[v], 0))
    return pl.pallas_call(
        _experts_kernel,
        grid_spec=pltpu.PrefetchScalarGridSpec(
            num_scalar_prefetch=4,
            grid=(n_visits,),
            in_specs=[rows,
                      pl.BlockSpec((1, D_MODEL, D_EXPERT), lambda v, t, e, lo, hi: (e[v], 0, 0)),
                      pl.BlockSpec((1, D_MODEL, D_EXPERT), lambda v, t, e, lo, hi: (e[v], 0, 0)),
                      pl.BlockSpec((1, D_EXPERT, D_MODEL), lambda v, t, e, lo, hi: (e[v], 0, 0))],
            out_specs=rows),
        out_shape=jax.ShapeDtypeStruct((n_rows * ROW_SUB, LANES), F32),
        compiler_params=pltpu.CompilerParams(dimension_semantics=("arbitrary",),
                                             vmem_limit_bytes=VMEM_LIMIT),
        name="experts",
    )(*meta, xs, w_gate, w_up, w_down)


def _visit_metadata(counts, n_rows, bm):
    n_tiles = n_rows // bm
    n_visits = n_tiles + N_EXPERTS - 1
    ends = jnp.cumsum(counts)
    starts = ends - counts
    first_tile = starts // bm
    last_tile = jnp.maximum(ends - 1, 0) // bm
    n_vis = jnp.where(counts > 0, last_tile - first_tile + 1, 0)
    vis_end = jnp.cumsum(n_vis)
    vis_start = vis_end - n_vis
    v = jnp.arange(n_visits, dtype=jnp.int32)
    total = vis_end[-1]
    vc = jnp.minimum(v, total - 1)
    e = jnp.sum((vis_end[None, :] <= vc[:, None]).astype(jnp.int32), axis=1)
    tile = first_tile[e] + (vc - vis_start[e])
    lo = jnp.maximum(starts[e], tile * bm)
    hi = jnp.minimum(ends[e], (tile + 1) * bm)
    hi = jnp.where(v < total, hi, lo)
    i32 = lambda a: a.astype(jnp.int32)
    return i32(tile), i32(e), i32(lo), i32(hi)


def _combine_kernel(alpha, dest_ref, h_ref, route_ref, g_ref, b_ref, y_ref, o_ref, buf, sem):
    tm = h_ref.shape[0]
    base = pl.program_id(0) * (tm * TOP_K)

    def row_copy(src, slot):
        return pltpu.make_async_copy(_row_tile(y_ref, src), _row_tile(buf, slot), sem)

    def start(i, c):
        for u in range(DMA_UNROLL):
            j = i * DMA_UNROLL + u
            slot = (u % TOP_K) * tm + i * (DMA_UNROLL // TOP_K) + u // TOP_K
            row_copy(dest_ref[base + j], slot).start(priority=u % 2)
        return c

    lax.fori_loop(0, tm * TOP_K // DMA_UNROLL, start, 0)

    def wait(i, c):
        for u in range(DMA_UNROLL):
            row_copy(0, 0).wait()
        return c

    lax.fori_loop(0, tm * TOP_K // DMA_UNROLL, wait, 0)

    rec = route_ref[...]
    ffn = (rec[:, R_G0:R_G0 + 1] * _tiles_to_rows(buf, 0, tm)
           + rec[:, R_G1:R_G1 + 1] * _tiles_to_rows(buf, tm, tm))
    o_ref[...] = _layer_norm(alpha * h_ref[...] + ffn, g_ref[...], b_ref[...])


def _combine(dest_flat, h, route, ln2_g, ln2_b, y_sorted, alpha, tm):
    T = h.shape[0]
    return pl.pallas_call(
        functools.partial(_combine_kernel, alpha),
        grid_spec=pltpu.PrefetchScalarGridSpec(
            num_scalar_prefetch=1,
            grid=(T // tm,),
            in_specs=[pl.BlockSpec((tm, D_MODEL), lambda i, d: (i, 0)),
                      pl.BlockSpec((tm, LANES), lambda i, d: (i, 0)),
                      pl.BlockSpec((1, D_MODEL), lambda i, d: (0, 0)),
                      pl.BlockSpec((1, D_MODEL), lambda i, d: (0, 0)),
                      pl.BlockSpec(memory_space=pl.ANY)],
            out_specs=pl.BlockSpec((tm, D_MODEL), lambda i, d: (i, 0)),
            scratch_shapes=[pltpu.VMEM((TOP_K * tm * ROW_SUB, LANES), F32), pltpu.SemaphoreType.DMA(())]),
        out_shape=jax.ShapeDtypeStruct((T, D_MODEL), F32),
        compiler_params=pltpu.CompilerParams(dimension_semantics=("arbitrary",),
                                             vmem_limit_bytes=VMEM_LIMIT),
        name="combine",
    )(dest_flat, h, route, ln2_g, ln2_b, y_sorted)


def _rope_column_order():
    within = np.concatenate([np.arange(0, DA_HEAD_DIM, 2), np.arange(1, DA_HEAD_DIM, 2)])
    return np.concatenate([m * DA_HEAD_DIM + within for m in range(2 * DA_HEADS)])


def kernel(x, positions, ln_in_g, ln_in_b, w_in, lam_q1, lam_k1, lam_q2, lam_k2, da_subln_g, gla_w_gate2, gla_b_gate2, gla_norm_g, w_o, ln1_g, ln1_b, router_w_group, router_b_group, router_w_expert, router_b_expert, w_gate, w_up, w_down, ln2_g, ln2_b):
    B, S, D = x.shape
    T = B * S
    depth = w_in.shape[0]
    assert depth == 1, "only a single layer is supported"
    alpha = (2 * depth) ** 0.25
    row2 = lambda a: a.reshape(1, -1)

    inv_freq = ROPE_THETA ** (-jnp.arange(0, DA_HEAD_DIM, 2, dtype=F32) / DA_HEAD_DIM)
    inv_freq = jnp.tile(inv_freq, LANES // (DA_HEAD_DIM // 2)).reshape(1, LANES)
    perm = _rope_column_order()
    pos2 = positions.reshape(T, 1)

    cur = x.reshape(T, D)
    cur_g, cur_b = row2(ln_in_g), row2(ln_in_b)
    for l in range(depth):
        w = w_in[l]
        w_q = w[:, :DA_Q][:, perm]
        w_k = w[:, DA_Q:DA_Q + DA_K][:, perm]
        w_main = jnp.concatenate([w_q, w_k, w[:, DA_Q + DA_K:D_MAIN]], axis=1).astype(BF16)
        w_glow = jnp.pad(w[:, D_MAIN:], ((0, 0), (0, LANES - GLA_GATE_RANK))).astype(BF16)
        w_gate2 = jnp.pad(gla_w_gate2[l], ((0, LANES - GLA_GATE_RANK), (0, 0))).astype(BF16)

        q, k, v, gq, gk, gv, go, la = _in_proj(cur, pos2, cur_g, cur_b, inv_freq, w_main, w_glow,
                                               w_gate2, row2(gla_b_gate2[l]), tm=512)
        lam_init = 0.8 - 0.6 * math.exp(-0.3 * l)
        sh = lambda a: a.reshape(B, S, a.shape[-1])
        da = _diff_attn(sh(q), sh(k), sh(v), row2(lam_q1[l]), row2(lam_k1[l]), row2(lam_q2[l]),
                        row2(lam_k2[l]), row2(da_subln_g[l]), lam_init)
        gl = _gla(sh(gq), sh(gk), sh(la), sh(gv), sh(go), row2(gla_norm_g[l]))

        w_r = jnp.zeros((D, LANES), F32)
        w_r = w_r.at[:, :N_GROUPS].set(router_w_group[l])
        w_r = w_r.at[:, EXPERT_LANE0:EXPERT_LANE0 + N_EXPERTS].set(router_w_expert[l])
        b_r = jnp.zeros((1, LANES), F32)
        b_r = b_r.at[0, :N_GROUPS].set(router_b_group[l])
        b_r = b_r.at[0, EXPERT_LANE0:EXPERT_LANE0 + N_EXPERTS].set(router_b_expert[l])
        wr_hi = w_r.astype(BF16)
        wr_lo = (w_r - wr_hi.astype(F32)).astype(BF16)

        h, route, cnt = _mix_out(cur, da.reshape(T, DA_V), gl.reshape(T, GLA_V), cur_g, cur_b,
                                 w_o[l].astype(BF16), row2(ln1_g[l]), row2(ln1_b[l]), wr_hi, wr_lo, b_r,
                                 alpha, tm=512)

        counts = cnt[0, :N_EXPERTS].astype(jnp.int32)
        seg_start = jnp.cumsum(counts) - counts
        eid = route[:, R_E0:R_E1 + 1].astype(jnp.int32)
        rank = route[:, R_RANK0:R_RANK1 + 1].astype(jnp.int32)
        onehot = eid[..., None] == jnp.arange(N_EXPERTS, dtype=jnp.int32)
        dest = jnp.sum(jnp.where(onehot, seg_start, 0), axis=-1) + rank
        dest_flat = dest.reshape(T * TOP_K)

        xs = _dispatch(dest_flat, h, tm=256)
        meta = _visit_metadata(counts, T * TOP_K, bm=256)
        ys = _experts(meta, xs, w_gate[l], w_up[l], w_down[l], bm=256)
        cur = _combine(dest_flat, h, route, row2(ln2_g[l]), row2(ln2_b[l]), ys, alpha, tm=256)
    return cur.reshape(B, S, D)
```

```python
import functools
import math

import jax
import jax.numpy as jnp
import numpy as np
from jax import lax
from jax.experimental import pallas as pl
from jax.experimental.pallas import tpu as pltpu

F32 = jnp.float32
BF16 = jnp.bfloat16

D_MODEL = 1024
CHUNK = 64
ROPE_THETA = 10000.0
LN_EPS = 1e-5

DA_HEADS = 4
DA_V_DIM = D_MODEL // (2 * DA_HEADS)
DA_HEAD_DIM = DA_V_DIM // 2
GLA_HEADS = 4
GLA_V_DIM = D_MODEL // (2 * GLA_HEADS)
GLA_KEY_DIM = GLA_V_DIM // 2
GLA_GATE_RANK = 16
GLA_GATE_NORMALIZER = 16.0

DA_Q = DA_HEADS * 2 * DA_HEAD_DIM
DA_K = DA_Q
DA_V = DA_HEADS * DA_V_DIM
GLA_Q = GLA_HEADS * GLA_KEY_DIM
GLA_K = GLA_Q
GLA_V = GLA_HEADS * GLA_V_DIM
GLA_OG = GLA_V
D_MAIN = DA_Q + DA_K + DA_V + GLA_Q + GLA_K + GLA_V + GLA_OG

N_GROUPS = 4
EXPERTS_PER_GROUP = 8
N_EXPERTS = N_GROUPS * EXPERTS_PER_GROUP
TOP_K = 2
D_EXPERT = D_MODEL // 2

LANES = 128
ROW_SUB = D_MODEL // LANES
ATTN_BLOCK = 256
DMA_UNROLL = 8
VMEM_LIMIT = 48 * 1024 * 1024

R_E0, R_E1, R_G0, R_G1, R_RANK0, R_RANK1 = 0, 1, 2, 3, 4, 5
EXPERT_LANE0 = 32


def _layer_norm(x, g, b):
    mu = jnp.mean(x, axis=-1, keepdims=True)
    xc = x - mu
    var = jnp.mean(xc * xc, axis=-1, keepdims=True)
    return xc * lax.rsqrt(var + LN_EPS) * g + b


def _dot(a, b):
    return jnp.dot(a, b, preferred_element_type=F32)


def _dot_nt(a, b):
    return lax.dot_general(a, b, (((1,), (1,)), ((), ())), preferred_element_type=F32)


def _dot_tn(a, b):
    return lax.dot_general(a, b, (((0,), (0,)), ((), ())), preferred_element_type=F32)


def _in_proj_kernel(x_ref, pos_ref, g_ref, b_ref, invf_ref, w_ref, wgl_ref, wg2_ref, bg2_ref,
                    q_ref, k_ref, v_ref, gq_ref, gk_ref, gv_ref, go_ref, la_ref):
    tm = x_ref.shape[0]
    xn = _layer_norm(x_ref[...], g_ref[...], b_ref[...])
    xb = xn.astype(BF16)
    proj = _dot(xb, w_ref[...])

    ang = pos_ref[...].astype(F32) * invf_ref[...]
    c = jnp.cos(ang)
    s = jnp.sin(ang)
    lane = lax.broadcasted_iota(jnp.int32, (tm, LANES), 1)
    first = (lane & (DA_HEAD_DIM // 2)) == 0
    s_lo = jnp.where(first, -s, 0.0)
    s_hi = jnp.where(first, 0.0, s)
    reps = DA_Q // LANES
    c4 = jnp.concatenate([c] * reps, axis=1)
    s_lo4 = jnp.concatenate([s_lo] * reps, axis=1)
    s_hi4 = jnp.concatenate([s_hi] * reps, axis=1)
    half = DA_HEAD_DIM // 2

    def rope(t):
        up = pltpu.roll(t, DA_Q - half, 1)
        dn = pltpu.roll(t, half, 1)
        return t * c4 + up * s_lo4 + dn * s_hi4

    o = 0
    q = rope(proj[:, o:o + DA_Q]) * (DA_HEAD_DIM ** -0.5)
    o += DA_Q
    k = rope(proj[:, o:o + DA_K])
    o += DA_K
    q_ref[...] = q.astype(BF16)
    k_ref[...] = k.astype(BF16)
    v_ref[...] = proj[:, o:o + DA_V].astype(BF16)
    o += DA_V
    gq_ref[...] = proj[:, o:o + GLA_Q].astype(BF16)
    o += GLA_Q
    gk_ref[...] = proj[:, o:o + GLA_K].astype(BF16)
    o += GLA_K
    gv_ref[...] = proj[:, o:o + GLA_V].astype(BF16)
    o += GLA_V
    go_ref[...] = proj[:, o:o + GLA_OG].astype(BF16)

    g_low = _dot(xb, wgl_ref[...])
    z = _dot(g_low.astype(BF16), wg2_ref[...]) + bg2_ref[...]
    log_sig = jnp.minimum(z, 0.0) - jnp.log1p(jnp.exp(-jnp.abs(z)))
    la_ref[...] = log_sig / GLA_GATE_NORMALIZER


def _in_proj(x2, pos2, ln_g, ln_b, inv_freq, w_main, w_glow, w_gate2, b_gate2, tm):
    T = x2.shape[0]
    row = lambda n: pl.BlockSpec((tm, n), lambda i: (i, 0))
    full = lambda a: pl.BlockSpec(a.shape, lambda i: (0,) * a.ndim)
    out_shape = [jax.ShapeDtypeStruct((T, n), dt) for n, dt in (
        (DA_Q, BF16), (DA_K, BF16), (DA_V, BF16), (GLA_Q, BF16), (GLA_K, BF16),
        (GLA_V, BF16), (GLA_OG, BF16), (GLA_K, F32))]
    return pl.pallas_call(
        _in_proj_kernel,
        grid=(T // tm,),
        in_specs=[row(D_MODEL), row(1), full(ln_g), full(ln_b), full(inv_freq), full(w_main),
                  full(w_glow), full(w_gate2), full(b_gate2)],
        out_specs=[row(s.shape[1]) for s in out_shape],
        out_shape=out_shape,
        compiler_params=pltpu.CompilerParams(dimension_semantics=("arbitrary",),
                                             vmem_limit_bytes=VMEM_LIMIT),
        name="in_proj",
    )(x2, pos2, ln_g, ln_b, inv_freq, w_main, w_glow, w_gate2, b_gate2)


def _diff_attn_kernel(lam_init, lq1_ref, lk1_ref, lq2_ref, lk2_ref, g_ref, q_ref, k_ref, v_ref, o_ref,
                      s_scr, p_scr):
    S = q_ref.shape[1]
    tq = ATTN_BLOCK
    lam = (jnp.exp(jnp.sum(lq1_ref[...] * lk1_ref[...], axis=-1, keepdims=True))
           - jnp.exp(jnp.sum(lq2_ref[...] * lk2_ref[...], axis=-1, keepdims=True)) + lam_init)
    lane = lax.broadcasted_iota(jnp.int32, (tq, LANES), 1)
    rq = lax.broadcasted_iota(jnp.int32, (2 * tq, tq), 0) % tq // CHUNK
    ck = lax.broadcasted_iota(jnp.int32, (2 * tq, tq), 1) // CHUNK
    diag_mask = ck <= rq

    for qi in range(S // tq):
        q = q_ref[0, qi * tq:(qi + 1) * tq, :]
        zero = jnp.zeros_like(q)
        qq = jnp.concatenate([jnp.where(lane < DA_HEAD_DIM, q, zero),
                              jnp.where(lane >= DA_HEAD_DIM, q, zero)], axis=0)
        m = None
        for j in range(qi + 1):
            s = _dot_nt(qq, k_ref[0, j * tq:(j + 1) * tq, :])
            if j == qi:
                s = jnp.where(diag_mask, s, -jnp.inf)
            s_scr[:, j * tq:(j + 1) * tq] = s
            mj = jnp.max(s, axis=-1, keepdims=True)
            m = mj if m is None else jnp.maximum(m, mj)
        l = None
        for j in range(qi + 1):
            p = jnp.exp(s_scr[:, j * tq:(j + 1) * tq] - m)
            p_scr[:, j * tq:(j + 1) * tq] = p.astype(BF16)
            lj = jnp.sum(p, axis=-1, keepdims=True)
            l = lj if l is None else l + lj
        nk = (qi + 1) * tq
        a = _dot(p_scr[:, 0:nk], v_ref[0, 0:nk, :]) / l
        o = a[0:tq] - lam * a[tq:2 * tq]
        o = o * lax.rsqrt(jnp.mean(o * o, axis=-1, keepdims=True) + LN_EPS) * g_ref[...]
        o_ref[0, qi * tq:(qi + 1) * tq, :] = (o * (1.0 - lam_init)).astype(o_ref.dtype)


def _diff_attn(q, k, v, lam_q1, lam_k1, lam_q2, lam_k2, subln_g, lam_init):
    B, S, _ = q.shape
    vec = pl.BlockSpec((1, DA_HEAD_DIM), lambda b, h: (0, 0))
    seq = pl.BlockSpec((1, S, LANES), lambda b, h: (b, 0, h))
    return pl.pallas_call(
        functools.partial(_diff_attn_kernel, lam_init),
        grid=(B, DA_HEADS),
        in_specs=[vec, vec, vec, vec, pl.BlockSpec((1, DA_V_DIM), lambda b, h: (0, 0)), seq, seq, seq],
        out_specs=seq,
        out_shape=jax.ShapeDtypeStruct((B, S, DA_V), BF16),
        scratch_shapes=[pltpu.VMEM((2 * ATTN_BLOCK, S), F32), pltpu.VMEM((2 * ATTN_BLOCK, S), BF16)],
        compiler_params=pltpu.CompilerParams(dimension_semantics=("arbitrary",) * 2,
                                             vmem_limit_bytes=VMEM_LIMIT),
        name="diff_attn",
    )(lam_q1, lam_k1, lam_q2, lam_k2, subln_g, q, k, v)


def _gla_kernel(q_ref, k_ref, la_ref, v_ref, go_ref, ng_ref, o_ref, *, unroll):
    S = q_ref.shape[1]
    C = CHUNK
    row = lax.broadcasted_iota(jnp.int32, (C, C), 0)
    col = lax.broadcasted_iota(jnp.int32, (C, C), 1)
    causal = col <= row
    tri = jnp.where(causal, 1.0, 0.0).astype(BF16)
    lane = lax.broadcasted_iota(jnp.int32, (C, LANES), 1)
    lane_sq = lax.broadcasted_iota(jnp.int32, (LANES, LANES), 1)
    head_lanes = (lane < GLA_KEY_DIM, lane >= GLA_KEY_DIM)
    head_lanes_sq = (lane_sq < GLA_KEY_DIM, lane_sq >= GLA_KEY_DIM)

    def one_chunk(r0, states):
        g = la_ref[0, pl.ds(r0, C), :]
        g1 = g.astype(BF16)
        e1 = g - g1.astype(F32)
        g2 = e1.astype(BF16)
        g3 = (e1 - g2.astype(F32)).astype(BF16)
        bcum = _dot(tri, g1) + _dot(tri, g2) + _dot(tri, g3)
        b_last = bcum[C - 1:C, :]
        qf = q_ref[0, pl.ds(r0, C), :].astype(F32) * (GLA_KEY_DIM ** -0.5)
        kf = k_ref[0, pl.ds(r0, C), :].astype(F32)
        q_t = (qf * jnp.exp(bcum)).astype(BF16)
        k_t = (kf * jnp.exp(-bcum)).astype(BF16)
        k_end = (kf * jnp.exp(b_last - bcum)).astype(BF16)
        decay = jnp.exp(b_last)
        new_states = []
        for hh in range(2):
            qm = jnp.where(head_lanes[hh], q_t, jnp.zeros_like(q_t))
            att = jnp.where(causal, _dot_nt(qm, k_t), 0.0).astype(BF16)
            vh = v_ref[0, pl.ds(r0, C), hh * GLA_V_DIM:(hh + 1) * GLA_V_DIM]
            st = states[hh]
            o = _dot(att, vh) + _dot_nt(qm, st.astype(BF16))
            ds = jnp.where(head_lanes_sq[hh], _dot_tn(vh, k_end), 0.0)
            new_states.append(st * decay + ds)
            o = o * lax.rsqrt(jnp.mean(o * o, axis=-1, keepdims=True) + LN_EPS) * ng_ref[...]
            gate = go_ref[0, pl.ds(r0, C), hh * GLA_V_DIM:(hh + 1) * GLA_V_DIM].astype(F32)
            o = o * (gate * jax.nn.sigmoid(gate))
            o_ref[0, pl.ds(r0, C), hh * GLA_V_DIM:(hh + 1) * GLA_V_DIM] = o.astype(o_ref.dtype)
        return tuple(new_states)

    def body(i, states):
        for u in range(unroll):
            r0 = pl.multiple_of((i * unroll + u) * C, C)
            states = one_chunk(r0, states)
        return states

    init = (jnp.zeros((GLA_V_DIM, LANES), F32),) * 2
    lax.fori_loop(0, S // (C * unroll), body, init)


def _gla(gq, gk, la, gv, go, norm_g, unroll=2):
    B, S, _ = gq.shape
    pairs = GLA_HEADS // 2
    narrow = pl.BlockSpec((1, S, LANES), lambda b, p: (b, 0, p))
    wide = pl.BlockSpec((1, S, 2 * GLA_V_DIM), lambda b, p: (b, 0, p))
    return pl.pallas_call(
        functools.partial(_gla_kernel, unroll=unroll),
        grid=(B, pairs),
        in_specs=[narrow, narrow, narrow, wide, wide,
                  pl.BlockSpec((1, GLA_V_DIM), lambda b, p: (0, 0))],
        out_specs=wide,
        out_shape=jax.ShapeDtypeStruct((B, S, GLA_V), BF16),
        compiler_params=pltpu.CompilerParams(dimension_semantics=("arbitrary",) * 2,
                                             vmem_limit_bytes=VMEM_LIMIT),
        name="gla",
    )(gq, gk, la, gv, go, norm_g)


def _split3(a):
    hi = a.astype(BF16)
    lo = (a - hi.astype(F32)).astype(BF16)
    return hi, lo


def _mix_out_kernel(alpha, x_ref, da_ref, gl_ref, lng_ref, lnb_ref, wo_ref, g1_ref, b1_ref,
                    wr_hi_ref, wr_lo_ref, br_ref, h_ref, route_ref, cnt_ref):
    tm = x_ref.shape[0]
    i = pl.program_id(0)

    @pl.when(i == 0)
    def _():
        cnt_ref[...] = jnp.zeros_like(cnt_ref)

    xn = _layer_norm(x_ref[...], lng_ref[...], lnb_ref[...])
    mix = _dot(da_ref[...], wo_ref[0:DA_V, :]) + _dot(gl_ref[...], wo_ref[DA_V:, :])
    h = _layer_norm(alpha * xn + mix, g1_ref[...], b1_ref[...])
    h_ref[...] = h

    h_hi, h_lo = _split3(h)
    logits = (_dot(h_hi, wr_hi_ref[...]) + _dot(h_hi, wr_lo_ref[...]) + _dot(h_lo, wr_hi_ref[...])
              + br_ref[...])
    lane = lax.broadcasted_iota(jnp.int32, (tm, LANES), 1)
    neg = -jnp.inf
    big = jnp.int32(LANES)

    def first_argmax(vals, valid):
        v = jnp.where(valid, vals, neg)
        mx = jnp.max(v, axis=-1, keepdims=True)
        idx = jnp.min(jnp.where(valid & (v == mx), lane, big), axis=-1, keepdims=True)
        return mx, idx

    is_group = lane < N_GROUPS
    g_max, g_top = first_argmax(logits, is_group)
    p_g = 1.0 / jnp.sum(jnp.where(is_group, jnp.exp(logits - g_max), 0.0), axis=-1, keepdims=True)

    e_lo = EXPERT_LANE0 + g_top * EXPERTS_PER_GROUP
    in_group = (lane >= e_lo) & (lane < e_lo + EXPERTS_PER_GROUP)
    v0, i0 = first_argmax(logits, in_group)
    v1, i1 = first_argmax(logits, in_group & (lane != i0))
    w1 = jnp.exp(v1 - v0)
    gate0 = p_g / (1.0 + w1)
    gate1 = p_g * w1 / (1.0 + w1)
    e0 = i0 - EXPERT_LANE0
    e1 = i1 - EXPERT_LANE0

    oh0 = jnp.where(lane == e0, 1.0, 0.0)
    oh1 = jnp.where(lane == e1, 1.0, 0.0)
    oh = oh0 + oh1
    r = lax.broadcasted_iota(jnp.int32, (tm, tm), 0)
    cidx = lax.broadcasted_iota(jnp.int32, (tm, tm), 1)
    strict_lower = jnp.where(cidx < r, 1.0, 0.0).astype(BF16)
    before = _dot(strict_lower, oh.astype(BF16)) + cnt_ref[0:1, :]
    rank0 = jnp.sum(oh0 * before, axis=-1, keepdims=True)
    rank1 = jnp.sum(oh1 * before, axis=-1, keepdims=True)
    cnt_ref[...] = cnt_ref[...] + jnp.sum(oh, axis=0, keepdims=True)

    rec = jnp.zeros((tm, LANES), F32)
    for ln, val in ((R_E0, e0.astype(F32)), (R_E1, e1.astype(F32)), (R_G0, gate0), (R_G1, gate1),
                    (R_RANK0, rank0), (R_RANK1, rank1)):
        rec = jnp.where(lane == ln, val, rec)
    route_ref[...] = rec


def _mix_out(x2, da2, gl2, ln_g, ln_b, w_o, ln1_g, ln1_b, wr_hi, wr_lo, b_r, alpha, tm):
    T = x2.shape[0]
    row = lambda n: pl.BlockSpec((tm, n), lambda i: (i, 0))
    full = lambda a: pl.BlockSpec(a.shape, lambda i: (0,) * a.ndim)
    return pl.pallas_call(
        functools.partial(_mix_out_kernel, alpha),
        grid=(T // tm,),
        in_specs=[row(D_MODEL), row(DA_V), row(GLA_V), full(ln_g), full(ln_b), full(w_o),
                  full(ln1_g), full(ln1_b), full(wr_hi), full(wr_lo), full(b_r)],
        out_specs=[row(D_MODEL), row(LANES), pl.BlockSpec((8, LANES), lambda i: (0, 0))],
        out_shape=[jax.ShapeDtypeStruct((T, D_MODEL), F32), jax.ShapeDtypeStruct((T, LANES), F32),
                   jax.ShapeDtypeStruct((8, LANES), F32)],
        compiler_params=pltpu.CompilerParams(dimension_semantics=("arbitrary",),
                                             vmem_limit_bytes=VMEM_LIMIT),
        name="mix_out",
    )(x2, da2, gl2, ln_g, ln_b, w_o, ln1_g, ln1_b, wr_hi, wr_lo, b_r)


def _rows_to_tiles(dst_ref, val):
    n = val.shape[0]
    for s in range(ROW_SUB):
        dst_ref[pl.ds(s, n, stride=ROW_SUB), :] = val[:, s * LANES:(s + 1) * LANES]


def _tiles_to_rows(src_ref, r0, n):
    return jnp.concatenate([src_ref[pl.ds(r0 * ROW_SUB + s, n, stride=ROW_SUB), :] for s in range(ROW_SUB)],
                           axis=1)


def _row_tile(ref, r):
    return ref.at[pl.ds(pl.multiple_of(r * ROW_SUB, ROW_SUB), ROW_SUB), :]


def _dispatch_kernel(dest_ref, h_ref, xs_ref, stage, sems):
    tm = h_ref.shape[0]
    step = pl.program_id(0)
    half = step & 1
    base = step * (tm * TOP_K)
    n_iter = tm * TOP_K // DMA_UNROLL
    _rows_to_tiles(stage.at[half], h_ref[...])

    def row_copy(hf, t, slot):
        return pltpu.make_async_copy(_row_tile(stage.at[hf], t), _row_tile(xs_ref, slot), sems.at[hf])

    def start(i, c):
        for u in range(DMA_UNROLL):
            t = i * (DMA_UNROLL // TOP_K) + u // TOP_K
            row_copy(half, t, dest_ref[base + i * DMA_UNROLL + u]).start(priority=u % 2)
        return c

    lax.fori_loop(0, n_iter, start, 0)

    def drain(hf):
        def wait(i, c):
            for u in range(DMA_UNROLL):
                row_copy(hf, 0, 0).wait()
            return c
        lax.fori_loop(0, n_iter, wait, 0)

    @pl.when(step > 0)
    def _():
        drain(1 - half)

    @pl.when(step == pl.num_programs(0) - 1)
    def _():
        drain(half)


def _dispatch(dest_flat, h, tm):
    T = h.shape[0]
    return pl.pallas_call(
        _dispatch_kernel,
        grid_spec=pltpu.PrefetchScalarGridSpec(
            num_scalar_prefetch=1,
            grid=(T // tm,),
            in_specs=[pl.BlockSpec((tm, D_MODEL), lambda i, d: (i, 0))],
            out_specs=pl.BlockSpec(memory_space=pl.ANY),
            scratch_shapes=[pltpu.VMEM((2, tm * ROW_SUB, LANES), F32), pltpu.SemaphoreType.DMA((2,))]),
        out_shape=jax.ShapeDtypeStruct((T * TOP_K * ROW_SUB, LANES), F32),
        compiler_params=pltpu.CompilerParams(dimension_semantics=("arbitrary",),
                                             vmem_limit_bytes=VMEM_LIMIT),
        name="dispatch",
    )(dest_flat, h)


def _experts_kernel(tile_ref, exp_ref, lo_ref, hi_ref, x_ref, wg_ref, wu_ref, wd_ref, y_ref, wgu_b, wd_b):
    bm = x_ref.shape[0] // ROW_SUB
    v = pl.program_id(0)
    lo = lo_ref[v]
    hi = hi_ref[v]

    @pl.when((v == 0) | (exp_ref[v] != exp_ref[jnp.maximum(v - 1, 0)]))
    def _():
        wgu_b[:, 0:D_EXPERT] = wg_ref[0].astype(BF16)
        wgu_b[:, D_EXPERT:] = wu_ref[0].astype(BF16)
        wd_b[...] = wd_ref[0].astype(BF16)

    @pl.when(hi > lo)
    def _():
        xb = _tiles_to_rows(x_ref, 0, bm).astype(BF16)
        gu = _dot(xb, wgu_b[...])
        g = gu[:, 0:D_EXPERT]
        mid = (g * jax.nn.sigmoid(g) * gu[:, D_EXPERT:]).astype(BF16)
        y = _dot(mid, wd_b[...])
        rows = tile_ref[v] * bm + lax.broadcasted_iota(jnp.int32, (bm, 1), 0)
        mine = (rows >= lo) & (rows < hi)
        starts_tile = lo == tile_ref[v] * bm

        @pl.when(starts_tile)
        def _():
            _rows_to_tiles(y_ref, jnp.where(mine, y, 0.0))

        @pl.when(jnp.logical_not(starts_tile))
        def _():
            _rows_to_tiles(y_ref, jnp.where(mine, y, _tiles_to_rows(y_ref, 0, bm)))


def _experts(meta, xs, w_gate, w_up, w_down, bm):
    n_rows = xs.shape[0] // ROW_SUB
    n_visits = meta[0].shape[0]
    rows = pl.BlockSpec((bm * ROW_SUB, LANES), lambda v, t, e, lo, hi: (t[v], 0))
    return pl.pallas_call(
        _experts_kernel,
        grid_spec=pltpu.PrefetchScalarGridSpec(
            num_scalar_prefetch=4,
            grid=(n_visits,),
            in_specs=[rows,
                      pl.BlockSpec((1, D_MODEL, D_EXPERT), lambda v, t, e, lo, hi: (e[v], 0, 0)),
                      pl.BlockSpec((1, D_MODEL, D_EXPERT), lambda v, t, e, lo, hi: (e[v], 0, 0)),
                      pl.BlockSpec((1, D_EXPERT, D_MODEL), lambda v, t, e, lo, hi: (e[v], 0, 0))],
            out_specs=rows,
            scratch_shapes=[pltpu.VMEM((D_MODEL, 2 * D_EXPERT), BF16), pltpu.VMEM((D_EXPERT, D_MODEL), BF16)]),
        out_shape=jax.ShapeDtypeStruct((n_rows * ROW_SUB, LANES), F32),
        compiler_params=pltpu.CompilerParams(dimension_semantics=("arbitrary",),
                                             vmem_limit_bytes=VMEM_LIMIT),
        name="experts",
    )(*meta, xs, w_gate, w_up, w_down)


def _visit_metadata(counts, n_rows, bm):
    n_tiles = n_rows // bm
    n_visits = n_tiles + N_EXPERTS - 1
    ends = jnp.cumsum(counts)
    starts = ends - counts
    first_tile = starts // bm
    last_tile = jnp.maximum(ends - 1, 0) // bm
    n_vis = jnp.where(counts > 0, last_tile - first_tile + 1, 0)
    vis_end = jnp.cumsum(n_vis)
    vis_start = vis_end - n_vis
    v = jnp.arange(n_visits, dtype=jnp.int32)
    total = vis_end[-1]
    vc = jnp.minimum(v, total - 1)
    e = jnp.sum((vis_end[None, :] <= vc[:, None]).astype(jnp.int32), axis=1)
    tile = first_tile[e] + (vc - vis_start[e])
    lo = jnp.maximum(starts[e], tile * bm)
    hi = jnp.minimum(ends[e], (tile + 1) * bm)
    hi = jnp.where(v < total, hi, lo)
    i32 = lambda a: a.astype(jnp.int32)
    return i32(tile), i32(e), i32(lo), i32(hi)


def _combine_kernel(alpha, dest_ref, h_ref, route_ref, g_ref, b_ref, y_ref, o_ref, buf, sems):
    tm = h_ref.shape[0]
    step = pl.program_id(0)
    half = step & 1
    n_iter = tm * TOP_K // DMA_UNROLL

    def row_copy(hf, src, slot):
        return pltpu.make_async_copy(_row_tile(y_ref, src), _row_tile(buf.at[hf], slot), sems.at[hf])

    def gather(st, hf):
        base = st * (tm * TOP_K)

        def start(i, c):
            for u in range(DMA_UNROLL):
                slot = (u % TOP_K) * tm + i * (DMA_UNROLL // TOP_K) + u // TOP_K
                row_copy(hf, dest_ref[base + i * DMA_UNROLL + u], slot).start(priority=u % 2)
            return c

        lax.fori_loop(0, n_iter, start, 0)

    @pl.when(step == 0)
    def _():
        gather(0, 0)

    @pl.when(step + 1 < pl.num_programs(0))
    def _():
        gather(step + 1, 1 - half)

    def wait(i, c):
        for u in range(DMA_UNROLL):
            row_copy(half, 0, 0).wait()
        return c

    lax.fori_loop(0, n_iter, wait, 0)

    rec = route_ref[...]
    cur = buf.at[half]
    ffn = (rec[:, R_G0:R_G0 + 1] * _tiles_to_rows(cur, 0, tm)
           + rec[:, R_G1:R_G1 + 1] * _tiles_to_rows(cur, tm, tm))
    o_ref[...] = _layer_norm(alpha * h_ref[...] + ffn, g_ref[...], b_ref[...])


def _combine(dest_flat, h, route, ln2_g, ln2_b, y_sorted, alpha, tm):
    T = h.shape[0]
    return pl.pallas_call(
        functools.partial(_combine_kernel, alpha),
        grid_spec=pltpu.PrefetchScalarGridSpec(
            num_scalar_prefetch=1,
            grid=(T // tm,),
            in_specs=[pl.BlockSpec((tm, D_MODEL), lambda i, d: (i, 0)),
                      pl.BlockSpec((tm, LANES), lambda i, d: (i, 0)),
                      pl.BlockSpec((1, D_MODEL), lambda i, d: (0, 0)),
                      pl.BlockSpec((1, D_MODEL), lambda i, d: (0, 0)),
                      pl.BlockSpec(memory_space=pl.ANY)],
            out_specs=pl.BlockSpec((tm, D_MODEL), lambda i, d: (i, 0)),
            scratch_shapes=[pltpu.VMEM((2, TOP_K * tm * ROW_SUB, LANES), F32), pltpu.SemaphoreType.DMA((2,))]),
        out_shape=jax.ShapeDtypeStruct((T, D_MODEL), F32),
        compiler_params=pltpu.CompilerParams(dimension_semantics=("arbitrary",),
                                             vmem_limit_bytes=VMEM_LIMIT),
        name="combine",
    )(dest_flat, h, route, ln2_g, ln2_b, y_sorted)


def _rope_column_order():
    within = np.concatenate([np.arange(0, DA_HEAD_DIM, 2), np.arange(1, DA_HEAD_DIM, 2)])
    return np.concatenate([m * DA_HEAD_DIM + within for m in range(2 * DA_HEADS)])


def kernel(x, positions, ln_in_g, ln_in_b, w_in, lam_q1, lam_k1, lam_q2, lam_k2, da_subln_g, gla_w_gate2, gla_b_gate2, gla_norm_g, w_o, ln1_g, ln1_b, router_w_group, router_b_group, router_w_expert, router_b_expert, w_gate, w_up, w_down, ln2_g, ln2_b):
    B, S, D = x.shape
    T = B * S
    depth = w_in.shape[0]
    assert depth == 1, "only a single layer is supported"
    alpha = (2 * depth) ** 0.25
    row2 = lambda a: a.reshape(1, -1)

    inv_freq = ROPE_THETA ** (-jnp.arange(0, DA_HEAD_DIM, 2, dtype=F32) / DA_HEAD_DIM)
    inv_freq = jnp.tile(inv_freq, LANES // (DA_HEAD_DIM // 2)).reshape(1, LANES)
    perm = _rope_column_order()
    pos2 = positions.reshape(T, 1)

    cur = x.reshape(T, D)
    cur_g, cur_b = row2(ln_in_g), row2(ln_in_b)
    for l in range(depth):
        w = w_in[l]
        w_q = w[:, :DA_Q][:, perm]
        w_k = w[:, DA_Q:DA_Q + DA_K][:, perm]
        w_main = jnp.concatenate([w_q, w_k, w[:, DA_Q + DA_K:D_MAIN]], axis=1).astype(BF16)
        w_glow = jnp.pad(w[:, D_MAIN:], ((0, 0), (0, LANES - GLA_GATE_RANK))).astype(BF16)
        w_gate2 = jnp.pad(gla_w_gate2[l], ((0, LANES - GLA_GATE_RANK), (0, 0))).astype(BF16)

        q, k, v, gq, gk, gv, go, la = _in_proj(cur, pos2, cur_g, cur_b, inv_freq, w_main, w_glow,
                                               w_gate2, row2(gla_b_gate2[l]), tm=512)
        lam_init = 0.8 - 0.6 * math.exp(-0.3 * l)
        sh = lambda a: a.reshape(B, S, a.shape[-1])
        da = _diff_attn(sh(q), sh(k), sh(v), row2(lam_q1[l]), row2(lam_k1[l]), row2(lam_q2[l]),
                        row2(lam_k2[l]), row2(da_subln_g[l]), lam_init)
        gl = _gla(sh(gq), sh(gk), sh(la), sh(gv), sh(go), row2(gla_norm_g[l]))

        w_r = jnp.zeros((D, LANES), F32)
        w_r = w_r.at[:, :N_GROUPS].set(router_w_group[l])
        w_r = w_r.at[:, EXPERT_LANE0:EXPERT_LANE0 + N_EXPERTS].set(router_w_expert[l])
        b_r = jnp.zeros((1, LANES), F32)
        b_r = b_r.at[0, :N_GROUPS].set(router_b_group[l])
        b_r = b_r.at[0, EXPERT_LANE0:EXPERT_LANE0 + N_EXPERTS].set(router_b_expert[l])
        wr_hi = w_r.astype(BF16)
        wr_lo = (w_r - wr_hi.astype(F32)).astype(BF16)

        h, route, cnt = _mix_out(cur, da.reshape(T, DA_V), gl.reshape(T, GLA_V), cur_g, cur_b,
                                 w_o[l].astype(BF16), row2(ln1_g[l]), row2(ln1_b[l]), wr_hi, wr_lo, b_r,
                                 alpha, tm=512)

        counts = cnt[0, :N_EXPERTS].astype(jnp.int32)
        seg_start = jnp.cumsum(counts) - counts
        eid = route[:, R_E0:R_E1 + 1].astype(jnp.int32)
        rank = route[:, R_RANK0:R_RANK1 + 1].astype(jnp.int32)
        onehot = eid[..., None] == jnp.arange(N_EXPERTS, dtype=jnp.int32)
        dest = jnp.sum(jnp.where(onehot, seg_start, 0), axis=-1) + rank
        dest_flat = dest.reshape(T * TOP_K)

        xs = _dispatch(dest_flat, h, tm=256)
        meta = _visit_metadata(counts, T * TOP_K, bm=256)
        ys = _experts(meta, xs, w_gate[l], w_up[l], w_down[l], bm=256)
        cur = _combine(dest_flat, h, route, row2(ln2_g[l]), row2(ln2_b[l]), ys, alpha, tm=256)
    return cur.reshape(B, S, D)
```

```python
import functools
import math

import jax
import jax.numpy as jnp
import numpy as np
from jax import lax
from jax.experimental import pallas as pl
from jax.experimental.pallas import tpu as pltpu

F32 = jnp.float32
BF16 = jnp.bfloat16

D_MODEL = 1024
CHUNK = 64
ROPE_THETA = 10000.0
LN_EPS = 1e-5

DA_HEADS = 4
DA_V_DIM = D_MODEL // (2 * DA_HEADS)
DA_HEAD_DIM = DA_V_DIM // 2
GLA_HEADS = 4
GLA_V_DIM = D_MODEL // (2 * GLA_HEADS)
GLA_KEY_DIM = GLA_V_DIM // 2
GLA_GATE_RANK = 16
GLA_GATE_NORMALIZER = 16.0

DA_Q = DA_HEADS * 2 * DA_HEAD_DIM
DA_K = DA_Q
DA_V = DA_HEADS * DA_V_DIM
GLA_Q = GLA_HEADS * GLA_KEY_DIM
GLA_K = GLA_Q
GLA_V = GLA_HEADS * GLA_V_DIM
GLA_OG = GLA_V
D_MAIN = DA_Q + DA_K + DA_V + GLA_Q + GLA_K + GLA_V + GLA_OG

N_GROUPS = 4
EXPERTS_PER_GROUP = 8
N_EXPERTS = N_GROUPS * EXPERTS_PER_GROUP
TOP_K = 2
D_EXPERT = D_MODEL // 2

LANES = 128
ROW_SUB = D_MODEL // (2 * LANES)
ATTN_BLOCK = 256
DMA_UNROLL = 8
VMEM_LIMIT = 48 * 1024 * 1024

R_E0, R_E1, R_G0, R_G1, R_RANK0, R_RANK1 = 0, 1, 2, 3, 4, 5
EXPERT_LANE0 = 32


def _layer_norm(x, g, b):
    mu = jnp.mean(x, axis=-1, keepdims=True)
    xc = x - mu
    var = jnp.mean(xc * xc, axis=-1, keepdims=True)
    return xc * lax.rsqrt(var + LN_EPS) * g + b


def _dot(a, b):
    return jnp.dot(a, b, preferred_element_type=F32)


def _dot_nt(a, b):
    return lax.dot_general(a, b, (((1,), (1,)), ((), ())), preferred_element_type=F32)


def _dot_tn(a, b):
    return lax.dot_general(a, b, (((0,), (0,)), ((), ())), preferred_element_type=F32)


def _in_proj_kernel(x_ref, pos_ref, g_ref, b_ref, invf_ref, w_ref, wgl_ref, wg2_ref, bg2_ref,
                    q_ref, k_ref, v_ref, gq_ref, gk_ref, gv_ref, go_ref, la_ref):
    tm = x_ref.shape[0]
    xn = _layer_norm(x_ref[...], g_ref[...], b_ref[...])
    xb = xn.astype(BF16)
    proj = _dot(xb, w_ref[...])

    ang = pos_ref[...].astype(F32) * invf_ref[...]
    c = jnp.cos(ang)
    s = jnp.sin(ang)
    lane = lax.broadcasted_iota(jnp.int32, (tm, LANES), 1)
    first = (lane & (DA_HEAD_DIM // 2)) == 0
    s_lo = jnp.where(first, -s, 0.0)
    s_hi = jnp.where(first, 0.0, s)
    reps = DA_Q // LANES
    c4 = jnp.concatenate([c] * reps, axis=1)
    s_lo4 = jnp.concatenate([s_lo] * reps, axis=1)
    s_hi4 = jnp.concatenate([s_hi] * reps, axis=1)
    half = DA_HEAD_DIM // 2

    def rope(t):
        up = pltpu.roll(t, DA_Q - half, 1)
        dn = pltpu.roll(t, half, 1)
        return t * c4 + up * s_lo4 + dn * s_hi4

    o = 0
    q = rope(proj[:, o:o + DA_Q]) * (DA_HEAD_DIM ** -0.5)
    o += DA_Q
    k = rope(proj[:, o:o + DA_K])
    o += DA_K
    q_ref[...] = q.astype(BF16)
    k_ref[...] = k.astype(BF16)
    v_ref[...] = proj[:, o:o + DA_V].astype(BF16)
    o += DA_V
    gq_ref[...] = proj[:, o:o + GLA_Q].astype(BF16)
    o += GLA_Q
    gk_ref[...] = proj[:, o:o + GLA_K].astype(BF16)
    o += GLA_K
    gv_ref[...] = proj[:, o:o + GLA_V].astype(BF16)
    o += GLA_V
    go_ref[...] = proj[:, o:o + GLA_OG].astype(BF16)

    g_low = _dot(xb, wgl_ref[...])
    z = _dot(g_low.astype(BF16), wg2_ref[...]) + bg2_ref[...]
    log_sig = jnp.minimum(z, 0.0) - jnp.log1p(jnp.exp(-jnp.abs(z)))
    la_ref[...] = log_sig / GLA_GATE_NORMALIZER


def _in_proj(x2, pos2, ln_g, ln_b, inv_freq, w_main, w_glow, w_gate2, b_gate2, tm):
    T = x2.shape[0]
    row = lambda n: pl.BlockSpec((tm, n), lambda i: (i, 0))
    full = lambda a: pl.BlockSpec(a.shape, lambda i: (0,) * a.ndim)
    out_shape = [jax.ShapeDtypeStruct((T, n), dt) for n, dt in (
        (DA_Q, BF16), (DA_K, BF16), (DA_V, BF16), (GLA_Q, BF16), (GLA_K, BF16),
        (GLA_V, BF16), (GLA_OG, BF16), (GLA_K, F32))]
    return pl.pallas_call(
        _in_proj_kernel,
        grid=(T // tm,),
        in_specs=[row(D_MODEL), row(1), full(ln_g), full(ln_b), full(inv_freq), full(w_main),
                  full(w_glow), full(w_gate2), full(b_gate2)],
        out_specs=[row(s.shape[1]) for s in out_shape],
        out_shape=out_shape,
        compiler_params=pltpu.CompilerParams(dimension_semantics=("arbitrary",),
                                             vmem_limit_bytes=VMEM_LIMIT),
        name="in_proj",
    )(x2, pos2, ln_g, ln_b, inv_freq, w_main, w_glow, w_gate2, b_gate2)


def _diff_attn_kernel(lam_init, lq1_ref, lk1_ref, lq2_ref, lk2_ref, g_ref, q_ref, k_ref, v_ref, o_ref,
                      s_scr, p_scr):
    S = q_ref.shape[1]
    tq = ATTN_BLOCK
    lam = (jnp.exp(jnp.sum(lq1_ref[...] * lk1_ref[...], axis=-1, keepdims=True))
           - jnp.exp(jnp.sum(lq2_ref[...] * lk2_ref[...], axis=-1, keepdims=True)) + lam_init)
    lane = lax.broadcasted_iota(jnp.int32, (tq, LANES), 1)
    rq = lax.broadcasted_iota(jnp.int32, (2 * tq, tq), 0) % tq // CHUNK
    ck = lax.broadcasted_iota(jnp.int32, (2 * tq, tq), 1) // CHUNK
    diag_mask = ck <= rq

    for qi in range(S // tq):
        q = q_ref[0, qi * tq:(qi + 1) * tq, :]
        zero = jnp.zeros_like(q)
        qq = jnp.concatenate([jnp.where(lane < DA_HEAD_DIM, q, zero),
                              jnp.where(lane >= DA_HEAD_DIM, q, zero)], axis=0)
        m = None
        for j in range(qi + 1):
            s = _dot_nt(qq, k_ref[0, j * tq:(j + 1) * tq, :])
            if j == qi:
                s = jnp.where(diag_mask, s, -jnp.inf)
            s_scr[:, j * tq:(j + 1) * tq] = s
            mj = jnp.max(s, axis=-1, keepdims=True)
            m = mj if m is None else jnp.maximum(m, mj)
        l = None
        for j in range(qi + 1):
            p = jnp.exp(s_scr[:, j * tq:(j + 1) * tq] - m)
            p_scr[:, j * tq:(j + 1) * tq] = p.astype(BF16)
            lj = jnp.sum(p, axis=-1, keepdims=True)
            l = lj if l is None else l + lj
        nk = (qi + 1) * tq
        a = _dot(p_scr[:, 0:nk], v_ref[0, 0:nk, :]) / l
        o = a[0:tq] - lam * a[tq:2 * tq]
        o = o * lax.rsqrt(jnp.mean(o * o, axis=-1, keepdims=True) + LN_EPS) * g_ref[...]
        o_ref[0, qi * tq:(qi + 1) * tq, :] = (o * (1.0 - lam_init)).astype(o_ref.dtype)


def _diff_attn(q, k, v, lam_q1, lam_k1, lam_q2, lam_k2, subln_g, lam_init):
    B, S, _ = q.shape
    vec = pl.BlockSpec((1, DA_HEAD_DIM), lambda b, h: (0, 0))
    seq = pl.BlockSpec((1, S, LANES), lambda b, h: (b, 0, h))
    return pl.pallas_call(
        functools.partial(_diff_attn_kernel, lam_init),
        grid=(B, DA_HEADS),
        in_specs=[vec, vec, vec, vec, pl.BlockSpec((1, DA_V_DIM), lambda b, h: (0, 0)), seq, seq, seq],
        out_specs=seq,
        out_shape=jax.ShapeDtypeStruct((B, S, DA_V), BF16),
        scratch_shapes=[pltpu.VMEM((2 * ATTN_BLOCK, S), F32), pltpu.VMEM((2 * ATTN_BLOCK, S), BF16)],
        compiler_params=pltpu.CompilerParams(dimension_semantics=("arbitrary",) * 2,
                                             vmem_limit_bytes=VMEM_LIMIT),
        name="diff_attn",
    )(lam_q1, lam_k1, lam_q2, lam_k2, subln_g, q, k, v)


def _gla_kernel(q_ref, k_ref, la_ref, v_ref, go_ref, ng_ref, o_ref, *, unroll):
    S = q_ref.shape[1]
    C = CHUNK
    row = lax.broadcasted_iota(jnp.int32, (C, C), 0)
    col = lax.broadcasted_iota(jnp.int32, (C, C), 1)
    causal = col <= row
    tri = jnp.where(causal, 1.0, 0.0).astype(BF16)
    lane = lax.broadcasted_iota(jnp.int32, (C, LANES), 1)
    lane_sq = lax.broadcasted_iota(jnp.int32, (LANES, LANES), 1)
    head_lanes = (lane < GLA_KEY_DIM, lane >= GLA_KEY_DIM)
    head_lanes_sq = (lane_sq < GLA_KEY_DIM, lane_sq >= GLA_KEY_DIM)

    def one_chunk(r0, states):
        g = la_ref[0, pl.ds(r0, C), :]
        g1 = g.astype(BF16)
        e1 = g - g1.astype(F32)
        g2 = e1.astype(BF16)
        g3 = (e1 - g2.astype(F32)).astype(BF16)
        bcum = _dot(tri, g1) + _dot(tri, g2) + _dot(tri, g3)
        b_last = bcum[C - 1:C, :]
        qf = q_ref[0, pl.ds(r0, C), :].astype(F32) * (GLA_KEY_DIM ** -0.5)
        kf = k_ref[0, pl.ds(r0, C), :].astype(F32)
        q_t = (qf * jnp.exp(bcum)).astype(BF16)
        k_t = (kf * jnp.exp(-bcum)).astype(BF16)
        k_end = (kf * jnp.exp(b_last - bcum)).astype(BF16)
        decay = jnp.exp(b_last)
        new_states = []
        for hh in range(2):
            qm = jnp.where(head_lanes[hh], q_t, jnp.zeros_like(q_t))
            att = jnp.where(causal, _dot_nt(qm, k_t), 0.0).astype(BF16)
            vh = v_ref[0, pl.ds(r0, C), hh * GLA_V_DIM:(hh + 1) * GLA_V_DIM]
            st = states[hh]
            o = _dot(att, vh) + _dot_nt(qm, st.astype(BF16))
            ds = jnp.where(head_lanes_sq[hh], _dot_tn(vh, k_end), 0.0)
            new_states.append(st * decay + ds)
            o = o * lax.rsqrt(jnp.mean(o * o, axis=-1, keepdims=True) + LN_EPS) * ng_ref[...]
            gate = go_ref[0, pl.ds(r0, C), hh * GLA_V_DIM:(hh + 1) * GLA_V_DIM].astype(F32)
            o = o * (gate * jax.nn.sigmoid(gate))
            o_ref[0, pl.ds(r0, C), hh * GLA_V_DIM:(hh + 1) * GLA_V_DIM] = o.astype(o_ref.dtype)
        return tuple(new_states)

    def body(i, states):
        for u in range(unroll):
            r0 = pl.multiple_of((i * unroll + u) * C, C)
            states = one_chunk(r0, states)
        return states

    init = (jnp.zeros((GLA_V_DIM, LANES), F32),) * 2
    lax.fori_loop(0, S // (C * unroll), body, init)


def _gla(gq, gk, la, gv, go, norm_g, unroll=2):
    B, S, _ = gq.shape
    pairs = GLA_HEADS // 2
    narrow = pl.BlockSpec((1, S, LANES), lambda b, p: (b, 0, p))
    wide = pl.BlockSpec((1, S, 2 * GLA_V_DIM), lambda b, p: (b, 0, p))
    return pl.pallas_call(
        functools.partial(_gla_kernel, unroll=unroll),
        grid=(B, pairs),
        in_specs=[narrow, narrow, narrow, wide, wide,
                  pl.BlockSpec((1, GLA_V_DIM), lambda b, p: (0, 0))],
        out_specs=wide,
        out_shape=jax.ShapeDtypeStruct((B, S, GLA_V), BF16),
        compiler_params=pltpu.CompilerParams(dimension_semantics=("arbitrary",) * 2,
                                             vmem_limit_bytes=VMEM_LIMIT),
        name="gla",
    )(gq, gk, la, gv, go, norm_g)


def _split3(a):
    hi = a.astype(BF16)
    lo = (a - hi.astype(F32)).astype(BF16)
    return hi, lo


def _mix_out_kernel(alpha, x_ref, da_ref, gl_ref, lng_ref, lnb_ref, wo_ref, g1_ref, b1_ref,
                    wr_hi_ref, wr_lo_ref, br_ref, h_ref, route_ref, cnt_ref):
    tm = x_ref.shape[0]
    i = pl.program_id(0)

    @pl.when(i == 0)
    def _():
        cnt_ref[...] = jnp.zeros_like(cnt_ref)

    xn = _layer_norm(x_ref[...], lng_ref[...], lnb_ref[...])
    mix = _dot(da_ref[...], wo_ref[0:DA_V, :]) + _dot(gl_ref[...], wo_ref[DA_V:, :])
    h = _layer_norm(alpha * xn + mix, g1_ref[...], b1_ref[...])
    h_ref[...] = h

    h_hi, h_lo = _split3(h)
    logits = (_dot(h_hi, wr_hi_ref[...]) + _dot(h_hi, wr_lo_ref[...]) + _dot(h_lo, wr_hi_ref[...])
              + br_ref[...])
    lane = lax.broadcasted_iota(jnp.int32, (tm, LANES), 1)
    neg = -jnp.inf
    big = jnp.int32(LANES)

    def first_argmax(vals, valid):
        v = jnp.where(valid, vals, neg)
        mx = jnp.max(v, axis=-1, keepdims=True)
        idx = jnp.min(jnp.where(valid & (v == mx), lane, big), axis=-1, keepdims=True)
        return mx, idx

    is_group = lane < N_GROUPS
    g_max, g_top = first_argmax(logits, is_group)
    p_g = 1.0 / jnp.sum(jnp.where(is_group, jnp.exp(logits - g_max), 0.0), axis=-1, keepdims=True)

    e_lo = EXPERT_LANE0 + g_top * EXPERTS_PER_GROUP
    in_group = (lane >= e_lo) & (lane < e_lo + EXPERTS_PER_GROUP)
    v0, i0 = first_argmax(logits, in_group)
    v1, i1 = first_argmax(logits, in_group & (lane != i0))
    w1 = jnp.exp(v1 - v0)
    gate0 = p_g / (1.0 + w1)
    gate1 = p_g * w1 / (1.0 + w1)
    e0 = i0 - EXPERT_LANE0
    e1 = i1 - EXPERT_LANE0

    oh0 = jnp.where(lane == e0, 1.0, 0.0)
    oh1 = jnp.where(lane == e1, 1.0, 0.0)
    oh = oh0 + oh1
    r = lax.broadcasted_iota(jnp.int32, (tm, tm), 0)
    cidx = lax.broadcasted_iota(jnp.int32, (tm, tm), 1)
    strict_lower = jnp.where(cidx < r, 1.0, 0.0).astype(BF16)
    before = _dot(strict_lower, oh.astype(BF16)) + cnt_ref[0:1, :]
    rank0 = jnp.sum(oh0 * before, axis=-1, keepdims=True)
    rank1 = jnp.sum(oh1 * before, axis=-1, keepdims=True)
    cnt_ref[...] = cnt_ref[...] + jnp.sum(oh, axis=0, keepdims=True)

    rec = jnp.zeros((tm, LANES), F32)
    for ln, val in ((R_E0, e0.astype(F32)), (R_E1, e1.astype(F32)), (R_G0, gate0), (R_G1, gate1),
                    (R_RANK0, rank0), (R_RANK1, rank1)):
        rec = jnp.where(lane == ln, val, rec)
    route_ref[...] = rec


def _mix_out(x2, da2, gl2, ln_g, ln_b, w_o, ln1_g, ln1_b, wr_hi, wr_lo, b_r, alpha, tm):
    T = x2.shape[0]
    row = lambda n: pl.BlockSpec((tm, n), lambda i: (i, 0))
    full = lambda a: pl.BlockSpec(a.shape, lambda i: (0,) * a.ndim)
    return pl.pallas_call(
        functools.partial(_mix_out_kernel, alpha),
        grid=(T // tm,),
        in_specs=[row(D_MODEL), row(DA_V), row(GLA_V), full(ln_g), full(ln_b), full(w_o),
                  full(ln1_g), full(ln1_b), full(wr_hi), full(wr_lo), full(b_r)],
        out_specs=[row(D_MODEL), row(LANES), pl.BlockSpec((8, LANES), lambda i: (0, 0))],
        out_shape=[jax.ShapeDtypeStruct((T, D_MODEL), F32), jax.ShapeDtypeStruct((T, LANES), F32),
                   jax.ShapeDtypeStruct((8, LANES), F32)],
        compiler_params=pltpu.CompilerParams(dimension_semantics=("arbitrary",),
                                             vmem_limit_bytes=VMEM_LIMIT),
        name="mix_out",
    )(x2, da2, gl2, ln_g, ln_b, w_o, ln1_g, ln1_b, wr_hi, wr_lo, b_r)


HIGH_HALF = 0xFFFF0000


def _pack_pairs(val):
    bits = lambda a: lax.bitcast_convert_type(a.astype(BF16).astype(F32), jnp.uint32)
    half = val.shape[1] // 2
    return (bits(val[:, :half]) >> 16) | (bits(val[:, half:]) & jnp.uint32(HIGH_HALF))


def _unpack_pairs(words):
    lo = lax.bitcast_convert_type(words << 16, F32)
    hi = lax.bitcast_convert_type(words & jnp.uint32(HIGH_HALF), F32)
    return jnp.concatenate([lo, hi], axis=1)


def _words_to_tiles(dst_ref, words):
    n = words.shape[0]
    for s in range(ROW_SUB):
        dst_ref[pl.ds(s, n, stride=ROW_SUB), :] = words[:, s * LANES:(s + 1) * LANES]


def _tiles_to_words(src_ref, r0, n):
    return jnp.concatenate([src_ref[pl.ds(r0 * ROW_SUB + s, n, stride=ROW_SUB), :] for s in range(ROW_SUB)],
                           axis=1)


def _rows_to_tiles(dst_ref, val):
    _words_to_tiles(dst_ref, _pack_pairs(val))


def _tiles_to_rows(src_ref, r0, n):
    return _unpack_pairs(_tiles_to_words(src_ref, r0, n))


def _row_tile(ref, r):
    return ref.at[pl.ds(pl.multiple_of(r * ROW_SUB, ROW_SUB), ROW_SUB), :]


def _dispatch_kernel(dest_ref, h_ref, xs_ref, stage, sems):
    tm = h_ref.shape[0]
    step = pl.program_id(0)
    half = step & 1
    base = step * (tm * TOP_K)
    n_iter = tm * TOP_K // DMA_UNROLL
    _rows_to_tiles(stage.at[half], h_ref[...])

    def row_copy(hf, t, slot):
        return pltpu.make_async_copy(_row_tile(stage.at[hf], t), _row_tile(xs_ref, slot), sems.at[hf])

    def start(i, c):
        for u in range(DMA_UNROLL):
            t = i * (DMA_UNROLL // TOP_K) + u // TOP_K
            row_copy(half, t, dest_ref[base + i * DMA_UNROLL + u]).start(priority=u % 2)
        return c

    lax.fori_loop(0, n_iter, start, 0)

    def drain(hf):
        def wait(i, c):
            for u in range(DMA_UNROLL):
                row_copy(hf, 0, 0).wait()
            return c
        lax.fori_loop(0, n_iter, wait, 0)

    @pl.when(step > 0)
    def _():
        drain(1 - half)

    @pl.when(step == pl.num_programs(0) - 1)
    def _():
        drain(half)


def _dispatch(dest_flat, h, tm):
    T = h.shape[0]
    return pl.pallas_call(
        _dispatch_kernel,
        grid_spec=pltpu.PrefetchScalarGridSpec(
            num_scalar_prefetch=1,
            grid=(T // tm,),
            in_specs=[pl.BlockSpec((tm, D_MODEL), lambda i, d: (i, 0))],
            out_specs=pl.BlockSpec(memory_space=pl.ANY),
            scratch_shapes=[pltpu.VMEM((2, tm * ROW_SUB, LANES), jnp.uint32), pltpu.SemaphoreType.DMA((2,))]),
        out_shape=jax.ShapeDtypeStruct((T * TOP_K * ROW_SUB, LANES), jnp.uint32),
        compiler_params=pltpu.CompilerParams(dimension_semantics=("arbitrary",),
                                             vmem_limit_bytes=VMEM_LIMIT),
        name="dispatch",
    )(dest_flat, h)


def _experts_kernel(tile_ref, exp_ref, lo_ref, hi_ref, x_ref, wg_ref, wu_ref, wd_ref, y_ref, wgu_b, wd_b):
    bm = x_ref.shape[0] // ROW_SUB
    v = pl.program_id(0)
    lo = lo_ref[v]
    hi = hi_ref[v]

    @pl.when((v == 0) | (exp_ref[v] != exp_ref[jnp.maximum(v - 1, 0)]))
    def _():
        wgu_b[:, 0:D_EXPERT] = wg_ref[0].astype(BF16)
        wgu_b[:, D_EXPERT:] = wu_ref[0].astype(BF16)
        wd_b[...] = wd_ref[0].astype(BF16)

    @pl.when(hi > lo)
    def _():
        xb = _tiles_to_rows(x_ref, 0, bm).astype(BF16)
        gu = _dot(xb, wgu_b[...])
        g = gu[:, 0:D_EXPERT]
        mid = (g * jax.nn.sigmoid(g) * gu[:, D_EXPERT:]).astype(BF16)
        y = _dot(mid, wd_b[...])
        rows = tile_ref[v] * bm + lax.broadcasted_iota(jnp.int32, (bm, 1), 0)
        mine = (rows >= lo) & (rows < hi)
        starts_tile = lo == tile_ref[v] * bm

        @pl.when(starts_tile)
        def _():
            _words_to_tiles(y_ref, jnp.where(mine, _pack_pairs(y), jnp.uint32(0)))

        @pl.when(jnp.logical_not(starts_tile))
        def _():
            _words_to_tiles(y_ref, jnp.where(mine, _pack_pairs(y), _tiles_to_words(y_ref, 0, bm)))


def _experts(meta, xs, w_gate, w_up, w_down, bm):
    n_rows = xs.shape[0] // ROW_SUB
    n_visits = meta[0].shape[0]
    rows = pl.BlockSpec((bm * ROW_SUB, LANES), lambda v, t, e, lo, hi: (t[v], 0))
    return pl.pallas_call(
        _experts_kernel,
        grid_spec=pltpu.PrefetchScalarGridSpec(
            num_scalar_prefetch=4,
            grid=(n_visits,),
            in_specs=[rows,
                      pl.BlockSpec((1, D_MODEL, D_EXPERT), lambda v, t, e, lo, hi: (e[v], 0, 0)),
                      pl.BlockSpec((1, D_MODEL, D_EXPERT), lambda v, t, e, lo, hi: (e[v], 0, 0)),
                      pl.BlockSpec((1, D_EXPERT, D_MODEL), lambda v, t, e, lo, hi: (e[v], 0, 0))],
            out_specs=rows,
            scratch_shapes=[pltpu.VMEM((D_MODEL, 2 * D_EXPERT), BF16), pltpu.VMEM((D_EXPERT, D_MODEL), BF16)]),
        out_shape=jax.ShapeDtypeStruct((n_rows * ROW_SUB, LANES), jnp.uint32),
        compiler_params=pltpu.CompilerParams(dimension_semantics=("arbitrary",),
                                             vmem_limit_bytes=VMEM_LIMIT),
        name="experts",
    )(*meta, xs, w_gate, w_up, w_down)


def _visit_metadata(counts, n_rows, bm):
    n_tiles = n_rows // bm
    n_visits = n_tiles + N_EXPERTS - 1
    ends = jnp.cumsum(counts)
    starts = ends - counts
    first_tile = starts // bm
    last_tile = jnp.maximum(ends - 1, 0) // bm
    n_vis = jnp.where(counts > 0, last_tile - first_tile + 1, 0)
    vis_end = jnp.cumsum(n_vis)
    vis_start = vis_end - n_vis
    v = jnp.arange(n_visits, dtype=jnp.int32)
    total = vis_end[-1]
    vc = jnp.minimum(v, total - 1)
    e = jnp.sum((vis_end[None, :] <= vc[:, None]).astype(jnp.int32), axis=1)
    tile = first_tile[e] + (vc - vis_start[e])
    lo = jnp.maximum(starts[e], tile * bm)
    hi = jnp.minimum(ends[e], (tile + 1) * bm)
    hi = jnp.where(v < total, hi, lo)
    i32 = lambda a: a.astype(jnp.int32)
    return i32(tile), i32(e), i32(lo), i32(hi)


def _combine_kernel(alpha, dest_ref, h_ref, route_ref, g_ref, b_ref, y_ref, o_ref, buf, sems):
    tm = h_ref.shape[0]
    step = pl.program_id(0)
    half = step & 1
    n_iter = tm * TOP_K // DMA_UNROLL

    def row_copy(hf, src, slot):
        return pltpu.make_async_copy(_row_tile(y_ref, src), _row_tile(buf.at[hf], slot), sems.at[hf])

    def gather(st, hf):
        base = st * (tm * TOP_K)

        def start(i, c):
            for u in range(DMA_UNROLL):
                slot = (u % TOP_K) * tm + i * (DMA_UNROLL // TOP_K) + u // TOP_K
                row_copy(hf, dest_ref[base + i * DMA_UNROLL + u], slot).start(priority=u % 2)
            return c

        lax.fori_loop(0, n_iter, start, 0)

    @pl.when(step == 0)
    def _():
        gather(0, 0)

    @pl.when(step + 1 < pl.num_programs(0))
    def _():
        gather(step + 1, 1 - half)

    def wait(i, c):
        for u in range(DMA_UNROLL):
            row_copy(half, 0, 0).wait()
        return c

    lax.fori_loop(0, n_iter, wait, 0)

    rec = route_ref[...]
    cur = buf.at[half]
    ffn = (rec[:, R_G0:R_G0 + 1] * _tiles_to_rows(cur, 0, tm)
           + rec[:, R_G1:R_G1 + 1] * _tiles_to_rows(cur, tm, tm))
    o_ref[...] = _layer_norm(alpha * h_ref[...] + ffn, g_ref[...], b_ref[...])


def _combine(dest_flat, h, route, ln2_g, ln2_b, y_sorted, alpha, tm):
    T = h.shape[0]
    return pl.pallas_call(
        functools.partial(_combine_kernel, alpha),
        grid_spec=pltpu.PrefetchScalarGridSpec(
            num_scalar_prefetch=1,
            grid=(T // tm,),
            in_specs=[pl.BlockSpec((tm, D_MODEL), lambda i, d: (i, 0)),
                      pl.BlockSpec((tm, LANES), lambda i, d: (i, 0)),
                      pl.BlockSpec((1, D_MODEL), lambda i, d: (0, 0)),
                      pl.BlockSpec((1, D_MODEL), lambda i, d: (0, 0)),
                      pl.BlockSpec(memory_space=pl.ANY)],
            out_specs=pl.BlockSpec((tm, D_MODEL), lambda i, d: (i, 0)),
            scratch_shapes=[pltpu.VMEM((2, TOP_K * tm * ROW_SUB, LANES), jnp.uint32), pltpu.SemaphoreType.DMA((2,))]),
        out_shape=jax.ShapeDtypeStruct((T, D_MODEL), F32),
        compiler_params=pltpu.CompilerParams(dimension_semantics=("arbitrary",),
                                             vmem_limit_bytes=VMEM_LIMIT),
        name="combine",
    )(dest_flat, h, route, ln2_g, ln2_b, y_sorted)


def _rope_column_order():
    within = np.concatenate([np.arange(0, DA_HEAD_DIM, 2), np.arange(1, DA_HEAD_DIM, 2)])
    return np.concatenate([m * DA_HEAD_DIM + within for m in range(2 * DA_HEADS)])


def kernel(x, positions, ln_in_g, ln_in_b, w_in, lam_q1, lam_k1, lam_q2, lam_k2, da_subln_g, gla_w_gate2, gla_b_gate2, gla_norm_g, w_o, ln1_g, ln1_b, router_w_group, router_b_group, router_w_expert, router_b_expert, w_gate, w_up, w_down, ln2_g, ln2_b):
    B, S, D = x.shape
    T = B * S
    depth = w_in.shape[0]
    assert depth == 1, "only a single layer is supported"
    alpha = (2 * depth) ** 0.25
    row2 = lambda a: a.reshape(1, -1)

    inv_freq = ROPE_THETA ** (-jnp.arange(0, DA_HEAD_DIM, 2, dtype=F32) / DA_HEAD_DIM)
    inv_freq = jnp.tile(inv_freq, LANES // (DA_HEAD_DIM // 2)).reshape(1, LANES)
    perm = _rope_column_order()
    pos2 = positions.reshape(T, 1)

    cur = x.reshape(T, D)
    cur_g, cur_b = row2(ln_in_g), row2(ln_in_b)
    for l in range(depth):
        w = w_in[l]
        w_q = w[:, :DA_Q][:, perm]
        w_k = w[:, DA_Q:DA_Q + DA_K][:, perm]
        w_main = jnp.concatenate([w_q, w_k, w[:, DA_Q + DA_K:D_MAIN]], axis=1).astype(BF16)
        w_glow = jnp.pad(w[:, D_MAIN:], ((0, 0), (0, LANES - GLA_GATE_RANK))).astype(BF16)
        w_gate2 = jnp.pad(gla_w_gate2[l], ((0, LANES - GLA_GATE_RANK), (0, 0))).astype(BF16)

        q, k, v, gq, gk, gv, go, la = _in_proj(cur, pos2, cur_g, cur_b, inv_freq, w_main, w_glow,
                                               w_gate2, row2(gla_b_gate2[l]), tm=512)
        lam_init = 0.8 - 0.6 * math.exp(-0.3 * l)
        sh = lambda a: a.reshape(B, S, a.shape[-1])
        da = _diff_attn(sh(q), sh(k), sh(v), row2(lam_q1[l]), row2(lam_k1[l]), row2(lam_q2[l]),
                        row2(lam_k2[l]), row2(da_subln_g[l]), lam_init)
        gl = _gla(sh(gq), sh(gk), sh(la), sh(gv), sh(go), row2(gla_norm_g[l]))

        w_r = jnp.zeros((D, LANES), F32)
        w_r = w_r.at[:, :N_GROUPS].set(router_w_group[l])
        w_r = w_r.at[:, EXPERT_LANE0:EXPERT_LANE0 + N_EXPERTS].set(router_w_expert[l])
        b_r = jnp.zeros((1, LANES), F32)
        b_r = b_r.at[0, :N_GROUPS].set(router_b_group[l])
        b_r = b_r.at[0, EXPERT_LANE0:EXPERT_LANE0 + N_EXPERTS].set(router_b_expert[l])
        wr_hi = w_r.astype(BF16)
        wr_lo = (w_r - wr_hi.astype(F32)).astype(BF16)

        h, route, cnt = _mix_out(cur, da.reshape(T, DA_V), gl.reshape(T, GLA_V), cur_g, cur_b,
                                 w_o[l].astype(BF16), row2(ln1_g[l]), row2(ln1_b[l]), wr_hi, wr_lo, b_r,
                                 alpha, tm=512)

        counts = cnt[0, :N_EXPERTS].astype(jnp.int32)
        seg_start = jnp.cumsum(counts) - counts
        eid = route[:, R_E0:R_E1 + 1].astype(jnp.int32)
        rank = route[:, R_RANK0:R_RANK1 + 1].astype(jnp.int32)
        onehot = eid[..., None] == jnp.arange(N_EXPERTS, dtype=jnp.int32)
        dest = jnp.sum(jnp.where(onehot, seg_start, 0), axis=-1) + rank
        dest_flat = dest.reshape(T * TOP_K)

        xs = _dispatch(dest_flat, h, tm=256)
        meta = _visit_metadata(counts, T * TOP_K, bm=256)
        ys = _experts(meta, xs, w_gate[l], w_up[l], w_down[l], bm=256)
        cur = _combine(dest_flat, h, route, row2(ln2_g[l]), row2(ln2_b[l]), ys, alpha, tm=256)
    return cur.reshape(B, S, D)
```

```python
import functools
import math

import jax
import jax.numpy as jnp
import numpy as np
from jax import lax
from jax.experimental import pallas as pl
from jax.experimental.pallas import tpu as pltpu

F32 = jnp.float32
BF16 = jnp.bfloat16

D_MODEL = 1024
CHUNK = 64
ROPE_THETA = 10000.0
LN_EPS = 1e-5

DA_HEADS = 4
DA_V_DIM = D_MODEL // (2 * DA_HEADS)
DA_HEAD_DIM = DA_V_DIM // 2
GLA_HEADS = 4
GLA_V_DIM = D_MODEL // (2 * GLA_HEADS)
GLA_KEY_DIM = GLA_V_DIM // 2
GLA_GATE_RANK = 16
GLA_GATE_NORMALIZER = 16.0

DA_Q = DA_HEADS * 2 * DA_HEAD_DIM
DA_K = DA_Q
DA_V = DA_HEADS * DA_V_DIM
GLA_Q = GLA_HEADS * GLA_KEY_DIM
GLA_K = GLA_Q
GLA_V = GLA_HEADS * GLA_V_DIM
GLA_OG = GLA_V
D_MAIN = DA_Q + DA_K + DA_V + GLA_Q + GLA_K + GLA_V + GLA_OG

N_GROUPS = 4
EXPERTS_PER_GROUP = 8
N_EXPERTS = N_GROUPS * EXPERTS_PER_GROUP
TOP_K = 2
D_EXPERT = D_MODEL // 2

LANES = 128
ROW_SUB = D_MODEL // (2 * LANES)
GLA_BLOCK = 256
ATTN_BLOCK = 256
DMA_UNROLL = 8
VMEM_LIMIT = 48 * 1024 * 1024

R_E0, R_E1, R_G0, R_G1, R_RANK0, R_RANK1 = 0, 1, 2, 3, 4, 5
EXPERT_LANE0 = 32


def _layer_norm(x, g, b):
    mu = jnp.mean(x, axis=-1, keepdims=True)
    xc = x - mu
    var = jnp.mean(xc * xc, axis=-1, keepdims=True)
    return xc * lax.rsqrt(var + LN_EPS) * g + b


def _dot(a, b):
    return jnp.dot(a, b, preferred_element_type=F32)


def _dot_nt(a, b):
    return lax.dot_general(a, b, (((1,), (1,)), ((), ())), preferred_element_type=F32)


def _dot_tn(a, b):
    return lax.dot_general(a, b, (((0,), (0,)), ((), ())), preferred_element_type=F32)


def _in_proj_kernel(x_ref, pos_ref, g_ref, b_ref, invf_ref, w_ref, wgl_ref, wg2_ref, bg2_ref,
                    q_ref, k_ref, v_ref, gq_ref, gk_ref, gv_ref, go_ref, la_ref):
    tm = x_ref.shape[0]
    xn = _layer_norm(x_ref[...], g_ref[...], b_ref[...])
    xb = xn.astype(BF16)
    proj = _dot(xb, w_ref[...])

    ang = pos_ref[...].astype(F32) * invf_ref[...]
    c = jnp.cos(ang)
    s = jnp.sin(ang)
    lane = lax.broadcasted_iota(jnp.int32, (tm, LANES), 1)
    first = (lane & (DA_HEAD_DIM // 2)) == 0
    s_lo = jnp.where(first, -s, 0.0)
    s_hi = jnp.where(first, 0.0, s)
    half = DA_HEAD_DIM // 2

    def rope(t):
        out = []
        for j in range(t.shape[1] // LANES):
            tj = t[:, j * LANES:(j + 1) * LANES]
            up = pltpu.roll(tj, LANES - half, 1)
            dn = pltpu.roll(tj, half, 1)
            out.append(tj * c + up * s_lo + dn * s_hi)
        return jnp.concatenate(out, axis=1)

    o = 0
    q = rope(proj[:, o:o + DA_Q]) * (DA_HEAD_DIM ** -0.5)
    o += DA_Q
    k = rope(proj[:, o:o + DA_K])
    o += DA_K
    q_ref[...] = q.astype(BF16)
    k_ref[...] = k.astype(BF16)
    v_ref[...] = proj[:, o:o + DA_V].astype(BF16)
    o += DA_V
    gq_ref[...] = proj[:, o:o + GLA_Q].astype(BF16)
    o += GLA_Q
    gk_ref[...] = proj[:, o:o + GLA_K].astype(BF16)
    o += GLA_K
    gv_ref[...] = proj[:, o:o + GLA_V].astype(BF16)
    o += GLA_V
    go_ref[...] = proj[:, o:o + GLA_OG].astype(BF16)

    g_low = _dot(xb, wgl_ref[...])
    z = _dot(g_low.astype(BF16), wg2_ref[...]) + bg2_ref[...]
    log_sig = jnp.minimum(z, 0.0) - jnp.log1p(jnp.exp(-jnp.abs(z)))
    la_ref[...] = log_sig / GLA_GATE_NORMALIZER


def _in_proj(x2, pos2, ln_g, ln_b, inv_freq, w_main, w_glow, w_gate2, b_gate2, tm):
    T = x2.shape[0]
    row = lambda n: pl.BlockSpec((tm, n), lambda i: (i, 0))
    full = lambda a: pl.BlockSpec(a.shape, lambda i: (0,) * a.ndim)
    out_shape = [jax.ShapeDtypeStruct((T, n), dt) for n, dt in (
        (DA_Q, BF16), (DA_K, BF16), (DA_V, BF16), (GLA_Q, BF16), (GLA_K, BF16),
        (GLA_V, BF16), (GLA_OG, BF16), (GLA_K, F32))]
    return pl.pallas_call(
        _in_proj_kernel,
        grid=(T // tm,),
        in_specs=[row(D_MODEL), row(1), full(ln_g), full(ln_b), full(inv_freq), full(w_main),
                  full(w_glow), full(w_gate2), full(b_gate2)],
        out_specs=[row(s.shape[1]) for s in out_shape],
        out_shape=out_shape,
        compiler_params=pltpu.CompilerParams(dimension_semantics=("arbitrary",),
                                             vmem_limit_bytes=VMEM_LIMIT),
        name="in_proj",
    )(x2, pos2, ln_g, ln_b, inv_freq, w_main, w_glow, w_gate2, b_gate2)


def _diff_attn_kernel(lam_init, lq1_ref, lk1_ref, lq2_ref, lk2_ref, g_ref, q_ref, k_ref, v_ref, o_ref,
                      s_scr, p_scr):
    S = q_ref.shape[1]
    tq = ATTN_BLOCK
    lam = (jnp.exp(jnp.sum(lq1_ref[...] * lk1_ref[...], axis=-1, keepdims=True))
           - jnp.exp(jnp.sum(lq2_ref[...] * lk2_ref[...], axis=-1, keepdims=True)) + lam_init)
    lane = lax.broadcasted_iota(jnp.int32, (tq, LANES), 1)
    rq = lax.broadcasted_iota(jnp.int32, (2 * tq, tq), 0) % tq // CHUNK
    ck = lax.broadcasted_iota(jnp.int32, (2 * tq, tq), 1) // CHUNK
    diag_mask = ck <= rq

    for qi in range(S // tq):
        q = q_ref[0, qi * tq:(qi + 1) * tq, :]
        zero = jnp.zeros_like(q)
        qq = jnp.concatenate([jnp.where(lane < DA_HEAD_DIM, q, zero),
                              jnp.where(lane >= DA_HEAD_DIM, q, zero)], axis=0)
        m = None
        for j in range(qi + 1):
            s = _dot_nt(qq, k_ref[0, j * tq:(j + 1) * tq, :])
            if j == qi:
                s = jnp.where(diag_mask, s, -jnp.inf)
            s_scr[:, j * tq:(j + 1) * tq] = s
            mj = jnp.max(s, axis=-1, keepdims=True)
            m = mj if m is None else jnp.maximum(m, mj)
        l = None
        for j in range(qi + 1):
            p = jnp.exp(s_scr[:, j * tq:(j + 1) * tq] - m)
            p_scr[:, j * tq:(j + 1) * tq] = p.astype(BF16)
            lj = jnp.sum(p, axis=-1, keepdims=True)
            l = lj if l is None else l + lj
        nk = (qi + 1) * tq
        a = _dot(p_scr[:, 0:nk], v_ref[0, 0:nk, :]) / l
        o = a[0:tq] - lam * a[tq:2 * tq]
        o = o * lax.rsqrt(jnp.mean(o * o, axis=-1, keepdims=True) + LN_EPS) * g_ref[...]
        o_ref[0, qi * tq:(qi + 1) * tq, :] = (o * (1.0 - lam_init)).astype(o_ref.dtype)


def _diff_attn(q, k, v, lam_q1, lam_k1, lam_q2, lam_k2, subln_g, lam_init):
    B, S, _ = q.shape
    vec = pl.BlockSpec((1, DA_HEAD_DIM), lambda b, h: (0, 0))
    seq = pl.BlockSpec((1, S, LANES), lambda b, h: (b, 0, h))
    return pl.pallas_call(
        functools.partial(_diff_attn_kernel, lam_init),
        grid=(B, DA_HEADS),
        in_specs=[vec, vec, vec, vec, pl.BlockSpec((1, DA_V_DIM), lambda b, h: (0, 0)), seq, seq, seq],
        out_specs=seq,
        out_shape=jax.ShapeDtypeStruct((B, S, DA_V), BF16),
        scratch_shapes=[pltpu.VMEM((2 * ATTN_BLOCK, S), F32), pltpu.VMEM((2 * ATTN_BLOCK, S), BF16)],
        compiler_params=pltpu.CompilerParams(dimension_semantics=("arbitrary",) * 2,
                                             vmem_limit_bytes=VMEM_LIMIT),
        name="diff_attn",
    )(lam_q1, lam_k1, lam_q2, lam_k2, subln_g, q, k, v)


def _gla_kernel(q_ref, k_ref, la_ref, v_ref, go_ref, ng_ref, o_ref, qt_s, oi_s, ds_s, dec_s):
    S = q_ref.shape[1]
    C = CHUNK
    BLK = GLA_BLOCK
    per_blk = BLK // C
    r = lax.broadcasted_iota(jnp.int32, (BLK, BLK), 0)
    c = lax.broadcasted_iota(jnp.int32, (BLK, BLK), 1)
    chunk_causal = (r // C == c // C) & (c <= r)
    tri = jnp.where(chunk_causal, 1.0, 0.0).astype(BF16)
    lane = lax.broadcasted_iota(jnp.int32, (BLK, LANES), 1)
    head_lanes = (lane < GLA_KEY_DIM, lane >= GLA_KEY_DIM)
    st_row = lax.broadcasted_iota(jnp.int32, (2 * GLA_V_DIM, LANES), 0)
    st_lane = lax.broadcasted_iota(jnp.int32, (2 * GLA_V_DIM, LANES), 1)
    own_keys = (st_row < GLA_V_DIM) == (st_lane < GLA_KEY_DIM)

    for b in range(S // BLK):
        r0 = b * BLK
        g = la_ref[0, r0:r0 + BLK, :]
        g1 = g.astype(BF16)
        e1 = g - g1.astype(F32)
        g2 = e1.astype(BF16)
        g3 = (e1 - g2.astype(F32)).astype(BF16)
        bcum = _dot(tri, g1) + _dot(tri, g2) + _dot(tri, g3)
        b_last = jnp.concatenate(
            [jnp.broadcast_to(bcum[i * C + C - 1:i * C + C, :], (C, LANES)) for i in range(per_blk)], axis=0)
        qf = q_ref[0, r0:r0 + BLK, :].astype(F32) * (GLA_KEY_DIM ** -0.5)
        kf = k_ref[0, r0:r0 + BLK, :].astype(F32)
        q_t = (qf * jnp.exp(bcum)).astype(BF16)
        k_t = (kf * jnp.exp(-bcum)).astype(BF16)
        k_end = (kf * jnp.exp(b_last - bcum)).astype(BF16)
        decay = jnp.exp(b_last)
        qt_s[r0:r0 + BLK, :] = q_t
        zero = jnp.zeros_like(q_t)
        for hh in range(2):
            att = jnp.where(chunk_causal, _dot_nt(jnp.where(head_lanes[hh], q_t, zero), k_t), 0.0).astype(BF16)
            oi_s[r0:r0 + BLK, hh * GLA_V_DIM:(hh + 1) * GLA_V_DIM] = _dot(
                att, v_ref[0, r0:r0 + BLK, hh * GLA_V_DIM:(hh + 1) * GLA_V_DIM])
        for i in range(per_blk):
            n = b * per_blk + i
            rows = slice(r0 + i * C, r0 + (i + 1) * C)
            inc = _dot_tn(v_ref[0, rows, :], k_end[i * C:(i + 1) * C, :])
            ds_s[n] = jnp.where(own_keys, inc, 0.0)
            dec_s[n:n + 1, :] = decay[i * C:i * C + 1, :]

    state = jnp.zeros((2 * GLA_V_DIM, LANES), F32)
    for n in range(S // C):
        rows = slice(n * C, (n + 1) * C)
        o = oi_s[rows, :] + _dot_nt(qt_s[rows, :], state.astype(BF16))
        state = state * dec_s[n:n + 1, :] + ds_s[n]
        for hh in range(2):
            cols = slice(hh * GLA_V_DIM, (hh + 1) * GLA_V_DIM)
            oh = o[:, cols]
            oh = oh * lax.rsqrt(jnp.mean(oh * oh, axis=-1, keepdims=True) + LN_EPS) * ng_ref[...]
            gate = go_ref[0, rows, cols].astype(F32)
            o_ref[0, rows, cols] = (oh * (gate * jax.nn.sigmoid(gate))).astype(o_ref.dtype)


def _gla(gq, gk, la, gv, go, norm_g):
    B, S, _ = gq.shape
    pairs = GLA_HEADS // 2
    narrow = pl.BlockSpec((1, S, LANES), lambda b, p: (b, 0, p))
    wide = pl.BlockSpec((1, S, 2 * GLA_V_DIM), lambda b, p: (b, 0, p))
    n_chunks = S // CHUNK
    return pl.pallas_call(
        _gla_kernel,
        grid=(B, pairs),
        in_specs=[narrow, narrow, narrow, wide, wide,
                  pl.BlockSpec((1, GLA_V_DIM), lambda b, p: (0, 0))],
        out_specs=wide,
        out_shape=jax.ShapeDtypeStruct((B, S, GLA_V), BF16),
        scratch_shapes=[pltpu.VMEM((S, LANES), BF16),
                        pltpu.VMEM((S, 2 * GLA_V_DIM), F32),
                        pltpu.VMEM((n_chunks, 2 * GLA_V_DIM, LANES), F32),
                        pltpu.VMEM((n_chunks, LANES), F32)],
        compiler_params=pltpu.CompilerParams(dimension_semantics=("arbitrary",) * 2,
                                             vmem_limit_bytes=VMEM_LIMIT),
        name="gla",
    )(gq, gk, la, gv, go, norm_g)


def _split3(a):
    hi = a.astype(BF16)
    lo = (a - hi.astype(F32)).astype(BF16)
    return hi, lo


def _mix_out_kernel(alpha, x_ref, da_ref, gl_ref, lng_ref, lnb_ref, wo_ref, g1_ref, b1_ref,
                    wr_hi_ref, wr_lo_ref, br_ref, h_ref, route_ref, cnt_ref, lower_ref):
    tm = x_ref.shape[0]
    i = pl.program_id(0)

    @pl.when(i == 0)
    def _():
        cnt_ref[...] = jnp.zeros_like(cnt_ref)
        r = lax.broadcasted_iota(jnp.int32, (tm, tm), 0)
        c = lax.broadcasted_iota(jnp.int32, (tm, tm), 1)
        lower_ref[...] = jnp.where(c < r, 1.0, 0.0).astype(BF16)

    xn = _layer_norm(x_ref[...], lng_ref[...], lnb_ref[...])
    mix = _dot(da_ref[...], wo_ref[0:DA_V, :]) + _dot(gl_ref[...], wo_ref[DA_V:, :])
    h = _layer_norm(alpha * xn + mix, g1_ref[...], b1_ref[...])
    h_ref[...] = h

    h_hi, h_lo = _split3(h)
    logits = (_dot(h_hi, wr_hi_ref[...]) + _dot(h_hi, wr_lo_ref[...]) + _dot(h_lo, wr_hi_ref[...])
              + br_ref[...])
    lane = lax.broadcasted_iota(jnp.int32, (tm, LANES), 1)
    neg = -jnp.inf
    big = jnp.int32(LANES)

    def first_argmax(vals, valid):
        v = jnp.where(valid, vals, neg)
        mx = jnp.max(v, axis=-1, keepdims=True)
        idx = jnp.min(jnp.where(valid & (v == mx), lane, big), axis=-1, keepdims=True)
        return mx, idx

    is_group = lane < N_GROUPS
    g_max, g_top = first_argmax(logits, is_group)
    p_g = 1.0 / jnp.sum(jnp.where(is_group, jnp.exp(logits - g_max), 0.0), axis=-1, keepdims=True)

    e_lo = EXPERT_LANE0 + g_top * EXPERTS_PER_GROUP
    in_group = (lane >= e_lo) & (lane < e_lo + EXPERTS_PER_GROUP)
    v0, i0 = first_argmax(logits, in_group)
    v1, i1 = first_argmax(logits, in_group & (lane != i0))
    w1 = jnp.exp(v1 - v0)
    gate0 = p_g / (1.0 + w1)
    gate1 = p_g * w1 / (1.0 + w1)
    e0 = i0 - EXPERT_LANE0
    e1 = i1 - EXPERT_LANE0

    oh0 = jnp.where(lane == e0, 1.0, 0.0)
    oh1 = jnp.where(lane == e1, 1.0, 0.0)
    oh = oh0 + oh1
    before = _dot(lower_ref[...], oh.astype(BF16)) + cnt_ref[0:1, :]
    rank0 = jnp.sum(oh0 * before, axis=-1, keepdims=True)
    rank1 = jnp.sum(oh1 * before, axis=-1, keepdims=True)
    cnt_ref[...] = cnt_ref[...] + jnp.sum(oh, axis=0, keepdims=True)

    rec = jnp.zeros((tm, LANES), F32)
    for ln, val in ((R_E0, e0.astype(F32)), (R_E1, e1.astype(F32)), (R_G0, gate0), (R_G1, gate1),
                    (R_RANK0, rank0), (R_RANK1, rank1)):
        rec = jnp.where(lane == ln, val, rec)
    route_ref[...] = rec


def _mix_out(x2, da2, gl2, ln_g, ln_b, w_o, ln1_g, ln1_b, wr_hi, wr_lo, b_r, alpha, tm):
    T = x2.shape[0]
    row = lambda n: pl.BlockSpec((tm, n), lambda i: (i, 0))
    full = lambda a: pl.BlockSpec(a.shape, lambda i: (0,) * a.ndim)
    return pl.pallas_call(
        functools.partial(_mix_out_kernel, alpha),
        grid=(T // tm,),
        in_specs=[row(D_MODEL), row(DA_V), row(GLA_V), full(ln_g), full(ln_b), full(w_o),
                  full(ln1_g), full(ln1_b), full(wr_hi), full(wr_lo), full(b_r)],
        out_specs=[row(D_MODEL), row(LANES), pl.BlockSpec((8, LANES), lambda i: (0, 0))],
        out_shape=[jax.ShapeDtypeStruct((T, D_MODEL), F32), jax.ShapeDtypeStruct((T, LANES), F32),
                   jax.ShapeDtypeStruct((8, LANES), F32)],
        scratch_shapes=[pltpu.VMEM((tm, tm), BF16)],
        compiler_params=pltpu.CompilerParams(dimension_semantics=("arbitrary",),
                                             vmem_limit_bytes=VMEM_LIMIT),
        name="mix_out",
    )(x2, da2, gl2, ln_g, ln_b, w_o, ln1_g, ln1_b, wr_hi, wr_lo, b_r)


HIGH_HALF = 0xFFFF0000


def _pack_pairs(val):
    bits = lambda a: lax.bitcast_convert_type(a.astype(BF16).astype(F32), jnp.uint32)
    half = val.shape[1] // 2
    return (bits(val[:, :half]) >> 16) | (bits(val[:, half:]) & jnp.uint32(HIGH_HALF))


def _unpack_pairs(words):
    lo = lax.bitcast_convert_type(words << 16, F32)
    hi = lax.bitcast_convert_type(words & jnp.uint32(HIGH_HALF), F32)
    return jnp.concatenate([lo, hi], axis=1)


def _words_to_tiles(dst_ref, words):
    n = words.shape[0]
    for s in range(ROW_SUB):
        dst_ref[pl.ds(s, n, stride=ROW_SUB), :] = words[:, s * LANES:(s + 1) * LANES]


def _tiles_to_words(src_ref, r0, n):
    return jnp.concatenate([src_ref[pl.ds(r0 * ROW_SUB + s, n, stride=ROW_SUB), :] for s in range(ROW_SUB)],
                           axis=1)


def _rows_to_tiles(dst_ref, val):
    _words_to_tiles(dst_ref, _pack_pairs(val))


def _tiles_to_rows(src_ref, r0, n):
    return _unpack_pairs(_tiles_to_words(src_ref, r0, n))


def _row_tile(ref, r):
    return ref.at[pl.ds(pl.multiple_of(r * ROW_SUB, ROW_SUB), ROW_SUB), :]


def _dispatch_kernel(dest_ref, h_ref, xs_ref, stage, sems):
    tm = h_ref.shape[0]
    step = pl.program_id(0)
    half = step & 1
    base = step * (tm * TOP_K)
    n_iter = tm * TOP_K // DMA_UNROLL
    _rows_to_tiles(stage.at[half], h_ref[...])

    def row_copy(hf, t, slot):
        return pltpu.make_async_copy(_row_tile(stage.at[hf], t), _row_tile(xs_ref, slot), sems.at[hf])

    def start(i, c):
        for u in range(DMA_UNROLL):
            t = i * (DMA_UNROLL // TOP_K) + u // TOP_K
            row_copy(half, t, dest_ref[base + i * DMA_UNROLL + u]).start(priority=u % 2)
        return c

    lax.fori_loop(0, n_iter, start, 0)

    def drain(hf):
        def wait(i, c):
            for u in range(DMA_UNROLL):
                row_copy(hf, 0, 0).wait()
            return c
        lax.fori_loop(0, n_iter, wait, 0)

    @pl.when(step > 0)
    def _():
        drain(1 - half)

    @pl.when(step == pl.num_programs(0) - 1)
    def _():
        drain(half)


def _dispatch(dest_flat, h, tm):
    T = h.shape[0]
    return pl.pallas_call(
        _dispatch_kernel,
        grid_spec=pltpu.PrefetchScalarGridSpec(
            num_scalar_prefetch=1,
            grid=(T // tm,),
            in_specs=[pl.BlockSpec((tm, D_MODEL), lambda i, d: (i, 0))],
            out_specs=pl.BlockSpec(memory_space=pl.ANY),
            scratch_shapes=[pltpu.VMEM((2, tm * ROW_SUB, LANES), jnp.uint32), pltpu.SemaphoreType.DMA((2,))]),
        out_shape=jax.ShapeDtypeStruct((T * TOP_K * ROW_SUB, LANES), jnp.uint32),
        compiler_params=pltpu.CompilerParams(dimension_semantics=("arbitrary",),
                                             vmem_limit_bytes=VMEM_LIMIT),
        name="dispatch",
    )(dest_flat, h)


def _experts_kernel(tile_ref, exp_ref, lo_ref, hi_ref, x_ref, wg_ref, wu_ref, wd_ref, y_ref, wgu_b, wd_b):
    bm = x_ref.shape[0] // ROW_SUB
    v = pl.program_id(0)
    lo = lo_ref[v]
    hi = hi_ref[v]

    @pl.when((v == 0) | (exp_ref[v] != exp_ref[jnp.maximum(v - 1, 0)]))
    def _():
        wgu_b[:, 0:D_EXPERT] = wg_ref[0].astype(BF16)
        wgu_b[:, D_EXPERT:] = wu_ref[0].astype(BF16)
        wd_b[...] = wd_ref[0].astype(BF16)

    @pl.when(hi > lo)
    def _():
        xb = _tiles_to_rows(x_ref, 0, bm).astype(BF16)
        gu = _dot(xb, wgu_b[...])
        g = gu[:, 0:D_EXPERT]
        mid = (g * jax.nn.sigmoid(g) * gu[:, D_EXPERT:]).astype(BF16)
        y = _dot(mid, wd_b[...])
        rows = tile_ref[v] * bm + lax.broadcasted_iota(jnp.int32, (bm, 1), 0)
        mine = (rows >= lo) & (rows < hi)
        starts_tile = lo == tile_ref[v] * bm

        @pl.when(starts_tile)
        def _():
            _words_to_tiles(y_ref, jnp.where(mine, _pack_pairs(y), jnp.uint32(0)))

        @pl.when(jnp.logical_not(starts_tile))
        def _():
            _words_to_tiles(y_ref, jnp.where(mine, _pack_pairs(y), _tiles_to_words(y_ref, 0, bm)))


def _experts(meta, xs, w_gate, w_up, w_down, bm):
    n_rows = xs.shape[0] // ROW_SUB
    n_visits = meta[0].shape[0]
    rows = pl.BlockSpec((bm * ROW_SUB, LANES), lambda v, t, e, lo, hi: (t[v], 0))
    return pl.pallas_call(
        _experts_kernel,
        grid_spec=pltpu.PrefetchScalarGridSpec(
            num_scalar_prefetch=4,
            grid=(n_visits,),
            in_specs=[rows,
                      pl.BlockSpec((1, D_MODEL, D_EXPERT), lambda v, t, e, lo, hi: (e[v], 0, 0)),
                      pl.BlockSpec((1, D_MODEL, D_EXPERT), lambda v, t, e, lo, hi: (e[v], 0, 0)),
                      pl.BlockSpec((1, D_EXPERT, D_MODEL), lambda v, t, e, lo, hi: (e[v], 0, 0))],
            out_specs=rows,
            scratch_shapes=[pltpu.VMEM((D_MODEL, 2 * D_EXPERT), BF16), pltpu.VMEM((D_EXPERT, D_MODEL), BF16)]),
        out_shape=jax.ShapeDtypeStruct((n_rows * ROW_SUB, LANES), jnp.uint32),
        compiler_params=pltpu.CompilerParams(dimension_semantics=("arbitrary",),
                                             vmem_limit_bytes=VMEM_LIMIT),
        name="experts",
    )(*meta, xs, w_gate, w_up, w_down)


def _visit_metadata(counts, n_rows, bm):
    n_tiles = n_rows // bm
    n_visits = n_tiles + N_EXPERTS - 1
    ends = jnp.cumsum(counts)
    starts = ends - counts
    first_tile = starts // bm
    last_tile = jnp.maximum(ends - 1, 0) // bm
    n_vis = jnp.where(counts > 0, last_tile - first_tile + 1, 0)
    vis_end = jnp.cumsum(n_vis)
    vis_start = vis_end - n_vis
    v = jnp.arange(n_visits, dtype=jnp.int32)
    total = vis_end[-1]
    vc = jnp.minimum(v, total - 1)
    e = jnp.sum((vis_end[None, :] <= vc[:, None]).astype(jnp.int32), axis=1)
    tile = first_tile[e] + (vc - vis_start[e])
    lo = jnp.maximum(starts[e], tile * bm)
    hi = jnp.minimum(ends[e], (tile + 1) * bm)
    hi = jnp.where(v < total, hi, lo)
    i32 = lambda a: a.astype(jnp.int32)
    return i32(tile), i32(e), i32(lo), i32(hi)


def _combine_kernel(alpha, dest_ref, h_ref, route_ref, g_ref, b_ref, y_ref, o_ref, buf, sems):
    tm = h_ref.shape[0]
    step = pl.program_id(0)
    half = step & 1
    n_iter = tm * TOP_K // DMA_UNROLL

    def row_copy(hf, src, slot):
        return pltpu.make_async_copy(_row_tile(y_ref, src), _row_tile(buf.at[hf], slot), sems.at[hf])

    def gather(st, hf):
        base = st * (tm * TOP_K)

        def start(i, c):
            for u in range(DMA_UNROLL):
                slot = (u % TOP_K) * tm + i * (DMA_UNROLL // TOP_K) + u // TOP_K
                row_copy(hf, dest_ref[base + i * DMA_UNROLL + u], slot).start(priority=u % 2)
            return c

        lax.fori_loop(0, n_iter, start, 0)

    @pl.when(step == 0)
    def _():
        gather(0, 0)

    @pl.when(step + 1 < pl.num_programs(0))
    def _():
        gather(step + 1, 1 - half)

    def wait(i, c):
        for u in range(DMA_UNROLL):
            row_copy(half, 0, 0).wait()
        return c

    lax.fori_loop(0, n_iter, wait, 0)

    rec = route_ref[...]
    cur = buf.at[half]
    ffn = (rec[:, R_G0:R_G0 + 1] * _tiles_to_rows(cur, 0, tm)
           + rec[:, R_G1:R_G1 + 1] * _tiles_to_rows(cur, tm, tm))
    o_ref[...] = _layer_norm(alpha * h_ref[...] + ffn, g_ref[...], b_ref[...])


def _combine(dest_flat, h, route, ln2_g, ln2_b, y_sorted, alpha, tm):
    T = h.shape[0]
    return pl.pallas_call(
        functools.partial(_combine_kernel, alpha),
        grid_spec=pltpu.PrefetchScalarGridSpec(
            num_scalar_prefetch=1,
            grid=(T // tm,),
            in_specs=[pl.BlockSpec((tm, D_MODEL), lambda i, d: (i, 0)),
                      pl.BlockSpec((tm, LANES), lambda i, d: (i, 0)),
                      pl.BlockSpec((1, D_MODEL), lambda i, d: (0, 0)),
                      pl.BlockSpec((1, D_MODEL), lambda i, d: (0, 0)),
                      pl.BlockSpec(memory_space=pl.ANY)],
            out_specs=pl.BlockSpec((tm, D_MODEL), lambda i, d: (i, 0)),
            scratch_shapes=[pltpu.VMEM((2, TOP_K * tm * ROW_SUB, LANES), jnp.uint32), pltpu.SemaphoreType.DMA((2,))]),
        out_shape=jax.ShapeDtypeStruct((T, D_MODEL), F32),
        compiler_params=pltpu.CompilerParams(dimension_semantics=("arbitrary",),
                                             vmem_limit_bytes=VMEM_LIMIT),
        name="combine",
    )(dest_flat, h, route, ln2_g, ln2_b, y_sorted)


def _rope_column_order():
    within = np.concatenate([np.arange(0, DA_HEAD_DIM, 2), np.arange(1, DA_HEAD_DIM, 2)])
    return np.concatenate([m * DA_HEAD_DIM + within for m in range(2 * DA_HEADS)])


def kernel(x, positions, ln_in_g, ln_in_b, w_in, lam_q1, lam_k1, lam_q2, lam_k2, da_subln_g, gla_w_gate2, gla_b_gate2, gla_norm_g, w_o, ln1_g, ln1_b, router_w_group, router_b_group, router_w_expert, router_b_expert, w_gate, w_up, w_down, ln2_g, ln2_b):
    B, S, D = x.shape
    T = B * S
    depth = w_in.shape[0]
    assert depth == 1, "only a single layer is supported"
    alpha = (2 * depth) ** 0.25
    row2 = lambda a: a.reshape(1, -1)

    inv_freq = ROPE_THETA ** (-jnp.arange(0, DA_HEAD_DIM, 2, dtype=F32) / DA_HEAD_DIM)
    inv_freq = jnp.tile(inv_freq, LANES // (DA_HEAD_DIM // 2)).reshape(1, LANES)
    perm = _rope_column_order()
    pos2 = positions.reshape(T, 1)

    cur = x.reshape(T, D)
    cur_g, cur_b = row2(ln_in_g), row2(ln_in_b)
    for l in range(depth):
        w = w_in[l]
        w_q = w[:, :DA_Q][:, perm]
        w_k = w[:, DA_Q:DA_Q + DA_K][:, perm]
        w_main = jnp.concatenate([w_q, w_k, w[:, DA_Q + DA_K:D_MAIN]], axis=1).astype(BF16)
        w_glow = jnp.pad(w[:, D_MAIN:], ((0, 0), (0, LANES - GLA_GATE_RANK))).astype(BF16)
        w_gate2 = jnp.pad(gla_w_gate2[l], ((0, LANES - GLA_GATE_RANK), (0, 0))).astype(BF16)

        q, k, v, gq, gk, gv, go, la = _in_proj(cur, pos2, cur_g, cur_b, inv_freq, w_main, w_glow,
                                               w_gate2, row2(gla_b_gate2[l]), tm=512)
        lam_init = 0.8 - 0.6 * math.exp(-0.3 * l)
        sh = lambda a: a.reshape(B, S, a.shape[-1])
        da = _diff_attn(sh(q), sh(k), sh(v), row2(lam_q1[l]), row2(lam_k1[l]), row2(lam_q2[l]),
                        row2(lam_k2[l]), row2(da_subln_g[l]), lam_init)
        gl = _gla(sh(gq), sh(gk), sh(la), sh(gv), sh(go), row2(gla_norm_g[l]))

        w_r = jnp.zeros((D, LANES), F32)
        w_r = w_r.at[:, :N_GROUPS].set(router_w_group[l])
        w_r = w_r.at[:, EXPERT_LANE0:EXPERT_LANE0 + N_EXPERTS].set(router_w_expert[l])
        b_r = jnp.zeros((1, LANES), F32)
        b_r = b_r.at[0, :N_GROUPS].set(router_b_group[l])
        b_r = b_r.at[0, EXPERT_LANE0:EXPERT_LANE0 + N_EXPERTS].set(router_b_expert[l])
        wr_hi = w_r.astype(BF16)
        wr_lo = (w_r - wr_hi.astype(F32)).astype(BF16)

        h, route, cnt = _mix_out(cur, da.reshape(T, DA_V), gl.reshape(T, GLA_V), cur_g, cur_b,
                                 w_o[l].astype(BF16), row2(ln1_g[l]), row2(ln1_b[l]), wr_hi, wr_lo, b_r,
                                 alpha, tm=512)

        counts = cnt[0, :N_EXPERTS].astype(jnp.int32)
        seg_start = jnp.cumsum(counts) - counts
        eid = route[:, R_E0:R_E1 + 1].astype(jnp.int32)
        rank = route[:, R_RANK0:R_RANK1 + 1].astype(jnp.int32)
        onehot = eid[..., None] == jnp.arange(N_EXPERTS, dtype=jnp.int32)
        dest = jnp.sum(jnp.where(onehot, seg_start, 0), axis=-1) + rank
        dest_flat = dest.reshape(T * TOP_K)

        xs = _dispatch(dest_flat, h, tm=256)
        meta = _visit_metadata(counts, T * TOP_K, bm=256)
        ys = _experts(meta, xs, w_gate[l], w_up[l], w_down[l], bm=256)
        cur = _combine(dest_flat, h, route, row2(ln2_g[l]), row2(ln2_b[l]), ys, alpha, tm=256)
    return cur.reshape(B, S, D)
```

```python
import functools
import math

import jax
import jax.numpy as jnp
import numpy as np
from jax import lax
from jax.experimental import pallas as pl
from jax.experimental.pallas import tpu as pltpu

F32 = jnp.float32
BF16 = jnp.bfloat16

D_MODEL = 1024
CHUNK = 64
ROPE_THETA = 10000.0
LN_EPS = 1e-5
LOG2_E = math.log2(math.e)

DA_HEADS = 4
DA_V_DIM = D_MODEL // (2 * DA_HEADS)
DA_HEAD_DIM = DA_V_DIM // 2
GLA_HEADS = 4
GLA_V_DIM = D_MODEL // (2 * GLA_HEADS)
GLA_KEY_DIM = GLA_V_DIM // 2
GLA_GATE_RANK = 16
GLA_GATE_NORMALIZER = 16.0

DA_Q = DA_HEADS * 2 * DA_HEAD_DIM
DA_K = DA_Q
DA_V = DA_HEADS * DA_V_DIM
GLA_Q = GLA_HEADS * GLA_KEY_DIM
GLA_K = GLA_Q
GLA_V = GLA_HEADS * GLA_V_DIM
GLA_OG = GLA_V
D_MAIN = DA_Q + DA_K + DA_V + GLA_Q + GLA_K + GLA_V + GLA_OG

N_GROUPS = 4
EXPERTS_PER_GROUP = 8
N_EXPERTS = N_GROUPS * EXPERTS_PER_GROUP
TOP_K = 2
D_EXPERT = D_MODEL // 2

LANES = 128
ROW_SUB = D_MODEL // (2 * LANES)
ROW_BLOCK = 256
EXPERT_CHUNK = 128
GLA_BLOCK = 256
ATTN_BLOCK = 256
DMA_UNROLL = 8
VMEM_LIMIT = 48 * 1024 * 1024

R_E0, R_E1, R_G0, R_G1, R_RANK0, R_RANK1 = 0, 1, 2, 3, 4, 5
EXPERT_LANE0 = 32


def _layer_norm(x, g, b):
    mu = jnp.mean(x, axis=-1, keepdims=True)
    xc = x - mu
    var = jnp.mean(xc * xc, axis=-1, keepdims=True)
    return xc * lax.rsqrt(var + LN_EPS) * g + b


def _dot(a, b):
    return jnp.dot(a, b, preferred_element_type=F32)


def _dot_nt(a, b):
    return lax.dot_general(a, b, (((1,), (1,)), ((), ())), preferred_element_type=F32)


def _dot_tn(a, b):
    return lax.dot_general(a, b, (((0,), (0,)), ((), ())), preferred_element_type=F32)


def _in_proj_kernel(x_ref, pos_ref, g_ref, b_ref, invf_ref, w_ref, wgl_ref, wg2_ref, bg2_ref,
                    q_ref, k_ref, v_ref, gq_ref, gk_ref, gv_ref, go_ref, la_ref):
    tm = x_ref.shape[0]
    half = DA_HEAD_DIM // 2
    lane = lax.broadcasted_iota(jnp.int32, (ROW_BLOCK, LANES), 1)
    first = (lane & half) == 0

    for r0 in range(0, tm, ROW_BLOCK):
        rows = slice(r0, r0 + ROW_BLOCK)
        xn = _layer_norm(x_ref[rows, :], g_ref[...], b_ref[...])
        xb = xn.astype(BF16)
        proj = _dot(xb, w_ref[...])

        ang = pos_ref[rows, :].astype(F32) * invf_ref[...]
        c = jnp.cos(ang)
        s = jnp.sin(ang)
        s_lo = jnp.where(first, -s, 0.0)
        s_hi = jnp.where(first, 0.0, s)

        def rope(t):
            out = []
            for j in range(t.shape[1] // LANES):
                tj = t[:, j * LANES:(j + 1) * LANES]
                up = pltpu.roll(tj, LANES - half, 1)
                dn = pltpu.roll(tj, half, 1)
                out.append(tj * c + up * s_lo + dn * s_hi)
            return jnp.concatenate(out, axis=1)

        o = 0
        q = rope(proj[:, o:o + DA_Q]) * (DA_HEAD_DIM ** -0.5 * LOG2_E)
        o += DA_Q
        k = rope(proj[:, o:o + DA_K])
        o += DA_K
        q_ref[rows, :] = q.astype(BF16)
        k_ref[rows, :] = k.astype(BF16)
        v_ref[rows, :] = proj[:, o:o + DA_V].astype(BF16)
        o += DA_V
        gq_ref[rows, :] = proj[:, o:o + GLA_Q].astype(BF16)
        o += GLA_Q
        gk_ref[rows, :] = proj[:, o:o + GLA_K].astype(BF16)
        o += GLA_K
        gv_ref[rows, :] = proj[:, o:o + GLA_V].astype(BF16)
        o += GLA_V
        go_ref[rows, :] = proj[:, o:o + GLA_OG].astype(BF16)

        g_low = _dot(xb, wgl_ref[...])
        z = _dot(g_low.astype(BF16), wg2_ref[...]) + bg2_ref[...]
        log_sig = jnp.minimum(z, 0.0) - jnp.log1p(jnp.exp(-jnp.abs(z)))
        la_ref[rows, :] = log_sig / GLA_GATE_NORMALIZER


def _in_proj(x2, pos2, ln_g, ln_b, inv_freq, w_main, w_glow, w_gate2, b_gate2, tm):
    T = x2.shape[0]
    row = lambda n: pl.BlockSpec((tm, n), lambda i: (i, 0))
    full = lambda a: pl.BlockSpec(a.shape, lambda i: (0,) * a.ndim)
    out_shape = [jax.ShapeDtypeStruct((T, n), dt) for n, dt in (
        (DA_Q, BF16), (DA_K, BF16), (DA_V, BF16), (GLA_Q, BF16), (GLA_K, BF16),
        (GLA_V, BF16), (GLA_OG, BF16), (GLA_K, F32))]
    return pl.pallas_call(
        _in_proj_kernel,
        grid=(T // tm,),
        in_specs=[row(D_MODEL), row(1), full(ln_g), full(ln_b), full(inv_freq), full(w_main),
                  full(w_glow), full(w_gate2), full(b_gate2)],
        out_specs=[row(s.shape[1]) for s in out_shape],
        out_shape=out_shape,
        compiler_params=pltpu.CompilerParams(dimension_semantics=("arbitrary",),
                                             vmem_limit_bytes=VMEM_LIMIT),
        name="in_proj",
    )(x2, pos2, ln_g, ln_b, inv_freq, w_main, w_glow, w_gate2, b_gate2)


def _diff_attn_kernel(lam_init, lq1_ref, lk1_ref, lq2_ref, lk2_ref, g_ref, q_ref, k_ref, v_ref, o_ref,
                      s_scr, p_scr):
    S = q_ref.shape[1]
    tq = ATTN_BLOCK
    lam = (jnp.exp(jnp.sum(lq1_ref[...] * lk1_ref[...], axis=-1, keepdims=True))
           - jnp.exp(jnp.sum(lq2_ref[...] * lk2_ref[...], axis=-1, keepdims=True)) + lam_init)
    lane = lax.broadcasted_iota(jnp.int32, (tq, LANES), 1)
    rq = lax.broadcasted_iota(jnp.int32, (2 * tq, tq), 0) % tq // CHUNK
    ck = lax.broadcasted_iota(jnp.int32, (2 * tq, tq), 1) // CHUNK
    diag_mask = ck <= rq

    for qi in range(S // tq):
        q = q_ref[0, qi * tq:(qi + 1) * tq, :]
        zero = jnp.zeros_like(q)
        qq = jnp.concatenate([jnp.where(lane < DA_HEAD_DIM, q, zero),
                              jnp.where(lane >= DA_HEAD_DIM, q, zero)], axis=0)
        m = None
        for j in range(qi + 1):
            s = _dot_nt(qq, k_ref[0, j * tq:(j + 1) * tq, :])
            if j == qi:
                s = jnp.where(diag_mask, s, -jnp.inf)
            s_scr[:, j * tq:(j + 1) * tq] = s
            mj = jnp.max(s, axis=-1, keepdims=True)
            m = mj if m is None else jnp.maximum(m, mj)
        l = None
        for j in range(qi + 1):
            p = jnp.exp2(s_scr[:, j * tq:(j + 1) * tq] - m)
            p_scr[:, j * tq:(j + 1) * tq] = p.astype(BF16)
            lj = jnp.sum(p, axis=-1, keepdims=True)
            l = lj if l is None else l + lj
        nk = (qi + 1) * tq
        a = _dot(p_scr[:, 0:nk], v_ref[0, 0:nk, :]) / l
        o = a[0:tq] - lam * a[tq:2 * tq]
        o = o * lax.rsqrt(jnp.mean(o * o, axis=-1, keepdims=True) + LN_EPS) * g_ref[...]
        o_ref[0, qi * tq:(qi + 1) * tq, :] = (o * (1.0 - lam_init)).astype(o_ref.dtype)


def _diff_attn(q, k, v, lam_q1, lam_k1, lam_q2, lam_k2, subln_g, lam_init):
    B, S, _ = q.shape
    vec = pl.BlockSpec((1, DA_HEAD_DIM), lambda b, h: (0, 0))
    seq = pl.BlockSpec((1, S, LANES), lambda b, h: (b, 0, h))
    return pl.pallas_call(
        functools.partial(_diff_attn_kernel, lam_init),
        grid=(B, DA_HEADS),
        in_specs=[vec, vec, vec, vec, pl.BlockSpec((1, DA_V_DIM), lambda b, h: (0, 0)), seq, seq, seq],
        out_specs=seq,
        out_shape=jax.ShapeDtypeStruct((B, S, DA_V), BF16),
        scratch_shapes=[pltpu.VMEM((2 * ATTN_BLOCK, S), F32), pltpu.VMEM((2 * ATTN_BLOCK, S), BF16)],
        compiler_params=pltpu.CompilerParams(dimension_semantics=("arbitrary",) * 2,
                                             vmem_limit_bytes=VMEM_LIMIT),
        name="diff_attn",
    )(lam_q1, lam_k1, lam_q2, lam_k2, subln_g, q, k, v)


def _gla_kernel(q_ref, k_ref, la_ref, v_ref, go_ref, ng_ref, o_ref, qt_s, oi_s, ds_s, dec_s):
    S = q_ref.shape[1]
    C = CHUNK
    BLK = GLA_BLOCK
    per_blk = BLK // C
    r = lax.broadcasted_iota(jnp.int32, (BLK, BLK), 0)
    c = lax.broadcasted_iota(jnp.int32, (BLK, BLK), 1)
    chunk_causal = (r // C == c // C) & (c <= r)
    tri = jnp.where(chunk_causal, 1.0, 0.0).astype(BF16)
    lane = lax.broadcasted_iota(jnp.int32, (BLK, LANES), 1)
    head_lanes = (lane < GLA_KEY_DIM, lane >= GLA_KEY_DIM)
    st_row = lax.broadcasted_iota(jnp.int32, (2 * GLA_V_DIM, LANES), 0)
    st_lane = lax.broadcasted_iota(jnp.int32, (2 * GLA_V_DIM, LANES), 1)
    own_keys = (st_row < GLA_V_DIM) == (st_lane < GLA_KEY_DIM)

    for b in range(S // BLK):
        r0 = b * BLK
        g = la_ref[0, r0:r0 + BLK, :]
        g1 = g.astype(BF16)
        e1 = g - g1.astype(F32)
        g2 = e1.astype(BF16)
        g3 = (e1 - g2.astype(F32)).astype(BF16)
        bcum = _dot(tri, g1) + _dot(tri, g2) + _dot(tri, g3)
        b_last = jnp.concatenate(
            [jnp.broadcast_to(bcum[i * C + C - 1:i * C + C, :], (C, LANES)) for i in range(per_blk)], axis=0)
        qf = q_ref[0, r0:r0 + BLK, :].astype(F32) * (GLA_KEY_DIM ** -0.5)
        kf = k_ref[0, r0:r0 + BLK, :].astype(F32)
        q_t = (qf * jnp.exp(bcum)).astype(BF16)
        k_t = (kf * jnp.exp(-bcum)).astype(BF16)
        k_end = (kf * jnp.exp(b_last - bcum)).astype(BF16)
        decay = jnp.exp(b_last)
        qt_s[r0:r0 + BLK, :] = q_t
        zero = jnp.zeros_like(q_t)
        for hh in range(2):
            att = jnp.where(chunk_causal, _dot_nt(jnp.where(head_lanes[hh], q_t, zero), k_t), 0.0).astype(BF16)
            oi_s[r0:r0 + BLK, hh * GLA_V_DIM:(hh + 1) * GLA_V_DIM] = _dot(
                att, v_ref[0, r0:r0 + BLK, hh * GLA_V_DIM:(hh + 1) * GLA_V_DIM])
        for i in range(per_blk):
            n = b * per_blk + i
            rows = slice(r0 + i * C, r0 + (i + 1) * C)
            inc = _dot_tn(v_ref[0, rows, :], k_end[i * C:(i + 1) * C, :])
            ds_s[n] = jnp.where(own_keys, inc, 0.0)
            dec_s[n:n + 1, :] = decay[i * C:i * C + 1, :]

    state = jnp.zeros((2 * GLA_V_DIM, LANES), F32)
    for n in range(S // C):
        rows = slice(n * C, (n + 1) * C)
        o = oi_s[rows, :] + _dot_nt(qt_s[rows, :], state.astype(BF16))
        state = state * dec_s[n:n + 1, :] + ds_s[n]
        for hh in range(2):
            cols = slice(hh * GLA_V_DIM, (hh + 1) * GLA_V_DIM)
            oh = o[:, cols]
            oh = oh * lax.rsqrt(jnp.mean(oh * oh, axis=-1, keepdims=True) + LN_EPS) * ng_ref[...]
            gate = go_ref[0, rows, cols].astype(F32)
            o_ref[0, rows, cols] = (oh * (gate * jax.nn.sigmoid(gate))).astype(o_ref.dtype)


def _gla(gq, gk, la, gv, go, norm_g):
    B, S, _ = gq.shape
    pairs = GLA_HEADS // 2
    narrow = pl.BlockSpec((1, S, LANES), lambda b, p: (b, 0, p))
    wide = pl.BlockSpec((1, S, 2 * GLA_V_DIM), lambda b, p: (b, 0, p))
    n_chunks = S // CHUNK
    return pl.pallas_call(
        _gla_kernel,
        grid=(B, pairs),
        in_specs=[narrow, narrow, narrow, wide, wide,
                  pl.BlockSpec((1, GLA_V_DIM), lambda b, p: (0, 0))],
        out_specs=wide,
        out_shape=jax.ShapeDtypeStruct((B, S, GLA_V), BF16),
        scratch_shapes=[pltpu.VMEM((S, LANES), BF16),
                        pltpu.VMEM((S, 2 * GLA_V_DIM), F32),
                        pltpu.VMEM((n_chunks, 2 * GLA_V_DIM, LANES), F32),
                        pltpu.VMEM((n_chunks, LANES), F32)],
        compiler_params=pltpu.CompilerParams(dimension_semantics=("arbitrary",) * 2,
                                             vmem_limit_bytes=VMEM_LIMIT),
        name="gla",
    )(gq, gk, la, gv, go, norm_g)


def _split3(a):
    hi = a.astype(BF16)
    lo = (a - hi.astype(F32)).astype(BF16)
    return hi, lo


def _mix_out_kernel(alpha, x_ref, da_ref, gl_ref, lng_ref, lnb_ref, wo_ref, g1_ref, b1_ref,
                    wr_hi_ref, wr_lo_ref, br_ref, lower_ref, h_ref, route_ref, cnt_ref):
    tm = x_ref.shape[0]
    i = pl.program_id(0)

    @pl.when(i == 0)
    def _():
        cnt_ref[...] = jnp.zeros_like(cnt_ref)

    logit_blocks = []
    for r0 in range(0, tm, ROW_BLOCK):
        rows = slice(r0, r0 + ROW_BLOCK)
        xn = _layer_norm(x_ref[rows, :], lng_ref[...], lnb_ref[...])
        mix = _dot(da_ref[rows, :], wo_ref[0:DA_V, :]) + _dot(gl_ref[rows, :], wo_ref[DA_V:, :])
        h = _layer_norm(alpha * xn + mix, g1_ref[...], b1_ref[...])
        h_ref[rows, :] = h
        h_hi, h_lo = _split3(h)
        logit_blocks.append(_dot(h_hi, wr_hi_ref[...]) + _dot(h_hi, wr_lo_ref[...]) + _dot(h_lo, wr_hi_ref[...])
                            + br_ref[...])
    logits = jnp.concatenate(logit_blocks, axis=0)
    lane = lax.broadcasted_iota(jnp.int32, (tm, LANES), 1)
    neg = -jnp.inf
    big = jnp.int32(LANES)

    def first_argmax(vals, valid):
        v = jnp.where(valid, vals, neg)
        mx = jnp.max(v, axis=-1, keepdims=True)
        idx = jnp.min(jnp.where(valid & (v == mx), lane, big), axis=-1, keepdims=True)
        return mx, idx

    is_group = lane < N_GROUPS
    g_max, g_top = first_argmax(logits, is_group)
    p_g = 1.0 / jnp.sum(jnp.where(is_group, jnp.exp(logits - g_max), 0.0), axis=-1, keepdims=True)

    e_lo = EXPERT_LANE0 + g_top * EXPERTS_PER_GROUP
    in_group = (lane >= e_lo) & (lane < e_lo + EXPERTS_PER_GROUP)
    v0, i0 = first_argmax(logits, in_group)
    v1, i1 = first_argmax(logits, in_group & (lane != i0))
    w1 = jnp.exp(v1 - v0)
    gate0 = p_g / (1.0 + w1)
    gate1 = p_g * w1 / (1.0 + w1)
    e0 = i0 - EXPERT_LANE0
    e1 = i1 - EXPERT_LANE0

    oh0 = jnp.where(lane == e0, 1.0, 0.0)
    oh1 = jnp.where(lane == e1, 1.0, 0.0)
    oh = oh0 + oh1
    before = _dot(lower_ref[...], oh.astype(BF16)) + cnt_ref[0:1, :]
    rank0 = jnp.sum(oh0 * before, axis=-1, keepdims=True)
    rank1 = jnp.sum(oh1 * before, axis=-1, keepdims=True)
    cnt_ref[...] = cnt_ref[...] + jnp.sum(oh, axis=0, keepdims=True)

    rec = jnp.zeros((tm, LANES), F32)
    for ln, val in ((R_E0, e0.astype(F32)), (R_E1, e1.astype(F32)), (R_G0, gate0), (R_G1, gate1),
                    (R_RANK0, rank0), (R_RANK1, rank1)):
        rec = jnp.where(lane == ln, val, rec)
    route_ref[...] = rec


def _mix_out(x2, da2, gl2, ln_g, ln_b, w_o, ln1_g, ln1_b, wr_hi, wr_lo, b_r, alpha, tm):
    T = x2.shape[0]
    row = lambda n: pl.BlockSpec((tm, n), lambda i: (i, 0))
    full = lambda a: pl.BlockSpec(a.shape, lambda i: (0,) * a.ndim)
    lower = jnp.tril(jnp.ones((tm, tm), BF16), -1)
    return pl.pallas_call(
        functools.partial(_mix_out_kernel, alpha),
        grid=(T // tm,),
        in_specs=[row(D_MODEL), row(DA_V), row(GLA_V), full(ln_g), full(ln_b), full(w_o),
                  full(ln1_g), full(ln1_b), full(wr_hi), full(wr_lo), full(b_r), full(lower)],
        out_specs=[row(D_MODEL), row(LANES), pl.BlockSpec((8, LANES), lambda i: (0, 0))],
        out_shape=[jax.ShapeDtypeStruct((T, D_MODEL), F32), jax.ShapeDtypeStruct((T, LANES), F32),
                   jax.ShapeDtypeStruct((8, LANES), F32)],
        compiler_params=pltpu.CompilerParams(dimension_semantics=("arbitrary",),
                                             vmem_limit_bytes=VMEM_LIMIT),
        name="mix_out",
    )(x2, da2, gl2, ln_g, ln_b, w_o, ln1_g, ln1_b, wr_hi, wr_lo, b_r, lower)


HIGH_HALF = 0xFFFF0000


def _pack_pairs(val):
    bits = lambda a: lax.bitcast_convert_type(a.astype(BF16).astype(F32), jnp.uint32)
    half = val.shape[1] // 2
    return (bits(val[:, :half]) >> 16) | (bits(val[:, half:]) & jnp.uint32(HIGH_HALF))


def _unpack_pairs(words):
    lo = lax.bitcast_convert_type(words << 16, F32)
    hi = lax.bitcast_convert_type(words & jnp.uint32(HIGH_HALF), F32)
    return jnp.concatenate([lo, hi], axis=1)


def _words_to_tiles(dst_ref, words):
    n = words.shape[0]
    for s in range(ROW_SUB):
        dst_ref[pl.ds(s, n, stride=ROW_SUB), :] = words[:, s * LANES:(s + 1) * LANES]


def _tiles_to_words(src_ref, r0, n):
    return jnp.concatenate([src_ref[pl.ds(r0 * ROW_SUB + s, n, stride=ROW_SUB), :] for s in range(ROW_SUB)],
                           axis=1)


def _rows_to_tiles(dst_ref, val):
    _words_to_tiles(dst_ref, _pack_pairs(val))


def _tiles_to_rows(src_ref, r0, n):
    return _unpack_pairs(_tiles_to_words(src_ref, r0, n))


def _row_tile(ref, r):
    return ref.at[pl.ds(pl.multiple_of(r * ROW_SUB, ROW_SUB), ROW_SUB), :]


def _dispatch_kernel(dest_ref, h_ref, xs_ref, stage, sems):
    tm = h_ref.shape[0]
    step = pl.program_id(0)
    half = step & 1
    base = step * (tm * TOP_K)
    n_iter = tm * TOP_K // DMA_UNROLL
    _rows_to_tiles(stage.at[half], h_ref[...])

    def row_copy(hf, t, slot):
        return pltpu.make_async_copy(_row_tile(stage.at[hf], t), _row_tile(xs_ref, slot), sems.at[hf])

    def start(i, c):
        for u in range(DMA_UNROLL):
            t = i * (DMA_UNROLL // TOP_K) + u // TOP_K
            row_copy(half, t, dest_ref[base + i * DMA_UNROLL + u]).start(priority=u % 2)
        return c

    lax.fori_loop(0, n_iter, start, 0)

    def drain(hf):
        def wait(i, c):
            for u in range(DMA_UNROLL):
                row_copy(hf, 0, 0).wait()
            return c
        lax.fori_loop(0, n_iter, wait, 0)

    @pl.when(step > 0)
    def _():
        drain(1 - half)

    @pl.when(step == pl.num_programs(0) - 1)
    def _():
        drain(half)
        n_pad = EXPERT_CHUNK * ROW_SUB
        stage[0, 0:n_pad, :] = jnp.zeros((n_pad, LANES), stage.dtype)
        pad = pltpu.make_async_copy(stage.at[0, 0:n_pad, :],
                                    xs_ref.at[pl.ds(pl.num_programs(0) * tm * TOP_K * ROW_SUB, n_pad), :], sems.at[0])
        pad.start()
        pad.wait()


def _dispatch(dest_flat, h, tm):
    T = h.shape[0]
    return pl.pallas_call(
        _dispatch_kernel,
        grid_spec=pltpu.PrefetchScalarGridSpec(
            num_scalar_prefetch=1,
            grid=(T // tm,),
            in_specs=[pl.BlockSpec((tm, D_MODEL), lambda i, d: (i, 0))],
            out_specs=pl.BlockSpec(memory_space=pl.ANY),
            scratch_shapes=[pltpu.VMEM((2, tm * ROW_SUB, LANES), jnp.uint32), pltpu.SemaphoreType.DMA((2,))]),
        out_shape=jax.ShapeDtypeStruct(((T * TOP_K + EXPERT_CHUNK) * ROW_SUB, LANES), jnp.uint32),
        compiler_params=pltpu.CompilerParams(dimension_semantics=("arbitrary",),
                                             vmem_limit_bytes=VMEM_LIMIT),
        name="dispatch",
    )(dest_flat, h)


def _experts_kernel(start_ref, count_ref, xs_ref, wg_ref, wu_ref, wd_ref, ys_ref,
                    wgu_b, wd_b, xbuf, ybuf, xsem, ysem, done_ref, *, n_rows):
    e = pl.program_id(0)
    ch = EXPERT_CHUNK
    start = start_ref[e]
    count = count_ref[e]

    def x_copy(row0, half):
        return pltpu.make_async_copy(xs_ref.at[pl.ds(pl.multiple_of(row0 * ROW_SUB, ROW_SUB), ch * ROW_SUB), :],
                                     xbuf.at[half], xsem.at[half])

    def y_copy(row0, half):
        return pltpu.make_async_copy(ybuf.at[half],
                                     ys_ref.at[pl.ds(pl.multiple_of(row0 * ROW_SUB, ROW_SUB), ch * ROW_SUB), :],
                                     ysem)

    @pl.when(e == 0)
    def _():
        done_ref[0] = 0
        ybuf[0] = jnp.zeros(ybuf.shape[1:], ybuf.dtype)
        y_copy(n_rows, 0).start()
        y_copy(n_rows, 0).wait()

    @pl.when(count > 0)
    def _():
        wgu_b[:, 0:D_EXPERT] = wg_ref[0].astype(BF16)
        wgu_b[:, D_EXPERT:] = wu_ref[0].astype(BF16)
        wd_b[...] = wd_ref[0].astype(BF16)

        @pl.when(done_ref[0] == 0)
        def _():
            x_copy(start, 0).start()

        n_chunks = (count + (ch - 1)) // ch

        def chunk(i, c):
            g = done_ref[0]
            half = g & 1
            row0 = start + i * ch
            x_copy(row0, half).wait()
            nxt = jnp.where(i + 1 < n_chunks, row0 + ch, start + count)

            @pl.when(nxt < n_rows)
            def _():
                x_copy(nxt, 1 - half).start()

            xb = _tiles_to_rows(xbuf.at[half], 0, ch).astype(BF16)
            gu = _dot(xb, wgu_b[...])
            gate = gu[:, 0:D_EXPERT]
            mid = (gate * jax.nn.sigmoid(gate) * gu[:, D_EXPERT:]).astype(BF16)
            _rows_to_tiles(ybuf.at[half], _dot(mid, wd_b[...]))

            @pl.when(g > 0)
            def _():
                y_copy(0, 1 - half).wait()

            y_copy(row0, half).start()
            done_ref[0] = g + 1
            return c

        lax.fori_loop(0, n_chunks, chunk, 0)

    @pl.when(e == pl.num_programs(0) - 1)
    def _():
        y_copy(0, 0).wait()


def _experts(seg_start, counts, xs, w_gate, w_up, w_down):
    n_rows = xs.shape[0] // ROW_SUB - EXPERT_CHUNK
    per_expert = lambda shape: pl.BlockSpec((1,) + shape, lambda e, s, c: (e, 0, 0))
    slab = (EXPERT_CHUNK * ROW_SUB, LANES)
    return pl.pallas_call(
        functools.partial(_experts_kernel, n_rows=n_rows),
        grid_spec=pltpu.PrefetchScalarGridSpec(
            num_scalar_prefetch=2,
            grid=(N_EXPERTS,),
            in_specs=[pl.BlockSpec(memory_space=pl.ANY),
                      per_expert((D_MODEL, D_EXPERT)), per_expert((D_MODEL, D_EXPERT)),
                      per_expert((D_EXPERT, D_MODEL))],
            out_specs=pl.BlockSpec(memory_space=pl.ANY),
            scratch_shapes=[pltpu.VMEM((D_MODEL, 2 * D_EXPERT), BF16), pltpu.VMEM((D_EXPERT, D_MODEL), BF16),
                            pltpu.VMEM((2,) + slab, jnp.uint32), pltpu.VMEM((2,) + slab, jnp.uint32),
                            pltpu.SemaphoreType.DMA((2,)), pltpu.SemaphoreType.DMA(()),
                            pltpu.SMEM((1,), jnp.int32)]),
        out_shape=jax.ShapeDtypeStruct(xs.shape, jnp.uint32),
        compiler_params=pltpu.CompilerParams(dimension_semantics=("arbitrary",),
                                             vmem_limit_bytes=VMEM_LIMIT),
        name="experts",
    )(seg_start, counts, xs, w_gate, w_up, w_down)


def _combine_kernel(alpha, dest_ref, h_ref, route_ref, g_ref, b_ref, y_ref, o_ref, buf, sems):
    tm = h_ref.shape[0]
    step = pl.program_id(0)
    half = step & 1
    n_iter = tm * TOP_K // DMA_UNROLL

    def row_copy(hf, src, slot):
        return pltpu.make_async_copy(_row_tile(y_ref, src), _row_tile(buf.at[hf], slot), sems.at[hf])

    def gather(st, hf):
        base = st * (tm * TOP_K)

        def start(i, c):
            for u in range(DMA_UNROLL):
                slot = (u % TOP_K) * tm + i * (DMA_UNROLL // TOP_K) + u // TOP_K
                row_copy(hf, dest_ref[base + i * DMA_UNROLL + u], slot).start(priority=u % 2)
            return c

        lax.fori_loop(0, n_iter, start, 0)

    @pl.when(step == 0)
    def _():
        gather(0, 0)

    @pl.when(step + 1 < pl.num_programs(0))
    def _():
        gather(step + 1, 1 - half)

    def wait(i, c):
        for u in range(DMA_UNROLL):
            row_copy(half, 0, 0).wait()
        return c

    lax.fori_loop(0, n_iter, wait, 0)

    rec = route_ref[...]
    cur = buf.at[half]
    ffn = (rec[:, R_G0:R_G0 + 1] * _tiles_to_rows(cur, 0, tm)
           + rec[:, R_G1:R_G1 + 1] * _tiles_to_rows(cur, tm, tm))
    o_ref[...] = _layer_norm(alpha * h_ref[...] + ffn, g_ref[...], b_ref[...])


def _combine(dest_flat, h, route, ln2_g, ln2_b, y_sorted, alpha, tm):
    T = h.shape[0]
    return pl.pallas_call(
        functools.partial(_combine_kernel, alpha),
        grid_spec=pltpu.PrefetchScalarGridSpec(
            num_scalar_prefetch=1,
            grid=(T // tm,),
            in_specs=[pl.BlockSpec((tm, D_MODEL), lambda i, d: (i, 0)),
                      pl.BlockSpec((tm, LANES), lambda i, d: (i, 0)),
                      pl.BlockSpec((1, D_MODEL), lambda i, d: (0, 0)),
                      pl.BlockSpec((1, D_MODEL), lambda i, d: (0, 0)),
                      pl.BlockSpec(memory_space=pl.ANY)],
            out_specs=pl.BlockSpec((tm, D_MODEL), lambda i, d: (i, 0)),
            scratch_shapes=[pltpu.VMEM((2, TOP_K * tm * ROW_SUB, LANES), jnp.uint32), pltpu.SemaphoreType.DMA((2,))]),
        out_shape=jax.ShapeDtypeStruct((T, D_MODEL), F32),
        compiler_params=pltpu.CompilerParams(dimension_semantics=("arbitrary",),
                                             vmem_limit_bytes=VMEM_LIMIT),
        name="combine",
    )(dest_flat, h, route, ln2_g, ln2_b, y_sorted)


def _rope_column_order():
    within = np.concatenate([np.arange(0, DA_HEAD_DIM, 2), np.arange(1, DA_HEAD_DIM, 2)])
    return np.concatenate([m * DA_HEAD_DIM + within for m in range(2 * DA_HEADS)])


def kernel(x, positions, ln_in_g, ln_in_b, w_in, lam_q1, lam_k1, lam_q2, lam_k2, da_subln_g, gla_w_gate2, gla_b_gate2, gla_norm_g, w_o, ln1_g, ln1_b, router_w_group, router_b_group, router_w_expert, router_b_expert, w_gate, w_up, w_down, ln2_g, ln2_b):
    B, S, D = x.shape
    T = B * S
    depth = w_in.shape[0]
    assert depth == 1, "only a single layer is supported"
    alpha = (2 * depth) ** 0.25
    row2 = lambda a: a.reshape(1, -1)

    inv_freq = ROPE_THETA ** (-jnp.arange(0, DA_HEAD_DIM, 2, dtype=F32) / DA_HEAD_DIM)
    inv_freq = jnp.tile(inv_freq, LANES // (DA_HEAD_DIM // 2)).reshape(1, LANES)
    perm = _rope_column_order()
    pos2 = positions.reshape(T, 1)

    cur = x.reshape(T, D)
    cur_g, cur_b = row2(ln_in_g), row2(ln_in_b)
    for l in range(depth):
        w = w_in[l]
        w_q = w[:, :DA_Q][:, perm]
        w_k = w[:, DA_Q:DA_Q + DA_K][:, perm]
        w_main = jnp.concatenate([w_q, w_k, w[:, DA_Q + DA_K:D_MAIN]], axis=1).astype(BF16)
        w_glow = jnp.pad(w[:, D_MAIN:], ((0, 0), (0, LANES - GLA_GATE_RANK))).astype(BF16)
        w_gate2 = jnp.pad(gla_w_gate2[l], ((0, LANES - GLA_GATE_RANK), (0, 0))).astype(BF16)

        q, k, v, gq, gk, gv, go, la = _in_proj(cur, pos2, cur_g, cur_b, inv_freq, w_main, w_glow,
                                               w_gate2, row2(gla_b_gate2[l]), tm=512)
        lam_init = 0.8 - 0.6 * math.exp(-0.3 * l)
        sh = lambda a: a.reshape(B, S, a.shape[-1])
        da = _diff_attn(sh(q), sh(k), sh(v), row2(lam_q1[l]), row2(lam_k1[l]), row2(lam_q2[l]),
                        row2(lam_k2[l]), row2(da_subln_g[l]), lam_init)
        gl = _gla(sh(gq), sh(gk), sh(la), sh(gv), sh(go), row2(gla_norm_g[l]))

        w_r = jnp.zeros((D, LANES), F32)
        w_r = w_r.at[:, :N_GROUPS].set(router_w_group[l])
        w_r = w_r.at[:, EXPERT_LANE0:EXPERT_LANE0 + N_EXPERTS].set(router_w_expert[l])
        b_r = jnp.zeros((1, LANES), F32)
        b_r = b_r.at[0, :N_GROUPS].set(router_b_group[l])
        b_r = b_r.at[0, EXPERT_LANE0:EXPERT_LANE0 + N_EXPERTS].set(router_b_expert[l])
        wr_hi = w_r.astype(BF16)
        wr_lo = (w_r - wr_hi.astype(F32)).astype(BF16)

        h, route, cnt = _mix_out(cur, da.reshape(T, DA_V), gl.reshape(T, GLA_V), cur_g, cur_b,
                                 w_o[l].astype(BF16), row2(ln1_g[l]), row2(ln1_b[l]), wr_hi, wr_lo, b_r,
                                 alpha, tm=512)

        counts = cnt[0, :N_EXPERTS].astype(jnp.int32)
        seg_start = jnp.cumsum(counts) - counts
        eid = route[:, R_E0:R_E1 + 1].astype(jnp.int32)
        rank = route[:, R_RANK0:R_RANK1 + 1].astype(jnp.int32)
        onehot = eid[..., None] == jnp.arange(N_EXPERTS, dtype=jnp.int32)
        dest = jnp.sum(jnp.where(onehot, seg_start, 0), axis=-1) + rank
        dest_flat = dest.reshape(T * TOP_K)

        xs = _dispatch(dest_flat, h, tm=256)
        ys = _experts(seg_start, counts, xs, w_gate[l], w_up[l], w_down[l])
        cur = _combine(dest_flat, h, route, row2(ln2_g[l]), row2(ln2_b[l]), ys, alpha, tm=256)
    return cur.reshape(B, S, D)
```

```python
import functools
import math

import jax
import jax.numpy as jnp
import numpy as np
from jax import lax
from jax.experimental import pallas as pl
from jax.experimental.pallas import tpu as pltpu

F32 = jnp.float32
BF16 = jnp.bfloat16

D_MODEL = 1024
CHUNK = 64
ROPE_THETA = 10000.0
LN_EPS = 1e-5
LOG2_E = math.log2(math.e)

DA_HEADS = 4
DA_V_DIM = D_MODEL // (2 * DA_HEADS)
DA_HEAD_DIM = DA_V_DIM // 2
GLA_HEADS = 4
GLA_V_DIM = D_MODEL // (2 * GLA_HEADS)
GLA_KEY_DIM = GLA_V_DIM // 2
GLA_GATE_RANK = 16
GLA_GATE_NORMALIZER = 16.0

DA_Q = DA_HEADS * 2 * DA_HEAD_DIM
DA_K = DA_Q
DA_V = DA_HEADS * DA_V_DIM
GLA_Q = GLA_HEADS * GLA_KEY_DIM
GLA_K = GLA_Q
GLA_V = GLA_HEADS * GLA_V_DIM
GLA_OG = GLA_V
D_MAIN = DA_Q + DA_K + DA_V + GLA_Q + GLA_K + GLA_V + GLA_OG

N_GROUPS = 4
EXPERTS_PER_GROUP = 8
N_EXPERTS = N_GROUPS * EXPERTS_PER_GROUP
TOP_K = 2
D_EXPERT = D_MODEL // 2

LANES = 128
ROW_SUB = D_MODEL // (2 * LANES)
ROW_BLOCK = 256
EXPERT_CHUNK = 128
ROW_STREAM_PRIORITY = 1
GLA_BLOCK = 256
ATTN_BLOCK = 256
DMA_UNROLL = 8
VMEM_LIMIT = 48 * 1024 * 1024

R_E0, R_E1, R_G0, R_G1, R_RANK0, R_RANK1 = 0, 1, 2, 3, 4, 5
EXPERT_LANE0 = 32


def _layer_norm(x, g, b):
    mu = jnp.mean(x, axis=-1, keepdims=True)
    xc = x - mu
    var = jnp.mean(xc * xc, axis=-1, keepdims=True)
    return xc * lax.rsqrt(var + LN_EPS) * g + b


def _dot(a, b):
    return jnp.dot(a, b, preferred_element_type=F32)


def _dot_nt(a, b):
    return lax.dot_general(a, b, (((1,), (1,)), ((), ())), preferred_element_type=F32)


def _dot_tn(a, b):
    return lax.dot_general(a, b, (((0,), (0,)), ((), ())), preferred_element_type=F32)


def _in_proj_kernel(x_ref, pos_ref, g_ref, b_ref, invf_ref, w_ref, wgl_ref, wg2_ref, bg2_ref,
                    q_ref, k_ref, v_ref, gq_ref, gk_ref, gv_ref, go_ref, la_ref):
    tm = x_ref.shape[0]
    half = DA_HEAD_DIM // 2
    lane = lax.broadcasted_iota(jnp.int32, (ROW_BLOCK, LANES), 1)
    first = (lane & half) == 0

    for r0 in range(0, tm, ROW_BLOCK):
        rows = slice(r0, r0 + ROW_BLOCK)
        xn = _layer_norm(x_ref[rows, :], g_ref[...], b_ref[...])
        xb = xn.astype(BF16)
        proj = _dot(xb, w_ref[...])

        ang = pos_ref[rows, :].astype(F32) * invf_ref[...]
        c = jnp.cos(ang)
        s = jnp.sin(ang)
        s_lo = jnp.where(first, -s, 0.0)
        s_hi = jnp.where(first, 0.0, s)

        def rope(t):
            out = []
            for j in range(t.shape[1] // LANES):
                tj = t[:, j * LANES:(j + 1) * LANES]
                up = pltpu.roll(tj, LANES - half, 1)
                dn = pltpu.roll(tj, half, 1)
                out.append(tj * c + up * s_lo + dn * s_hi)
            return jnp.concatenate(out, axis=1)

        o = 0
        q = rope(proj[:, o:o + DA_Q]) * (DA_HEAD_DIM ** -0.5 * LOG2_E)
        o += DA_Q
        k = rope(proj[:, o:o + DA_K])
        o += DA_K
        q_ref[rows, :] = q.astype(BF16)
        k_ref[rows, :] = k.astype(BF16)
        v_ref[rows, :] = proj[:, o:o + DA_V].astype(BF16)
        o += DA_V
        gq_ref[rows, :] = proj[:, o:o + GLA_Q].astype(BF16)
        o += GLA_Q
        gk_ref[rows, :] = proj[:, o:o + GLA_K].astype(BF16)
        o += GLA_K
        gv_ref[rows, :] = proj[:, o:o + GLA_V].astype(BF16)
        o += GLA_V
        go_ref[rows, :] = proj[:, o:o + GLA_OG].astype(BF16)

        g_low = _dot(xb, wgl_ref[...])
        z = _dot(g_low.astype(BF16), wg2_ref[...]) + bg2_ref[...]
        log_sig = jnp.minimum(z, 0.0) - jnp.log1p(jnp.exp(-jnp.abs(z)))
        la_ref[rows, :] = log_sig / GLA_GATE_NORMALIZER


def _in_proj(x2, pos2, ln_g, ln_b, inv_freq, w_main, w_glow, w_gate2, b_gate2, tm):
    T = x2.shape[0]
    row = lambda n: pl.BlockSpec((tm, n), lambda i: (i, 0))
    full = lambda a: pl.BlockSpec(a.shape, lambda i: (0,) * a.ndim)
    out_shape = [jax.ShapeDtypeStruct((T, n), dt) for n, dt in (
        (DA_Q, BF16), (DA_K, BF16), (DA_V, BF16), (GLA_Q, BF16), (GLA_K, BF16),
        (GLA_V, BF16), (GLA_OG, BF16), (GLA_K, F32))]
    return pl.pallas_call(
        _in_proj_kernel,
        grid=(T // tm,),
        in_specs=[row(D_MODEL), row(1), full(ln_g), full(ln_b), full(inv_freq), full(w_main),
                  full(w_glow), full(w_gate2), full(b_gate2)],
        out_specs=[row(s.shape[1]) for s in out_shape],
        out_shape=out_shape,
        compiler_params=pltpu.CompilerParams(dimension_semantics=("arbitrary",),
                                             vmem_limit_bytes=VMEM_LIMIT),
        name="in_proj",
    )(x2, pos2, ln_g, ln_b, inv_freq, w_main, w_glow, w_gate2, b_gate2)


def _diff_attn_kernel(lam_init, lq1_ref, lk1_ref, lq2_ref, lk2_ref, g_ref, q_ref, k_ref, v_ref, o_ref,
                      s_scr, p_scr):
    S = q_ref.shape[1]
    tq = ATTN_BLOCK
    lam = (jnp.exp(jnp.sum(lq1_ref[...] * lk1_ref[...], axis=-1, keepdims=True))
           - jnp.exp(jnp.sum(lq2_ref[...] * lk2_ref[...], axis=-1, keepdims=True)) + lam_init)
    lane = lax.broadcasted_iota(jnp.int32, (tq, LANES), 1)
    rq = lax.broadcasted_iota(jnp.int32, (2 * tq, tq), 0) % tq // CHUNK
    ck = lax.broadcasted_iota(jnp.int32, (2 * tq, tq), 1) // CHUNK
    diag_mask = ck <= rq

    for qi in range(S // tq):
        q = q_ref[0, qi * tq:(qi + 1) * tq, :]
        zero = jnp.zeros_like(q)
        qq = jnp.concatenate([jnp.where(lane < DA_HEAD_DIM, q, zero),
                              jnp.where(lane >= DA_HEAD_DIM, q, zero)], axis=0)
        m = None
        for j in range(qi + 1):
            s = _dot_nt(qq, k_ref[0, j * tq:(j + 1) * tq, :])
            if j == qi:
                s = jnp.where(diag_mask, s, -jnp.inf)
            s_scr[:, j * tq:(j + 1) * tq] = s
            mj = jnp.max(s, axis=-1, keepdims=True)
            m = mj if m is None else jnp.maximum(m, mj)
        l = None
        for j in range(qi + 1):
            p = jnp.exp2(s_scr[:, j * tq:(j + 1) * tq] - m)
            p_scr[:, j * tq:(j + 1) * tq] = p.astype(BF16)
            lj = jnp.sum(p, axis=-1, keepdims=True)
            l = lj if l is None else l + lj
        nk = (qi + 1) * tq
        a = _dot(p_scr[:, 0:nk], v_ref[0, 0:nk, :]) / l
        o = a[0:tq] - lam * a[tq:2 * tq]
        o = o * lax.rsqrt(jnp.mean(o * o, axis=-1, keepdims=True) + LN_EPS) * g_ref[...]
        o_ref[0, qi * tq:(qi + 1) * tq, :] = (o * (1.0 - lam_init)).astype(o_ref.dtype)


def _diff_attn(q, k, v, lam_q1, lam_k1, lam_q2, lam_k2, subln_g, lam_init):
    B, S, _ = q.shape
    vec = pl.BlockSpec((1, DA_HEAD_DIM), lambda b, h: (0, 0))
    seq = pl.BlockSpec((1, S, LANES), lambda b, h: (b, 0, h))
    return pl.pallas_call(
        functools.partial(_diff_attn_kernel, lam_init),
        grid=(B, DA_HEADS),
        in_specs=[vec, vec, vec, vec, pl.BlockSpec((1, DA_V_DIM), lambda b, h: (0, 0)), seq, seq, seq],
        out_specs=seq,
        out_shape=jax.ShapeDtypeStruct((B, S, DA_V), BF16),
        scratch_shapes=[pltpu.VMEM((2 * ATTN_BLOCK, S), F32), pltpu.VMEM((2 * ATTN_BLOCK, S), BF16)],
        compiler_params=pltpu.CompilerParams(dimension_semantics=("arbitrary",) * 2,
                                             vmem_limit_bytes=VMEM_LIMIT),
        name="diff_attn",
    )(lam_q1, lam_k1, lam_q2, lam_k2, subln_g, q, k, v)


def _gla_kernel(q_ref, k_ref, la_ref, v_ref, go_ref, ng_ref, o_ref, qt_s, oi_s, ds_s, dec_s):
    S = q_ref.shape[1]
    C = CHUNK
    BLK = GLA_BLOCK
    per_blk = BLK // C
    r = lax.broadcasted_iota(jnp.int32, (BLK, BLK), 0)
    c = lax.broadcasted_iota(jnp.int32, (BLK, BLK), 1)
    chunk_causal = (r // C == c // C) & (c <= r)
    tri = jnp.where(chunk_causal, 1.0, 0.0).astype(BF16)
    lane = lax.broadcasted_iota(jnp.int32, (BLK, LANES), 1)
    head_lanes = (lane < GLA_KEY_DIM, lane >= GLA_KEY_DIM)
    st_row = lax.broadcasted_iota(jnp.int32, (2 * GLA_V_DIM, LANES), 0)
    st_lane = lax.broadcasted_iota(jnp.int32, (2 * GLA_V_DIM, LANES), 1)
    own_keys = (st_row < GLA_V_DIM) == (st_lane < GLA_KEY_DIM)

    for b in range(S // BLK):
        r0 = b * BLK
        g = la_ref[0, r0:r0 + BLK, :]
        g1 = g.astype(BF16)
        e1 = g - g1.astype(F32)
        g2 = e1.astype(BF16)
        g3 = (e1 - g2.astype(F32)).astype(BF16)
        bcum = _dot(tri, g1) + _dot(tri, g2) + _dot(tri, g3)
        b_last = jnp.concatenate(
            [jnp.broadcast_to(bcum[i * C + C - 1:i * C + C, :], (C, LANES)) for i in range(per_blk)], axis=0)
        qf = q_ref[0, r0:r0 + BLK, :].astype(F32) * (GLA_KEY_DIM ** -0.5)
        kf = k_ref[0, r0:r0 + BLK, :].astype(F32)
        q_t = (qf * jnp.exp(bcum)).astype(BF16)
        k_t = (kf * jnp.exp(-bcum)).astype(BF16)
        k_end = (kf * jnp.exp(b_last - bcum)).astype(BF16)
        decay = jnp.exp(b_last)
        qt_s[r0:r0 + BLK, :] = q_t
        zero = jnp.zeros_like(q_t)
        for hh in range(2):
            att = jnp.where(chunk_causal, _dot_nt(jnp.where(head_lanes[hh], q_t, zero), k_t), 0.0).astype(BF16)
            oi_s[r0:r0 + BLK, hh * GLA_V_DIM:(hh + 1) * GLA_V_DIM] = _dot(
                att, v_ref[0, r0:r0 + BLK, hh * GLA_V_DIM:(hh + 1) * GLA_V_DIM])
        for i in range(per_blk):
            n = b * per_blk + i
            rows = slice(r0 + i * C, r0 + (i + 1) * C)
            inc = _dot_tn(v_ref[0, rows, :], k_end[i * C:(i + 1) * C, :])
            ds_s[n] = jnp.where(own_keys, inc, 0.0)
            dec_s[n:n + 1, :] = decay[i * C:i * C + 1, :]

    state = jnp.zeros((2 * GLA_V_DIM, LANES), F32)
    for n in range(S // C):
        rows = slice(n * C, (n + 1) * C)
        o = oi_s[rows, :] + _dot_nt(qt_s[rows, :], state.astype(BF16))
        state = state * dec_s[n:n + 1, :] + ds_s[n]
        for hh in range(2):
            cols = slice(hh * GLA_V_DIM, (hh + 1) * GLA_V_DIM)
            oh = o[:, cols]
            oh = oh * lax.rsqrt(jnp.mean(oh * oh, axis=-1, keepdims=True) + LN_EPS) * ng_ref[...]
            gate = go_ref[0, rows, cols].astype(F32)
            o_ref[0, rows, cols] = (oh * (gate * jax.nn.sigmoid(gate))).astype(o_ref.dtype)


def _gla(gq, gk, la, gv, go, norm_g):
    B, S, _ = gq.shape
    pairs = GLA_HEADS // 2
    narrow = pl.BlockSpec((1, S, LANES), lambda b, p: (b, 0, p))
    wide = pl.BlockSpec((1, S, 2 * GLA_V_DIM), lambda b, p: (b, 0, p))
    n_chunks = S // CHUNK
    return pl.pallas_call(
        _gla_kernel,
        grid=(B, pairs),
        in_specs=[narrow, narrow, narrow, wide, wide,
                  pl.BlockSpec((1, GLA_V_DIM), lambda b, p: (0, 0))],
        out_specs=wide,
        out_shape=jax.ShapeDtypeStruct((B, S, GLA_V), BF16),
        scratch_shapes=[pltpu.VMEM((S, LANES), BF16),
                        pltpu.VMEM((S, 2 * GLA_V_DIM), F32),
                        pltpu.VMEM((n_chunks, 2 * GLA_V_DIM, LANES), F32),
                        pltpu.VMEM((n_chunks, LANES), F32)],
        compiler_params=pltpu.CompilerParams(dimension_semantics=("arbitrary",) * 2,
                                             vmem_limit_bytes=VMEM_LIMIT),
        name="gla",
    )(gq, gk, la, gv, go, norm_g)


def _split3(a):
    hi = a.astype(BF16)
    lo = (a - hi.astype(F32)).astype(BF16)
    return hi, lo


def _mix_out_kernel(alpha, x_ref, da_ref, gl_ref, lng_ref, lnb_ref, wo_ref, g1_ref, b1_ref,
                    wr_hi_ref, wr_lo_ref, br_ref, lower_ref, h_ref, route_ref, cnt_ref):
    tm = x_ref.shape[0]
    i = pl.program_id(0)

    @pl.when(i == 0)
    def _():
        cnt_ref[...] = jnp.zeros_like(cnt_ref)

    logit_blocks = []
    for r0 in range(0, tm, ROW_BLOCK):
        rows = slice(r0, r0 + ROW_BLOCK)
        xn = _layer_norm(x_ref[rows, :], lng_ref[...], lnb_ref[...])
        mix = _dot(da_ref[rows, :], wo_ref[0:DA_V, :]) + _dot(gl_ref[rows, :], wo_ref[DA_V:, :])
        h = _layer_norm(alpha * xn + mix, g1_ref[...], b1_ref[...])
        h_ref[rows, :] = h
        h_hi, h_lo = _split3(h)
        logit_blocks.append(_dot(h_hi, wr_hi_ref[...]) + _dot(h_hi, wr_lo_ref[...]) + _dot(h_lo, wr_hi_ref[...])
                            + br_ref[...])
    logits = jnp.concatenate(logit_blocks, axis=0)
    lane = lax.broadcasted_iota(jnp.int32, (tm, LANES), 1)
    neg = -jnp.inf
    big = jnp.int32(LANES)

    def first_argmax(vals, valid):
        v = jnp.where(valid, vals, neg)
        mx = jnp.max(v, axis=-1, keepdims=True)
        idx = jnp.min(jnp.where(valid & (v == mx), lane, big), axis=-1, keepdims=True)
        return mx, idx

    is_group = lane < N_GROUPS
    g_max, g_top = first_argmax(logits, is_group)
    p_g = 1.0 / jnp.sum(jnp.where(is_group, jnp.exp(logits - g_max), 0.0), axis=-1, keepdims=True)

    e_lo = EXPERT_LANE0 + g_top * EXPERTS_PER_GROUP
    in_group = (lane >= e_lo) & (lane < e_lo + EXPERTS_PER_GROUP)
    v0, i0 = first_argmax(logits, in_group)
    v1, i1 = first_argmax(logits, in_group & (lane != i0))
    w1 = jnp.exp(v1 - v0)
    gate0 = p_g / (1.0 + w1)
    gate1 = p_g * w1 / (1.0 + w1)
    e0 = i0 - EXPERT_LANE0
    e1 = i1 - EXPERT_LANE0

    oh0 = jnp.where(lane == e0, 1.0, 0.0)
    oh1 = jnp.where(lane == e1, 1.0, 0.0)
    oh = oh0 + oh1
    before = _dot(lower_ref[...], oh.astype(BF16)) + cnt_ref[0:1, :]
    rank0 = jnp.sum(oh0 * before, axis=-1, keepdims=True)
    rank1 = jnp.sum(oh1 * before, axis=-1, keepdims=True)
    cnt_ref[...] = cnt_ref[...] + jnp.sum(oh, axis=0, keepdims=True)

    rec = jnp.zeros((tm, LANES), F32)
    for ln, val in ((R_E0, e0.astype(F32)), (R_E1, e1.astype(F32)), (R_G0, gate0), (R_G1, gate1),
                    (R_RANK0, rank0), (R_RANK1, rank1)):
        rec = jnp.where(lane == ln, val, rec)
    route_ref[...] = rec


def _mix_out(x2, da2, gl2, ln_g, ln_b, w_o, ln1_g, ln1_b, wr_hi, wr_lo, b_r, alpha, tm):
    T = x2.shape[0]
    row = lambda n: pl.BlockSpec((tm, n), lambda i: (i, 0))
    full = lambda a: pl.BlockSpec(a.shape, lambda i: (0,) * a.ndim)
    lower = jnp.tril(jnp.ones((tm, tm), BF16), -1)
    return pl.pallas_call(
        functools.partial(_mix_out_kernel, alpha),
        grid=(T // tm,),
        in_specs=[row(D_MODEL), row(DA_V), row(GLA_V), full(ln_g), full(ln_b), full(w_o),
                  full(ln1_g), full(ln1_b), full(wr_hi), full(wr_lo), full(b_r), full(lower)],
        out_specs=[row(D_MODEL), row(LANES), pl.BlockSpec((8, LANES), lambda i: (0, 0))],
        out_shape=[jax.ShapeDtypeStruct((T, D_MODEL), F32), jax.ShapeDtypeStruct((T, LANES), F32),
                   jax.ShapeDtypeStruct((8, LANES), F32)],
        compiler_params=pltpu.CompilerParams(dimension_semantics=("arbitrary",),
                                             vmem_limit_bytes=VMEM_LIMIT),
        name="mix_out",
    )(x2, da2, gl2, ln_g, ln_b, w_o, ln1_g, ln1_b, wr_hi, wr_lo, b_r, lower)


HIGH_HALF = 0xFFFF0000


def _pack_pairs(val):
    bits = lambda a: lax.bitcast_convert_type(a.astype(BF16).astype(F32), jnp.uint32)
    half = val.shape[1] // 2
    return (bits(val[:, :half]) >> 16) | (bits(val[:, half:]) & jnp.uint32(HIGH_HALF))


def _unpack_pairs(words):
    lo = lax.bitcast_convert_type(words << 16, F32)
    hi = lax.bitcast_convert_type(words & jnp.uint32(HIGH_HALF), F32)
    return jnp.concatenate([lo, hi], axis=1)


def _words_to_tiles(dst_ref, words):
    n = words.shape[0]
    for s in range(ROW_SUB):
        dst_ref[pl.ds(s, n, stride=ROW_SUB), :] = words[:, s * LANES:(s + 1) * LANES]


def _tiles_to_words(src_ref, r0, n):
    return jnp.concatenate([src_ref[pl.ds(r0 * ROW_SUB + s, n, stride=ROW_SUB), :] for s in range(ROW_SUB)],
                           axis=1)


def _rows_to_tiles(dst_ref, val):
    _words_to_tiles(dst_ref, _pack_pairs(val))


def _tiles_to_rows(src_ref, r0, n):
    return _unpack_pairs(_tiles_to_words(src_ref, r0, n))


def _row_tile(ref, r):
    return ref.at[pl.ds(pl.multiple_of(r * ROW_SUB, ROW_SUB), ROW_SUB), :]


def _dispatch_kernel(dest_ref, h_ref, xs_ref, stage, sems):
    tm = h_ref.shape[0]
    step = pl.program_id(0)
    half = step & 1
    base = step * (tm * TOP_K)
    n_iter = tm * TOP_K // DMA_UNROLL
    _rows_to_tiles(stage.at[half], h_ref[...])

    def row_copy(hf, t, slot):
        return pltpu.make_async_copy(_row_tile(stage.at[hf], t), _row_tile(xs_ref, slot), sems.at[hf])

    def start(i, c):
        for u in range(DMA_UNROLL):
            t = i * (DMA_UNROLL // TOP_K) + u // TOP_K
            row_copy(half, t, dest_ref[base + i * DMA_UNROLL + u]).start(priority=u % 2)
        return c

    lax.fori_loop(0, n_iter, start, 0)

    def drain(hf):
        def wait(i, c):
            for u in range(DMA_UNROLL):
                row_copy(hf, 0, 0).wait()
            return c
        lax.fori_loop(0, n_iter, wait, 0)

    @pl.when(step > 0)
    def _():
        drain(1 - half)

    @pl.when(step == pl.num_programs(0) - 1)
    def _():
        drain(half)
        n_pad = EXPERT_CHUNK * ROW_SUB
        stage[0, 0:n_pad, :] = jnp.zeros((n_pad, LANES), stage.dtype)
        pad = pltpu.make_async_copy(stage.at[0, 0:n_pad, :],
                                    xs_ref.at[pl.ds(pl.num_programs(0) * tm * TOP_K * ROW_SUB, n_pad), :], sems.at[0])
        pad.start()
        pad.wait()


def _dispatch(dest_flat, h, tm):
    T = h.shape[0]
    return pl.pallas_call(
        _dispatch_kernel,
        grid_spec=pltpu.PrefetchScalarGridSpec(
            num_scalar_prefetch=1,
            grid=(T // tm,),
            in_specs=[pl.BlockSpec((tm, D_MODEL), lambda i, d: (i, 0))],
            out_specs=pl.BlockSpec(memory_space=pl.ANY),
            scratch_shapes=[pltpu.VMEM((2, tm * ROW_SUB, LANES), jnp.uint32), pltpu.SemaphoreType.DMA((2,))]),
        out_shape=jax.ShapeDtypeStruct(((T * TOP_K + EXPERT_CHUNK) * ROW_SUB, LANES), jnp.uint32),
        compiler_params=pltpu.CompilerParams(dimension_semantics=("arbitrary",),
                                             vmem_limit_bytes=VMEM_LIMIT),
        name="dispatch",
    )(dest_flat, h)


def _experts_kernel(start_ref, count_ref, xs_ref, wg_ref, wu_ref, wd_ref, ys_ref,
                    wgu_b, wd_b, xbuf, ybuf, xsem, ysem, done_ref, *, n_rows):
    e = pl.program_id(0)
    ch = EXPERT_CHUNK
    start = start_ref[e]
    count = count_ref[e]

    def x_copy(row0, half):
        return pltpu.make_async_copy(xs_ref.at[pl.ds(pl.multiple_of(row0 * ROW_SUB, ROW_SUB), ch * ROW_SUB), :],
                                     xbuf.at[half], xsem.at[half])

    def y_copy(row0, half):
        return pltpu.make_async_copy(ybuf.at[half],
                                     ys_ref.at[pl.ds(pl.multiple_of(row0 * ROW_SUB, ROW_SUB), ch * ROW_SUB), :],
                                     ysem)

    @pl.when(e == 0)
    def _():
        done_ref[0] = 0
        ybuf[0] = jnp.zeros(ybuf.shape[1:], ybuf.dtype)
        y_copy(n_rows, 0).start()
        y_copy(n_rows, 0).wait()

    @pl.when(count > 0)
    def _():
        wgu_b[:, 0:D_EXPERT] = wg_ref[0].astype(BF16)
        wgu_b[:, D_EXPERT:] = wu_ref[0].astype(BF16)
        wd_b[...] = wd_ref[0].astype(BF16)

        @pl.when(done_ref[0] == 0)
        def _():
            x_copy(start, 0).start(priority=ROW_STREAM_PRIORITY)

        n_chunks = (count + (ch - 1)) // ch

        def chunk(i, c):
            g = done_ref[0]
            half = g & 1
            row0 = start + i * ch
            x_copy(row0, half).wait()
            nxt = jnp.where(i + 1 < n_chunks, row0 + ch, start + count)

            @pl.when(nxt < n_rows)
            def _():
                x_copy(nxt, 1 - half).start(priority=ROW_STREAM_PRIORITY)

            xb = _tiles_to_rows(xbuf.at[half], 0, ch).astype(BF16)
            gu = _dot(xb, wgu_b[...])
            gate = gu[:, 0:D_EXPERT]
            mid = (gate * jax.nn.sigmoid(gate) * gu[:, D_EXPERT:]).astype(BF16)
            _rows_to_tiles(ybuf.at[half], _dot(mid, wd_b[...]))

            @pl.when(g > 0)
            def _():
                y_copy(0, 1 - half).wait()

            y_copy(row0, half).start(priority=ROW_STREAM_PRIORITY)
            done_ref[0] = g + 1
            return c

        lax.fori_loop(0, n_chunks, chunk, 0)

    @pl.when(e == pl.num_programs(0) - 1)
    def _():
        y_copy(0, 0).wait()


def _experts(seg_start, counts, xs, w_gate, w_up, w_down):
    n_rows = xs.shape[0] // ROW_SUB - EXPERT_CHUNK
    per_expert = lambda shape: pl.BlockSpec((1,) + shape, lambda e, s, c: (e, 0, 0))
    slab = (EXPERT_CHUNK * ROW_SUB, LANES)
    return pl.pallas_call(
        functools.partial(_experts_kernel, n_rows=n_rows),
        grid_spec=pltpu.PrefetchScalarGridSpec(
            num_scalar_prefetch=2,
            grid=(N_EXPERTS,),
            in_specs=[pl.BlockSpec(memory_space=pl.ANY),
                      per_expert((D_MODEL, D_EXPERT)), per_expert((D_MODEL, D_EXPERT)),
                      per_expert((D_EXPERT, D_MODEL))],
            out_specs=pl.BlockSpec(memory_space=pl.ANY),
            scratch_shapes=[pltpu.VMEM((D_MODEL, 2 * D_EXPERT), BF16), pltpu.VMEM((D_EXPERT, D_MODEL), BF16),
                            pltpu.VMEM((2,) + slab, jnp.uint32), pltpu.VMEM((2,) + slab, jnp.uint32),
                            pltpu.SemaphoreType.DMA((2,)), pltpu.SemaphoreType.DMA(()),
                            pltpu.SMEM((1,), jnp.int32)]),
        out_shape=jax.ShapeDtypeStruct(xs.shape, jnp.uint32),
        compiler_params=pltpu.CompilerParams(dimension_semantics=("arbitrary",),
                                             vmem_limit_bytes=VMEM_LIMIT),
        name="experts",
    )(seg_start, counts, xs, w_gate, w_up, w_down)


def _combine_kernel(alpha, dest_ref, h_ref, route_ref, g_ref, b_ref, y_ref, o_ref, buf, sems):
    tm = h_ref.shape[0]
    step = pl.program_id(0)
    half = step & 1
    n_iter = tm * TOP_K // DMA_UNROLL

    def row_copy(hf, src, slot):
        return pltpu.make_async_copy(_row_tile(y_ref, src), _row_tile(buf.at[hf], slot), sems.at[hf])

    def gather(st, hf):
        base = st * (tm * TOP_K)

        def start(i, c):
            for u in range(DMA_UNROLL):
                slot = (u % TOP_K) * tm + i * (DMA_UNROLL // TOP_K) + u // TOP_K
                row_copy(hf, dest_ref[base + i * DMA_UNROLL + u], slot).start(priority=u % 2)
            return c

        lax.fori_loop(0, n_iter, start, 0)

    @pl.when(step == 0)
    def _():
        gather(0, 0)

    @pl.when(step + 1 < pl.num_programs(0))
    def _():
        gather(step + 1, 1 - half)

    def wait(i, c):
        for u in range(DMA_UNROLL):
            row_copy(half, 0, 0).wait()
        return c

    lax.fori_loop(0, n_iter, wait, 0)

    rec = route_ref[...]
    cur = buf.at[half]
    ffn = (rec[:, R_G0:R_G0 + 1] * _tiles_to_rows(cur, 0, tm)
           + rec[:, R_G1:R_G1 + 1] * _tiles_to_rows(cur, tm, tm))
    o_ref[...] = _layer_norm(alpha * h_ref[...] + ffn, g_ref[...], b_ref[...])


def _combine(dest_flat, h, route, ln2_g, ln2_b, y_sorted, alpha, tm):
    T = h.shape[0]
    return pl.pallas_call(
        functools.partial(_combine_kernel, alpha),
        grid_spec=pltpu.PrefetchScalarGridSpec(
            num_scalar_prefetch=1,
            grid=(T // tm,),
            in_specs=[pl.BlockSpec((tm, D_MODEL), lambda i, d: (i, 0)),
                      pl.BlockSpec((tm, LANES), lambda i, d: (i, 0)),
                      pl.BlockSpec((1, D_MODEL), lambda i, d: (0, 0)),
                      pl.BlockSpec((1, D_MODEL), lambda i, d: (0, 0)),
                      pl.BlockSpec(memory_space=pl.ANY)],
            out_specs=pl.BlockSpec((tm, D_MODEL), lambda i, d: (i, 0)),
            scratch_shapes=[pltpu.VMEM((2, TOP_K * tm * ROW_SUB, LANES), jnp.uint32), pltpu.SemaphoreType.DMA((2,))]),
        out_shape=jax.ShapeDtypeStruct((T, D_MODEL), F32),
        compiler_params=pltpu.CompilerParams(dimension_semantics=("arbitrary",),
                                             vmem_limit_bytes=VMEM_LIMIT),
        name="combine",
    )(dest_flat, h, route, ln2_g, ln2_b, y_sorted)


def _rope_column_order():
    within = np.concatenate([np.arange(0, DA_HEAD_DIM, 2), np.arange(1, DA_HEAD_DIM, 2)])
    return np.concatenate([m * DA_HEAD_DIM + within for m in range(2 * DA_HEADS)])


def kernel(x, positions, ln_in_g, ln_in_b, w_in, lam_q1, lam_k1, lam_q2, lam_k2, da_subln_g, gla_w_gate2, gla_b_gate2, gla_norm_g, w_o, ln1_g, ln1_b, router_w_group, router_b_group, router_w_expert, router_b_expert, w_gate, w_up, w_down, ln2_g, ln2_b):
    B, S, D = x.shape
    T = B * S
    depth = w_in.shape[0]
    assert depth == 1, "only a single layer is supported"
    alpha = (2 * depth) ** 0.25
    row2 = lambda a: a.reshape(1, -1)

    inv_freq = ROPE_THETA ** (-jnp.arange(0, DA_HEAD_DIM, 2, dtype=F32) / DA_HEAD_DIM)
    inv_freq = jnp.tile(inv_freq, LANES // (DA_HEAD_DIM // 2)).reshape(1, LANES)
    perm = _rope_column_order()
    pos2 = positions.reshape(T, 1)

    cur = x.reshape(T, D)
    cur_g, cur_b = row2(ln_in_g), row2(ln_in_b)
    for l in range(depth):
        w = w_in[l]
        w_q = w[:, :DA_Q][:, perm]
        w_k = w[:, DA_Q:DA_Q + DA_K][:, perm]
        w_main = jnp.concatenate([w_q, w_k, w[:, DA_Q + DA_K:D_MAIN]], axis=1).astype(BF16)
        w_glow = jnp.pad(w[:, D_MAIN:], ((0, 0), (0, LANES - GLA_GATE_RANK))).astype(BF16)
        w_gate2 = jnp.pad(gla_w_gate2[l], ((0, LANES - GLA_GATE_RANK), (0, 0))).astype(BF16)

        q, k, v, gq, gk, gv, go, la = _in_proj(cur, pos2, cur_g, cur_b, inv_freq, w_main, w_glow,
                                               w_gate2, row2(gla_b_gate2[l]), tm=512)
        lam_init = 0.8 - 0.6 * math.exp(-0.3 * l)
        sh = lambda a: a.reshape(B, S, a.shape[-1])
        da = _diff_attn(sh(q), sh(k), sh(v), row2(lam_q1[l]), row2(lam_k1[l]), row2(lam_q2[l]),
                        row2(lam_k2[l]), row2(da_subln_g[l]), lam_init)
        gl = _gla(sh(gq), sh(gk), sh(la), sh(gv), sh(go), row2(gla_norm_g[l]))

        w_r = jnp.zeros((D, LANES), F32)
        w_r = w_r.at[:, :N_GROUPS].set(router_w_group[l])
        w_r = w_r.at[:, EXPERT_LANE0:EXPERT_LANE0 + N_EXPERTS].set(router_w_expert[l])
        b_r = jnp.zeros((1, LANES), F32)
        b_r = b_r.at[0, :N_GROUPS].set(router_b_group[l])
        b_r = b_r.at[0, EXPERT_LANE0:EXPERT_LANE0 + N_EXPERTS].set(router_b_expert[l])
        wr_hi = w_r.astype(BF16)
        wr_lo = (w_r - wr_hi.astype(F32)).astype(BF16)

        h, route, cnt = _mix_out(cur, da.reshape(T, DA_V), gl.reshape(T, GLA_V), cur_g, cur_b,
                                 w_o[l].astype(BF16), row2(ln1_g[l]), row2(ln1_b[l]), wr_hi, wr_lo, b_r,
                                 alpha, tm=512)

        counts = cnt[0, :N_EXPERTS].astype(jnp.int32)
        seg_start = jnp.cumsum(counts) - counts
        eid = route[:, R_E0:R_E1 + 1].astype(jnp.int32)
        rank = route[:, R_RANK0:R_RANK1 + 1].astype(jnp.int32)
        onehot = eid[..., None] == jnp.arange(N_EXPERTS, dtype=jnp.int32)
        dest = jnp.sum(jnp.where(onehot, seg_start, 0), axis=-1) + rank
        dest_flat = dest.reshape(T * TOP_K)

        xs = _dispatch(dest_flat, h, tm=256)
        ys = _experts(seg_start, counts, xs, w_gate[l], w_up[l], w_down[l])
        cur = _combine(dest_flat, h, route, row2(ln2_g[l]), row2(ln2_b[l]), ys, alpha, tm=256)
    return cur.reshape(B, S, D)
```

```python
import functools
import math

import jax
import jax.numpy as jnp
import numpy as np
from jax import lax
from jax.experimental import pallas as pl
from jax.experimental.pallas import tpu as pltpu

F32 = jnp.float32
BF16 = jnp.bfloat16

D_MODEL = 1024
CHUNK = 64
ROPE_THETA = 10000.0
LN_EPS = 1e-5
LOG2_E = math.log2(math.e)

DA_HEADS = 4
DA_V_DIM = D_MODEL // (2 * DA_HEADS)
DA_HEAD_DIM = DA_V_DIM // 2
GLA_HEADS = 4
GLA_V_DIM = D_MODEL // (2 * GLA_HEADS)
GLA_KEY_DIM = GLA_V_DIM // 2
GLA_GATE_RANK = 16
GLA_GATE_NORMALIZER = 16.0

DA_Q = DA_HEADS * 2 * DA_HEAD_DIM
DA_K = DA_Q
DA_V = DA_HEADS * DA_V_DIM
GLA_Q = GLA_HEADS * GLA_KEY_DIM
GLA_K = GLA_Q
GLA_V = GLA_HEADS * GLA_V_DIM
GLA_OG = GLA_V
D_MAIN = DA_Q + DA_K + DA_V + GLA_Q + GLA_K + GLA_V + GLA_OG

N_GROUPS = 4
EXPERTS_PER_GROUP = 8
N_EXPERTS = N_GROUPS * EXPERTS_PER_GROUP
TOP_K = 2
D_EXPERT = D_MODEL // 2

LANES = 128
ROW_SUB = D_MODEL // (2 * LANES)
ROW_BLOCK = 256
EXPERT_CHUNK = 128
EXPERT_XBUFS = 4
ROW_STREAM_PRIORITY = 1
GLA_BLOCK = 256
ATTN_BLOCK = 256
DMA_UNROLL = 8
VMEM_LIMIT = 48 * 1024 * 1024

R_E0, R_E1, R_G0, R_G1, R_RANK0, R_RANK1 = 0, 1, 2, 3, 4, 5
EXPERT_LANE0 = 32


def _layer_norm(x, g, b):
    mu = jnp.mean(x, axis=-1, keepdims=True)
    xc = x - mu
    var = jnp.mean(xc * xc, axis=-1, keepdims=True)
    return xc * lax.rsqrt(var + LN_EPS) * g + b


def _dot(a, b):
    return jnp.dot(a, b, preferred_element_type=F32)


def _dot_nt(a, b):
    return lax.dot_general(a, b, (((1,), (1,)), ((), ())), preferred_element_type=F32)


def _dot_tn(a, b):
    return lax.dot_general(a, b, (((0,), (0,)), ((), ())), preferred_element_type=F32)


def _in_proj_kernel(x_ref, pos_ref, g_ref, b_ref, invf_ref, w_ref, wgl_ref, wg2_ref, bg2_ref,
                    q_ref, k_ref, v_ref, gq_ref, gk_ref, gv_ref, go_ref, la_ref):
    tm = x_ref.shape[0]
    half = DA_HEAD_DIM // 2
    lane = lax.broadcasted_iota(jnp.int32, (ROW_BLOCK, LANES), 1)
    first = (lane & half) == 0

    for r0 in range(0, tm, ROW_BLOCK):
        rows = slice(r0, r0 + ROW_BLOCK)
        xn = _layer_norm(x_ref[rows, :], g_ref[...], b_ref[...])
        xb = xn.astype(BF16)
        proj = _dot(xb, w_ref[...])

        ang = pos_ref[rows, :].astype(F32) * invf_ref[...]
        c = jnp.cos(ang)
        s = jnp.sin(ang)
        s_lo = jnp.where(first, -s, 0.0)
        s_hi = jnp.where(first, 0.0, s)

        def rope(t):
            out = []
            for j in range(t.shape[1] // LANES):
                tj = t[:, j * LANES:(j + 1) * LANES]
                up = pltpu.roll(tj, LANES - half, 1)
                dn = pltpu.roll(tj, half, 1)
                out.append(tj * c + up * s_lo + dn * s_hi)
            return jnp.concatenate(out, axis=1)

        o = 0
        q = rope(proj[:, o:o + DA_Q]) * (DA_HEAD_DIM ** -0.5 * LOG2_E)
        o += DA_Q
        k = rope(proj[:, o:o + DA_K])
        o += DA_K
        q_ref[rows, :] = q.astype(BF16)
        k_ref[rows, :] = k.astype(BF16)
        v_ref[rows, :] = proj[:, o:o + DA_V].astype(BF16)
        o += DA_V
        gq_ref[rows, :] = proj[:, o:o + GLA_Q].astype(BF16)
        o += GLA_Q
        gk_ref[rows, :] = proj[:, o:o + GLA_K].astype(BF16)
        o += GLA_K
        gv_ref[rows, :] = proj[:, o:o + GLA_V].astype(BF16)
        o += GLA_V
        go_ref[rows, :] = proj[:, o:o + GLA_OG].astype(BF16)

        g_low = _dot(xb, wgl_ref[...])
        z = _dot(g_low.astype(BF16), wg2_ref[...]) + bg2_ref[...]
        log_sig = jnp.minimum(z, 0.0) - jnp.log1p(jnp.exp(-jnp.abs(z)))
        la_ref[rows, :] = log_sig / GLA_GATE_NORMALIZER


def _in_proj(x2, pos2, ln_g, ln_b, inv_freq, w_main, w_glow, w_gate2, b_gate2, tm):
    T = x2.shape[0]
    row = lambda n: pl.BlockSpec((tm, n), lambda i: (i, 0))
    full = lambda a: pl.BlockSpec(a.shape, lambda i: (0,) * a.ndim)
    out_shape = [jax.ShapeDtypeStruct((T, n), dt) for n, dt in (
        (DA_Q, BF16), (DA_K, BF16), (DA_V, BF16), (GLA_Q, BF16), (GLA_K, BF16),
        (GLA_V, BF16), (GLA_OG, BF16), (GLA_K, F32))]
    return pl.pallas_call(
        _in_proj_kernel,
        grid=(T // tm,),
        in_specs=[row(D_MODEL), row(1), full(ln_g), full(ln_b), full(inv_freq), full(w_main),
                  full(w_glow), full(w_gate2), full(b_gate2)],
        out_specs=[row(s.shape[1]) for s in out_shape],
        out_shape=out_shape,
        compiler_params=pltpu.CompilerParams(dimension_semantics=("arbitrary",),
                                             vmem_limit_bytes=VMEM_LIMIT),
        name="in_proj",
    )(x2, pos2, ln_g, ln_b, inv_freq, w_main, w_glow, w_gate2, b_gate2)


def _diff_attn_kernel(lam_init, lq1_ref, lk1_ref, lq2_ref, lk2_ref, g_ref, q_ref, k_ref, v_ref, o_ref,
                      s_scr, p_scr):
    S = q_ref.shape[1]
    tq = ATTN_BLOCK
    lam = (jnp.exp(jnp.sum(lq1_ref[...] * lk1_ref[...], axis=-1, keepdims=True))
           - jnp.exp(jnp.sum(lq2_ref[...] * lk2_ref[...], axis=-1, keepdims=True)) + lam_init)
    lane = lax.broadcasted_iota(jnp.int32, (tq, LANES), 1)
    rq = lax.broadcasted_iota(jnp.int32, (2 * tq, tq), 0) % tq // CHUNK
    ck = lax.broadcasted_iota(jnp.int32, (2 * tq, tq), 1) // CHUNK
    diag_mask = ck <= rq

    for qi in range(S // tq):
        q = q_ref[0, qi * tq:(qi + 1) * tq, :]
        zero = jnp.zeros_like(q)
        qq = jnp.concatenate([jnp.where(lane < DA_HEAD_DIM, q, zero),
                              jnp.where(lane >= DA_HEAD_DIM, q, zero)], axis=0)
        m = None
        for j in range(qi + 1):
            s = _dot_nt(qq, k_ref[0, j * tq:(j + 1) * tq, :])
            if j == qi:
                s = jnp.where(diag_mask, s, -jnp.inf)
            s_scr[:, j * tq:(j + 1) * tq] = s
            mj = jnp.max(s, axis=-1, keepdims=True)
            m = mj if m is None else jnp.maximum(m, mj)
        l = None
        for j in range(qi + 1):
            p = jnp.exp2(s_scr[:, j * tq:(j + 1) * tq] - m)
            p_scr[:, j * tq:(j + 1) * tq] = p.astype(BF16)
            lj = jnp.sum(p, axis=-1, keepdims=True)
            l = lj if l is None else l + lj
        nk = (qi + 1) * tq
        a = _dot(p_scr[:, 0:nk], v_ref[0, 0:nk, :]) / l
        o = a[0:tq] - lam * a[tq:2 * tq]
        o = o * lax.rsqrt(jnp.mean(o * o, axis=-1, keepdims=True) + LN_EPS) * g_ref[...]
        o_ref[0, qi * tq:(qi + 1) * tq, :] = (o * (1.0 - lam_init)).astype(o_ref.dtype)


def _diff_attn(q, k, v, lam_q1, lam_k1, lam_q2, lam_k2, subln_g, lam_init):
    B, S, _ = q.shape
    vec = pl.BlockSpec((1, DA_HEAD_DIM), lambda b, h: (0, 0))
    seq = pl.BlockSpec((1, S, LANES), lambda b, h: (b, 0, h))
    return pl.pallas_call(
        functools.partial(_diff_attn_kernel, lam_init),
        grid=(B, DA_HEADS),
        in_specs=[vec, vec, vec, vec, pl.BlockSpec((1, DA_V_DIM), lambda b, h: (0, 0)), seq, seq, seq],
        out_specs=seq,
        out_shape=jax.ShapeDtypeStruct((B, S, DA_V), BF16),
        scratch_shapes=[pltpu.VMEM((2 * ATTN_BLOCK, S), F32), pltpu.VMEM((2 * ATTN_BLOCK, S), BF16)],
        compiler_params=pltpu.CompilerParams(dimension_semantics=("arbitrary",) * 2,
                                             vmem_limit_bytes=VMEM_LIMIT),
        name="diff_attn",
    )(lam_q1, lam_k1, lam_q2, lam_k2, subln_g, q, k, v)


def _gla_kernel(q_ref, k_ref, la_ref, v_ref, go_ref, ng_ref, o_ref, qt_s, oi_s, ds_s, dec_s):
    S = q_ref.shape[1]
    C = CHUNK
    BLK = GLA_BLOCK
    per_blk = BLK // C
    r = lax.broadcasted_iota(jnp.int32, (BLK, BLK), 0)
    c = lax.broadcasted_iota(jnp.int32, (BLK, BLK), 1)
    chunk_causal = (r // C == c // C) & (c <= r)
    tri = jnp.where(chunk_causal, 1.0, 0.0).astype(BF16)
    lane = lax.broadcasted_iota(jnp.int32, (BLK, LANES), 1)
    head_lanes = (lane < GLA_KEY_DIM, lane >= GLA_KEY_DIM)
    st_row = lax.broadcasted_iota(jnp.int32, (2 * GLA_V_DIM, LANES), 0)
    st_lane = lax.broadcasted_iota(jnp.int32, (2 * GLA_V_DIM, LANES), 1)
    own_keys = (st_row < GLA_V_DIM) == (st_lane < GLA_KEY_DIM)

    for b in range(S // BLK):
        r0 = b * BLK
        g = la_ref[0, r0:r0 + BLK, :]
        g1 = g.astype(BF16)
        e1 = g - g1.astype(F32)
        g2 = e1.astype(BF16)
        g3 = (e1 - g2.astype(F32)).astype(BF16)
        bcum = _dot(tri, g1) + _dot(tri, g2) + _dot(tri, g3)
        b_last = jnp.concatenate(
            [jnp.broadcast_to(bcum[i * C + C - 1:i * C + C, :], (C, LANES)) for i in range(per_blk)], axis=0)
        qf = q_ref[0, r0:r0 + BLK, :].astype(F32) * (GLA_KEY_DIM ** -0.5)
        kf = k_ref[0, r0:r0 + BLK, :].astype(F32)
        q_t = (qf * jnp.exp(bcum)).astype(BF16)
        k_t = (kf * jnp.exp(-bcum)).astype(BF16)
        k_end = (kf * jnp.exp(b_last - bcum)).astype(BF16)
        decay = jnp.exp(b_last)
        qt_s[r0:r0 + BLK, :] = q_t
        zero = jnp.zeros_like(q_t)
        for hh in range(2):
            att = jnp.where(chunk_causal, _dot_nt(jnp.where(head_lanes[hh], q_t, zero), k_t), 0.0).astype(BF16)
            oi_s[r0:r0 + BLK, hh * GLA_V_DIM:(hh + 1) * GLA_V_DIM] = _dot(
                att, v_ref[0, r0:r0 + BLK, hh * GLA_V_DIM:(hh + 1) * GLA_V_DIM])
        for i in range(per_blk):
            n = b * per_blk + i
            rows = slice(r0 + i * C, r0 + (i + 1) * C)
            inc = _dot_tn(v_ref[0, rows, :], k_end[i * C:(i + 1) * C, :])
            ds_s[n] = jnp.where(own_keys, inc, 0.0)
            dec_s[n:n + 1, :] = decay[i * C:i * C + 1, :]

    state = jnp.zeros((2 * GLA_V_DIM, LANES), F32)
    for n in range(S // C):
        rows = slice(n * C, (n + 1) * C)
        o = oi_s[rows, :] + _dot_nt(qt_s[rows, :], state.astype(BF16))
        state = state * dec_s[n:n + 1, :] + ds_s[n]
        for hh in range(2):
            cols = slice(hh * GLA_V_DIM, (hh + 1) * GLA_V_DIM)
            oh = o[:, cols]
            oh = oh * lax.rsqrt(jnp.mean(oh * oh, axis=-1, keepdims=True) + LN_EPS) * ng_ref[...]
            gate = go_ref[0, rows, cols].astype(F32)
            o_ref[0, rows, cols] = (oh * (gate * jax.nn.sigmoid(gate))).astype(o_ref.dtype)


def _gla(gq, gk, la, gv, go, norm_g):
    B, S, _ = gq.shape
    pairs = GLA_HEADS // 2
    narrow = pl.BlockSpec((1, S, LANES), lambda b, p: (b, 0, p))
    wide = pl.BlockSpec((1, S, 2 * GLA_V_DIM), lambda b, p: (b, 0, p))
    n_chunks = S // CHUNK
    return pl.pallas_call(
        _gla_kernel,
        grid=(B, pairs),
        in_specs=[narrow, narrow, narrow, wide, wide,
                  pl.BlockSpec((1, GLA_V_DIM), lambda b, p: (0, 0))],
        out_specs=wide,
        out_shape=jax.ShapeDtypeStruct((B, S, GLA_V), BF16),
        scratch_shapes=[pltpu.VMEM((S, LANES), BF16),
                        pltpu.VMEM((S, 2 * GLA_V_DIM), F32),
                        pltpu.VMEM((n_chunks, 2 * GLA_V_DIM, LANES), F32),
                        pltpu.VMEM((n_chunks, LANES), F32)],
        compiler_params=pltpu.CompilerParams(dimension_semantics=("arbitrary",) * 2,
                                             vmem_limit_bytes=VMEM_LIMIT),
        name="gla",
    )(gq, gk, la, gv, go, norm_g)


def _split3(a):
    hi = a.astype(BF16)
    lo = (a - hi.astype(F32)).astype(BF16)
    return hi, lo


def _mix_out_kernel(alpha, x_ref, da_ref, gl_ref, lng_ref, lnb_ref, wo_ref, g1_ref, b1_ref,
                    wr_hi_ref, wr_lo_ref, br_ref, lower_ref, h_ref, route_ref, cnt_ref):
    tm = x_ref.shape[0]
    i = pl.program_id(0)

    @pl.when(i == 0)
    def _():
        cnt_ref[...] = jnp.zeros_like(cnt_ref)

    logit_blocks = []
    for r0 in range(0, tm, ROW_BLOCK):
        rows = slice(r0, r0 + ROW_BLOCK)
        xn = _layer_norm(x_ref[rows, :], lng_ref[...], lnb_ref[...])
        mix = _dot(da_ref[rows, :], wo_ref[0:DA_V, :]) + _dot(gl_ref[rows, :], wo_ref[DA_V:, :])
        h = _layer_norm(alpha * xn + mix, g1_ref[...], b1_ref[...])
        h_ref[rows, :] = h
        h_hi, h_lo = _split3(h)
        logit_blocks.append(_dot(h_hi, wr_hi_ref[...]) + _dot(h_hi, wr_lo_ref[...]) + _dot(h_lo, wr_hi_ref[...])
                            + br_ref[...])
    logits = jnp.concatenate(logit_blocks, axis=0)
    lane = lax.broadcasted_iota(jnp.int32, (tm, LANES), 1)
    neg = -jnp.inf
    big = jnp.int32(LANES)

    def first_argmax(vals, valid):
        v = jnp.where(valid, vals, neg)
        mx = jnp.max(v, axis=-1, keepdims=True)
        idx = jnp.min(jnp.where(valid & (v == mx), lane, big), axis=-1, keepdims=True)
        return mx, idx

    is_group = lane < N_GROUPS
    g_max, g_top = first_argmax(logits, is_group)
    p_g = 1.0 / jnp.sum(jnp.where(is_group, jnp.exp(logits - g_max), 0.0), axis=-1, keepdims=True)

    e_lo = EXPERT_LANE0 + g_top * EXPERTS_PER_GROUP
    in_group = (lane >= e_lo) & (lane < e_lo + EXPERTS_PER_GROUP)
    v0, i0 = first_argmax(logits, in_group)
    v1, i1 = first_argmax(logits, in_group & (lane != i0))
    w1 = jnp.exp(v1 - v0)
    gate0 = p_g / (1.0 + w1)
    gate1 = p_g * w1 / (1.0 + w1)
    e0 = i0 - EXPERT_LANE0
    e1 = i1 - EXPERT_LANE0

    oh0 = jnp.where(lane == e0, 1.0, 0.0)
    oh1 = jnp.where(lane == e1, 1.0, 0.0)
    oh = oh0 + oh1
    before = _dot(lower_ref[...], oh.astype(BF16)) + cnt_ref[0:1, :]
    rank0 = jnp.sum(oh0 * before, axis=-1, keepdims=True)
    rank1 = jnp.sum(oh1 * before, axis=-1, keepdims=True)
    cnt_ref[...] = cnt_ref[...] + jnp.sum(oh, axis=0, keepdims=True)

    rec = jnp.zeros((tm, LANES), F32)
    for ln, val in ((R_E0, e0.astype(F32)), (R_E1, e1.astype(F32)), (R_G0, gate0), (R_G1, gate1),
                    (R_RANK0, rank0), (R_RANK1, rank1)):
        rec = jnp.where(lane == ln, val, rec)
    route_ref[...] = rec


def _mix_out(x2, da2, gl2, ln_g, ln_b, w_o, ln1_g, ln1_b, wr_hi, wr_lo, b_r, alpha, tm):
    T = x2.shape[0]
    row = lambda n: pl.BlockSpec((tm, n), lambda i: (i, 0))
    full = lambda a: pl.BlockSpec(a.shape, lambda i: (0,) * a.ndim)
    lower = jnp.tril(jnp.ones((tm, tm), BF16), -1)
    return pl.pallas_call(
        functools.partial(_mix_out_kernel, alpha),
        grid=(T // tm,),
        in_specs=[row(D_MODEL), row(DA_V), row(GLA_V), full(ln_g), full(ln_b), full(w_o),
                  full(ln1_g), full(ln1_b), full(wr_hi), full(wr_lo), full(b_r), full(lower)],
        out_specs=[row(D_MODEL), row(LANES), pl.BlockSpec((8, LANES), lambda i: (0, 0))],
        out_shape=[jax.ShapeDtypeStruct((T, D_MODEL), F32), jax.ShapeDtypeStruct((T, LANES), F32),
                   jax.ShapeDtypeStruct((8, LANES), F32)],
        compiler_params=pltpu.CompilerParams(dimension_semantics=("arbitrary",),
                                             vmem_limit_bytes=VMEM_LIMIT),
        name="mix_out",
    )(x2, da2, gl2, ln_g, ln_b, w_o, ln1_g, ln1_b, wr_hi, wr_lo, b_r, lower)


HIGH_HALF = 0xFFFF0000


def _pack_pairs(val):
    bits = lambda a: lax.bitcast_convert_type(a.astype(BF16).astype(F32), jnp.uint32)
    half = val.shape[1] // 2
    return (bits(val[:, :half]) >> 16) | (bits(val[:, half:]) & jnp.uint32(HIGH_HALF))


def _unpack_pairs(words):
    lo = lax.bitcast_convert_type(words << 16, F32)
    hi = lax.bitcast_convert_type(words & jnp.uint32(HIGH_HALF), F32)
    return jnp.concatenate([lo, hi], axis=1)


def _words_to_tiles(dst_ref, words):
    n = words.shape[0]
    for s in range(ROW_SUB):
        dst_ref[pl.ds(s, n, stride=ROW_SUB), :] = words[:, s * LANES:(s + 1) * LANES]


def _tiles_to_words(src_ref, r0, n):
    return jnp.concatenate([src_ref[pl.ds(r0 * ROW_SUB + s, n, stride=ROW_SUB), :] for s in range(ROW_SUB)],
                           axis=1)


def _rows_to_tiles(dst_ref, val):
    _words_to_tiles(dst_ref, _pack_pairs(val))


def _tiles_to_rows(src_ref, r0, n):
    return _unpack_pairs(_tiles_to_words(src_ref, r0, n))


def _row_tile(ref, r):
    return ref.at[pl.ds(pl.multiple_of(r * ROW_SUB, ROW_SUB), ROW_SUB), :]


def _dispatch_kernel(dest_ref, h_ref, xs_ref, stage, sems):
    tm = h_ref.shape[0]
    step = pl.program_id(0)
    half = step & 1
    base = step * (tm * TOP_K)
    n_iter = tm * TOP_K // DMA_UNROLL
    _rows_to_tiles(stage.at[half], h_ref[...])

    def row_copy(hf, t, slot):
        return pltpu.make_async_copy(_row_tile(stage.at[hf], t), _row_tile(xs_ref, slot), sems.at[hf])

    def start(i, c):
        for u in range(DMA_UNROLL):
            t = i * (DMA_UNROLL // TOP_K) + u // TOP_K
            row_copy(half, t, dest_ref[base + i * DMA_UNROLL + u]).start(priority=u % 2)
        return c

    lax.fori_loop(0, n_iter, start, 0)

    def drain(hf):
        def wait(i, c):
            for u in range(DMA_UNROLL):
                row_copy(hf, 0, 0).wait()
            return c
        lax.fori_loop(0, n_iter, wait, 0)

    @pl.when(step > 0)
    def _():
        drain(1 - half)

    @pl.when(step == pl.num_programs(0) - 1)
    def _():
        drain(half)
        n_pad = EXPERT_CHUNK * ROW_SUB
        stage[0, 0:n_pad, :] = jnp.zeros((n_pad, LANES), stage.dtype)
        pad = pltpu.make_async_copy(stage.at[0, 0:n_pad, :],
                                    xs_ref.at[pl.ds(pl.num_programs(0) * tm * TOP_K * ROW_SUB, n_pad), :], sems.at[0])
        pad.start()
        pad.wait()


def _dispatch(dest_flat, h, tm):
    T = h.shape[0]
    return pl.pallas_call(
        _dispatch_kernel,
        grid_spec=pltpu.PrefetchScalarGridSpec(
            num_scalar_prefetch=1,
            grid=(T // tm,),
            in_specs=[pl.BlockSpec((tm, D_MODEL), lambda i, d: (i, 0))],
            out_specs=pl.BlockSpec(memory_space=pl.ANY),
            scratch_shapes=[pltpu.VMEM((2, tm * ROW_SUB, LANES), jnp.uint32), pltpu.SemaphoreType.DMA((2,))]),
        out_shape=jax.ShapeDtypeStruct(((T * TOP_K + EXPERT_CHUNK) * ROW_SUB, LANES), jnp.uint32),
        compiler_params=pltpu.CompilerParams(dimension_semantics=("arbitrary",),
                                             vmem_limit_bytes=VMEM_LIMIT),
        name="dispatch",
    )(dest_flat, h)


def _experts_kernel(row0_ref, first_ref, xs_ref, wg_ref, wu_ref, wd_ref, ys_ref,
                    wgu_b, wd_b, xbuf, ybuf, xsem, ysem, pend_ref, *, n_rows):
    e = pl.program_id(0)
    ch = EXPERT_CHUNK
    depth = EXPERT_XBUFS - 1
    g_lo = first_ref[e]
    g_hi = first_ref[e + 1]
    total = first_ref[N_EXPERTS]

    def slab(ref, row0):
        return ref.at[pl.ds(pl.multiple_of(row0 * ROW_SUB, ROW_SUB), ch * ROW_SUB), :]

    def x_copy(g):
        slot = g & (EXPERT_XBUFS - 1)
        return pltpu.make_async_copy(slab(xs_ref, row0_ref[g]), xbuf.at[slot], xsem.at[slot])

    def y_copy(row0, half):
        return pltpu.make_async_copy(ybuf.at[half], slab(ys_ref, row0), ysem.at[half])

    def drain_y(half):
        @pl.when(pend_ref[half] == 1)
        def _():
            y_copy(0, half).wait()
            pend_ref[half] = 0

    @pl.when(e == 0)
    def _():
        pend_ref[0] = 0
        pend_ref[1] = 0
        ybuf[0] = jnp.zeros(ybuf.shape[1:], ybuf.dtype)
        y_copy(n_rows, 0).start()
        y_copy(n_rows, 0).wait()
        for d in range(depth):
            @pl.when(d < total)
            def _():
                x_copy(d).start(priority=ROW_STREAM_PRIORITY)

    @pl.when(g_hi > g_lo)
    def _():
        wgu_b[:, 0:D_EXPERT] = wg_ref[0].astype(BF16)
        wgu_b[:, D_EXPERT:] = wu_ref[0].astype(BF16)
        wd_b[...] = wd_ref[0].astype(BF16)

        def chunk(g, c):
            half = g & 1
            x_copy(g).wait()

            @pl.when(g + depth < total)
            def _():
                x_copy(g + depth).start(priority=ROW_STREAM_PRIORITY)

            xb = _tiles_to_rows(xbuf.at[g & (EXPERT_XBUFS - 1)], 0, ch).astype(BF16)
            gu = _dot(xb, wgu_b[...])
            gate = gu[:, 0:D_EXPERT]
            mid = (gate * jax.nn.sigmoid(gate) * gu[:, D_EXPERT:]).astype(BF16)
            words = _pack_pairs(_dot(mid, wd_b[...]))

            drain_y(half)

            @pl.when(g == g_lo)
            def _():
                drain_y(1 - half)

            _words_to_tiles(ybuf.at[half], words)
            y_copy(row0_ref[g], half).start(priority=ROW_STREAM_PRIORITY)
            pend_ref[half] = 1
            return c

        lax.fori_loop(g_lo, g_hi, chunk, 0)

    @pl.when(e == pl.num_programs(0) - 1)
    def _():
        drain_y(0)
        drain_y(1)


def _chunk_metadata(seg_start, counts, n_rows):
    ch = EXPERT_CHUNK
    max_chunks = n_rows // ch + N_EXPERTS
    n_ch = (counts + (ch - 1)) // ch
    first = jnp.concatenate([jnp.zeros((1,), jnp.int32), jnp.cumsum(n_ch).astype(jnp.int32)])
    g = jnp.arange(max_chunks, dtype=jnp.int32)
    owner = jnp.minimum(jnp.sum((first[None, 1:] <= g[:, None]).astype(jnp.int32), axis=1), N_EXPERTS - 1)
    onehot = owner[:, None] == jnp.arange(N_EXPERTS, dtype=jnp.int32)
    pick = lambda tab: jnp.sum(jnp.where(onehot, tab[None, :], 0), axis=1)
    row0 = pick(seg_start) + (g - pick(first[:-1])) * ch
    row0 = jnp.where(g < first[-1], row0, 0)
    return row0.astype(jnp.int32), first


def _experts(seg_start, counts, xs, w_gate, w_up, w_down):
    n_rows = xs.shape[0] // ROW_SUB - EXPERT_CHUNK
    row0, first = _chunk_metadata(seg_start, counts, n_rows)
    per_expert = lambda shape: pl.BlockSpec((1,) + shape, lambda e, r, f: (e, 0, 0))
    slab = (EXPERT_CHUNK * ROW_SUB, LANES)
    return pl.pallas_call(
        functools.partial(_experts_kernel, n_rows=n_rows),
        grid_spec=pltpu.PrefetchScalarGridSpec(
            num_scalar_prefetch=2,
            grid=(N_EXPERTS,),
            in_specs=[pl.BlockSpec(memory_space=pl.ANY),
                      per_expert((D_MODEL, D_EXPERT)), per_expert((D_MODEL, D_EXPERT)),
                      per_expert((D_EXPERT, D_MODEL))],
            out_specs=pl.BlockSpec(memory_space=pl.ANY),
            scratch_shapes=[pltpu.VMEM((D_MODEL, 2 * D_EXPERT), BF16), pltpu.VMEM((D_EXPERT, D_MODEL), BF16),
                            pltpu.VMEM((EXPERT_XBUFS,) + slab, jnp.uint32), pltpu.VMEM((2,) + slab, jnp.uint32),
                            pltpu.SemaphoreType.DMA((EXPERT_XBUFS,)), pltpu.SemaphoreType.DMA((2,)),
                            pltpu.SMEM((2,), jnp.int32)]),
        out_shape=jax.ShapeDtypeStruct(xs.shape, jnp.uint32),
        compiler_params=pltpu.CompilerParams(dimension_semantics=("arbitrary",),
                                             vmem_limit_bytes=VMEM_LIMIT),
        name="experts",
    )(row0, first, xs, w_gate, w_up, w_down)


def _combine_kernel(alpha, dest_ref, h_ref, route_ref, g_ref, b_ref, y_ref, o_ref, buf, sems):
    tm = h_ref.shape[0]
    step = pl.program_id(0)
    half = step & 1
    n_iter = tm * TOP_K // DMA_UNROLL

    def row_copy(hf, src, slot):
        return pltpu.make_async_copy(_row_tile(y_ref, src), _row_tile(buf.at[hf], slot), sems.at[hf])

    def gather(st, hf):
        base = st * (tm * TOP_K)

        def start(i, c):
            for u in range(DMA_UNROLL):
                slot = (u % TOP_K) * tm + i * (DMA_UNROLL // TOP_K) + u // TOP_K
                row_copy(hf, dest_ref[base + i * DMA_UNROLL + u], slot).start(priority=u % 2)
            return c

        lax.fori_loop(0, n_iter, start, 0)

    @pl.when(step == 0)
    def _():
        gather(0, 0)

    @pl.when(step + 1 < pl.num_programs(0))
    def _():
        gather(step + 1, 1 - half)

    def wait(i, c):
        for u in range(DMA_UNROLL):
            row_copy(half, 0, 0).wait()
        return c

    lax.fori_loop(0, n_iter, wait, 0)

    rec = route_ref[...]
    cur = buf.at[half]
    ffn = (rec[:, R_G0:R_G0 + 1] * _tiles_to_rows(cur, 0, tm)
           + rec[:, R_G1:R_G1 + 1] * _tiles_to_rows(cur, tm, tm))
    o_ref[...] = _layer_norm(alpha * h_ref[...] + ffn, g_ref[...], b_ref[...])


def _combine(dest_flat, h, route, ln2_g, ln2_b, y_sorted, alpha, tm):
    T = h.shape[0]
    return pl.pallas_call(
        functools.partial(_combine_kernel, alpha),
        grid_spec=pltpu.PrefetchScalarGridSpec(
            num_scalar_prefetch=1,
            grid=(T // tm,),
            in_specs=[pl.BlockSpec((tm, D_MODEL), lambda i, d: (i, 0)),
                      pl.BlockSpec((tm, LANES), lambda i, d: (i, 0)),
                      pl.BlockSpec((1, D_MODEL), lambda i, d: (0, 0)),
                      pl.BlockSpec((1, D_MODEL), lambda i, d: (0, 0)),
                      pl.BlockSpec(memory_space=pl.ANY)],
            out_specs=pl.BlockSpec((tm, D_MODEL), lambda i, d: (i, 0)),
            scratch_shapes=[pltpu.VMEM((2, TOP_K * tm * ROW_SUB, LANES), jnp.uint32), pltpu.SemaphoreType.DMA((2,))]),
        out_shape=jax.ShapeDtypeStruct((T, D_MODEL), F32),
        compiler_params=pltpu.CompilerParams(dimension_semantics=("arbitrary",),
                                             vmem_limit_bytes=VMEM_LIMIT),
        name="combine",
    )(dest_flat, h, route, ln2_g, ln2_b, y_sorted)


def _rope_column_order():
    within = np.concatenate([np.arange(0, DA_HEAD_DIM, 2), np.arange(1, DA_HEAD_DIM, 2)])
    return np.concatenate([m * DA_HEAD_DIM + within for m in range(2 * DA_HEADS)])


def kernel(x, positions, ln_in_g, ln_in_b, w_in, lam_q1, lam_k1, lam_q2, lam_k2, da_subln_g, gla_w_gate2, gla_b_gate2, gla_norm_g, w_o, ln1_g, ln1_b, router_w_group, router_b_group, router_w_expert, router_b_expert, w_gate, w_up, w_down, ln2_g, ln2_b):
    B, S, D = x.shape
    T = B * S
    depth = w_in.shape[0]
    assert depth == 1, "only a single layer is supported"
    alpha = (2 * depth) ** 0.25
    row2 = lambda a: a.reshape(1, -1)

    inv_freq = ROPE_THETA ** (-jnp.arange(0, DA_HEAD_DIM, 2, dtype=F32) / DA_HEAD_DIM)
    inv_freq = jnp.tile(inv_freq, LANES // (DA_HEAD_DIM // 2)).reshape(1, LANES)
    perm = _rope_column_order()
    pos2 = positions.reshape(T, 1)

    cur = x.reshape(T, D)
    cur_g, cur_b = row2(ln_in_g), row2(ln_in_b)
    for l in range(depth):
        w = w_in[l]
        w_q = w[:, :DA_Q][:, perm]
        w_k = w[:, DA_Q:DA_Q + DA_K][:, perm]
        w_main = jnp.concatenate([w_q, w_k, w[:, DA_Q + DA_K:D_MAIN]], axis=1).astype(BF16)
        w_glow = jnp.pad(w[:, D_MAIN:], ((0, 0), (0, LANES - GLA_GATE_RANK))).astype(BF16)
        w_gate2 = jnp.pad(gla_w_gate2[l], ((0, LANES - GLA_GATE_RANK), (0, 0))).astype(BF16)

        q, k, v, gq, gk, gv, go, la = _in_proj(cur, pos2, cur_g, cur_b, inv_freq, w_main, w_glow,
                                               w_gate2, row2(gla_b_gate2[l]), tm=512)
        lam_init = 0.8 - 0.6 * math.exp(-0.3 * l)
        sh = lambda a: a.reshape(B, S, a.shape[-1])
        da = _diff_attn(sh(q), sh(k), sh(v), row2(lam_q1[l]), row2(lam_k1[l]), row2(lam_q2[l]),
                        row2(lam_k2[l]), row2(da_subln_g[l]), lam_init)
        gl = _gla(sh(gq), sh(gk), sh(la), sh(gv), sh(go), row2(gla_norm_g[l]))

        w_r = jnp.zeros((D, LANES), F32)
        w_r = w_r.at[:, :N_GROUPS].set(router_w_group[l])
        w_r = w_r.at[:, EXPERT_LANE0:EXPERT_LANE0 + N_EXPERTS].set(router_w_expert[l])
        b_r = jnp.zeros((1, LANES), F32)
        b_r = b_r.at[0, :N_GROUPS].set(router_b_group[l])
        b_r = b_r.at[0, EXPERT_LANE0:EXPERT_LANE0 + N_EXPERTS].set(router_b_expert[l])
        wr_hi = w_r.astype(BF16)
        wr_lo = (w_r - wr_hi.astype(F32)).astype(BF16)

        h, route, cnt = _mix_out(cur, da.reshape(T, DA_V), gl.reshape(T, GLA_V), cur_g, cur_b,
                                 w_o[l].astype(BF16), row2(ln1_g[l]), row2(ln1_b[l]), wr_hi, wr_lo, b_r,
                                 alpha, tm=512)

        counts = cnt[0, :N_EXPERTS].astype(jnp.int32)
        seg_start = jnp.cumsum(counts) - counts
        eid = route[:, R_E0:R_E1 + 1].astype(jnp.int32)
        rank = route[:, R_RANK0:R_RANK1 + 1].astype(jnp.int32)
        onehot = eid[..., None] == jnp.arange(N_EXPERTS, dtype=jnp.int32)
        dest = jnp.sum(jnp.where(onehot, seg_start, 0), axis=-1) + rank
        dest_flat = dest.reshape(T * TOP_K)

        xs = _dispatch(dest_flat, h, tm=256)
        ys = _experts(seg_start, counts, xs, w_gate[l], w_up[l], w_down[l])
        cur = _combine(dest_flat, h, route, row2(ln2_g[l]), row2(ln2_b[l]), ys, alpha, tm=256)
    return cur.reshape(B, S, D)
```

```python
import functools
import math

import jax
import jax.numpy as jnp
from jax import lax
from jax.experimental import pallas as pl
from jax.experimental.pallas import tpu as pltpu

F32 = jnp.float32
BF16 = jnp.bfloat16

D_MODEL = 1024
CHUNK = 64
ROPE_THETA = 10000.0
LN_EPS = 1e-5
LOG2_E = math.log2(math.e)

DA_HEADS = 4
DA_V_DIM = D_MODEL // (2 * DA_HEADS)
DA_HEAD_DIM = DA_V_DIM // 2
GLA_HEADS = 4
GLA_V_DIM = D_MODEL // (2 * GLA_HEADS)
GLA_KEY_DIM = GLA_V_DIM // 2
GLA_GATE_RANK = 16
GLA_GATE_NORMALIZER = 16.0

DA_Q = DA_HEADS * 2 * DA_HEAD_DIM
DA_K = DA_Q
DA_V = DA_HEADS * DA_V_DIM
GLA_Q = GLA_HEADS * GLA_KEY_DIM
GLA_K = GLA_Q
GLA_V = GLA_HEADS * GLA_V_DIM
GLA_OG = GLA_V
D_MAIN = DA_Q + DA_K + DA_V + GLA_Q + GLA_K + GLA_V + GLA_OG

N_GROUPS = 4
EXPERTS_PER_GROUP = 8
N_EXPERTS = N_GROUPS * EXPERTS_PER_GROUP
TOP_K = 2
D_EXPERT = D_MODEL // 2

LANES = 128
ROW_SUB = D_MODEL // (2 * LANES)
ROW_BLOCK = 256
EXPERT_CHUNK = 128
EXPERT_XBUFS = 4
EXPERT_YBUFS = 4
ROW_STREAM_PRIORITY = 1
GLA_BLOCK = 256
ATTN_BLOCK = 256
DMA_UNROLL = 8
VMEM_LIMIT = 48 * 1024 * 1024

R_E0, R_E1, R_G0, R_G1, R_RANK0, R_RANK1 = 0, 1, 2, 3, 4, 5
EXPERT_LANE0 = 32


def _layer_norm(x, g, b):
    mu = jnp.mean(x, axis=-1, keepdims=True)
    xc = x - mu
    var = jnp.mean(xc * xc, axis=-1, keepdims=True)
    return xc * lax.rsqrt(var + LN_EPS) * g + b


def _dot(a, b):
    return jnp.dot(a, b, preferred_element_type=F32)


def _dot_nt(a, b):
    return lax.dot_general(a, b, (((1,), (1,)), ((), ())), preferred_element_type=F32)


def _dot_tn(a, b):
    return lax.dot_general(a, b, (((0,), (0,)), ((), ())), preferred_element_type=F32)


def _in_proj_kernel(x_ref, pos_ref, g_ref, b_ref, invf_ref, w_ref, wgl_ref, wg2_ref, bg2_ref,
                    q_ref, k_ref, v_ref, gq_ref, gk_ref, gv_ref, go_ref, la_ref):
    tm = x_ref.shape[0]
    lane = lax.broadcasted_iota(jnp.int32, (ROW_BLOCK, LANES), 1)
    first = (lane & 1) == 0

    for r0 in range(0, tm, ROW_BLOCK):
        rows = slice(r0, r0 + ROW_BLOCK)
        xn = _layer_norm(x_ref[rows, :], g_ref[...], b_ref[...])
        xb = xn.astype(BF16)
        proj = _dot(xb, w_ref[...])

        ang = pos_ref[rows, :].astype(F32) * invf_ref[...]
        c = jnp.cos(ang)
        s = jnp.sin(ang)
        s_lo = jnp.where(first, -s, 0.0)
        s_hi = jnp.where(first, 0.0, s)

        def rope(t):
            out = []
            for j in range(t.shape[1] // LANES):
                tj = t[:, j * LANES:(j + 1) * LANES]
                up = pltpu.roll(tj, LANES - 1, 1)
                dn = pltpu.roll(tj, 1, 1)
                out.append(tj * c + up * s_lo + dn * s_hi)
            return jnp.concatenate(out, axis=1)

        o = 0
        q = rope(proj[:, o:o + DA_Q]) * (DA_HEAD_DIM ** -0.5 * LOG2_E)
        o += DA_Q
        k = rope(proj[:, o:o + DA_K])
        o += DA_K
        q_ref[rows, :] = q.astype(BF16)
        k_ref[rows, :] = k.astype(BF16)
        v_ref[rows, :] = proj[:, o:o + DA_V].astype(BF16)
        o += DA_V
        gq_ref[rows, :] = proj[:, o:o + GLA_Q].astype(BF16)
        o += GLA_Q
        gk_ref[rows, :] = proj[:, o:o + GLA_K].astype(BF16)
        o += GLA_K
        gv_ref[rows, :] = proj[:, o:o + GLA_V].astype(BF16)
        o += GLA_V
        go_ref[rows, :] = proj[:, o:o + GLA_OG].astype(BF16)

        g_low = _dot(xb, wgl_ref[...])
        z = _dot(g_low.astype(BF16), wg2_ref[...]) + bg2_ref[...]
        log_sig = jnp.minimum(z, 0.0) - jnp.log1p(jnp.exp(-jnp.abs(z)))
        la_ref[rows, :] = log_sig / GLA_GATE_NORMALIZER


def _in_proj(x2, pos2, ln_g, ln_b, inv_freq, w_main, w_glow, w_gate2, b_gate2, tm):
    T = x2.shape[0]
    row = lambda n: pl.BlockSpec((tm, n), lambda i: (i, 0))
    full = lambda a: pl.BlockSpec(a.shape, lambda i: (0,) * a.ndim)
    out_shape = [jax.ShapeDtypeStruct((T, n), dt) for n, dt in (
        (DA_Q, BF16), (DA_K, BF16), (DA_V, BF16), (GLA_Q, BF16), (GLA_K, BF16),
        (GLA_V, BF16), (GLA_OG, BF16), (GLA_K, F32))]
    return pl.pallas_call(
        _in_proj_kernel,
        grid=(T // tm,),
        in_specs=[row(D_MODEL), row(1), full(ln_g), full(ln_b), full(inv_freq), full(w_main),
                  full(w_glow), full(w_gate2), full(b_gate2)],
        out_specs=[row(s.shape[1]) for s in out_shape],
        out_shape=out_shape,
        compiler_params=pltpu.CompilerParams(dimension_semantics=("arbitrary",),
                                             vmem_limit_bytes=VMEM_LIMIT),
        name="in_proj",
    )(x2, pos2, ln_g, ln_b, inv_freq, w_main, w_glow, w_gate2, b_gate2)


def _diff_attn_kernel(lam_init, lq1_ref, lk1_ref, lq2_ref, lk2_ref, g_ref, q_ref, k_ref, v_ref, o_ref,
                      s_scr, p_scr):
    S = q_ref.shape[1]
    tq = ATTN_BLOCK
    lam = (jnp.exp(jnp.sum(lq1_ref[...] * lk1_ref[...], axis=-1, keepdims=True))
           - jnp.exp(jnp.sum(lq2_ref[...] * lk2_ref[...], axis=-1, keepdims=True)) + lam_init)
    lane = lax.broadcasted_iota(jnp.int32, (tq, LANES), 1)
    rq = lax.broadcasted_iota(jnp.int32, (2 * tq, tq), 0) % tq // CHUNK
    ck = lax.broadcasted_iota(jnp.int32, (2 * tq, tq), 1) // CHUNK
    diag_mask = ck <= rq

    for qi in range(S // tq):
        q = q_ref[0, qi * tq:(qi + 1) * tq, :]
        zero = jnp.zeros_like(q)
        qq = jnp.concatenate([jnp.where(lane < DA_HEAD_DIM, q, zero),
                              jnp.where(lane >= DA_HEAD_DIM, q, zero)], axis=0)
        m = None
        for j in range(qi + 1):
            s = _dot_nt(qq, k_ref[0, j * tq:(j + 1) * tq, :])
            if j == qi:
                s = jnp.where(diag_mask, s, -jnp.inf)
            s_scr[:, j * tq:(j + 1) * tq] = s
            mj = jnp.max(s, axis=-1, keepdims=True)
            m = mj if m is None else jnp.maximum(m, mj)
        l = None
        for j in range(qi + 1):
            p = jnp.exp2(s_scr[:, j * tq:(j + 1) * tq] - m)
            p_scr[:, j * tq:(j + 1) * tq] = p.astype(BF16)
            lj = jnp.sum(p, axis=-1, keepdims=True)
            l = lj if l is None else l + lj
        nk = (qi + 1) * tq
        a = _dot(p_scr[:, 0:nk], v_ref[0, 0:nk, :]) / l
        o = a[0:tq] - lam * a[tq:2 * tq]
        o = o * lax.rsqrt(jnp.mean(o * o, axis=-1, keepdims=True) + LN_EPS) * g_ref[...]
        o_ref[0, qi * tq:(qi + 1) * tq, :] = (o * (1.0 - lam_init)).astype(o_ref.dtype)


def _diff_attn(q, k, v, lam_q1, lam_k1, lam_q2, lam_k2, subln_g, lam_init):
    B, S, _ = q.shape
    vec = pl.BlockSpec((1, DA_HEAD_DIM), lambda b, h: (0, 0))
    seq = pl.BlockSpec((1, S, LANES), lambda b, h: (b, 0, h))
    return pl.pallas_call(
        functools.partial(_diff_attn_kernel, lam_init),
        grid=(B, DA_HEADS),
        in_specs=[vec, vec, vec, vec, pl.BlockSpec((1, DA_V_DIM), lambda b, h: (0, 0)), seq, seq, seq],
        out_specs=seq,
        out_shape=jax.ShapeDtypeStruct((B, S, DA_V), BF16),
        scratch_shapes=[pltpu.VMEM((2 * ATTN_BLOCK, S), F32), pltpu.VMEM((2 * ATTN_BLOCK, S), BF16)],
        compiler_params=pltpu.CompilerParams(dimension_semantics=("arbitrary",) * 2,
                                             vmem_limit_bytes=VMEM_LIMIT),
        name="diff_attn",
    )(lam_q1, lam_k1, lam_q2, lam_k2, subln_g, q, k, v)


def _gla_kernel(q_ref, k_ref, la_ref, v_ref, go_ref, ng_ref, o_ref, qt_s, oi_s, ds_s, dec_s):
    S = q_ref.shape[1]
    C = CHUNK
    BLK = GLA_BLOCK
    per_blk = BLK // C
    r = lax.broadcasted_iota(jnp.int32, (BLK, BLK), 0)
    c = lax.broadcasted_iota(jnp.int32, (BLK, BLK), 1)
    chunk_causal = (r // C == c // C) & (c <= r)
    tri = jnp.where(chunk_causal, 1.0, 0.0).astype(BF16)
    lane = lax.broadcasted_iota(jnp.int32, (BLK, LANES), 1)
    head_lanes = (lane < GLA_KEY_DIM, lane >= GLA_KEY_DIM)
    st_row = lax.broadcasted_iota(jnp.int32, (2 * GLA_V_DIM, LANES), 0)
    st_lane = lax.broadcasted_iota(jnp.int32, (2 * GLA_V_DIM, LANES), 1)
    own_keys = (st_row < GLA_V_DIM) == (st_lane < GLA_KEY_DIM)

    for b in range(S // BLK):
        r0 = b * BLK
        g = la_ref[0, r0:r0 + BLK, :]
        g1 = g.astype(BF16)
        e1 = g - g1.astype(F32)
        g2 = e1.astype(BF16)
        g3 = (e1 - g2.astype(F32)).astype(BF16)
        bcum = _dot(tri, g1) + _dot(tri, g2) + _dot(tri, g3)
        b_last = jnp.concatenate(
            [jnp.broadcast_to(bcum[i * C + C - 1:i * C + C, :], (C, LANES)) for i in range(per_blk)], axis=0)
        qf = q_ref[0, r0:r0 + BLK, :].astype(F32) * (GLA_KEY_DIM ** -0.5)
        kf = k_ref[0, r0:r0 + BLK, :].astype(F32)
        q_t = (qf * jnp.exp(bcum)).astype(BF16)
        k_t = (kf * jnp.exp(-bcum)).astype(BF16)
        k_end = (kf * jnp.exp(b_last - bcum)).astype(BF16)
        decay = jnp.exp(b_last)
        qt_s[r0:r0 + BLK, :] = q_t
        zero = jnp.zeros_like(q_t)
        for hh in range(2):
            att = jnp.where(chunk_causal, _dot_nt(jnp.where(head_lanes[hh], q_t, zero), k_t), 0.0).astype(BF16)
            oi_s[r0:r0 + BLK, hh * GLA_V_DIM:(hh + 1) * GLA_V_DIM] = _dot(
                att, v_ref[0, r0:r0 + BLK, hh * GLA_V_DIM:(hh + 1) * GLA_V_DIM])
        for i in range(per_blk):
            n = b * per_blk + i
            rows = slice(r0 + i * C, r0 + (i + 1) * C)
            inc = _dot_tn(v_ref[0, rows, :], k_end[i * C:(i + 1) * C, :])
            ds_s[n] = jnp.where(own_keys, inc, 0.0)
            dec_s[n:n + 1, :] = decay[i * C:i * C + 1, :]

    state = jnp.zeros((2 * GLA_V_DIM, LANES), F32)
    for n in range(S // C):
        rows = slice(n * C, (n + 1) * C)
        o = oi_s[rows, :] + _dot_nt(qt_s[rows, :], state.astype(BF16))
        state = state * dec_s[n:n + 1, :] + ds_s[n]
        for hh in range(2):
            cols = slice(hh * GLA_V_DIM, (hh + 1) * GLA_V_DIM)
            oh = o[:, cols]
            oh = oh * lax.rsqrt(jnp.mean(oh * oh, axis=-1, keepdims=True) + LN_EPS) * ng_ref[...]
            gate = go_ref[0, rows, cols].astype(F32)
            o_ref[0, rows, cols] = (oh * (gate * jax.nn.sigmoid(gate))).astype(o_ref.dtype)


def _gla(gq, gk, la, gv, go, norm_g):
    B, S, _ = gq.shape
    pairs = GLA_HEADS // 2
    narrow = pl.BlockSpec((1, S, LANES), lambda b, p: (b, 0, p))
    wide = pl.BlockSpec((1, S, 2 * GLA_V_DIM), lambda b, p: (b, 0, p))
    n_chunks = S // CHUNK
    return pl.pallas_call(
        _gla_kernel,
        grid=(B, pairs),
        in_specs=[narrow, narrow, narrow, wide, wide,
                  pl.BlockSpec((1, GLA_V_DIM), lambda b, p: (0, 0))],
        out_specs=wide,
        out_shape=jax.ShapeDtypeStruct((B, S, GLA_V), BF16),
        scratch_shapes=[pltpu.VMEM((S, LANES), BF16),
                        pltpu.VMEM((S, 2 * GLA_V_DIM), F32),
                        pltpu.VMEM((n_chunks, 2 * GLA_V_DIM, LANES), F32),
                        pltpu.VMEM((n_chunks, LANES), F32)],
        compiler_params=pltpu.CompilerParams(dimension_semantics=("arbitrary",) * 2,
                                             vmem_limit_bytes=VMEM_LIMIT),
        name="gla",
    )(gq, gk, la, gv, go, norm_g)


def _split3(a):
    hi = a.astype(BF16)
    lo = (a - hi.astype(F32)).astype(BF16)
    return hi, lo


def _mix_out_kernel(alpha, x_ref, da_ref, gl_ref, lng_ref, lnb_ref, wo_ref, g1_ref, b1_ref,
                    wr_hi_ref, wr_lo_ref, br_ref, lower_ref, h_ref, route_ref, cnt_ref):
    tm = x_ref.shape[0]
    i = pl.program_id(0)

    @pl.when(i == 0)
    def _():
        cnt_ref[...] = jnp.zeros_like(cnt_ref)

    logit_blocks = []
    for r0 in range(0, tm, ROW_BLOCK):
        rows = slice(r0, r0 + ROW_BLOCK)
        xn = _layer_norm(x_ref[rows, :], lng_ref[...], lnb_ref[...])
        mix = _dot(da_ref[rows, :], wo_ref[0:DA_V, :]) + _dot(gl_ref[rows, :], wo_ref[DA_V:, :])
        h = _layer_norm(alpha * xn + mix, g1_ref[...], b1_ref[...])
        h_ref[rows, :] = h
        h_hi, h_lo = _split3(h)
        logit_blocks.append(_dot(h_hi, wr_hi_ref[...]) + _dot(h_hi, wr_lo_ref[...]) + _dot(h_lo, wr_hi_ref[...])
                            + br_ref[...])
    logits = jnp.concatenate(logit_blocks, axis=0)
    lane = lax.broadcasted_iota(jnp.int32, (tm, LANES), 1)
    neg = -jnp.inf
    big = jnp.int32(LANES)

    def first_argmax(vals, valid):
        v = jnp.where(valid, vals, neg)
        mx = jnp.max(v, axis=-1, keepdims=True)
        idx = jnp.min(jnp.where(valid & (v == mx), lane, big), axis=-1, keepdims=True)
        return mx, idx

    is_group = lane < N_GROUPS
    g_max, g_top = first_argmax(logits, is_group)
    p_g = 1.0 / jnp.sum(jnp.where(is_group, jnp.exp(logits - g_max), 0.0), axis=-1, keepdims=True)

    e_lo = EXPERT_LANE0 + g_top * EXPERTS_PER_GROUP
    in_group = (lane >= e_lo) & (lane < e_lo + EXPERTS_PER_GROUP)
    v0, i0 = first_argmax(logits, in_group)
    v1, i1 = first_argmax(logits, in_group & (lane != i0))
    w1 = jnp.exp(v1 - v0)
    gate0 = p_g / (1.0 + w1)
    gate1 = p_g * w1 / (1.0 + w1)
    e0 = i0 - EXPERT_LANE0
    e1 = i1 - EXPERT_LANE0

    oh0 = jnp.where(lane == e0, 1.0, 0.0)
    oh1 = jnp.where(lane == e1, 1.0, 0.0)
    oh = oh0 + oh1
    before = _dot(lower_ref[...], oh.astype(BF16)) + cnt_ref[0:1, :]
    rank0 = jnp.sum(oh0 * before, axis=-1, keepdims=True)
    rank1 = jnp.sum(oh1 * before, axis=-1, keepdims=True)
    cnt_ref[...] = cnt_ref[...] + jnp.sum(oh, axis=0, keepdims=True)

    rec = jnp.zeros((tm, LANES), F32)
    for ln, val in ((R_E0, e0.astype(F32)), (R_E1, e1.astype(F32)), (R_G0, gate0), (R_G1, gate1),
                    (R_RANK0, rank0), (R_RANK1, rank1)):
        rec = jnp.where(lane == ln, val, rec)
    route_ref[...] = rec


def _mix_out(x2, da2, gl2, ln_g, ln_b, w_o, ln1_g, ln1_b, wr_hi, wr_lo, b_r, alpha, tm):
    T = x2.shape[0]
    row = lambda n: pl.BlockSpec((tm, n), lambda i: (i, 0))
    full = lambda a: pl.BlockSpec(a.shape, lambda i: (0,) * a.ndim)
    lower = jnp.tril(jnp.ones((tm, tm), BF16), -1)
    return pl.pallas_call(
        functools.partial(_mix_out_kernel, alpha),
        grid=(T // tm,),
        in_specs=[row(D_MODEL), row(DA_V), row(GLA_V), full(ln_g), full(ln_b), full(w_o),
                  full(ln1_g), full(ln1_b), full(wr_hi), full(wr_lo), full(b_r), full(lower)],
        out_specs=[row(D_MODEL), row(LANES), pl.BlockSpec((8, LANES), lambda i: (0, 0))],
        out_shape=[jax.ShapeDtypeStruct((T, D_MODEL), F32), jax.ShapeDtypeStruct((T, LANES), F32),
                   jax.ShapeDtypeStruct((8, LANES), F32)],
        compiler_params=pltpu.CompilerParams(dimension_semantics=("arbitrary",),
                                             vmem_limit_bytes=VMEM_LIMIT),
        name="mix_out",
    )(x2, da2, gl2, ln_g, ln_b, w_o, ln1_g, ln1_b, wr_hi, wr_lo, b_r, lower)


HIGH_HALF = 0xFFFF0000


def _pack_pairs(val):
    bits = lambda a: lax.bitcast_convert_type(a.astype(BF16).astype(F32), jnp.uint32)
    half = val.shape[1] // 2
    return (bits(val[:, :half]) >> 16) | (bits(val[:, half:]) & jnp.uint32(HIGH_HALF))


def _unpack_pairs(words):
    lo = lax.bitcast_convert_type(words << 16, F32)
    hi = lax.bitcast_convert_type(words & jnp.uint32(HIGH_HALF), F32)
    return jnp.concatenate([lo, hi], axis=1)


def _words_to_tiles(dst_ref, words):
    n = words.shape[0]
    for s in range(ROW_SUB):
        dst_ref[pl.ds(s, n, stride=ROW_SUB), :] = words[:, s * LANES:(s + 1) * LANES]


def _tiles_to_words(src_ref, r0, n):
    return jnp.concatenate([src_ref[pl.ds(r0 * ROW_SUB + s, n, stride=ROW_SUB), :] for s in range(ROW_SUB)],
                           axis=1)


def _rows_to_tiles(dst_ref, val):
    _words_to_tiles(dst_ref, _pack_pairs(val))


def _tiles_to_rows(src_ref, r0, n):
    return _unpack_pairs(_tiles_to_words(src_ref, r0, n))


def _row_tile(ref, r):
    return ref.at[pl.ds(pl.multiple_of(r * ROW_SUB, ROW_SUB), ROW_SUB), :]


def _dispatch_kernel(dest_ref, h_ref, xs_ref, stage, sems):
    tm = h_ref.shape[0]
    step = pl.program_id(0)
    half = step & 1
    base = step * (tm * TOP_K)
    n_iter = tm * TOP_K // DMA_UNROLL
    _rows_to_tiles(stage.at[half], h_ref[...])

    def row_copy(hf, t, slot):
        return pltpu.make_async_copy(_row_tile(stage.at[hf], t), _row_tile(xs_ref, slot), sems.at[hf])

    def start(i, c):
        for u in range(DMA_UNROLL):
            t = i * (DMA_UNROLL // TOP_K) + u // TOP_K
            row_copy(half, t, dest_ref[base + i * DMA_UNROLL + u]).start(priority=u % 2)
        return c

    lax.fori_loop(0, n_iter, start, 0)

    def drain(hf):
        def wait(i, c):
            for u in range(DMA_UNROLL):
                row_copy(hf, 0, 0).wait()
            return c
        lax.fori_loop(0, n_iter, wait, 0)

    @pl.when(step > 0)
    def _():
        drain(1 - half)

    @pl.when(step == pl.num_programs(0) - 1)
    def _():
        drain(half)
        n_pad = EXPERT_CHUNK * ROW_SUB
        stage[0, 0:n_pad, :] = jnp.zeros((n_pad, LANES), stage.dtype)
        pad = pltpu.make_async_copy(stage.at[0, 0:n_pad, :],
                                    xs_ref.at[pl.ds(pl.num_programs(0) * tm * TOP_K * ROW_SUB, n_pad), :], sems.at[0])
        pad.start()
        pad.wait()


def _dispatch(dest_flat, h, tm):
    T = h.shape[0]
    return pl.pallas_call(
        _dispatch_kernel,
        grid_spec=pltpu.PrefetchScalarGridSpec(
            num_scalar_prefetch=1,
            grid=(T // tm,),
            in_specs=[pl.BlockSpec((tm, D_MODEL), lambda i, d: (i, 0))],
            out_specs=pl.BlockSpec(memory_space=pl.ANY),
            scratch_shapes=[pltpu.VMEM((2, tm * ROW_SUB, LANES), jnp.uint32), pltpu.SemaphoreType.DMA((2,))]),
        out_shape=jax.ShapeDtypeStruct(((T * TOP_K + EXPERT_CHUNK) * ROW_SUB, LANES), jnp.uint32),
        compiler_params=pltpu.CompilerParams(dimension_semantics=("arbitrary",),
                                             vmem_limit_bytes=VMEM_LIMIT),
        name="dispatch",
    )(dest_flat, h)


def _experts_kernel(row0_ref, first_ref, xs_ref, wg_ref, wu_ref, wd_ref, ys_ref,
                    wgu_b, wd_b, xbuf, ybuf, xsem, ysem, pend_ref, *, n_rows):
    e = pl.program_id(0)
    ch = EXPERT_CHUNK
    depth = EXPERT_XBUFS - 1
    g_lo = first_ref[e]
    g_hi = first_ref[e + 1]
    total = first_ref[N_EXPERTS]

    def slab(ref, row0):
        return ref.at[pl.ds(pl.multiple_of(row0 * ROW_SUB, ROW_SUB), ch * ROW_SUB), :]

    def x_copy(g):
        slot = g & (EXPERT_XBUFS - 1)
        return pltpu.make_async_copy(slab(xs_ref, row0_ref[g]), xbuf.at[slot], xsem.at[slot])

    def y_copy(row0, half):
        return pltpu.make_async_copy(ybuf.at[half], slab(ys_ref, row0), ysem.at[half])

    def drain_y(half):
        @pl.when(pend_ref[half] == 1)
        def _():
            y_copy(0, half).wait()
            pend_ref[half] = 0

    @pl.when(e == 0)
    def _():
        for b in range(EXPERT_YBUFS):
            pend_ref[b] = 0
        ybuf[0] = jnp.zeros(ybuf.shape[1:], ybuf.dtype)
        y_copy(n_rows, 0).start()
        y_copy(n_rows, 0).wait()
        for d in range(depth):
            @pl.when(d < total)
            def _():
                x_copy(d).start(priority=ROW_STREAM_PRIORITY)

    @pl.when(g_hi > g_lo)
    def _():
        wgu_b[:, 0:D_EXPERT] = wg_ref[0].astype(BF16)
        wgu_b[:, D_EXPERT:] = wu_ref[0].astype(BF16)
        wd_b[...] = wd_ref[0].astype(BF16)

        def chunk(g, c):
            half = g & (EXPERT_YBUFS - 1)
            x_copy(g).wait()

            @pl.when(g + depth < total)
            def _():
                x_copy(g + depth).start(priority=ROW_STREAM_PRIORITY)

            xb = _tiles_to_rows(xbuf.at[g & (EXPERT_XBUFS - 1)], 0, ch).astype(BF16)
            gu = _dot(xb, wgu_b[...])
            gate = gu[:, 0:D_EXPERT]
            mid = (gate * jax.nn.sigmoid(gate) * gu[:, D_EXPERT:]).astype(BF16)
            words = _pack_pairs(_dot(mid, wd_b[...]))

            drain_y(half)

            @pl.when(g == g_lo)
            def _():
                drain_y((g - 1) & (EXPERT_YBUFS - 1))

            _words_to_tiles(ybuf.at[half], words)
            y_copy(row0_ref[g], half).start(priority=ROW_STREAM_PRIORITY)
            pend_ref[half] = 1
            return c

        lax.fori_loop(g_lo, g_hi, chunk, 0)

    @pl.when(e == pl.num_programs(0) - 1)
    def _():
        for b in range(EXPERT_YBUFS):
            drain_y(b)


def _chunk_metadata(seg_start, counts, n_rows):
    ch = EXPERT_CHUNK
    max_chunks = n_rows // ch + N_EXPERTS
    n_ch = (counts + (ch - 1)) // ch
    first = jnp.concatenate([jnp.zeros((1,), jnp.int32), jnp.cumsum(n_ch).astype(jnp.int32)])
    g = jnp.arange(max_chunks, dtype=jnp.int32)
    owner = jnp.minimum(jnp.sum((first[None, 1:] <= g[:, None]).astype(jnp.int32), axis=1), N_EXPERTS - 1)
    onehot = owner[:, None] == jnp.arange(N_EXPERTS, dtype=jnp.int32)
    pick = lambda tab: jnp.sum(jnp.where(onehot, tab[None, :], 0), axis=1)
    row0 = pick(seg_start) + (g - pick(first[:-1])) * ch
    row0 = jnp.where(g < first[-1], row0, 0)
    return row0.astype(jnp.int32), first


def _experts(seg_start, counts, xs, w_gate, w_up, w_down):
    n_rows = xs.shape[0] // ROW_SUB - EXPERT_CHUNK
    row0, first = _chunk_metadata(seg_start, counts, n_rows)
    per_expert = lambda shape: pl.BlockSpec((1,) + shape, lambda e, r, f: (e, 0, 0))
    slab = (EXPERT_CHUNK * ROW_SUB, LANES)
    return pl.pallas_call(
        functools.partial(_experts_kernel, n_rows=n_rows),
        grid_spec=pltpu.PrefetchScalarGridSpec(
            num_scalar_prefetch=2,
            grid=(N_EXPERTS,),
            in_specs=[pl.BlockSpec(memory_space=pl.ANY),
                      per_expert((D_MODEL, D_EXPERT)), per_expert((D_MODEL, D_EXPERT)),
                      per_expert((D_EXPERT, D_MODEL))],
            out_specs=pl.BlockSpec(memory_space=pl.ANY),
            scratch_shapes=[pltpu.VMEM((D_MODEL, 2 * D_EXPERT), BF16), pltpu.VMEM((D_EXPERT, D_MODEL), BF16),
                            pltpu.VMEM((EXPERT_XBUFS,) + slab, jnp.uint32), pltpu.VMEM((EXPERT_YBUFS,) + slab, jnp.uint32),
                            pltpu.SemaphoreType.DMA((EXPERT_XBUFS,)), pltpu.SemaphoreType.DMA((EXPERT_YBUFS,)),
                            pltpu.SMEM((EXPERT_YBUFS,), jnp.int32)]),
        out_shape=jax.ShapeDtypeStruct(xs.shape, jnp.uint32),
        compiler_params=pltpu.CompilerParams(dimension_semantics=("arbitrary",),
                                             vmem_limit_bytes=VMEM_LIMIT),
        name="experts",
    )(row0, first, xs, w_gate, w_up, w_down)


def _combine_kernel(alpha, dest_ref, h_ref, route_ref, g_ref, b_ref, y_ref, o_ref, buf, sems):
    tm = h_ref.shape[0]
    step = pl.program_id(0)
    half = step & 1
    n_iter = tm * TOP_K // DMA_UNROLL

    def row_copy(hf, src, slot):
        return pltpu.make_async_copy(_row_tile(y_ref, src), _row_tile(buf.at[hf], slot), sems.at[hf])

    def gather(st, hf):
        base = st * (tm * TOP_K)

        def start(i, c):
            for u in range(DMA_UNROLL):
                slot = (u % TOP_K) * tm + i * (DMA_UNROLL // TOP_K) + u // TOP_K
                row_copy(hf, dest_ref[base + i * DMA_UNROLL + u], slot).start(priority=u % 2)
            return c

        lax.fori_loop(0, n_iter, start, 0)

    @pl.when(step == 0)
    def _():
        gather(0, 0)

    @pl.when(step + 1 < pl.num_programs(0))
    def _():
        gather(step + 1, 1 - half)

    def wait(i, c):
        for u in range(DMA_UNROLL):
            row_copy(half, 0, 0).wait()
        return c

    lax.fori_loop(0, n_iter, wait, 0)

    rec = route_ref[...]
    cur = buf.at[half]
    ffn = (rec[:, R_G0:R_G0 + 1] * _tiles_to_rows(cur, 0, tm)
           + rec[:, R_G1:R_G1 + 1] * _tiles_to_rows(cur, tm, tm))
    o_ref[...] = _layer_norm(alpha * h_ref[...] + ffn, g_ref[...], b_ref[...])


def _combine(dest_flat, h, route, ln2_g, ln2_b, y_sorted, alpha, tm):
    T = h.shape[0]
    return pl.pallas_call(
        functools.partial(_combine_kernel, alpha),
        grid_spec=pltpu.PrefetchScalarGridSpec(
            num_scalar_prefetch=1,
            grid=(T // tm,),
            in_specs=[pl.BlockSpec((tm, D_MODEL), lambda i, d: (i, 0)),
                      pl.BlockSpec((tm, LANES), lambda i, d: (i, 0)),
                      pl.BlockSpec((1, D_MODEL), lambda i, d: (0, 0)),
                      pl.BlockSpec((1, D_MODEL), lambda i, d: (0, 0)),
                      pl.BlockSpec(memory_space=pl.ANY)],
            out_specs=pl.BlockSpec((tm, D_MODEL), lambda i, d: (i, 0)),
            scratch_shapes=[pltpu.VMEM((2, TOP_K * tm * ROW_SUB, LANES), jnp.uint32), pltpu.SemaphoreType.DMA((2,))]),
        out_shape=jax.ShapeDtypeStruct((T, D_MODEL), F32),
        compiler_params=pltpu.CompilerParams(dimension_semantics=("arbitrary",),
                                             vmem_limit_bytes=VMEM_LIMIT),
        name="combine",
    )(dest_flat, h, route, ln2_g, ln2_b, y_sorted)


def kernel(x, positions, ln_in_g, ln_in_b, w_in, lam_q1, lam_k1, lam_q2, lam_k2, da_subln_g, gla_w_gate2, gla_b_gate2, gla_norm_g, w_o, ln1_g, ln1_b, router_w_group, router_b_group, router_w_expert, router_b_expert, w_gate, w_up, w_down, ln2_g, ln2_b):
    B, S, D = x.shape
    T = B * S
    depth = w_in.shape[0]
    assert depth == 1, "only a single layer is supported"
    alpha = (2 * depth) ** 0.25
    row2 = lambda a: a.reshape(1, -1)

    inv_freq = ROPE_THETA ** (-jnp.arange(0, DA_HEAD_DIM, 2, dtype=F32) / DA_HEAD_DIM)
    inv_freq = jnp.tile(jnp.repeat(inv_freq, 2), LANES // DA_HEAD_DIM).reshape(1, LANES)
    pos2 = positions.reshape(T, 1)

    cur = x.reshape(T, D)
    cur_g, cur_b = row2(ln_in_g), row2(ln_in_b)
    for l in range(depth):
        w = w_in[l]
        w_main = w[:, :D_MAIN].astype(BF16)
        w_glow = jnp.pad(w[:, D_MAIN:], ((0, 0), (0, LANES - GLA_GATE_RANK))).astype(BF16)
        w_gate2 = jnp.pad(gla_w_gate2[l], ((0, LANES - GLA_GATE_RANK), (0, 0))).astype(BF16)

        q, k, v, gq, gk, gv, go, la = _in_proj(cur, pos2, cur_g, cur_b, inv_freq, w_main, w_glow,
                                               w_gate2, row2(gla_b_gate2[l]), tm=512)
        lam_init = 0.8 - 0.6 * math.exp(-0.3 * l)
        sh = lambda a: a.reshape(B, S, a.shape[-1])
        da = _diff_attn(sh(q), sh(k), sh(v), row2(lam_q1[l]), row2(lam_k1[l]), row2(lam_q2[l]),
                        row2(lam_k2[l]), row2(da_subln_g[l]), lam_init)
        gl = _gla(sh(gq), sh(gk), sh(la), sh(gv), sh(go), row2(gla_norm_g[l]))

        w_r = jnp.zeros((D, LANES), F32)
        w_r = w_r.at[:, :N_GROUPS].set(router_w_group[l])
        w_r = w_r.at[:, EXPERT_LANE0:EXPERT_LANE0 + N_EXPERTS].set(router_w_expert[l])
        b_r = jnp.zeros((1, LANES), F32)
        b_r = b_r.at[0, :N_GROUPS].set(router_b_group[l])
        b_r = b_r.at[0, EXPERT_LANE0:EXPERT_LANE0 + N_EXPERTS].set(router_b_expert[l])
        wr_hi = w_r.astype(BF16)
        wr_lo = (w_r - wr_hi.astype(F32)).astype(BF16)

        h, route, cnt = _mix_out(cur, da.reshape(T, DA_V), gl.reshape(T, GLA_V), cur_g, cur_b,
                                 w_o[l].astype(BF16), row2(ln1_g[l]), row2(ln1_b[l]), wr_hi, wr_lo, b_r,
                                 alpha, tm=512)

        counts = cnt[0, :N_EXPERTS].astype(jnp.int32)
        seg_start = jnp.cumsum(counts) - counts
        eid = route[:, R_E0:R_E1 + 1].astype(jnp.int32)
        rank = route[:, R_RANK0:R_RANK1 + 1].astype(jnp.int32)
        dest = jnp.take(seg_start, eid) + rank
        dest_flat = dest.reshape(T * TOP_K)

        xs = _dispatch(dest_flat, h, tm=256)
        ys = _experts(seg_start, counts, xs, w_gate[l], w_up[l], w_down[l])
        cur = _combine(dest_flat, h, route, row2(ln2_g[l]), row2(ln2_b[l]), ys, alpha, tm=256)
    return cur.reshape(B, S, D)
```

```python
import functools
import math

import jax
import jax.numpy as jnp
from jax import lax
from jax.experimental import pallas as pl
from jax.experimental.pallas import tpu as pltpu

F32 = jnp.float32
BF16 = jnp.bfloat16

D_MODEL = 1024
CHUNK = 64
ROPE_THETA = 10000.0
LN_EPS = 1e-5
LOG2_E = math.log2(math.e)

DA_HEADS = 4
DA_V_DIM = D_MODEL // (2 * DA_HEADS)
DA_HEAD_DIM = DA_V_DIM // 2
GLA_HEADS = 4
GLA_V_DIM = D_MODEL // (2 * GLA_HEADS)
GLA_KEY_DIM = GLA_V_DIM // 2
GLA_GATE_RANK = 16
GLA_GATE_NORMALIZER = 16.0

DA_Q = DA_HEADS * 2 * DA_HEAD_DIM
DA_K = DA_Q
DA_V = DA_HEADS * DA_V_DIM
GLA_Q = GLA_HEADS * GLA_KEY_DIM
GLA_K = GLA_Q
GLA_V = GLA_HEADS * GLA_V_DIM
GLA_OG = GLA_V
D_MAIN = DA_Q + DA_K + DA_V + GLA_Q + GLA_K + GLA_V + GLA_OG

N_GROUPS = 4
EXPERTS_PER_GROUP = 8
N_EXPERTS = N_GROUPS * EXPERTS_PER_GROUP
TOP_K = 2
D_EXPERT = D_MODEL // 2

LANES = 128
ROW_SUB = D_MODEL // (2 * LANES)
ROW_BLOCK = 256
EXPERT_CHUNK = 128
EXPERT_XBUFS = 8
EXPERT_YBUFS = 4
ROW_STREAM_PRIORITY = 1
GLA_BLOCK = 256
ATTN_BLOCK = 256
ATTN_PV_KEYS = 512
DMA_UNROLL = 8
VMEM_LIMIT = 48 * 1024 * 1024

R_E0, R_E1, R_G0, R_G1, R_RANK0, R_RANK1 = 0, 1, 2, 3, 4, 5
EXPERT_LANE0 = 32


def _layer_norm(x, g, b):
    mu = jnp.mean(x, axis=-1, keepdims=True)
    xc = x - mu
    var = jnp.mean(xc * xc, axis=-1, keepdims=True)
    return xc * lax.rsqrt(var + LN_EPS) * g + b


def _dot(a, b):
    return jnp.dot(a, b, preferred_element_type=F32)


def _dot_nt(a, b):
    return lax.dot_general(a, b, (((1,), (1,)), ((), ())), preferred_element_type=F32)


def _dot_tn(a, b):
    return lax.dot_general(a, b, (((0,), (0,)), ((), ())), preferred_element_type=F32)


def _in_proj_kernel(x_ref, pos_ref, g_ref, b_ref, invf_ref, w_ref, wgl_ref, wg2_ref, bg2_ref,
                    q_ref, k_ref, v_ref, gq_ref, gk_ref, gv_ref, go_ref, la_ref):
    tm = x_ref.shape[0]
    lane = lax.broadcasted_iota(jnp.int32, (ROW_BLOCK, LANES), 1)
    first = (lane & 1) == 0

    for r0 in range(0, tm, ROW_BLOCK):
        rows = slice(r0, r0 + ROW_BLOCK)
        xn = _layer_norm(x_ref[rows, :], g_ref[...], b_ref[...])
        xb = xn.astype(BF16)
        proj = _dot(xb, w_ref[...])

        ang = pos_ref[rows, :].astype(F32) * invf_ref[...]
        c = jnp.cos(ang)
        s = jnp.sin(ang)
        s_lo = jnp.where(first, -s, 0.0)
        s_hi = jnp.where(first, 0.0, s)

        def rope(t):
            out = []
            for j in range(t.shape[1] // LANES):
                tj = t[:, j * LANES:(j + 1) * LANES]
                up = pltpu.roll(tj, LANES - 1, 1)
                dn = pltpu.roll(tj, 1, 1)
                out.append(tj * c + up * s_lo + dn * s_hi)
            return jnp.concatenate(out, axis=1)

        o = 0
        q = rope(proj[:, o:o + DA_Q]) * (DA_HEAD_DIM ** -0.5 * LOG2_E)
        o += DA_Q
        k = rope(proj[:, o:o + DA_K])
        o += DA_K
        q_ref[rows, :] = q.astype(BF16)
        k_ref[rows, :] = k.astype(BF16)
        v_ref[rows, :] = proj[:, o:o + DA_V].astype(BF16)
        o += DA_V
        gq_ref[rows, :] = proj[:, o:o + GLA_Q].astype(BF16)
        o += GLA_Q
        gk_ref[rows, :] = proj[:, o:o + GLA_K].astype(BF16)
        o += GLA_K
        gv_ref[rows, :] = proj[:, o:o + GLA_V].astype(BF16)
        o += GLA_V
        go_ref[rows, :] = proj[:, o:o + GLA_OG].astype(BF16)

        g_low = _dot(xb, wgl_ref[...])
        z = _dot(g_low.astype(BF16), wg2_ref[...]) + bg2_ref[...]
        log_sig = jnp.minimum(z, 0.0) - jnp.log1p(jnp.exp(-jnp.abs(z)))
        la_ref[rows, :] = log_sig / GLA_GATE_NORMALIZER


def _in_proj(x2, pos2, ln_g, ln_b, inv_freq, w_main, w_glow, w_gate2, b_gate2, tm):
    T = x2.shape[0]
    row = lambda n: pl.BlockSpec((tm, n), lambda i: (i, 0))
    full = lambda a: pl.BlockSpec(a.shape, lambda i: (0,) * a.ndim)
    out_shape = [jax.ShapeDtypeStruct((T, n), dt) for n, dt in (
        (DA_Q, BF16), (DA_K, BF16), (DA_V, BF16), (GLA_Q, BF16), (GLA_K, BF16),
        (GLA_V, BF16), (GLA_OG, BF16), (GLA_K, F32))]
    return pl.pallas_call(
        _in_proj_kernel,
        grid=(T // tm,),
        in_specs=[row(D_MODEL), row(1), full(ln_g), full(ln_b), full(inv_freq), full(w_main),
                  full(w_glow), full(w_gate2), full(b_gate2)],
        out_specs=[row(s.shape[1]) for s in out_shape],
        out_shape=out_shape,
        compiler_params=pltpu.CompilerParams(dimension_semantics=("arbitrary",),
                                             vmem_limit_bytes=VMEM_LIMIT),
        name="in_proj",
    )(x2, pos2, ln_g, ln_b, inv_freq, w_main, w_glow, w_gate2, b_gate2)


def _diff_attn_kernel(lam_init, lq1_ref, lk1_ref, lq2_ref, lk2_ref, g_ref, q_ref, k_ref, v_ref, o_ref,
                      s_scr, p_scr):
    S = q_ref.shape[1]
    tq = ATTN_BLOCK
    lam = (jnp.exp(jnp.sum(lq1_ref[...] * lk1_ref[...], axis=-1, keepdims=True))
           - jnp.exp(jnp.sum(lq2_ref[...] * lk2_ref[...], axis=-1, keepdims=True)) + lam_init)
    lane = lax.broadcasted_iota(jnp.int32, (tq, LANES), 1)
    rq = lax.broadcasted_iota(jnp.int32, (2 * tq, tq), 0) % tq // CHUNK
    ck = lax.broadcasted_iota(jnp.int32, (2 * tq, tq), 1) // CHUNK
    diag_mask = ck <= rq

    n_blk = S // tq
    st = [dict() for _ in range(n_blk)]

    def stage_a(qi):
        s_buf = s_scr.at[qi % 2]

        def begin():
            q = q_ref[0, qi * tq:(qi + 1) * tq, :]
            zero = jnp.zeros_like(q)
            st[qi]["qq"] = jnp.concatenate([jnp.where(lane < DA_HEAD_DIM, q, zero),
                                            jnp.where(lane >= DA_HEAD_DIM, q, zero)], axis=0)
            st[qi]["m_acc"] = None

        def tile(j):
            s = _dot_nt(st[qi]["qq"], k_ref[0, j * tq:(j + 1) * tq, :])
            if j == qi:
                s = jnp.where(diag_mask, s, -jnp.inf)
            s_buf[:, j * tq:(j + 1) * tq] = s
            m_acc = st[qi]["m_acc"]
            for c0 in range(0, tq, LANES):
                sc = s[:, c0:c0 + LANES]
                m_acc = sc if m_acc is None else jnp.maximum(m_acc, sc)
            st[qi]["m_acc"] = m_acc

        def end():
            st[qi]["m"] = jnp.broadcast_to(jnp.max(st[qi]["m_acc"], axis=-1, keepdims=True), (2 * tq, LANES))

        return [begin] + [functools.partial(tile, j) for j in range(qi + 1)] + [end]

    def stage_b(qi):
        s_buf = s_scr.at[qi % 2]
        p_buf = p_scr.at[qi % 2]
        st[qi]["l_acc"] = None

        def cols(c0):
            p = jnp.exp2(s_buf[:, c0:c0 + LANES] - st[qi]["m"])
            p_buf[:, c0:c0 + LANES] = p.astype(BF16)
            l_acc = st[qi]["l_acc"]
            st[qi]["l_acc"] = p if l_acc is None else l_acc + p

        return [functools.partial(cols, c0) for c0 in range(0, (qi + 1) * tq, LANES)]

    def stage_c(qi):
        p_buf = p_scr.at[qi % 2]

        nk = (qi + 1) * tq
        st[qi]["a"] = None

        def part(k0):
            k1 = min(k0 + ATTN_PV_KEYS, nk)
            a = _dot(p_buf[:, k0:k1], v_ref[0, k0:k1, :])
            st[qi]["a"] = a if st[qi]["a"] is None else st[qi]["a"] + a

        def finish():
            l = jnp.sum(st[qi]["l_acc"], axis=-1, keepdims=True)
            a = st[qi]["a"] / l
            o = a[0:tq] - lam * a[tq:2 * tq]
            o = o * lax.rsqrt(jnp.mean(o * o, axis=-1, keepdims=True) + LN_EPS) * g_ref[...]
            o_ref[0, qi * tq:(qi + 1) * tq, :] = (o * (1.0 - lam_init)).astype(o_ref.dtype)

        return [functools.partial(part, k0) for k0 in range(0, nk, ATTN_PV_KEYS)] + [finish]

    for t in range(n_blk + 2):
        stages = []
        if t < n_blk:
            stages.append(stage_a(t))
        if 0 <= t - 1 < n_blk:
            stages.append(stage_b(t - 1))
        if 0 <= t - 2 < n_blk:
            stages.append(stage_c(t - 2))
        merged = sorted(((i + 0.5) / len(ops), k, i, op) for k, ops in enumerate(stages) for i, op in enumerate(ops))
        for _, _, _, op in merged:
            op()


def _diff_attn(q, k, v, lam_q1, lam_k1, lam_q2, lam_k2, subln_g, lam_init):
    B, S, _ = q.shape
    vec = pl.BlockSpec((1, DA_HEAD_DIM), lambda b, h: (0, 0))
    seq = pl.BlockSpec((1, S, LANES), lambda b, h: (b, 0, h))
    return pl.pallas_call(
        functools.partial(_diff_attn_kernel, lam_init),
        grid=(B, DA_HEADS),
        in_specs=[vec, vec, vec, vec, pl.BlockSpec((1, DA_V_DIM), lambda b, h: (0, 0)), seq, seq, seq],
        out_specs=seq,
        out_shape=jax.ShapeDtypeStruct((B, S, DA_V), BF16),
        scratch_shapes=[pltpu.VMEM((2, 2 * ATTN_BLOCK, S), F32), pltpu.VMEM((2, 2 * ATTN_BLOCK, S), BF16)],
        compiler_params=pltpu.CompilerParams(dimension_semantics=("arbitrary",) * 2,
                                             vmem_limit_bytes=VMEM_LIMIT),
        name="diff_attn",
    )(lam_q1, lam_k1, lam_q2, lam_k2, subln_g, q, k, v)


def _gla_kernel(q_ref, k_ref, la_ref, v_ref, go_ref, ng_ref, o_ref, qt_s, oi_s, ds_s, dec_s):
    S = q_ref.shape[1]
    C = CHUNK
    BLK = GLA_BLOCK
    per_blk = BLK // C
    r = lax.broadcasted_iota(jnp.int32, (BLK, BLK), 0)
    c = lax.broadcasted_iota(jnp.int32, (BLK, BLK), 1)
    chunk_causal = (r // C == c // C) & (c <= r)
    tri = jnp.where(chunk_causal, 1.0, 0.0).astype(BF16)
    lane = lax.broadcasted_iota(jnp.int32, (BLK, LANES), 1)
    head_lanes = (lane < GLA_KEY_DIM, lane >= GLA_KEY_DIM)
    st_row = lax.broadcasted_iota(jnp.int32, (2 * GLA_V_DIM, LANES), 0)
    st_lane = lax.broadcasted_iota(jnp.int32, (2 * GLA_V_DIM, LANES), 1)
    own_keys = (st_row < GLA_V_DIM) == (st_lane < GLA_KEY_DIM)

    for b in range(S // BLK):
        r0 = b * BLK
        g = la_ref[0, r0:r0 + BLK, :]
        g1 = g.astype(BF16)
        e1 = g - g1.astype(F32)
        g2 = e1.astype(BF16)
        g3 = (e1 - g2.astype(F32)).astype(BF16)
        bcum = _dot(tri, g1) + _dot(tri, g2) + _dot(tri, g3)
        b_last = jnp.concatenate(
            [jnp.broadcast_to(bcum[i * C + C - 1:i * C + C, :], (C, LANES)) for i in range(per_blk)], axis=0)
        qf = q_ref[0, r0:r0 + BLK, :].astype(F32) * (GLA_KEY_DIM ** -0.5)
        kf = k_ref[0, r0:r0 + BLK, :].astype(F32)
        q_t = (qf * jnp.exp(bcum)).astype(BF16)
        k_t = (kf * jnp.exp(-bcum)).astype(BF16)
        k_end = (kf * jnp.exp(b_last - bcum)).astype(BF16)
        decay = jnp.exp(b_last)
        qt_s[r0:r0 + BLK, :] = q_t
        zero = jnp.zeros_like(q_t)
        for hh in range(2):
            att = jnp.where(chunk_causal, _dot_nt(jnp.where(head_lanes[hh], q_t, zero), k_t), 0.0).astype(BF16)
            oi_s[r0:r0 + BLK, hh * GLA_V_DIM:(hh + 1) * GLA_V_DIM] = _dot(
                att, v_ref[0, r0:r0 + BLK, hh * GLA_V_DIM:(hh + 1) * GLA_V_DIM])
        for i in range(per_blk):
            n = b * per_blk + i
            rows = slice(r0 + i * C, r0 + (i + 1) * C)
            inc = _dot_tn(v_ref[0, rows, :], k_end[i * C:(i + 1) * C, :])
            ds_s[n] = jnp.where(own_keys, inc, 0.0)
            dec_s[n:n + 1, :] = decay[i * C:i * C + 1, :]

    state = jnp.zeros((2 * GLA_V_DIM, LANES), F32)
    for n in range(S // C):
        rows = slice(n * C, (n + 1) * C)
        o = oi_s[rows, :] + _dot_nt(qt_s[rows, :], state.astype(BF16))
        state = state * dec_s[n:n + 1, :] + ds_s[n]
        for hh in range(2):
            cols = slice(hh * GLA_V_DIM, (hh + 1) * GLA_V_DIM)
            oh = o[:, cols]
            oh = oh * lax.rsqrt(jnp.mean(oh * oh, axis=-1, keepdims=True) + LN_EPS) * ng_ref[...]
            gate = go_ref[0, rows, cols].astype(F32)
            o_ref[0, rows, cols] = (oh * (gate * jax.nn.sigmoid(gate))).astype(o_ref.dtype)


def _gla(gq, gk, la, gv, go, norm_g):
    B, S, _ = gq.shape
    pairs = GLA_HEADS // 2
    narrow = pl.BlockSpec((1, S, LANES), lambda b, p: (b, 0, p))
    wide = pl.BlockSpec((1, S, 2 * GLA_V_DIM), lambda b, p: (b, 0, p))
    n_chunks = S // CHUNK
    return pl.pallas_call(
        _gla_kernel,
        grid=(B, pairs),
        in_specs=[narrow, narrow, narrow, wide, wide,
                  pl.BlockSpec((1, GLA_V_DIM), lambda b, p: (0, 0))],
        out_specs=wide,
        out_shape=jax.ShapeDtypeStruct((B, S, GLA_V), BF16),
        scratch_shapes=[pltpu.VMEM((S, LANES), BF16),
                        pltpu.VMEM((S, 2 * GLA_V_DIM), F32),
                        pltpu.VMEM((n_chunks, 2 * GLA_V_DIM, LANES), F32),
                        pltpu.VMEM((n_chunks, LANES), F32)],
        compiler_params=pltpu.CompilerParams(dimension_semantics=("arbitrary",) * 2,
                                             vmem_limit_bytes=VMEM_LIMIT),
        name="gla",
    )(gq, gk, la, gv, go, norm_g)


def _split3(a):
    hi = a.astype(BF16)
    lo = (a - hi.astype(F32)).astype(BF16)
    return hi, lo


def _mix_out_kernel(alpha, x_ref, da_ref, gl_ref, lng_ref, lnb_ref, wo_ref, g1_ref, b1_ref,
                    wr_hi_ref, wr_lo_ref, br_ref, lower_ref, h_ref, route_ref, cnt_ref):
    tm = x_ref.shape[0]
    i = pl.program_id(0)

    @pl.when(i == 0)
    def _():
        cnt_ref[...] = jnp.zeros_like(cnt_ref)

    logit_blocks = []
    for r0 in range(0, tm, ROW_BLOCK):
        rows = slice(r0, r0 + ROW_BLOCK)
        xn = _layer_norm(x_ref[rows, :], lng_ref[...], lnb_ref[...])
        mix = _dot(da_ref[rows, :], wo_ref[0:DA_V, :]) + _dot(gl_ref[rows, :], wo_ref[DA_V:, :])
        h = _layer_norm(alpha * xn + mix, g1_ref[...], b1_ref[...])
        h_ref[rows, :] = h
        h_hi, h_lo = _split3(h)
        logit_blocks.append(_dot(h_hi, wr_hi_ref[...]) + _dot(h_hi, wr_lo_ref[...]) + _dot(h_lo, wr_hi_ref[...])
                            + br_ref[...])
    logits = jnp.concatenate(logit_blocks, axis=0)
    lane = lax.broadcasted_iota(jnp.int32, (tm, LANES), 1)
    neg = -jnp.inf
    big = jnp.int32(LANES)

    def first_argmax(vals, valid):
        v = jnp.where(valid, vals, neg)
        mx = jnp.max(v, axis=-1, keepdims=True)
        idx = jnp.min(jnp.where(valid & (v == mx), lane, big), axis=-1, keepdims=True)
        return mx, idx

    is_group = lane < N_GROUPS
    g_max, g_top = first_argmax(logits, is_group)
    p_g = 1.0 / jnp.sum(jnp.where(is_group, jnp.exp(logits - g_max), 0.0), axis=-1, keepdims=True)

    e_lo = EXPERT_LANE0 + g_top * EXPERTS_PER_GROUP
    in_group = (lane >= e_lo) & (lane < e_lo + EXPERTS_PER_GROUP)
    v0, i0 = first_argmax(logits, in_group)
    v1, i1 = first_argmax(logits, in_group & (lane != i0))
    w1 = jnp.exp(v1 - v0)
    gate0 = p_g / (1.0 + w1)
    gate1 = p_g * w1 / (1.0 + w1)
    e0 = i0 - EXPERT_LANE0
    e1 = i1 - EXPERT_LANE0

    oh0 = jnp.where(lane == e0, 1.0, 0.0)
    oh1 = jnp.where(lane == e1, 1.0, 0.0)
    oh = oh0 + oh1
    before = _dot(lower_ref[...], oh.astype(BF16)) + cnt_ref[0:1, :]
    rank0 = jnp.sum(oh0 * before, axis=-1, keepdims=True)
    rank1 = jnp.sum(oh1 * before, axis=-1, keepdims=True)
    cnt_ref[...] = cnt_ref[...] + jnp.sum(oh, axis=0, keepdims=True)

    rec = jnp.zeros((tm, LANES), F32)
    for ln, val in ((R_E0, e0.astype(F32)), (R_E1, e1.astype(F32)), (R_G0, gate0), (R_G1, gate1),
                    (R_RANK0, rank0), (R_RANK1, rank1)):
        rec = jnp.where(lane == ln, val, rec)
    route_ref[...] = rec


def _mix_out(x2, da2, gl2, ln_g, ln_b, w_o, ln1_g, ln1_b, wr_hi, wr_lo, b_r, alpha, tm):
    T = x2.shape[0]
    row = lambda n: pl.BlockSpec((tm, n), lambda i: (i, 0))
    full = lambda a: pl.BlockSpec(a.shape, lambda i: (0,) * a.ndim)
    lower = jnp.tril(jnp.ones((tm, tm), BF16), -1)
    return pl.pallas_call(
        functools.partial(_mix_out_kernel, alpha),
        grid=(T // tm,),
        in_specs=[row(D_MODEL), row(DA_V), row(GLA_V), full(ln_g), full(ln_b), full(w_o),
                  full(ln1_g), full(ln1_b), full(wr_hi), full(wr_lo), full(b_r), full(lower)],
        out_specs=[row(D_MODEL), row(LANES), pl.BlockSpec((8, LANES), lambda i: (0, 0))],
        out_shape=[jax.ShapeDtypeStruct((T, D_MODEL), F32), jax.ShapeDtypeStruct((T, LANES), F32),
                   jax.ShapeDtypeStruct((8, LANES), F32)],
        compiler_params=pltpu.CompilerParams(dimension_semantics=("arbitrary",),
                                             vmem_limit_bytes=VMEM_LIMIT),
        name="mix_out",
    )(x2, da2, gl2, ln_g, ln_b, w_o, ln1_g, ln1_b, wr_hi, wr_lo, b_r, lower)


HIGH_HALF = 0xFFFF0000


def _pack_pairs(val):
    bits = lambda a: lax.bitcast_convert_type(a.astype(BF16).astype(F32), jnp.uint32)
    half = val.shape[1] // 2
    return (bits(val[:, :half]) >> 16) | (bits(val[:, half:]) & jnp.uint32(HIGH_HALF))


def _unpack_pairs(words):
    lo = lax.bitcast_convert_type(words << 16, F32)
    hi = lax.bitcast_convert_type(words & jnp.uint32(HIGH_HALF), F32)
    return jnp.concatenate([lo, hi], axis=1)


def _words_to_tiles(dst_ref, words):
    n = words.shape[0]
    for s in range(ROW_SUB):
        dst_ref[pl.ds(s, n, stride=ROW_SUB), :] = words[:, s * LANES:(s + 1) * LANES]


def _tiles_to_words(src_ref, r0, n):
    return jnp.concatenate([src_ref[pl.ds(r0 * ROW_SUB + s, n, stride=ROW_SUB), :] for s in range(ROW_SUB)],
                           axis=1)


def _rows_to_tiles(dst_ref, val):
    _words_to_tiles(dst_ref, _pack_pairs(val))


def _tiles_to_rows(src_ref, r0, n):
    return _unpack_pairs(_tiles_to_words(src_ref, r0, n))


def _row_tile(ref, r):
    return ref.at[pl.ds(pl.multiple_of(r * ROW_SUB, ROW_SUB), ROW_SUB), :]


def _dispatch_kernel(dest_ref, h_ref, xs_ref, stage, sems):
    tm = h_ref.shape[0]
    step = pl.program_id(0)
    half = step & 1
    base = step * (tm * TOP_K)
    n_iter = tm * TOP_K // DMA_UNROLL
    _rows_to_tiles(stage.at[half], h_ref[...])

    def row_copy(hf, t, slot):
        return pltpu.make_async_copy(_row_tile(stage.at[hf], t), _row_tile(xs_ref, slot), sems.at[hf])

    def start(i, c):
        for u in range(DMA_UNROLL):
            t = i * (DMA_UNROLL // TOP_K) + u // TOP_K
            row_copy(half, t, dest_ref[base + i * DMA_UNROLL + u]).start(priority=u % 2)
        return c

    lax.fori_loop(0, n_iter, start, 0)

    def drain(hf):
        def wait(i, c):
            for u in range(DMA_UNROLL):
                row_copy(hf, 0, 0).wait()
            return c
        lax.fori_loop(0, n_iter, wait, 0)

    @pl.when(step > 0)
    def _():
        drain(1 - half)

    @pl.when(step == pl.num_programs(0) - 1)
    def _():
        drain(half)
        n_pad = EXPERT_CHUNK * ROW_SUB
        stage[0, 0:n_pad, :] = jnp.zeros((n_pad, LANES), stage.dtype)
        pad = pltpu.make_async_copy(stage.at[0, 0:n_pad, :],
                                    xs_ref.at[pl.ds(pl.num_programs(0) * tm * TOP_K * ROW_SUB, n_pad), :], sems.at[0])
        pad.start()
        pad.wait()


def _dispatch(dest_flat, h, tm):
    T = h.shape[0]
    return pl.pallas_call(
        _dispatch_kernel,
        grid_spec=pltpu.PrefetchScalarGridSpec(
            num_scalar_prefetch=1,
            grid=(T // tm,),
            in_specs=[pl.BlockSpec((tm, D_MODEL), lambda i, d: (i, 0))],
            out_specs=pl.BlockSpec(memory_space=pl.ANY),
            scratch_shapes=[pltpu.VMEM((2, tm * ROW_SUB, LANES), jnp.uint32), pltpu.SemaphoreType.DMA((2,))]),
        out_shape=jax.ShapeDtypeStruct(((T * TOP_K + EXPERT_CHUNK) * ROW_SUB, LANES), jnp.uint32),
        compiler_params=pltpu.CompilerParams(dimension_semantics=("arbitrary",),
                                             vmem_limit_bytes=VMEM_LIMIT),
        name="dispatch",
    )(dest_flat, h)


def _experts_kernel(row0_ref, first_ref, xs_ref, wg_ref, wu_ref, wd_ref, ys_ref,
                    wgu_b, wd_b, xbuf, ybuf, xsem, ysem, pend_ref, *, n_rows):
    e = pl.program_id(0)
    ch = EXPERT_CHUNK
    depth = EXPERT_XBUFS - 1
    g_lo = first_ref[e]
    g_hi = first_ref[e + 1]
    total = first_ref[N_EXPERTS]

    def slab(ref, row0):
        return ref.at[pl.ds(pl.multiple_of(row0 * ROW_SUB, ROW_SUB), ch * ROW_SUB), :]

    def x_copy(g):
        slot = g & (EXPERT_XBUFS - 1)
        return pltpu.make_async_copy(slab(xs_ref, row0_ref[g]), xbuf.at[slot], xsem.at[slot])

    def y_copy(row0, half):
        return pltpu.make_async_copy(ybuf.at[half], slab(ys_ref, row0), ysem.at[half])

    def drain_y(half):
        @pl.when(pend_ref[half] == 1)
        def _():
            y_copy(0, half).wait()
            pend_ref[half] = 0

    @pl.when(e == 0)
    def _():
        for b in range(EXPERT_YBUFS):
            pend_ref[b] = 0
        ybuf[0] = jnp.zeros(ybuf.shape[1:], ybuf.dtype)
        y_copy(n_rows, 0).start()
        y_copy(n_rows, 0).wait()
        for d in range(depth):
            @pl.when(d < total)
            def _():
                x_copy(d).start(priority=ROW_STREAM_PRIORITY)

    @pl.when(g_hi > g_lo)
    def _():
        wgu_b[:, 0:D_EXPERT] = wg_ref[0].astype(BF16)
        wgu_b[:, D_EXPERT:] = wu_ref[0].astype(BF16)
        wd_b[...] = wd_ref[0].astype(BF16)

        def chunk(g, c):
            half = g & (EXPERT_YBUFS - 1)
            x_copy(g).wait()

            @pl.when(g + depth < total)
            def _():
                x_copy(g + depth).start(priority=ROW_STREAM_PRIORITY)

            xb = _tiles_to_rows(xbuf.at[g & (EXPERT_XBUFS - 1)], 0, ch).astype(BF16)
            gu = _dot(xb, wgu_b[...])
            gate = gu[:, 0:D_EXPERT]
            mid = (gate * jax.nn.sigmoid(gate) * gu[:, D_EXPERT:]).astype(BF16)
            words = _pack_pairs(_dot(mid, wd_b[...]))

            drain_y(half)

            @pl.when(g == g_lo)
            def _():
                drain_y((g - 1) & (EXPERT_YBUFS - 1))

            _words_to_tiles(ybuf.at[half], words)
            y_copy(row0_ref[g], half).start(priority=ROW_STREAM_PRIORITY)
            pend_ref[half] = 1
            return c

        lax.fori_loop(g_lo, g_hi, chunk, 0)

    @pl.when(e == pl.num_programs(0) - 1)
    def _():
        for b in range(EXPERT_YBUFS):
            drain_y(b)


def _chunk_metadata(seg_start, counts, n_rows):
    ch = EXPERT_CHUNK
    max_chunks = n_rows // ch + N_EXPERTS
    n_ch = (counts + (ch - 1)) // ch
    first = jnp.concatenate([jnp.zeros((1,), jnp.int32), jnp.cumsum(n_ch).astype(jnp.int32)])
    g = jnp.arange(max_chunks, dtype=jnp.int32)
    owner = jnp.minimum(jnp.sum((first[None, 1:] <= g[:, None]).astype(jnp.int32), axis=1), N_EXPERTS - 1)
    onehot = owner[:, None] == jnp.arange(N_EXPERTS, dtype=jnp.int32)
    pick = lambda tab: jnp.sum(jnp.where(onehot, tab[None, :], 0), axis=1)
    row0 = pick(seg_start) + (g - pick(first[:-1])) * ch
    row0 = jnp.where(g < first[-1], row0, 0)
    return row0.astype(jnp.int32), first


def _experts(seg_start, counts, xs, w_gate, w_up, w_down):
    n_rows = xs.shape[0] // ROW_SUB - EXPERT_CHUNK
    row0, first = _chunk_metadata(seg_start, counts, n_rows)
    per_expert = lambda shape: pl.BlockSpec((1,) + shape, lambda e, r, f: (e, 0, 0))
    slab = (EXPERT_CHUNK * ROW_SUB, LANES)
    return pl.pallas_call(
        functools.partial(_experts_kernel, n_rows=n_rows),
        grid_spec=pltpu.PrefetchScalarGridSpec(
            num_scalar_prefetch=2,
            grid=(N_EXPERTS,),
            in_specs=[pl.BlockSpec(memory_space=pl.ANY),
                      per_expert((D_MODEL, D_EXPERT)), per_expert((D_MODEL, D_EXPERT)),
                      per_expert((D_EXPERT, D_MODEL))],
            out_specs=pl.BlockSpec(memory_space=pl.ANY),
            scratch_shapes=[pltpu.VMEM((D_MODEL, 2 * D_EXPERT), BF16), pltpu.VMEM((D_EXPERT, D_MODEL), BF16),
                            pltpu.VMEM((EXPERT_XBUFS,) + slab, jnp.uint32), pltpu.VMEM((EXPERT_YBUFS,) + slab, jnp.uint32),
                            pltpu.SemaphoreType.DMA((EXPERT_XBUFS,)), pltpu.SemaphoreType.DMA((EXPERT_YBUFS,)),
                            pltpu.SMEM((EXPERT_YBUFS,), jnp.int32)]),
        out_shape=jax.ShapeDtypeStruct(xs.shape, jnp.uint32),
        compiler_params=pltpu.CompilerParams(dimension_semantics=("arbitrary",),
                                             vmem_limit_bytes=VMEM_LIMIT),
        name="experts",
    )(row0, first, xs, w_gate, w_up, w_down)


def _combine_kernel(alpha, dest_ref, h_ref, route_ref, g_ref, b_ref, y_ref, o_ref, buf, sems):
    tm = h_ref.shape[0]
    step = pl.program_id(0)
    half = step & 1
    n_iter = tm * TOP_K // DMA_UNROLL

    def row_copy(hf, src, slot):
        return pltpu.make_async_copy(_row_tile(y_ref, src), _row_tile(buf.at[hf], slot), sems.at[hf])

    def gather(st, hf):
        base = st * (tm * TOP_K)

        def start(i, c):
            for u in range(DMA_UNROLL):
                slot = (u % TOP_K) * tm + i * (DMA_UNROLL // TOP_K) + u // TOP_K
                row_copy(hf, dest_ref[base + i * DMA_UNROLL + u], slot).start(priority=u % 2)
            return c

        lax.fori_loop(0, n_iter, start, 0)

    @pl.when(step == 0)
    def _():
        gather(0, 0)

    @pl.when(step + 1 < pl.num_programs(0))
    def _():
        gather(step + 1, 1 - half)

    def wait(i, c):
        for u in range(DMA_UNROLL):
            row_copy(half, 0, 0).wait()
        return c

    lax.fori_loop(0, n_iter, wait, 0)

    rec = route_ref[...]
    cur = buf.at[half]
    ffn = (rec[:, R_G0:R_G0 + 1] * _tiles_to_rows(cur, 0, tm)
           + rec[:, R_G1:R_G1 + 1] * _tiles_to_rows(cur, tm, tm))
    o_ref[...] = _layer_norm(alpha * h_ref[...] + ffn, g_ref[...], b_ref[...])


def _combine(dest_flat, h, route, ln2_g, ln2_b, y_sorted, alpha, tm):
    T = h.shape[0]
    return pl.pallas_call(
        functools.partial(_combine_kernel, alpha),
        grid_spec=pltpu.PrefetchScalarGridSpec(
            num_scalar_prefetch=1,
            grid=(T // tm,),
            in_specs=[pl.BlockSpec((tm, D_MODEL), lambda i, d: (i, 0)),
                      pl.BlockSpec((tm, LANES), lambda i, d: (i, 0)),
                      pl.BlockSpec((1, D_MODEL), lambda i, d: (0, 0)),
                      pl.BlockSpec((1, D_MODEL), lambda i, d: (0, 0)),
                      pl.BlockSpec(memory_space=pl.ANY)],
            out_specs=pl.BlockSpec((tm, D_MODEL), lambda i, d: (i, 0)),
            scratch_shapes=[pltpu.VMEM((2, TOP_K * tm * ROW_SUB, LANES), jnp.uint32), pltpu.SemaphoreType.DMA((2,))]),
        out_shape=jax.ShapeDtypeStruct((T, D_MODEL), F32),
        compiler_params=pltpu.CompilerParams(dimension_semantics=("arbitrary",),
                                             vmem_limit_bytes=VMEM_LIMIT),
        name="combine",
    )(dest_flat, h, route, ln2_g, ln2_b, y_sorted)


def kernel(x, positions, ln_in_g, ln_in_b, w_in, lam_q1, lam_k1, lam_q2, lam_k2, da_subln_g, gla_w_gate2, gla_b_gate2, gla_norm_g, w_o, ln1_g, ln1_b, router_w_group, router_b_group, router_w_expert, router_b_expert, w_gate, w_up, w_down, ln2_g, ln2_b):
    B, S, D = x.shape
    T = B * S
    depth = w_in.shape[0]
    assert depth == 1, "only a single layer is supported"
    alpha = (2 * depth) ** 0.25
    row2 = lambda a: a.reshape(1, -1)

    inv_freq = ROPE_THETA ** (-jnp.arange(0, DA_HEAD_DIM, 2, dtype=F32) / DA_HEAD_DIM)
    inv_freq = jnp.tile(jnp.repeat(inv_freq, 2), LANES // DA_HEAD_DIM).reshape(1, LANES)
    pos2 = positions.reshape(T, 1)

    cur = x.reshape(T, D)
    cur_g, cur_b = row2(ln_in_g), row2(ln_in_b)
    for l in range(depth):
        w = w_in[l]
        w_main = w[:, :D_MAIN].astype(BF16)
        w_glow = jnp.pad(w[:, D_MAIN:], ((0, 0), (0, LANES - GLA_GATE_RANK))).astype(BF16)
        w_gate2 = jnp.pad(gla_w_gate2[l], ((0, LANES - GLA_GATE_RANK), (0, 0))).astype(BF16)

        q, k, v, gq, gk, gv, go, la = _in_proj(cur, pos2, cur_g, cur_b, inv_freq, w_main, w_glow,
                                               w_gate2, row2(gla_b_gate2[l]), tm=512)
        lam_init = 0.8 - 0.6 * math.exp(-0.3 * l)
        sh = lambda a: a.reshape(B, S, a.shape[-1])
        da = _diff_attn(sh(q), sh(k), sh(v), row2(lam_q1[l]), row2(lam_k1[l]), row2(lam_q2[l]),
                        row2(lam_k2[l]), row2(da_subln_g[l]), lam_init)
        gl = _gla(sh(gq), sh(gk), sh(la), sh(gv), sh(go), row2(gla_norm_g[l]))

        w_r = jnp.zeros((D, LANES), F32)
        w_r = w_r.at[:, :N_GROUPS].set(router_w_group[l])
        w_r = w_r.at[:, EXPERT_LANE0:EXPERT_LANE0 + N_EXPERTS].set(router_w_expert[l])
        b_r = jnp.zeros((1, LANES), F32)
        b_r = b_r.at[0, :N_GROUPS].set(router_b_group[l])
        b_r = b_r.at[0, EXPERT_LANE0:EXPERT_LANE0 + N_EXPERTS].set(router_b_expert[l])
        wr_hi = w_r.astype(BF16)
        wr_lo = (w_r - wr_hi.astype(F32)).astype(BF16)

        h, route, cnt = _mix_out(cur, da.reshape(T, DA_V), gl.reshape(T, GLA_V), cur_g, cur_b,
                                 w_o[l].astype(BF16), row2(ln1_g[l]), row2(ln1_b[l]), wr_hi, wr_lo, b_r,
                                 alpha, tm=512)

        counts = cnt[0, :N_EXPERTS].astype(jnp.int32)
        seg_start = jnp.cumsum(counts) - counts
        eid = route[:, R_E0:R_E1 + 1].astype(jnp.int32)
        rank = route[:, R_RANK0:R_RANK1 + 1].astype(jnp.int32)
        onehot = eid[..., None] == jnp.arange(N_EXPERTS, dtype=jnp.int32)
        dest = jnp.sum(jnp.where(onehot, seg_start, 0), axis=-1) + rank
        dest_flat = dest.reshape(T * TOP_K)

        xs = _dispatch(dest_flat, h, tm=256)
        ys = _experts(seg_start, counts, xs, w_gate[l], w_up[l], w_down[l])
        cur = _combine(dest_flat, h, route, row2(ln2_g[l]), row2(ln2_b[l]), ys, alpha, tm=256)
    return cur.reshape(B, S, D)
```

```python
import functools
import math

import jax
import jax.numpy as jnp
from jax import lax
from jax.experimental import pallas as pl
from jax.experimental.pallas import tpu as pltpu

F32 = jnp.float32
BF16 = jnp.bfloat16

D_MODEL = 1024
CHUNK = 64
ROPE_THETA = 10000.0
LN_EPS = 1e-5
LOG2_E = math.log2(math.e)

DA_HEADS = 4
DA_V_DIM = D_MODEL // (2 * DA_HEADS)
DA_HEAD_DIM = DA_V_DIM // 2
GLA_HEADS = 4
GLA_V_DIM = D_MODEL // (2 * GLA_HEADS)
GLA_KEY_DIM = GLA_V_DIM // 2
GLA_GATE_RANK = 16
GLA_GATE_NORMALIZER = 16.0

DA_Q = DA_HEADS * 2 * DA_HEAD_DIM
DA_K = DA_Q
DA_V = DA_HEADS * DA_V_DIM
GLA_Q = GLA_HEADS * GLA_KEY_DIM
GLA_K = GLA_Q
GLA_V = GLA_HEADS * GLA_V_DIM
GLA_OG = GLA_V
D_MAIN = DA_Q + DA_K + DA_V + GLA_Q + GLA_K + GLA_V + GLA_OG

N_GROUPS = 4
EXPERTS_PER_GROUP = 8
N_EXPERTS = N_GROUPS * EXPERTS_PER_GROUP
TOP_K = 2
D_EXPERT = D_MODEL // 2

LANES = 128
ROW_SUB = D_MODEL // (2 * LANES)
ROW_BLOCK = 256
EXPERT_CHUNK = 128
EXPERT_XBUFS = 8
EXPERT_YBUFS = 4
ROW_STREAM_PRIORITY = 1
GLA_BLOCK = 256
ATTN_BLOCK = 256
ATTN_PV_KEYS = 512
DMA_UNROLL = 8
VMEM_LIMIT = 48 * 1024 * 1024

R_E0, R_E1, R_G0, R_G1, R_RANK0, R_RANK1 = 0, 1, 2, 3, 4, 5
EXPERT_LANE0 = 32


def _layer_norm(x, g, b):
    mu = jnp.mean(x, axis=-1, keepdims=True)
    xc = x - mu
    var = jnp.mean(xc * xc, axis=-1, keepdims=True)
    return xc * lax.rsqrt(var + LN_EPS) * g + b


def _dot(a, b):
    return jnp.dot(a, b, preferred_element_type=F32)


def _dot_nt(a, b):
    return lax.dot_general(a, b, (((1,), (1,)), ((), ())), preferred_element_type=F32)


def _dot_tn(a, b):
    return lax.dot_general(a, b, (((0,), (0,)), ((), ())), preferred_element_type=F32)


def _in_proj_kernel(x_ref, pos_ref, g_ref, b_ref, invf_ref, w_ref, wgl_ref, wg2_ref, bg2_ref,
                    q_ref, k_ref, v_ref, gq_ref, gk_ref, gv_ref, go_ref, la_ref):
    tm = x_ref.shape[0]
    lane = lax.broadcasted_iota(jnp.int32, (ROW_BLOCK, LANES), 1)
    first = (lane & 1) == 0

    for r0 in range(0, tm, ROW_BLOCK):
        rows = slice(r0, r0 + ROW_BLOCK)
        xn = _layer_norm(x_ref[rows, :], g_ref[...], b_ref[...])
        xb = xn.astype(BF16)
        proj = _dot(xb, w_ref[...])

        ang = pos_ref[rows, :].astype(F32) * invf_ref[...]
        c = jnp.cos(ang)
        s = jnp.sin(ang)
        s_lo = jnp.where(first, -s, 0.0)
        s_hi = jnp.where(first, 0.0, s)

        def rope(t):
            out = []
            for j in range(t.shape[1] // LANES):
                tj = t[:, j * LANES:(j + 1) * LANES]
                up = pltpu.roll(tj, LANES - 1, 1)
                dn = pltpu.roll(tj, 1, 1)
                out.append(tj * c + up * s_lo + dn * s_hi)
            return jnp.concatenate(out, axis=1)

        o = 0
        q = rope(proj[:, o:o + DA_Q]) * (DA_HEAD_DIM ** -0.5 * LOG2_E)
        o += DA_Q
        k = rope(proj[:, o:o + DA_K])
        o += DA_K
        q_ref[rows, :] = q.astype(BF16)
        k_ref[rows, :] = k.astype(BF16)
        v_ref[rows, :] = proj[:, o:o + DA_V].astype(BF16)
        o += DA_V
        gq_ref[rows, :] = proj[:, o:o + GLA_Q].astype(BF16)
        o += GLA_Q
        gk_ref[rows, :] = proj[:, o:o + GLA_K].astype(BF16)
        o += GLA_K
        gv_ref[rows, :] = proj[:, o:o + GLA_V].astype(BF16)
        o += GLA_V
        go_ref[rows, :] = proj[:, o:o + GLA_OG].astype(BF16)

        g_low = _dot(xb, wgl_ref[...])
        z = _dot(g_low.astype(BF16), wg2_ref[...]) + bg2_ref[...]
        log_sig = jnp.minimum(z, 0.0) - jnp.log1p(jnp.exp(-jnp.abs(z)))
        la_ref[rows, :] = log_sig / GLA_GATE_NORMALIZER


def _in_proj(x2, pos2, ln_g, ln_b, inv_freq, w_main, w_glow, w_gate2, b_gate2, tm):
    T = x2.shape[0]
    row = lambda n: pl.BlockSpec((tm, n), lambda i: (i, 0))
    full = lambda a: pl.BlockSpec(a.shape, lambda i: (0,) * a.ndim)
    out_shape = [jax.ShapeDtypeStruct((T, n), dt) for n, dt in (
        (DA_Q, BF16), (DA_K, BF16), (DA_V, BF16), (GLA_Q, BF16), (GLA_K, BF16),
        (GLA_V, BF16), (GLA_OG, BF16), (GLA_K, F32))]
    return pl.pallas_call(
        _in_proj_kernel,
        grid=(T // tm,),
        in_specs=[row(D_MODEL), row(1), full(ln_g), full(ln_b), full(inv_freq), full(w_main),
                  full(w_glow), full(w_gate2), full(b_gate2)],
        out_specs=[row(s.shape[1]) for s in out_shape],
        out_shape=out_shape,
        compiler_params=pltpu.CompilerParams(dimension_semantics=("arbitrary",),
                                             vmem_limit_bytes=VMEM_LIMIT),
        name="in_proj",
    )(x2, pos2, ln_g, ln_b, inv_freq, w_main, w_glow, w_gate2, b_gate2)


def _diff_attn_kernel(lam_init, lq1_ref, lk1_ref, lq2_ref, lk2_ref, g_ref, q_ref, k_ref, v_ref, o_ref,
                      s_scr, p_scr, v_ext):
    S = q_ref.shape[1]
    tq = ATTN_BLOCK
    lam = (jnp.exp(jnp.sum(lq1_ref[...] * lk1_ref[...], axis=-1, keepdims=True))
           - jnp.exp(jnp.sum(lq2_ref[...] * lk2_ref[...], axis=-1, keepdims=True)) + lam_init)
    lane = lax.broadcasted_iota(jnp.int32, (tq, LANES), 1)
    rq = lax.broadcasted_iota(jnp.int32, (2 * tq, tq), 0) % tq // CHUNK
    ck = lax.broadcasted_iota(jnp.int32, (2 * tq, tq), 1) // CHUNK
    diag_mask = ck <= rq

    n_blk = S // tq
    st = [dict() for _ in range(n_blk)]
    v_ext[:, 0:DA_V_DIM] = v_ref[0]
    ext_lane = lax.broadcasted_iota(jnp.int32, (S, DA_V_DIM), 1)
    v_ext[:, DA_V_DIM:] = jnp.where(ext_lane == 0, 1.0, 0.0).astype(BF16)

    def stage_a(qi):
        s_buf = s_scr.at[qi % 2]

        def begin():
            q = q_ref[0, qi * tq:(qi + 1) * tq, :]
            zero = jnp.zeros_like(q)
            st[qi]["qq"] = jnp.concatenate([jnp.where(lane < DA_HEAD_DIM, q, zero),
                                            jnp.where(lane >= DA_HEAD_DIM, q, zero)], axis=0)
            st[qi]["m_acc"] = None

        def tile(j):
            s = _dot_nt(st[qi]["qq"], k_ref[0, j * tq:(j + 1) * tq, :])
            if j == qi:
                s = jnp.where(diag_mask, s, -jnp.inf)
            s_buf[:, j * tq:(j + 1) * tq] = s
            m_acc = st[qi]["m_acc"]
            for c0 in range(0, tq, LANES):
                sc = s[:, c0:c0 + LANES]
                m_acc = sc if m_acc is None else jnp.maximum(m_acc, sc)
            st[qi]["m_acc"] = m_acc

        def end():
            st[qi]["m"] = jnp.broadcast_to(jnp.max(st[qi]["m_acc"], axis=-1, keepdims=True), (2 * tq, LANES))

        return [begin] + [functools.partial(tile, j) for j in range(qi + 1)] + [end]

    def stage_b(qi):
        s_buf = s_scr.at[qi % 2]
        p_buf = p_scr.at[qi % 2]

        def cols(c0):
            p_buf[:, c0:c0 + LANES] = jnp.exp2((s_buf[:, c0:c0 + LANES] - st[qi]["m"]).astype(BF16))

        return [functools.partial(cols, c0) for c0 in range(0, (qi + 1) * tq, LANES)]

    def stage_c(qi):
        p_buf = p_scr.at[qi % 2]

        nk = (qi + 1) * tq
        st[qi]["a"] = None

        def part(k0):
            k1 = min(k0 + ATTN_PV_KEYS, nk)
            a = _dot(p_buf[:, k0:k1], v_ext[k0:k1, :])
            st[qi]["a"] = a if st[qi]["a"] is None else st[qi]["a"] + a

        def finish():
            a = st[qi]["a"][:, 0:DA_V_DIM] / st[qi]["a"][:, DA_V_DIM:DA_V_DIM + 1]
            o = a[0:tq] - lam * a[tq:2 * tq]
            o = o * lax.rsqrt(jnp.mean(o * o, axis=-1, keepdims=True) + LN_EPS) * g_ref[...]
            o_ref[0, qi * tq:(qi + 1) * tq, :] = (o * (1.0 - lam_init)).astype(o_ref.dtype)

        return [functools.partial(part, k0) for k0 in range(0, nk, ATTN_PV_KEYS)] + [finish]

    for t in range(n_blk + 2):
        stages = []
        if t < n_blk:
            stages.append(stage_a(t))
        if 0 <= t - 1 < n_blk:
            stages.append(stage_b(t - 1))
        if 0 <= t - 2 < n_blk:
            stages.append(stage_c(t - 2))
        merged = sorted(((i + 0.5) / len(ops), k, i, op) for k, ops in enumerate(stages) for i, op in enumerate(ops))
        for _, _, _, op in merged:
            op()


def _diff_attn(q, k, v, lam_q1, lam_k1, lam_q2, lam_k2, subln_g, lam_init):
    B, S, _ = q.shape
    vec = pl.BlockSpec((1, DA_HEAD_DIM), lambda b, h: (0, 0))
    seq = pl.BlockSpec((1, S, LANES), lambda b, h: (b, 0, h))
    return pl.pallas_call(
        functools.partial(_diff_attn_kernel, lam_init),
        grid=(B, DA_HEADS),
        in_specs=[vec, vec, vec, vec, pl.BlockSpec((1, DA_V_DIM), lambda b, h: (0, 0)), seq, seq, seq],
        out_specs=seq,
        out_shape=jax.ShapeDtypeStruct((B, S, DA_V), BF16),
        scratch_shapes=[pltpu.VMEM((2, 2 * ATTN_BLOCK, S), F32), pltpu.VMEM((2, 2 * ATTN_BLOCK, S), BF16),
                        pltpu.VMEM((S, 2 * DA_V_DIM), BF16)],
        compiler_params=pltpu.CompilerParams(dimension_semantics=("arbitrary",) * 2,
                                             vmem_limit_bytes=VMEM_LIMIT),
        name="diff_attn",
    )(lam_q1, lam_k1, lam_q2, lam_k2, subln_g, q, k, v)


def _gla_kernel(q_ref, k_ref, la_ref, v_ref, go_ref, ng_ref, o_ref, qt_s, oi_s, ds_s, dec_s):
    S = q_ref.shape[1]
    C = CHUNK
    BLK = GLA_BLOCK
    per_blk = BLK // C
    r = lax.broadcasted_iota(jnp.int32, (BLK, BLK), 0)
    c = lax.broadcasted_iota(jnp.int32, (BLK, BLK), 1)
    chunk_causal = (r // C == c // C) & (c <= r)
    tri = jnp.where(chunk_causal, 1.0, 0.0).astype(BF16)
    lane = lax.broadcasted_iota(jnp.int32, (BLK, LANES), 1)
    head_lanes = (lane < GLA_KEY_DIM, lane >= GLA_KEY_DIM)
    st_row = lax.broadcasted_iota(jnp.int32, (2 * GLA_V_DIM, LANES), 0)
    st_lane = lax.broadcasted_iota(jnp.int32, (2 * GLA_V_DIM, LANES), 1)
    own_keys = (st_row < GLA_V_DIM) == (st_lane < GLA_KEY_DIM)

    for b in range(S // BLK):
        r0 = b * BLK
        g = la_ref[0, r0:r0 + BLK, :]
        g1 = g.astype(BF16)
        e1 = g - g1.astype(F32)
        g2 = e1.astype(BF16)
        g3 = (e1 - g2.astype(F32)).astype(BF16)
        bcum = _dot(tri, g1) + _dot(tri, g2) + _dot(tri, g3)
        b_last = jnp.concatenate(
            [jnp.broadcast_to(bcum[i * C + C - 1:i * C + C, :], (C, LANES)) for i in range(per_blk)], axis=0)
        qf = q_ref[0, r0:r0 + BLK, :].astype(F32) * (GLA_KEY_DIM ** -0.5)
        kf = k_ref[0, r0:r0 + BLK, :].astype(F32)
        q_t = (qf * jnp.exp(bcum)).astype(BF16)
        k_t = (kf * jnp.exp(-bcum)).astype(BF16)
        k_end = (kf * jnp.exp(b_last - bcum)).astype(BF16)
        decay = jnp.exp(b_last)
        qt_s[r0:r0 + BLK, :] = q_t
        zero = jnp.zeros_like(q_t)
        for hh in range(2):
            att = jnp.where(chunk_causal, _dot_nt(jnp.where(head_lanes[hh], q_t, zero), k_t), 0.0).astype(BF16)
            oi_s[r0:r0 + BLK, hh * GLA_V_DIM:(hh + 1) * GLA_V_DIM] = _dot(
                att, v_ref[0, r0:r0 + BLK, hh * GLA_V_DIM:(hh + 1) * GLA_V_DIM])
        for i in range(per_blk):
            n = b * per_blk + i
            rows = slice(r0 + i * C, r0 + (i + 1) * C)
            inc = _dot_tn(v_ref[0, rows, :], k_end[i * C:(i + 1) * C, :])
            ds_s[n] = jnp.where(own_keys, inc, 0.0)
            dec_s[n:n + 1, :] = decay[i * C:i * C + 1, :]

    state = jnp.zeros((2 * GLA_V_DIM, LANES), F32)
    for n in range(S // C):
        rows = slice(n * C, (n + 1) * C)
        o = oi_s[rows, :] + _dot_nt(qt_s[rows, :], state.astype(BF16))
        state = state * dec_s[n:n + 1, :] + ds_s[n]
        for hh in range(2):
            cols = slice(hh * GLA_V_DIM, (hh + 1) * GLA_V_DIM)
            oh = o[:, cols]
            oh = oh * lax.rsqrt(jnp.mean(oh * oh, axis=-1, keepdims=True) + LN_EPS) * ng_ref[...]
            gate = go_ref[0, rows, cols].astype(F32)
            o_ref[0, rows, cols] = (oh * (gate * jax.nn.sigmoid(gate))).astype(o_ref.dtype)


def _gla(gq, gk, la, gv, go, norm_g):
    B, S, _ = gq.shape
    pairs = GLA_HEADS // 2
    narrow = pl.BlockSpec((1, S, LANES), lambda b, p: (b, 0, p))
    wide = pl.BlockSpec((1, S, 2 * GLA_V_DIM), lambda b, p: (b, 0, p))
    n_chunks = S // CHUNK
    return pl.pallas_call(
        _gla_kernel,
        grid=(B, pairs),
        in_specs=[narrow, narrow, narrow, wide, wide,
                  pl.BlockSpec((1, GLA_V_DIM), lambda b, p: (0, 0))],
        out_specs=wide,
        out_shape=jax.ShapeDtypeStruct((B, S, GLA_V), BF16),
        scratch_shapes=[pltpu.VMEM((S, LANES), BF16),
                        pltpu.VMEM((S, 2 * GLA_V_DIM), F32),
                        pltpu.VMEM((n_chunks, 2 * GLA_V_DIM, LANES), F32),
                        pltpu.VMEM((n_chunks, LANES), F32)],
        compiler_params=pltpu.CompilerParams(dimension_semantics=("arbitrary",) * 2,
                                             vmem_limit_bytes=VMEM_LIMIT),
        name="gla",
    )(gq, gk, la, gv, go, norm_g)


def _split3(a):
    hi = a.astype(BF16)
    lo = (a - hi.astype(F32)).astype(BF16)
    return hi, lo


def _mix_out_kernel(alpha, x_ref, da_ref, gl_ref, lng_ref, lnb_ref, wo_ref, g1_ref, b1_ref,
                    wr_hi_ref, wr_lo_ref, br_ref, lower_ref, h_ref, route_ref, cnt_ref):
    tm = x_ref.shape[0]
    i = pl.program_id(0)

    @pl.when(i == 0)
    def _():
        cnt_ref[...] = jnp.zeros_like(cnt_ref)

    logit_blocks = []
    for r0 in range(0, tm, ROW_BLOCK):
        rows = slice(r0, r0 + ROW_BLOCK)
        xn = _layer_norm(x_ref[rows, :], lng_ref[...], lnb_ref[...])
        mix = _dot(da_ref[rows, :], wo_ref[0:DA_V, :]) + _dot(gl_ref[rows, :], wo_ref[DA_V:, :])
        h = _layer_norm(alpha * xn + mix, g1_ref[...], b1_ref[...])
        h_ref[rows, :] = h
        h_hi, h_lo = _split3(h)
        logit_blocks.append(_dot(h_hi, wr_hi_ref[...]) + _dot(h_hi, wr_lo_ref[...]) + _dot(h_lo, wr_hi_ref[...])
                            + br_ref[...])
    logits = jnp.concatenate(logit_blocks, axis=0)
    lane = lax.broadcasted_iota(jnp.int32, (tm, LANES), 1)
    neg = -jnp.inf
    big = jnp.int32(LANES)

    def first_argmax(vals, valid):
        v = jnp.where(valid, vals, neg)
        mx = jnp.max(v, axis=-1, keepdims=True)
        idx = jnp.min(jnp.where(valid & (v == mx), lane, big), axis=-1, keepdims=True)
        return mx, idx

    is_group = lane < N_GROUPS
    g_max, g_top = first_argmax(logits, is_group)
    p_g = 1.0 / jnp.sum(jnp.where(is_group, jnp.exp(logits - g_max), 0.0), axis=-1, keepdims=True)

    e_lo = EXPERT_LANE0 + g_top * EXPERTS_PER_GROUP
    in_group = (lane >= e_lo) & (lane < e_lo + EXPERTS_PER_GROUP)
    v0, i0 = first_argmax(logits, in_group)
    v1, i1 = first_argmax(logits, in_group & (lane != i0))
    w1 = jnp.exp(v1 - v0)
    gate0 = p_g / (1.0 + w1)
    gate1 = p_g * w1 / (1.0 + w1)
    e0 = i0 - EXPERT_LANE0
    e1 = i1 - EXPERT_LANE0

    oh0 = jnp.where(lane == e0, 1.0, 0.0)
    oh1 = jnp.where(lane == e1, 1.0, 0.0)
    oh = oh0 + oh1
    before = _dot(lower_ref[...], oh.astype(BF16)) + cnt_ref[0:1, :]
    rank0 = jnp.sum(oh0 * before, axis=-1, keepdims=True)
    rank1 = jnp.sum(oh1 * before, axis=-1, keepdims=True)
    cnt_ref[...] = cnt_ref[...] + jnp.sum(oh, axis=0, keepdims=True)

    rec = jnp.zeros((tm, LANES), F32)
    for ln, val in ((R_E0, e0.astype(F32)), (R_E1, e1.astype(F32)), (R_G0, gate0), (R_G1, gate1),
                    (R_RANK0, rank0), (R_RANK1, rank1)):
        rec = jnp.where(lane == ln, val, rec)
    route_ref[...] = rec


def _mix_out(x2, da2, gl2, ln_g, ln_b, w_o, ln1_g, ln1_b, wr_hi, wr_lo, b_r, alpha, tm):
    T = x2.shape[0]
    row = lambda n: pl.BlockSpec((tm, n), lambda i: (i, 0))
    full = lambda a: pl.BlockSpec(a.shape, lambda i: (0,) * a.ndim)
    lower = jnp.tril(jnp.ones((tm, tm), BF16), -1)
    return pl.pallas_call(
        functools.partial(_mix_out_kernel, alpha),
        grid=(T // tm,),
        in_specs=[row(D_MODEL), row(DA_V), row(GLA_V), full(ln_g), full(ln_b), full(w_o),
                  full(ln1_g), full(ln1_b), full(wr_hi), full(wr_lo), full(b_r), full(lower)],
        out_specs=[row(D_MODEL), row(LANES), pl.BlockSpec((8, LANES), lambda i: (0, 0))],
        out_shape=[jax.ShapeDtypeStruct((T, D_MODEL), F32), jax.ShapeDtypeStruct((T, LANES), F32),
                   jax.ShapeDtypeStruct((8, LANES), F32)],
        compiler_params=pltpu.CompilerParams(dimension_semantics=("arbitrary",),
                                             vmem_limit_bytes=VMEM_LIMIT),
        name="mix_out",
    )(x2, da2, gl2, ln_g, ln_b, w_o, ln1_g, ln1_b, wr_hi, wr_lo, b_r, lower)


HIGH_HALF = 0xFFFF0000


def _pack_pairs(val):
    bits = lambda a: lax.bitcast_convert_type(a.astype(BF16).astype(F32), jnp.uint32)
    half = val.shape[1] // 2
    return (bits(val[:, :half]) >> 16) | (bits(val[:, half:]) & jnp.uint32(HIGH_HALF))


def _unpack_pairs(words):
    lo = lax.bitcast_convert_type(words << 16, F32)
    hi = lax.bitcast_convert_type(words & jnp.uint32(HIGH_HALF), F32)
    return jnp.concatenate([lo, hi], axis=1)


def _words_to_tiles(dst_ref, words):
    n = words.shape[0]
    for s in range(ROW_SUB):
        dst_ref[pl.ds(s, n, stride=ROW_SUB), :] = words[:, s * LANES:(s + 1) * LANES]


def _tiles_to_words(src_ref, r0, n):
    return jnp.concatenate([src_ref[pl.ds(r0 * ROW_SUB + s, n, stride=ROW_SUB), :] for s in range(ROW_SUB)],
                           axis=1)


def _rows_to_tiles(dst_ref, val):
    _words_to_tiles(dst_ref, _pack_pairs(val))


def _tiles_to_rows(src_ref, r0, n):
    return _unpack_pairs(_tiles_to_words(src_ref, r0, n))


def _row_tile(ref, r):
    return ref.at[pl.ds(pl.multiple_of(r * ROW_SUB, ROW_SUB), ROW_SUB), :]


def _dispatch_kernel(dest_ref, h_ref, xs_ref, stage, sems):
    tm = h_ref.shape[0]
    step = pl.program_id(0)
    half = step & 1
    base = step * (tm * TOP_K)
    n_iter = tm * TOP_K // DMA_UNROLL
    _rows_to_tiles(stage.at[half], h_ref[...])

    def row_copy(hf, t, slot):
        return pltpu.make_async_copy(_row_tile(stage.at[hf], t), _row_tile(xs_ref, slot), sems.at[hf])

    def start(i, c):
        for u in range(DMA_UNROLL):
            t = i * (DMA_UNROLL // TOP_K) + u // TOP_K
            row_copy(half, t, dest_ref[base + i * DMA_UNROLL + u]).start(priority=u % 2)
        return c

    lax.fori_loop(0, n_iter, start, 0)

    def drain(hf):
        def wait(i, c):
            for u in range(DMA_UNROLL):
                row_copy(hf, 0, 0).wait()
            return c
        lax.fori_loop(0, n_iter, wait, 0)

    @pl.when(step > 0)
    def _():
        drain(1 - half)

    @pl.when(step == pl.num_programs(0) - 1)
    def _():
        drain(half)
        n_pad = EXPERT_CHUNK * ROW_SUB
        stage[0, 0:n_pad, :] = jnp.zeros((n_pad, LANES), stage.dtype)
        pad = pltpu.make_async_copy(stage.at[0, 0:n_pad, :],
                                    xs_ref.at[pl.ds(pl.num_programs(0) * tm * TOP_K * ROW_SUB, n_pad), :], sems.at[0])
        pad.start()
        pad.wait()


def _dispatch(dest_flat, h, tm):
    T = h.shape[0]
    return pl.pallas_call(
        _dispatch_kernel,
        grid_spec=pltpu.PrefetchScalarGridSpec(
            num_scalar_prefetch=1,
            grid=(T // tm,),
            in_specs=[pl.BlockSpec((tm, D_MODEL), lambda i, d: (i, 0))],
            out_specs=pl.BlockSpec(memory_space=pl.ANY),
            scratch_shapes=[pltpu.VMEM((2, tm * ROW_SUB, LANES), jnp.uint32), pltpu.SemaphoreType.DMA((2,))]),
        out_shape=jax.ShapeDtypeStruct(((T * TOP_K + EXPERT_CHUNK) * ROW_SUB, LANES), jnp.uint32),
        compiler_params=pltpu.CompilerParams(dimension_semantics=("arbitrary",),
                                             vmem_limit_bytes=VMEM_LIMIT),
        name="dispatch",
    )(dest_flat, h)


def _experts_kernel(row0_ref, first_ref, xs_ref, wg_ref, wu_ref, wd_ref, ys_ref,
                    wgu_b, wd_b, xbuf, ybuf, xsem, ysem, pend_ref, *, n_rows):
    e = pl.program_id(0)
    ch = EXPERT_CHUNK
    depth = EXPERT_XBUFS - 1
    g_lo = first_ref[e]
    g_hi = first_ref[e + 1]
    total = first_ref[N_EXPERTS]

    def slab(ref, row0):
        return ref.at[pl.ds(pl.multiple_of(row0 * ROW_SUB, ROW_SUB), ch * ROW_SUB), :]

    def x_copy(g):
        slot = g & (EXPERT_XBUFS - 1)
        return pltpu.make_async_copy(slab(xs_ref, row0_ref[g]), xbuf.at[slot], xsem.at[slot])

    def y_copy(row0, half):
        return pltpu.make_async_copy(ybuf.at[half], slab(ys_ref, row0), ysem.at[half])

    def drain_y(half):
        @pl.when(pend_ref[half] == 1)
        def _():
            y_copy(0, half).wait()
            pend_ref[half] = 0

    @pl.when(e == 0)
    def _():
        for b in range(EXPERT_YBUFS):
            pend_ref[b] = 0
        ybuf[0] = jnp.zeros(ybuf.shape[1:], ybuf.dtype)
        y_copy(n_rows, 0).start()
        y_copy(n_rows, 0).wait()
        for d in range(depth):
            @pl.when(d < total)
            def _():
                x_copy(d).start(priority=ROW_STREAM_PRIORITY)

    @pl.when(g_hi > g_lo)
    def _():
        wgu_b[:, 0:D_EXPERT] = wg_ref[0].astype(BF16)
        wgu_b[:, D_EXPERT:] = wu_ref[0].astype(BF16)
        wd_b[...] = wd_ref[0].astype(BF16)

        def chunk(g, c):
            half = g & (EXPERT_YBUFS - 1)
            x_copy(g).wait()

            @pl.when(g + depth < total)
            def _():
                x_copy(g + depth).start(priority=ROW_STREAM_PRIORITY)

            xb = _tiles_to_rows(xbuf.at[g & (EXPERT_XBUFS - 1)], 0, ch).astype(BF16)
            gu = _dot(xb, wgu_b[...])
            gate = gu[:, 0:D_EXPERT]
            mid = (gate * jax.nn.sigmoid(gate) * gu[:, D_EXPERT:]).astype(BF16)
            words = _pack_pairs(_dot(mid, wd_b[...]))

            drain_y(half)

            @pl.when(g == g_lo)
            def _():
                drain_y((g - 1) & (EXPERT_YBUFS - 1))

            _words_to_tiles(ybuf.at[half], words)
            y_copy(row0_ref[g], half).start(priority=ROW_STREAM_PRIORITY)
            pend_ref[half] = 1
            return c

        lax.fori_loop(g_lo, g_hi, chunk, 0)

    @pl.when(e == pl.num_programs(0) - 1)
    def _():
        for b in range(EXPERT_YBUFS):
            drain_y(b)


def _chunk_metadata(seg_start, counts, n_rows):
    ch = EXPERT_CHUNK
    max_chunks = n_rows // ch + N_EXPERTS
    n_ch = (counts + (ch - 1)) // ch
    first = jnp.concatenate([jnp.zeros((1,), jnp.int32), jnp.cumsum(n_ch).astype(jnp.int32)])
    g = jnp.arange(max_chunks, dtype=jnp.int32)
    owner = jnp.minimum(jnp.sum((first[None, 1:] <= g[:, None]).astype(jnp.int32), axis=1), N_EXPERTS - 1)
    onehot = owner[:, None] == jnp.arange(N_EXPERTS, dtype=jnp.int32)
    pick = lambda tab: jnp.sum(jnp.where(onehot, tab[None, :], 0), axis=1)
    row0 = pick(seg_start) + (g - pick(first[:-1])) * ch
    row0 = jnp.where(g < first[-1], row0, 0)
    return row0.astype(jnp.int32), first


def _experts(seg_start, counts, xs, w_gate, w_up, w_down):
    n_rows = xs.shape[0] // ROW_SUB - EXPERT_CHUNK
    row0, first = _chunk_metadata(seg_start, counts, n_rows)
    per_expert = lambda shape: pl.BlockSpec((1,) + shape, lambda e, r, f: (e, 0, 0))
    slab = (EXPERT_CHUNK * ROW_SUB, LANES)
    return pl.pallas_call(
        functools.partial(_experts_kernel, n_rows=n_rows),
        grid_spec=pltpu.PrefetchScalarGridSpec(
            num_scalar_prefetch=2,
            grid=(N_EXPERTS,),
            in_specs=[pl.BlockSpec(memory_space=pl.ANY),
                      per_expert((D_MODEL, D_EXPERT)), per_expert((D_MODEL, D_EXPERT)),
                      per_expert((D_EXPERT, D_MODEL))],
            out_specs=pl.BlockSpec(memory_space=pl.ANY),
            scratch_shapes=[pltpu.VMEM((D_MODEL, 2 * D_EXPERT), BF16), pltpu.VMEM((D_EXPERT, D_MODEL), BF16),
                            pltpu.VMEM((EXPERT_XBUFS,) + slab, jnp.uint32), pltpu.VMEM((EXPERT_YBUFS,) + slab, jnp.uint32),
                            pltpu.SemaphoreType.DMA((EXPERT_XBUFS,)), pltpu.SemaphoreType.DMA((EXPERT_YBUFS,)),
                            pltpu.SMEM((EXPERT_YBUFS,), jnp.int32)]),
        out_shape=jax.ShapeDtypeStruct(xs.shape, jnp.uint32),
        compiler_params=pltpu.CompilerParams(dimension_semantics=("arbitrary",),
                                             vmem_limit_bytes=VMEM_LIMIT),
        name="experts",
    )(row0, first, xs, w_gate, w_up, w_down)


def _combine_kernel(alpha, dest_ref, h_ref, route_ref, g_ref, b_ref, y_ref, o_ref, buf, sems):
    tm = h_ref.shape[0]
    step = pl.program_id(0)
    half = step & 1
    n_iter = tm * TOP_K // DMA_UNROLL

    def row_copy(hf, src, slot):
        return pltpu.make_async_copy(_row_tile(y_ref, src), _row_tile(buf.at[hf], slot), sems.at[hf])

    def gather(st, hf):
        base = st * (tm * TOP_K)

        def start(i, c):
            for u in range(DMA_UNROLL):
                slot = (u % TOP_K) * tm + i * (DMA_UNROLL // TOP_K) + u // TOP_K
                row_copy(hf, dest_ref[base + i * DMA_UNROLL + u], slot).start(priority=u % 2)
            return c

        lax.fori_loop(0, n_iter, start, 0)

    @pl.when(step == 0)
    def _():
        gather(0, 0)

    @pl.when(step + 1 < pl.num_programs(0))
    def _():
        gather(step + 1, 1 - half)

    def wait(i, c):
        for u in range(DMA_UNROLL):
            row_copy(half, 0, 0).wait()
        return c

    lax.fori_loop(0, n_iter, wait, 0)

    rec = route_ref[...]
    cur = buf.at[half]
    ffn = (rec[:, R_G0:R_G0 + 1] * _tiles_to_rows(cur, 0, tm)
           + rec[:, R_G1:R_G1 + 1] * _tiles_to_rows(cur, tm, tm))
    o_ref[...] = _layer_norm(alpha * h_ref[...] + ffn, g_ref[...], b_ref[...])


def _combine(dest_flat, h, route, ln2_g, ln2_b, y_sorted, alpha, tm):
    T = h.shape[0]
    return pl.pallas_call(
        functools.partial(_combine_kernel, alpha),
        grid_spec=pltpu.PrefetchScalarGridSpec(
            num_scalar_prefetch=1,
            grid=(T // tm,),
            in_specs=[pl.BlockSpec((tm, D_MODEL), lambda i, d: (i, 0)),
                      pl.BlockSpec((tm, LANES), lambda i, d: (i, 0)),
                      pl.BlockSpec((1, D_MODEL), lambda i, d: (0, 0)),
                      pl.BlockSpec((1, D_MODEL), lambda i, d: (0, 0)),
                      pl.BlockSpec(memory_space=pl.ANY)],
            out_specs=pl.BlockSpec((tm, D_MODEL), lambda i, d: (i, 0)),
            scratch_shapes=[pltpu.VMEM((2, TOP_K * tm * ROW_SUB, LANES), jnp.uint32), pltpu.SemaphoreType.DMA((2,))]),
        out_shape=jax.ShapeDtypeStruct((T, D_MODEL), F32),
        compiler_params=pltpu.CompilerParams(dimension_semantics=("arbitrary",),
                                             vmem_limit_bytes=VMEM_LIMIT),
        name="combine",
    )(dest_flat, h, route, ln2_g, ln2_b, y_sorted)


def kernel(x, positions, ln_in_g, ln_in_b, w_in, lam_q1, lam_k1, lam_q2, lam_k2, da_subln_g, gla_w_gate2, gla_b_gate2, gla_norm_g, w_o, ln1_g, ln1_b, router_w_group, router_b_group, router_w_expert, router_b_expert, w_gate, w_up, w_down, ln2_g, ln2_b):
    B, S, D = x.shape
    T = B * S
    depth = w_in.shape[0]
    assert depth == 1, "only a single layer is supported"
    alpha = (2 * depth) ** 0.25
    row2 = lambda a: a.reshape(1, -1)

    inv_freq = ROPE_THETA ** (-jnp.arange(0, DA_HEAD_DIM, 2, dtype=F32) / DA_HEAD_DIM)
    inv_freq = jnp.tile(jnp.repeat(inv_freq, 2), LANES // DA_HEAD_DIM).reshape(1, LANES)
    pos2 = positions.reshape(T, 1)

    cur = x.reshape(T, D)
    cur_g, cur_b = row2(ln_in_g), row2(ln_in_b)
    for l in range(depth):
        w = w_in[l]
        w_main = w[:, :D_MAIN].astype(BF16)
        w_glow = jnp.pad(w[:, D_MAIN:], ((0, 0), (0, LANES - GLA_GATE_RANK))).astype(BF16)
        w_gate2 = jnp.pad(gla_w_gate2[l], ((0, LANES - GLA_GATE_RANK), (0, 0))).astype(BF16)

        q, k, v, gq, gk, gv, go, la = _in_proj(cur, pos2, cur_g, cur_b, inv_freq, w_main, w_glow,
                                               w_gate2, row2(gla_b_gate2[l]), tm=512)
        lam_init = 0.8 - 0.6 * math.exp(-0.3 * l)
        sh = lambda a: a.reshape(B, S, a.shape[-1])
        da = _diff_attn(sh(q), sh(k), sh(v), row2(lam_q1[l]), row2(lam_k1[l]), row2(lam_q2[l]),
                        row2(lam_k2[l]), row2(da_subln_g[l]), lam_init)
        gl = _gla(sh(gq), sh(gk), sh(la), sh(gv), sh(go), row2(gla_norm_g[l]))

        w_r = jnp.zeros((D, LANES), F32)
        w_r = w_r.at[:, :N_GROUPS].set(router_w_group[l])
        w_r = w_r.at[:, EXPERT_LANE0:EXPERT_LANE0 + N_EXPERTS].set(router_w_expert[l])
        b_r = jnp.zeros((1, LANES), F32)
        b_r = b_r.at[0, :N_GROUPS].set(router_b_group[l])
        b_r = b_r.at[0, EXPERT_LANE0:EXPERT_LANE0 + N_EXPERTS].set(router_b_expert[l])
        wr_hi = w_r.astype(BF16)
        wr_lo = (w_r - wr_hi.astype(F32)).astype(BF16)

        h, route, cnt = _mix_out(cur, da.reshape(T, DA_V), gl.reshape(T, GLA_V), cur_g, cur_b,
                                 w_o[l].astype(BF16), row2(ln1_g[l]), row2(ln1_b[l]), wr_hi, wr_lo, b_r,
                                 alpha, tm=512)

        counts = cnt[0, :N_EXPERTS].astype(jnp.int32)
        seg_start = jnp.cumsum(counts) - counts
        eid = route[:, R_E0:R_E1 + 1].astype(jnp.int32)
        rank = route[:, R_RANK0:R_RANK1 + 1].astype(jnp.int32)
        onehot = eid[..., None] == jnp.arange(N_EXPERTS, dtype=jnp.int32)
        dest = jnp.sum(jnp.where(onehot, seg_start, 0), axis=-1) + rank
        dest_flat = dest.reshape(T * TOP_K)

        xs = _dispatch(dest_flat, h, tm=256)
        ys = _experts(seg_start, counts, xs, w_gate[l], w_up[l], w_down[l])
        cur = _combine(dest_flat, h, route, row2(ln2_g[l]), row2(ln2_b[l]), ys, alpha, tm=256)
    return cur.reshape(B, S, D)
```

```python
import functools
import math

import jax
import jax.numpy as jnp
from jax import lax
from jax.experimental import pallas as pl
from jax.experimental.pallas import tpu as pltpu

F32 = jnp.float32
BF16 = jnp.bfloat16

D_MODEL = 1024
CHUNK = 64
ROPE_THETA = 10000.0
LN_EPS = 1e-5
LOG2_E = math.log2(math.e)

DA_HEADS = 4
DA_V_DIM = D_MODEL // (2 * DA_HEADS)
DA_HEAD_DIM = DA_V_DIM // 2
GLA_HEADS = 4
GLA_V_DIM = D_MODEL // (2 * GLA_HEADS)
GLA_KEY_DIM = GLA_V_DIM // 2
GLA_GATE_RANK = 16
GLA_GATE_NORMALIZER = 16.0

DA_Q = DA_HEADS * 2 * DA_HEAD_DIM
DA_K = DA_Q
DA_V = DA_HEADS * DA_V_DIM
GLA_Q = GLA_HEADS * GLA_KEY_DIM
GLA_K = GLA_Q
GLA_V = GLA_HEADS * GLA_V_DIM
GLA_OG = GLA_V
D_MAIN = DA_Q + DA_K + DA_V + GLA_Q + GLA_K + GLA_V + GLA_OG

N_GROUPS = 4
EXPERTS_PER_GROUP = 8
N_EXPERTS = N_GROUPS * EXPERTS_PER_GROUP
TOP_K = 2
D_EXPERT = D_MODEL // 2

LANES = 128
ROW_SUB = D_MODEL // (2 * LANES)
ROW_BLOCK = 256
STAGGER_LAG = 0.5
EXPERT_CHUNK = 128
EXPERT_WBUFS = 3
EXPERT_WSPLIT = 2
EXPERT_XBUFS = 8
EXPERT_YBUFS = 4
ROW_STREAM_PRIORITY = 1
GLA_BLOCK = 256
ATTN_BLOCK = 256
ATTN_PV_KEYS = 512
DMA_UNROLL = 8
VMEM_LIMIT = 48 * 1024 * 1024

R_E0, R_E1, R_G0, R_G1, R_RANK0, R_RANK1 = 0, 1, 2, 3, 4, 5
EXPERT_LANE0 = 32


def _layer_norm(x, g, b):
    mu = jnp.mean(x, axis=-1, keepdims=True)
    xc = x - mu
    var = jnp.mean(xc * xc, axis=-1, keepdims=True)
    return xc * lax.rsqrt(var + LN_EPS) * g + b


def _dot(a, b):
    return jnp.dot(a, b, preferred_element_type=F32)


def _dot_nt(a, b):
    return lax.dot_general(a, b, (((1,), (1,)), ((), ())), preferred_element_type=F32)


def _dot_tn(a, b):
    return lax.dot_general(a, b, (((0,), (0,)), ((), ())), preferred_element_type=F32)


def _run_staggered(step_lists, lag=STAGGER_LAG):
    merged = sorted((b * lag + (i + 0.5) / len(steps), b, i, step)
                    for b, steps in enumerate(step_lists) for i, step in enumerate(steps))
    for _, _, _, step in merged:
        step()


def _in_proj_kernel(x_ref, pos_ref, g_ref, b_ref, invf_ref, w_ref, wgl_ref, wg2_ref, bg2_ref,
                    q_ref, k_ref, v_ref, gq_ref, gk_ref, gv_ref, go_ref, la_ref):
    tm = x_ref.shape[0]
    lane = lax.broadcasted_iota(jnp.int32, (ROW_BLOCK, LANES), 1)
    first = (lane & 1) == 0

    def block_steps(r0):
        rows = slice(r0, r0 + ROW_BLOCK)
        st = {}

        def norm():
            st["xb"] = _layer_norm(x_ref[rows, :], g_ref[...], b_ref[...]).astype(BF16)

        def tables():
            ang = pos_ref[rows, :].astype(F32) * invf_ref[...]
            s = jnp.sin(ang)
            st["c"] = jnp.cos(ang)
            st["s_lo"] = jnp.where(first, -s, 0.0)
            st["s_hi"] = jnp.where(first, 0.0, s)

        def rotary(col0, out_ref, scale):
            t = _dot(st["xb"], w_ref[:, col0:col0 + DA_Q])
            out = []
            for j in range(DA_Q // LANES):
                tj = t[:, j * LANES:(j + 1) * LANES]
                up = pltpu.roll(tj, LANES - 1, 1)
                dn = pltpu.roll(tj, 1, 1)
                out.append(tj * st["c"] + up * st["s_lo"] + dn * st["s_hi"])
            out_ref[rows, :] = (jnp.concatenate(out, axis=1) * scale).astype(BF16)

        def plain(col0, out_ref):
            n = out_ref.shape[1]
            out_ref[rows, :] = _dot(st["xb"], w_ref[:, col0:col0 + n]).astype(BF16)

        def decay_gate():
            g_low = _dot(st["xb"], wgl_ref[...])
            z = _dot(g_low.astype(BF16), wg2_ref[...]) + bg2_ref[...]
            log_sig = jnp.minimum(z, 0.0) - jnp.log1p(jnp.exp(-jnp.abs(z)))
            la_ref[rows, :] = log_sig / GLA_GATE_NORMALIZER

        steps = [norm, tables, functools.partial(rotary, 0, q_ref, DA_HEAD_DIM ** -0.5 * LOG2_E),
                 functools.partial(rotary, DA_Q, k_ref, 1.0)]
        col0 = DA_Q + DA_K
        for out_ref in (v_ref, gq_ref, gk_ref, gv_ref, go_ref):
            steps.append(functools.partial(plain, col0, out_ref))
            col0 += out_ref.shape[1]
        return steps + [decay_gate]

    _run_staggered([block_steps(r0) for r0 in range(0, tm, ROW_BLOCK)])


def _in_proj(x2, pos2, ln_g, ln_b, inv_freq, w_main, w_glow, w_gate2, b_gate2, tm):
    T = x2.shape[0]
    row = lambda n: pl.BlockSpec((tm, n), lambda i: (i, 0))
    full = lambda a: pl.BlockSpec(a.shape, lambda i: (0,) * a.ndim)
    out_shape = [jax.ShapeDtypeStruct((T, n), dt) for n, dt in (
        (DA_Q, BF16), (DA_K, BF16), (DA_V, BF16), (GLA_Q, BF16), (GLA_K, BF16),
        (GLA_V, BF16), (GLA_OG, BF16), (GLA_K, F32))]
    return pl.pallas_call(
        _in_proj_kernel,
        grid=(T // tm,),
        in_specs=[row(D_MODEL), row(1), full(ln_g), full(ln_b), full(inv_freq), full(w_main),
                  full(w_glow), full(w_gate2), full(b_gate2)],
        out_specs=[row(s.shape[1]) for s in out_shape],
        out_shape=out_shape,
        compiler_params=pltpu.CompilerParams(dimension_semantics=("arbitrary",),
                                             vmem_limit_bytes=VMEM_LIMIT),
        name="in_proj",
    )(x2, pos2, ln_g, ln_b, inv_freq, w_main, w_glow, w_gate2, b_gate2)


def _diff_attn_kernel(lam_init, lq1_ref, lk1_ref, lq2_ref, lk2_ref, g_ref, q_ref, k_ref, v_ref, o_ref,
                      s_scr, p_scr, v_ext):
    S = q_ref.shape[1]
    tq = ATTN_BLOCK
    lam = (jnp.exp(jnp.sum(lq1_ref[...] * lk1_ref[...], axis=-1, keepdims=True))
           - jnp.exp(jnp.sum(lq2_ref[...] * lk2_ref[...], axis=-1, keepdims=True)) + lam_init)
    lane = lax.broadcasted_iota(jnp.int32, (tq, LANES), 1)
    rq = lax.broadcasted_iota(jnp.int32, (2 * tq, tq), 0) % tq // CHUNK
    ck = lax.broadcasted_iota(jnp.int32, (2 * tq, tq), 1) // CHUNK
    diag_mask = ck <= rq

    n_blk = S // tq
    st = [dict() for _ in range(n_blk)]
    v_ext[:, 0:DA_V_DIM] = v_ref[0]
    ext_lane = lax.broadcasted_iota(jnp.int32, (S, DA_V_DIM), 1)
    v_ext[:, DA_V_DIM:] = jnp.where(ext_lane == 0, 1.0, 0.0).astype(BF16)

    def stage_a(qi):
        s_buf = s_scr.at[qi % 2]

        def begin():
            q = q_ref[0, qi * tq:(qi + 1) * tq, :]
            zero = jnp.zeros_like(q)
            st[qi]["qq"] = jnp.concatenate([jnp.where(lane < DA_HEAD_DIM, q, zero),
                                            jnp.where(lane >= DA_HEAD_DIM, q, zero)], axis=0)
            st[qi]["m_acc"] = None

        def tile(j):
            s = _dot_nt(st[qi]["qq"], k_ref[0, j * tq:(j + 1) * tq, :])
            if j == qi:
                s = jnp.where(diag_mask, s, -jnp.inf)
            s_buf[:, j * tq:(j + 1) * tq] = s
            m_acc = st[qi]["m_acc"]
            for c0 in range(0, tq, LANES):
                sc = s[:, c0:c0 + LANES]
                m_acc = sc if m_acc is None else jnp.maximum(m_acc, sc)
            st[qi]["m_acc"] = m_acc

        def end():
            st[qi]["m"] = jnp.broadcast_to(jnp.max(st[qi]["m_acc"], axis=-1, keepdims=True), (2 * tq, LANES))

        return [begin] + [functools.partial(tile, j) for j in range(qi + 1)] + [end]

    def stage_b(qi):
        s_buf = s_scr.at[qi % 2]
        p_buf = p_scr.at[qi % 2]

        def cols(c0):
            p_buf[:, c0:c0 + LANES] = jnp.exp2((s_buf[:, c0:c0 + LANES] - st[qi]["m"]).astype(BF16))

        return [functools.partial(cols, c0) for c0 in range(0, (qi + 1) * tq, LANES)]

    def stage_c(qi):
        p_buf = p_scr.at[qi % 2]

        nk = (qi + 1) * tq
        st[qi]["a"] = None

        def part(k0):
            k1 = min(k0 + ATTN_PV_KEYS, nk)
            a = _dot(p_buf[:, k0:k1], v_ext[k0:k1, :])
            st[qi]["a"] = a if st[qi]["a"] is None else st[qi]["a"] + a

        def finish():
            a = st[qi]["a"][:, 0:DA_V_DIM] / st[qi]["a"][:, DA_V_DIM:DA_V_DIM + 1]
            o = a[0:tq] - lam * a[tq:2 * tq]
            o = o * lax.rsqrt(jnp.mean(o * o, axis=-1, keepdims=True) + LN_EPS) * g_ref[...]
            o_ref[0, qi * tq:(qi + 1) * tq, :] = (o * (1.0 - lam_init)).astype(o_ref.dtype)

        return [functools.partial(part, k0) for k0 in range(0, nk, ATTN_PV_KEYS)] + [finish]

    for t in range(n_blk + 2):
        stages = []
        if t < n_blk:
            stages.append(stage_a(t))
        if 0 <= t - 1 < n_blk:
            stages.append(stage_b(t - 1))
        if 0 <= t - 2 < n_blk:
            stages.append(stage_c(t - 2))
        merged = sorted(((i + 0.5) / len(ops), k, i, op) for k, ops in enumerate(stages) for i, op in enumerate(ops))
        for _, _, _, op in merged:
            op()


def _diff_attn(q, k, v, lam_q1, lam_k1, lam_q2, lam_k2, subln_g, lam_init):
    B, S, _ = q.shape
    vec = pl.BlockSpec((1, DA_HEAD_DIM), lambda b, h: (0, 0))
    seq = pl.BlockSpec((1, S, LANES), lambda b, h: (b, 0, h))
    return pl.pallas_call(
        functools.partial(_diff_attn_kernel, lam_init),
        grid=(B, DA_HEADS),
        in_specs=[vec, vec, vec, vec, pl.BlockSpec((1, DA_V_DIM), lambda b, h: (0, 0)), seq, seq, seq],
        out_specs=seq,
        out_shape=jax.ShapeDtypeStruct((B, S, DA_V), BF16),
        scratch_shapes=[pltpu.VMEM((2, 2 * ATTN_BLOCK, S), F32), pltpu.VMEM((2, 2 * ATTN_BLOCK, S), BF16),
                        pltpu.VMEM((S, 2 * DA_V_DIM), BF16)],
        compiler_params=pltpu.CompilerParams(dimension_semantics=("arbitrary",) * 2,
                                             vmem_limit_bytes=VMEM_LIMIT),
        name="diff_attn",
    )(lam_q1, lam_k1, lam_q2, lam_k2, subln_g, q, k, v)


def _gla_kernel(q_ref, k_ref, la_ref, v_ref, go_ref, ng_ref, o_ref, qt_s, oi_s, ds_s, dec_s):
    S = q_ref.shape[1]
    C = CHUNK
    BLK = GLA_BLOCK
    per_blk = BLK // C
    r = lax.broadcasted_iota(jnp.int32, (BLK, BLK), 0)
    c = lax.broadcasted_iota(jnp.int32, (BLK, BLK), 1)
    chunk_causal = (r // C == c // C) & (c <= r)
    tri = jnp.where(chunk_causal, 1.0, 0.0).astype(BF16)
    lane = lax.broadcasted_iota(jnp.int32, (BLK, LANES), 1)
    head_lanes = (lane < GLA_KEY_DIM, lane >= GLA_KEY_DIM)
    st_row = lax.broadcasted_iota(jnp.int32, (2 * GLA_V_DIM, LANES), 0)
    st_lane = lax.broadcasted_iota(jnp.int32, (2 * GLA_V_DIM, LANES), 1)
    own_keys = (st_row < GLA_V_DIM) == (st_lane < GLA_KEY_DIM)

    for b in range(S // BLK):
        r0 = b * BLK
        g = la_ref[0, r0:r0 + BLK, :]
        g1 = g.astype(BF16)
        e1 = g - g1.astype(F32)
        g2 = e1.astype(BF16)
        g3 = (e1 - g2.astype(F32)).astype(BF16)
        bcum = _dot(tri, g1) + _dot(tri, g2) + _dot(tri, g3)
        b_last = jnp.concatenate(
            [jnp.broadcast_to(bcum[i * C + C - 1:i * C + C, :], (C, LANES)) for i in range(per_blk)], axis=0)
        qf = q_ref[0, r0:r0 + BLK, :].astype(F32) * (GLA_KEY_DIM ** -0.5)
        kf = k_ref[0, r0:r0 + BLK, :].astype(F32)
        q_t = (qf * jnp.exp(bcum)).astype(BF16)
        k_t = (kf * jnp.exp(-bcum)).astype(BF16)
        k_end = (kf * jnp.exp(b_last - bcum)).astype(BF16)
        decay = jnp.exp(b_last)
        qt_s[r0:r0 + BLK, :] = q_t
        zero = jnp.zeros_like(q_t)
        for hh in range(2):
            att = jnp.where(chunk_causal, _dot_nt(jnp.where(head_lanes[hh], q_t, zero), k_t), 0.0).astype(BF16)
            oi_s[r0:r0 + BLK, hh * GLA_V_DIM:(hh + 1) * GLA_V_DIM] = _dot(
                att, v_ref[0, r0:r0 + BLK, hh * GLA_V_DIM:(hh + 1) * GLA_V_DIM])
        for i in range(per_blk):
            n = b * per_blk + i
            rows = slice(r0 + i * C, r0 + (i + 1) * C)
            inc = _dot_tn(v_ref[0, rows, :], k_end[i * C:(i + 1) * C, :])
            ds_s[n] = jnp.where(own_keys, inc, 0.0)
            dec_s[n:n + 1, :] = decay[i * C:i * C + 1, :]

    state = jnp.zeros((2 * GLA_V_DIM, LANES), F32)
    for n in range(S // C):
        rows = slice(n * C, (n + 1) * C)
        o = oi_s[rows, :] + _dot_nt(qt_s[rows, :], state.astype(BF16))
        state = state * dec_s[n:n + 1, :] + ds_s[n]
        for hh in range(2):
            cols = slice(hh * GLA_V_DIM, (hh + 1) * GLA_V_DIM)
            oh = o[:, cols]
            oh = oh * lax.rsqrt(jnp.mean(oh * oh, axis=-1, keepdims=True) + LN_EPS) * ng_ref[...]
            gate = go_ref[0, rows, cols].astype(F32)
            o_ref[0, rows, cols] = (oh * (gate * jax.nn.sigmoid(gate))).astype(o_ref.dtype)


def _gla(gq, gk, la, gv, go, norm_g):
    B, S, _ = gq.shape
    pairs = GLA_HEADS // 2
    narrow = pl.BlockSpec((1, S, LANES), lambda b, p: (b, 0, p))
    wide = pl.BlockSpec((1, S, 2 * GLA_V_DIM), lambda b, p: (b, 0, p))
    n_chunks = S // CHUNK
    return pl.pallas_call(
        _gla_kernel,
        grid=(B, pairs),
        in_specs=[narrow, narrow, narrow, wide, wide,
                  pl.BlockSpec((1, GLA_V_DIM), lambda b, p: (0, 0))],
        out_specs=wide,
        out_shape=jax.ShapeDtypeStruct((B, S, GLA_V), BF16),
        scratch_shapes=[pltpu.VMEM((S, LANES), BF16),
                        pltpu.VMEM((S, 2 * GLA_V_DIM), F32),
                        pltpu.VMEM((n_chunks, 2 * GLA_V_DIM, LANES), F32),
                        pltpu.VMEM((n_chunks, LANES), F32)],
        compiler_params=pltpu.CompilerParams(dimension_semantics=("arbitrary",) * 2,
                                             vmem_limit_bytes=VMEM_LIMIT),
        name="gla",
    )(gq, gk, la, gv, go, norm_g)


def _split3(a):
    hi = a.astype(BF16)
    lo = (a - hi.astype(F32)).astype(BF16)
    return hi, lo


def _mix_out_kernel(alpha, x_ref, da_ref, gl_ref, lng_ref, lnb_ref, wo_ref, g1_ref, b1_ref,
                    wr_hi_ref, wr_lo_ref, br_ref, lower_ref, h_ref, route_ref, cnt_ref):
    tm = x_ref.shape[0]
    i = pl.program_id(0)

    @pl.when(i == 0)
    def _():
        cnt_ref[...] = jnp.zeros_like(cnt_ref)

    logit_blocks = []
    for r0 in range(0, tm, ROW_BLOCK):
        rows = slice(r0, r0 + ROW_BLOCK)
        xn = _layer_norm(x_ref[rows, :], lng_ref[...], lnb_ref[...])
        mix = _dot(da_ref[rows, :], wo_ref[0:DA_V, :]) + _dot(gl_ref[rows, :], wo_ref[DA_V:, :])
        h = _layer_norm(alpha * xn + mix, g1_ref[...], b1_ref[...])
        h_ref[rows, :] = h
        h_hi, h_lo = _split3(h)
        logit_blocks.append(_dot(h_hi, wr_hi_ref[...]) + _dot(h_hi, wr_lo_ref[...]) + _dot(h_lo, wr_hi_ref[...])
                            + br_ref[...])
    logits = jnp.concatenate(logit_blocks, axis=0)
    lane = lax.broadcasted_iota(jnp.int32, (tm, LANES), 1)
    neg = -jnp.inf
    big = jnp.int32(LANES)

    def first_argmax(vals, valid):
        v = jnp.where(valid, vals, neg)
        mx = jnp.max(v, axis=-1, keepdims=True)
        idx = jnp.min(jnp.where(valid & (v == mx), lane, big), axis=-1, keepdims=True)
        return mx, idx

    is_group = lane < N_GROUPS
    g_max, g_top = first_argmax(logits, is_group)
    p_g = 1.0 / jnp.sum(jnp.where(is_group, jnp.exp(logits - g_max), 0.0), axis=-1, keepdims=True)

    e_lo = EXPERT_LANE0 + g_top * EXPERTS_PER_GROUP
    in_group = (lane >= e_lo) & (lane < e_lo + EXPERTS_PER_GROUP)
    v0, i0 = first_argmax(logits, in_group)
    v1, i1 = first_argmax(logits, in_group & (lane != i0))
    w1 = jnp.exp(v1 - v0)
    gate0 = p_g / (1.0 + w1)
    gate1 = p_g * w1 / (1.0 + w1)
    e0 = i0 - EXPERT_LANE0
    e1 = i1 - EXPERT_LANE0

    oh0 = jnp.where(lane == e0, 1.0, 0.0)
    oh1 = jnp.where(lane == e1, 1.0, 0.0)
    oh = oh0 + oh1
    before = _dot(lower_ref[...], oh.astype(BF16)) + cnt_ref[0:1, :]
    rank0 = jnp.sum(oh0 * before, axis=-1, keepdims=True)
    rank1 = jnp.sum(oh1 * before, axis=-1, keepdims=True)
    cnt_ref[...] = cnt_ref[...] + jnp.sum(oh, axis=0, keepdims=True)

    rec = jnp.zeros((tm, LANES), F32)
    for ln, val in ((R_E0, e0.astype(F32)), (R_E1, e1.astype(F32)), (R_G0, gate0), (R_G1, gate1),
                    (R_RANK0, rank0), (R_RANK1, rank1)):
        rec = jnp.where(lane == ln, val, rec)
    route_ref[...] = rec


def _mix_out(x2, da2, gl2, ln_g, ln_b, w_o, ln1_g, ln1_b, wr_hi, wr_lo, b_r, alpha, tm):
    T = x2.shape[0]
    row = lambda n: pl.BlockSpec((tm, n), lambda i: (i, 0))
    full = lambda a: pl.BlockSpec(a.shape, lambda i: (0,) * a.ndim)
    lower = jnp.tril(jnp.ones((tm, tm), BF16), -1)
    return pl.pallas_call(
        functools.partial(_mix_out_kernel, alpha),
        grid=(T // tm,),
        in_specs=[row(D_MODEL), row(DA_V), row(GLA_V), full(ln_g), full(ln_b), full(w_o),
                  full(ln1_g), full(ln1_b), full(wr_hi), full(wr_lo), full(b_r), full(lower)],
        out_specs=[row(D_MODEL), row(LANES), pl.BlockSpec((8, LANES), lambda i: (0, 0))],
        out_shape=[jax.ShapeDtypeStruct((T, D_MODEL), F32), jax.ShapeDtypeStruct((T, LANES), F32),
                   jax.ShapeDtypeStruct((8, LANES), F32)],
        compiler_params=pltpu.CompilerParams(dimension_semantics=("arbitrary",),
                                             vmem_limit_bytes=VMEM_LIMIT),
        name="mix_out",
    )(x2, da2, gl2, ln_g, ln_b, w_o, ln1_g, ln1_b, wr_hi, wr_lo, b_r, lower)


HIGH_HALF = 0xFFFF0000


def _pack_pairs(val):
    bits = lambda a: lax.bitcast_convert_type(a.astype(BF16).astype(F32), jnp.uint32)
    half = val.shape[1] // 2
    return (bits(val[:, :half]) >> 16) | (bits(val[:, half:]) & jnp.uint32(HIGH_HALF))


def _unpack_pairs(words):
    lo = lax.bitcast_convert_type(words << 16, F32)
    hi = lax.bitcast_convert_type(words & jnp.uint32(HIGH_HALF), F32)
    return jnp.concatenate([lo, hi], axis=1)


def _words_to_tiles(dst_ref, words):
    n = words.shape[0]
    for s in range(ROW_SUB):
        dst_ref[pl.ds(s, n, stride=ROW_SUB), :] = words[:, s * LANES:(s + 1) * LANES]


def _tiles_to_words(src_ref, r0, n):
    return jnp.concatenate([src_ref[pl.ds(r0 * ROW_SUB + s, n, stride=ROW_SUB), :] for s in range(ROW_SUB)],
                           axis=1)


def _rows_to_tiles(dst_ref, val):
    _words_to_tiles(dst_ref, _pack_pairs(val))


def _tiles_to_rows(src_ref, r0, n):
    return _unpack_pairs(_tiles_to_words(src_ref, r0, n))


def _row_tile(ref, r):
    return ref.at[pl.ds(pl.multiple_of(r * ROW_SUB, ROW_SUB), ROW_SUB), :]


def _dispatch_kernel(dest_ref, h_ref, xs_ref, stage, sems):
    tm = h_ref.shape[0]
    step = pl.program_id(0)
    half = step & 1
    base = step * (tm * TOP_K)
    n_iter = tm * TOP_K // DMA_UNROLL
    _rows_to_tiles(stage.at[half], h_ref[...])

    def row_copy(hf, t, slot):
        return pltpu.make_async_copy(_row_tile(stage.at[hf], t), _row_tile(xs_ref, slot), sems.at[hf])

    def start(i, c):
        for u in range(DMA_UNROLL):
            t = i * (DMA_UNROLL // TOP_K) + u // TOP_K
            row_copy(half, t, dest_ref[base + i * DMA_UNROLL + u]).start(priority=u % 2)
        return c

    lax.fori_loop(0, n_iter, start, 0)

    def drain(hf):
        def wait(i, c):
            for u in range(DMA_UNROLL):
                row_copy(hf, 0, 0).wait()
            return c
        lax.fori_loop(0, n_iter, wait, 0)

    @pl.when(step > 0)
    def _():
        drain(1 - half)

    @pl.when(step == pl.num_programs(0) - 1)
    def _():
        drain(half)
        n_pad = EXPERT_CHUNK * ROW_SUB
        stage[0, 0:n_pad, :] = jnp.zeros((n_pad, LANES), stage.dtype)
        pad = pltpu.make_async_copy(stage.at[0, 0:n_pad, :],
                                    xs_ref.at[pl.ds(pl.num_programs(0) * tm * TOP_K * ROW_SUB, n_pad), :], sems.at[0])
        pad.start()
        pad.wait()


def _dispatch(dest_flat, h, tm):
    T = h.shape[0]
    return pl.pallas_call(
        _dispatch_kernel,
        grid_spec=pltpu.PrefetchScalarGridSpec(
            num_scalar_prefetch=1,
            grid=(T // tm,),
            in_specs=[pl.BlockSpec((tm, D_MODEL), lambda i, d: (i, 0))],
            out_specs=pl.BlockSpec(memory_space=pl.ANY),
            scratch_shapes=[pltpu.VMEM((2, tm * ROW_SUB, LANES), jnp.uint32), pltpu.SemaphoreType.DMA((2,))]),
        out_shape=jax.ShapeDtypeStruct(((T * TOP_K + EXPERT_CHUNK) * ROW_SUB, LANES), jnp.uint32),
        compiler_params=pltpu.CompilerParams(dimension_semantics=("arbitrary",),
                                             vmem_limit_bytes=VMEM_LIMIT),
        name="dispatch",
    )(dest_flat, h)


def _experts_kernel(row0_ref, first_ref, xs_ref, wg_ref, wu_ref, wd_ref, ys_ref,
                    wg_f, wu_f, wd_f, wgu_b, wd_b, xbuf, ybuf, wsem, xsem, ysem, pend_ref, *, n_rows):
    e = pl.program_id(0)
    ch = EXPERT_CHUNK
    depth = EXPERT_XBUFS - 1
    g_lo = first_ref[e]
    g_hi = first_ref[e + 1]
    total = first_ref[N_EXPERTS]

    def slab(ref, row0):
        return ref.at[pl.ds(pl.multiple_of(row0 * ROW_SUB, ROW_SUB), ch * ROW_SUB), :]

    def x_copy(g):
        slot = g & (EXPERT_XBUFS - 1)
        return pltpu.make_async_copy(slab(xs_ref, row0_ref[g]), xbuf.at[slot], xsem.at[slot])

    def y_copy(row0, half):
        return pltpu.make_async_copy(ybuf.at[half], slab(ys_ref, row0), ysem.at[half])

    def drain_y(half):
        @pl.when(pend_ref[half] == 1)
        def _():
            y_copy(0, half).wait()
            pend_ref[half] = 0

    def weight_copies(ex):
        slot = ex % EXPERT_WBUFS
        copies = []
        for src, dst in ((wg_ref, wg_f), (wu_ref, wu_f), (wd_ref, wd_f)):
            n = src.shape[1] // EXPERT_WSPLIT
            for i in range(EXPERT_WSPLIT):
                copies.append(pltpu.make_async_copy(src.at[ex, pl.ds(i * n, n), :],
                                                    dst.at[slot, pl.ds(i * n, n), :], wsem.at[slot]))
        return copies

    def fetch_weights(ex):
        for i, cp in enumerate(weight_copies(ex)):
            cp.start(priority=i % 2)

    @pl.when(e == 0)
    def _():
        for ex in range(EXPERT_WBUFS - 1):
            fetch_weights(ex)

    @pl.when(e + (EXPERT_WBUFS - 1) < pl.num_programs(0))
    def _():
        fetch_weights(e + (EXPERT_WBUFS - 1))

    for cp in weight_copies(e):
        cp.wait()

    @pl.when(e == 0)
    def _():
        for b in range(EXPERT_YBUFS):
            pend_ref[b] = 0
        ybuf[0] = jnp.zeros(ybuf.shape[1:], ybuf.dtype)
        y_copy(n_rows, 0).start()
        y_copy(n_rows, 0).wait()
        for d in range(depth):
            @pl.when(d < total)
            def _():
                x_copy(d).start(priority=ROW_STREAM_PRIORITY)

    @pl.when(g_hi > g_lo)
    def _():
        slot = e % EXPERT_WBUFS
        wgu_b[:, 0:D_EXPERT] = wg_f[slot].astype(BF16)
        wgu_b[:, D_EXPERT:] = wu_f[slot].astype(BF16)
        wd_b[...] = wd_f[slot].astype(BF16)

        def chunk(g, c):
            half = g & (EXPERT_YBUFS - 1)
            x_copy(g).wait()

            @pl.when(g + depth < total)
            def _():
                x_copy(g + depth).start(priority=ROW_STREAM_PRIORITY)

            xb = _tiles_to_rows(xbuf.at[g & (EXPERT_XBUFS - 1)], 0, ch).astype(BF16)
            gu = _dot(xb, wgu_b[...])
            gate = gu[:, 0:D_EXPERT]
            mid = (gate * jax.nn.sigmoid(gate) * gu[:, D_EXPERT:]).astype(BF16)
            words = _pack_pairs(_dot(mid, wd_b[...]))

            drain_y(half)

            @pl.when(g == g_lo)
            def _():
                drain_y((g - 1) & (EXPERT_YBUFS - 1))

            _words_to_tiles(ybuf.at[half], words)
            y_copy(row0_ref[g], half).start(priority=ROW_STREAM_PRIORITY)
            pend_ref[half] = 1
            return c

        lax.fori_loop(g_lo, g_hi, chunk, 0)

    @pl.when(e == pl.num_programs(0) - 1)
    def _():
        for b in range(EXPERT_YBUFS):
            drain_y(b)


def _chunk_metadata(seg_start, counts, n_rows):
    ch = EXPERT_CHUNK
    max_chunks = n_rows // ch + N_EXPERTS
    n_ch = (counts + (ch - 1)) // ch
    first = jnp.concatenate([jnp.zeros((1,), jnp.int32), jnp.cumsum(n_ch).astype(jnp.int32)])
    g = jnp.arange(max_chunks, dtype=jnp.int32)
    owner = jnp.minimum(jnp.sum((first[None, 1:] <= g[:, None]).astype(jnp.int32), axis=1), N_EXPERTS - 1)
    onehot = owner[:, None] == jnp.arange(N_EXPERTS, dtype=jnp.int32)
    pick = lambda tab: jnp.sum(jnp.where(onehot, tab[None, :], 0), axis=1)
    row0 = pick(seg_start) + (g - pick(first[:-1])) * ch
    row0 = jnp.where(g < first[-1], row0, 0)
    return row0.astype(jnp.int32), first


def _experts(seg_start, counts, xs, w_gate, w_up, w_down):
    n_rows = xs.shape[0] // ROW_SUB - EXPERT_CHUNK
    row0, first = _chunk_metadata(seg_start, counts, n_rows)
    slab = (EXPERT_CHUNK * ROW_SUB, LANES)
    return pl.pallas_call(
        functools.partial(_experts_kernel, n_rows=n_rows),
        grid_spec=pltpu.PrefetchScalarGridSpec(
            num_scalar_prefetch=2,
            grid=(N_EXPERTS,),
            in_specs=[pl.BlockSpec(memory_space=pl.ANY)] * 4,
            out_specs=pl.BlockSpec(memory_space=pl.ANY),
            scratch_shapes=[pltpu.VMEM((EXPERT_WBUFS, D_MODEL, D_EXPERT), F32),
                            pltpu.VMEM((EXPERT_WBUFS, D_MODEL, D_EXPERT), F32),
                            pltpu.VMEM((EXPERT_WBUFS, D_EXPERT, D_MODEL), F32),
                            pltpu.VMEM((D_MODEL, 2 * D_EXPERT), BF16), pltpu.VMEM((D_EXPERT, D_MODEL), BF16),
                            pltpu.VMEM((EXPERT_XBUFS,) + slab, jnp.uint32), pltpu.VMEM((EXPERT_YBUFS,) + slab, jnp.uint32),
                            pltpu.SemaphoreType.DMA((EXPERT_WBUFS,)),
                            pltpu.SemaphoreType.DMA((EXPERT_XBUFS,)), pltpu.SemaphoreType.DMA((EXPERT_YBUFS,)),
                            pltpu.SMEM((EXPERT_YBUFS,), jnp.int32)]),
        out_shape=jax.ShapeDtypeStruct(xs.shape, jnp.uint32),
        compiler_params=pltpu.CompilerParams(dimension_semantics=("arbitrary",),
                                             vmem_limit_bytes=VMEM_LIMIT),
        name="experts",
    )(row0, first, xs, w_gate, w_up, w_down)


def _combine_kernel(alpha, dest_ref, h_ref, route_ref, g_ref, b_ref, y_ref, o_ref, buf, sems):
    tm = h_ref.shape[0]
    step = pl.program_id(0)
    half = step & 1
    n_iter = tm * TOP_K // DMA_UNROLL

    def row_copy(hf, src, slot):
        return pltpu.make_async_copy(_row_tile(y_ref, src), _row_tile(buf.at[hf], slot), sems.at[hf])

    def gather(st, hf):
        base = st * (tm * TOP_K)

        def start(i, c):
            for u in range(DMA_UNROLL):
                slot = (u % TOP_K) * tm + i * (DMA_UNROLL // TOP_K) + u // TOP_K
                row_copy(hf, dest_ref[base + i * DMA_UNROLL + u], slot).start(priority=u % 2)
            return c

        lax.fori_loop(0, n_iter, start, 0)

    @pl.when(step == 0)
    def _():
        gather(0, 0)

    @pl.when(step + 1 < pl.num_programs(0))
    def _():
        gather(step + 1, 1 - half)

    def wait(i, c):
        for u in range(DMA_UNROLL):
            row_copy(half, 0, 0).wait()
        return c

    lax.fori_loop(0, n_iter, wait, 0)

    rec = route_ref[...]
    cur = buf.at[half]
    ffn = (rec[:, R_G0:R_G0 + 1] * _tiles_to_rows(cur, 0, tm)
           + rec[:, R_G1:R_G1 + 1] * _tiles_to_rows(cur, tm, tm))
    o_ref[...] = _layer_norm(alpha * h_ref[...] + ffn, g_ref[...], b_ref[...])


def _combine(dest_flat, h, route, ln2_g, ln2_b, y_sorted, alpha, tm):
    T = h.shape[0]
    return pl.pallas_call(
        functools.partial(_combine_kernel, alpha),
        grid_spec=pltpu.PrefetchScalarGridSpec(
            num_scalar_prefetch=1,
            grid=(T // tm,),
            in_specs=[pl.BlockSpec((tm, D_MODEL), lambda i, d: (i, 0)),
                      pl.BlockSpec((tm, LANES), lambda i, d: (i, 0)),
                      pl.BlockSpec((1, D_MODEL), lambda i, d: (0, 0)),
                      pl.BlockSpec((1, D_MODEL), lambda i, d: (0, 0)),
                      pl.BlockSpec(memory_space=pl.ANY)],
            out_specs=pl.BlockSpec((tm, D_MODEL), lambda i, d: (i, 0)),
            scratch_shapes=[pltpu.VMEM((2, TOP_K * tm * ROW_SUB, LANES), jnp.uint32), pltpu.SemaphoreType.DMA((2,))]),
        out_shape=jax.ShapeDtypeStruct((T, D_MODEL), F32),
        compiler_params=pltpu.CompilerParams(dimension_semantics=("arbitrary",),
                                             vmem_limit_bytes=VMEM_LIMIT),
        name="combine",
    )(dest_flat, h, route, ln2_g, ln2_b, y_sorted)


def kernel(x, positions, ln_in_g, ln_in_b, w_in, lam_q1, lam_k1, lam_q2, lam_k2, da_subln_g, gla_w_gate2, gla_b_gate2, gla_norm_g, w_o, ln1_g, ln1_b, router_w_group, router_b_group, router_w_expert, router_b_expert, w_gate, w_up, w_down, ln2_g, ln2_b):
    B, S, D = x.shape
    T = B * S
    depth = w_in.shape[0]
    assert depth == 1, "only a single layer is supported"
    alpha = (2 * depth) ** 0.25
    row2 = lambda a: a.reshape(1, -1)

    inv_freq = ROPE_THETA ** (-jnp.arange(0, DA_HEAD_DIM, 2, dtype=F32) / DA_HEAD_DIM)
    inv_freq = jnp.tile(jnp.repeat(inv_freq, 2), LANES // DA_HEAD_DIM).reshape(1, LANES)
    pos2 = positions.reshape(T, 1)

    cur = x.reshape(T, D)
    cur_g, cur_b = row2(ln_in_g), row2(ln_in_b)
    for l in range(depth):
        w = w_in[l]
        w_main = w[:, :D_MAIN].astype(BF16)
        w_glow = jnp.pad(w[:, D_MAIN:], ((0, 0), (0, LANES - GLA_GATE_RANK))).astype(BF16)
        w_gate2 = jnp.pad(gla_w_gate2[l], ((0, LANES - GLA_GATE_RANK), (0, 0))).astype(BF16)

        q, k, v, gq, gk, gv, go, la = _in_proj(cur, pos2, cur_g, cur_b, inv_freq, w_main, w_glow,
                                               w_gate2, row2(gla_b_gate2[l]), tm=512)
        lam_init = 0.8 - 0.6 * math.exp(-0.3 * l)
        sh = lambda a: a.reshape(B, S, a.shape[-1])
        da = _diff_attn(sh(q), sh(k), sh(v), row2(lam_q1[l]), row2(lam_k1[l]), row2(lam_q2[l]),
                        row2(lam_k2[l]), row2(da_subln_g[l]), lam_init)
        gl = _gla(sh(gq), sh(gk), sh(la), sh(gv), sh(go), row2(gla_norm_g[l]))

        w_r = jnp.zeros((D, LANES), F32)
        w_r = w_r.at[:, :N_GROUPS].set(router_w_group[l])
        w_r = w_r.at[:, EXPERT_LANE0:EXPERT_LANE0 + N_EXPERTS].set(router_w_expert[l])
        b_r = jnp.zeros((1, LANES), F32)
        b_r = b_r.at[0, :N_GROUPS].set(router_b_group[l])
        b_r = b_r.at[0, EXPERT_LANE0:EXPERT_LANE0 + N_EXPERTS].set(router_b_expert[l])
        wr_hi = w_r.astype(BF16)
        wr_lo = (w_r - wr_hi.astype(F32)).astype(BF16)

        h, route, cnt = _mix_out(cur, da.reshape(T, DA_V), gl.reshape(T, GLA_V), cur_g, cur_b,
                                 w_o[l].astype(BF16), row2(ln1_g[l]), row2(ln1_b[l]), wr_hi, wr_lo, b_r,
                                 alpha, tm=512)

        counts = cnt[0, :N_EXPERTS].astype(jnp.int32)
        seg_start = jnp.cumsum(counts) - counts
        eid = route[:, R_E0:R_E1 + 1].astype(jnp.int32)
        rank = route[:, R_RANK0:R_RANK1 + 1].astype(jnp.int32)
        onehot = eid[..., None] == jnp.arange(N_EXPERTS, dtype=jnp.int32)
        dest = jnp.sum(jnp.where(onehot, seg_start, 0), axis=-1) + rank
        dest_flat = dest.reshape(T * TOP_K)

        xs = _dispatch(dest_flat, h, tm=256)
        ys = _experts(seg_start, counts, xs, w_gate[l], w_up[l], w_down[l])
        cur = _combine(dest_flat, h, route, row2(ln2_g[l]), row2(ln2_b[l]), ys, alpha, tm=256)
    return cur.reshape(B, S, D)
```

```python
import functools
import math

import jax
import jax.numpy as jnp
from jax import lax
from jax.experimental import pallas as pl
from jax.experimental.pallas import tpu as pltpu

F32 = jnp.float32
BF16 = jnp.bfloat16

D_MODEL = 1024
CHUNK = 64
ROPE_THETA = 10000.0
LN_EPS = 1e-5
LOG2_E = math.log2(math.e)

DA_HEADS = 4
DA_V_DIM = D_MODEL // (2 * DA_HEADS)
DA_HEAD_DIM = DA_V_DIM // 2
GLA_HEADS = 4
GLA_V_DIM = D_MODEL // (2 * GLA_HEADS)
GLA_KEY_DIM = GLA_V_DIM // 2
GLA_GATE_RANK = 16
GLA_GATE_NORMALIZER = 16.0

DA_Q = DA_HEADS * 2 * DA_HEAD_DIM
DA_K = DA_Q
DA_V = DA_HEADS * DA_V_DIM
GLA_Q = GLA_HEADS * GLA_KEY_DIM
GLA_K = GLA_Q
GLA_V = GLA_HEADS * GLA_V_DIM
GLA_OG = GLA_V
D_MAIN = DA_Q + DA_K + DA_V + GLA_Q + GLA_K + GLA_V + GLA_OG

N_GROUPS = 4
EXPERTS_PER_GROUP = 8
N_EXPERTS = N_GROUPS * EXPERTS_PER_GROUP
TOP_K = 2
D_EXPERT = D_MODEL // 2

LANES = 128
ROW_SUB = D_MODEL // (2 * LANES)
ROW_BLOCK = 256
STAGGER_LAG = 0.5
EXPERT_CHUNK = 128
EXPERT_XBUFS = 8
EXPERT_YBUFS = 4
ROW_STREAM_PRIORITY = 1
GLA_BLOCK = 256
ATTN_BLOCK = 256
ATTN_PV_KEYS = 512
DMA_UNROLL = 8
VMEM_LIMIT = 48 * 1024 * 1024

R_E0, R_E1, R_G0, R_G1, R_RANK0, R_RANK1 = 0, 1, 2, 3, 4, 5
EXPERT_LANE0 = 32


def _layer_norm(x, g, b):
    mu = jnp.mean(x, axis=-1, keepdims=True)
    xc = x - mu
    var = jnp.mean(xc * xc, axis=-1, keepdims=True)
    return xc * lax.rsqrt(var + LN_EPS) * g + b


def _dot(a, b):
    return jnp.dot(a, b, preferred_element_type=F32)


def _dot_nt(a, b):
    return lax.dot_general(a, b, (((1,), (1,)), ((), ())), preferred_element_type=F32)


def _dot_tn(a, b):
    return lax.dot_general(a, b, (((0,), (0,)), ((), ())), preferred_element_type=F32)


def _run_staggered(step_lists, lag=STAGGER_LAG):
    merged = sorted((b * lag + (i + 0.5) / len(steps), b, i, step)
                    for b, steps in enumerate(step_lists) for i, step in enumerate(steps))
    for _, _, _, step in merged:
        step()


def _in_proj_kernel(x_ref, pos_ref, g_ref, b_ref, invf_ref, w_ref, wgl_ref, wg2_ref, bg2_ref,
                    q_ref, k_ref, v_ref, gq_ref, gk_ref, gv_ref, go_ref, la_ref):
    tm = x_ref.shape[0]
    lane = lax.broadcasted_iota(jnp.int32, (ROW_BLOCK, LANES), 1)
    first = (lane & 1) == 0

    def block_steps(r0):
        rows = slice(r0, r0 + ROW_BLOCK)
        st = {}

        def norm():
            st["xb"] = _layer_norm(x_ref[rows, :], g_ref[...], b_ref[...]).astype(BF16)

        def tables():
            ang = pos_ref[rows, :].astype(F32) * invf_ref[...]
            s = jnp.sin(ang)
            st["c"] = jnp.cos(ang)
            st["s_lo"] = jnp.where(first, -s, 0.0)
            st["s_hi"] = jnp.where(first, 0.0, s)

        def rotary(col0, out_ref, scale):
            t = _dot(st["xb"], w_ref[:, col0:col0 + DA_Q])
            out = []
            for j in range(DA_Q // LANES):
                tj = t[:, j * LANES:(j + 1) * LANES]
                up = pltpu.roll(tj, LANES - 1, 1)
                dn = pltpu.roll(tj, 1, 1)
                out.append(tj * st["c"] + up * st["s_lo"] + dn * st["s_hi"])
            out_ref[rows, :] = (jnp.concatenate(out, axis=1) * scale).astype(BF16)

        def plain(col0, out_ref):
            n = out_ref.shape[1]
            out_ref[rows, :] = _dot(st["xb"], w_ref[:, col0:col0 + n]).astype(BF16)

        def decay_gate():
            g_low = _dot(st["xb"], wgl_ref[...])
            z = _dot(g_low.astype(BF16), wg2_ref[...]) + bg2_ref[...]
            log_sig = jnp.minimum(z, 0.0) - jnp.log1p(jnp.exp(-jnp.abs(z)))
            la_ref[rows, :] = log_sig / GLA_GATE_NORMALIZER

        steps = [norm, tables, functools.partial(rotary, 0, q_ref, DA_HEAD_DIM ** -0.5 * LOG2_E),
                 functools.partial(rotary, DA_Q, k_ref, 1.0)]
        col0 = DA_Q + DA_K
        for out_ref in (v_ref, gq_ref, gk_ref, gv_ref, go_ref):
            steps.append(functools.partial(plain, col0, out_ref))
            col0 += out_ref.shape[1]
        return steps + [decay_gate]

    _run_staggered([block_steps(r0) for r0 in range(0, tm, ROW_BLOCK)])


def _in_proj(x2, pos2, ln_g, ln_b, inv_freq, w_main, w_glow, w_gate2, b_gate2, tm):
    T = x2.shape[0]
    row = lambda n: pl.BlockSpec((tm, n), lambda i: (i, 0))
    full = lambda a: pl.BlockSpec(a.shape, lambda i: (0,) * a.ndim)
    out_shape = [jax.ShapeDtypeStruct((T, n), dt) for n, dt in (
        (DA_Q, BF16), (DA_K, BF16), (DA_V, BF16), (GLA_Q, BF16), (GLA_K, BF16),
        (GLA_V, BF16), (GLA_OG, BF16), (GLA_K, F32))]
    return pl.pallas_call(
        _in_proj_kernel,
        grid=(T // tm,),
        in_specs=[row(D_MODEL), row(1), full(ln_g), full(ln_b), full(inv_freq), full(w_main),
                  full(w_glow), full(w_gate2), full(b_gate2)],
        out_specs=[row(s.shape[1]) for s in out_shape],
        out_shape=out_shape,
        compiler_params=pltpu.CompilerParams(dimension_semantics=("arbitrary",),
                                             vmem_limit_bytes=VMEM_LIMIT),
        name="in_proj",
    )(x2, pos2, ln_g, ln_b, inv_freq, w_main, w_glow, w_gate2, b_gate2)


def _diff_attn_kernel(lam_init, lq1_ref, lk1_ref, lq2_ref, lk2_ref, g_ref, q_ref, k_ref, v_ref, o_ref,
                      s_scr, p_scr, v_ext):
    S = q_ref.shape[1]
    tq = ATTN_BLOCK
    lam = (jnp.exp(jnp.sum(lq1_ref[...] * lk1_ref[...], axis=-1, keepdims=True))
           - jnp.exp(jnp.sum(lq2_ref[...] * lk2_ref[...], axis=-1, keepdims=True)) + lam_init)
    lane = lax.broadcasted_iota(jnp.int32, (tq, LANES), 1)
    rq = lax.broadcasted_iota(jnp.int32, (2 * tq, tq), 0) % tq // CHUNK
    ck = lax.broadcasted_iota(jnp.int32, (2 * tq, tq), 1) // CHUNK
    diag_mask = ck <= rq

    n_blk = S // tq
    st = [dict() for _ in range(n_blk)]
    v_ext[:, 0:DA_V_DIM] = v_ref[0]
    ext_lane = lax.broadcasted_iota(jnp.int32, (S, DA_V_DIM), 1)
    v_ext[:, DA_V_DIM:] = jnp.where(ext_lane == 0, 1.0, 0.0).astype(BF16)

    def stage_a(qi):
        s_buf = s_scr.at[qi % 2]

        def begin():
            q = q_ref[0, qi * tq:(qi + 1) * tq, :]
            zero = jnp.zeros_like(q)
            st[qi]["qq"] = jnp.concatenate([jnp.where(lane < DA_HEAD_DIM, q, zero),
                                            jnp.where(lane >= DA_HEAD_DIM, q, zero)], axis=0)
            st[qi]["m_acc"] = None

        def tile(j):
            s = _dot_nt(st[qi]["qq"], k_ref[0, j * tq:(j + 1) * tq, :])
            if j == qi:
                s = jnp.where(diag_mask, s, -jnp.inf)
            s_buf[:, j * tq:(j + 1) * tq] = s
            m_acc = st[qi]["m_acc"]
            for c0 in range(0, tq, LANES):
                sc = s[:, c0:c0 + LANES]
                m_acc = sc if m_acc is None else jnp.maximum(m_acc, sc)
            st[qi]["m_acc"] = m_acc

        def end():
            st[qi]["m"] = jnp.broadcast_to(jnp.max(st[qi]["m_acc"], axis=-1, keepdims=True), (2 * tq, LANES))

        return [begin] + [functools.partial(tile, j) for j in range(qi + 1)] + [end]

    def stage_b(qi):
        s_buf = s_scr.at[qi % 2]
        p_buf = p_scr.at[qi % 2]

        def cols(c0):
            p_buf[:, c0:c0 + LANES] = jnp.exp2((s_buf[:, c0:c0 + LANES] - st[qi]["m"]).astype(BF16))

        return [functools.partial(cols, c0) for c0 in range(0, (qi + 1) * tq, LANES)]

    def stage_c(qi):
        p_buf = p_scr.at[qi % 2]

        nk = (qi + 1) * tq
        st[qi]["a"] = None

        def part(k0):
            k1 = min(k0 + ATTN_PV_KEYS, nk)
            a = _dot(p_buf[:, k0:k1], v_ext[k0:k1, :])
            st[qi]["a"] = a if st[qi]["a"] is None else st[qi]["a"] + a

        def finish():
            a = st[qi]["a"][:, 0:DA_V_DIM] / st[qi]["a"][:, DA_V_DIM:DA_V_DIM + 1]
            o = a[0:tq] - lam * a[tq:2 * tq]
            o = o * lax.rsqrt(jnp.mean(o * o, axis=-1, keepdims=True) + LN_EPS) * g_ref[...]
            o_ref[0, qi * tq:(qi + 1) * tq, :] = (o * (1.0 - lam_init)).astype(o_ref.dtype)

        return [functools.partial(part, k0) for k0 in range(0, nk, ATTN_PV_KEYS)] + [finish]

    for t in range(n_blk + 2):
        stages = []
        if t < n_blk:
            stages.append(stage_a(t))
        if 0 <= t - 1 < n_blk:
            stages.append(stage_b(t - 1))
        if 0 <= t - 2 < n_blk:
            stages.append(stage_c(t - 2))
        merged = sorted(((i + 0.5) / len(ops), k, i, op) for k, ops in enumerate(stages) for i, op in enumerate(ops))
        for _, _, _, op in merged:
            op()


def _diff_attn(q, k, v, lam_q1, lam_k1, lam_q2, lam_k2, subln_g, lam_init):
    B, S, _ = q.shape
    vec = pl.BlockSpec((1, DA_HEAD_DIM), lambda b, h: (0, 0))
    seq = pl.BlockSpec((1, S, LANES), lambda b, h: (b, 0, h))
    return pl.pallas_call(
        functools.partial(_diff_attn_kernel, lam_init),
        grid=(B, DA_HEADS),
        in_specs=[vec, vec, vec, vec, pl.BlockSpec((1, DA_V_DIM), lambda b, h: (0, 0)), seq, seq, seq],
        out_specs=seq,
        out_shape=jax.ShapeDtypeStruct((B, S, DA_V), BF16),
        scratch_shapes=[pltpu.VMEM((2, 2 * ATTN_BLOCK, S), F32), pltpu.VMEM((2, 2 * ATTN_BLOCK, S), BF16),
                        pltpu.VMEM((S, 2 * DA_V_DIM), BF16)],
        compiler_params=pltpu.CompilerParams(dimension_semantics=("arbitrary",) * 2,
                                             vmem_limit_bytes=VMEM_LIMIT),
        name="diff_attn",
    )(lam_q1, lam_k1, lam_q2, lam_k2, subln_g, q, k, v)


def _gla_kernel(q_ref, k_ref, la_ref, v_ref, go_ref, ng_ref, o_ref, qt_s, oi_s, ds_s, dec_s):
    S = q_ref.shape[1]
    C = CHUNK
    BLK = GLA_BLOCK
    per_blk = BLK // C
    r = lax.broadcasted_iota(jnp.int32, (BLK, BLK), 0)
    c = lax.broadcasted_iota(jnp.int32, (BLK, BLK), 1)
    chunk_causal = (r // C == c // C) & (c <= r)
    tri = jnp.where(chunk_causal, 1.0, 0.0).astype(BF16)
    lane = lax.broadcasted_iota(jnp.int32, (BLK, LANES), 1)
    head_lanes = (lane < GLA_KEY_DIM, lane >= GLA_KEY_DIM)
    st_row = lax.broadcasted_iota(jnp.int32, (2 * GLA_V_DIM, LANES), 0)
    st_lane = lax.broadcasted_iota(jnp.int32, (2 * GLA_V_DIM, LANES), 1)
    own_keys = (st_row < GLA_V_DIM) == (st_lane < GLA_KEY_DIM)

    for b in range(S // BLK):
        r0 = b * BLK
        g = la_ref[0, r0:r0 + BLK, :]
        g1 = g.astype(BF16)
        e1 = g - g1.astype(F32)
        g2 = e1.astype(BF16)
        g3 = (e1 - g2.astype(F32)).astype(BF16)
        bcum = _dot(tri, g1) + _dot(tri, g2) + _dot(tri, g3)
        b_last = jnp.concatenate(
            [jnp.broadcast_to(bcum[i * C + C - 1:i * C + C, :], (C, LANES)) for i in range(per_blk)], axis=0)
        qf = q_ref[0, r0:r0 + BLK, :].astype(F32) * (GLA_KEY_DIM ** -0.5)
        kf = k_ref[0, r0:r0 + BLK, :].astype(F32)
        q_t = (qf * jnp.exp(bcum)).astype(BF16)
        k_t = (kf * jnp.exp(-bcum)).astype(BF16)
        k_end = (kf * jnp.exp(b_last - bcum)).astype(BF16)
        decay = jnp.exp(b_last)
        qt_s[r0:r0 + BLK, :] = q_t
        zero = jnp.zeros_like(q_t)
        for hh in range(2):
            att = jnp.where(chunk_causal, _dot_nt(jnp.where(head_lanes[hh], q_t, zero), k_t), 0.0).astype(BF16)
            oi_s[r0:r0 + BLK, hh * GLA_V_DIM:(hh + 1) * GLA_V_DIM] = _dot(
                att, v_ref[0, r0:r0 + BLK, hh * GLA_V_DIM:(hh + 1) * GLA_V_DIM])
        for i in range(per_blk):
            n = b * per_blk + i
            rows = slice(r0 + i * C, r0 + (i + 1) * C)
            inc = _dot_tn(v_ref[0, rows, :], k_end[i * C:(i + 1) * C, :])
            ds_s[n] = jnp.where(own_keys, inc, 0.0)
            dec_s[n:n + 1, :] = decay[i * C:i * C + 1, :]

    state = jnp.zeros((2 * GLA_V_DIM, LANES), F32)
    for n in range(S // C):
        rows = slice(n * C, (n + 1) * C)
        o = oi_s[rows, :] + _dot_nt(qt_s[rows, :], state.astype(BF16))
        state = state * dec_s[n:n + 1, :] + ds_s[n]
        for hh in range(2):
            cols = slice(hh * GLA_V_DIM, (hh + 1) * GLA_V_DIM)
            oh = o[:, cols]
            oh = oh * lax.rsqrt(jnp.mean(oh * oh, axis=-1, keepdims=True) + LN_EPS) * ng_ref[...]
            gate = go_ref[0, rows, cols].astype(F32)
            o_ref[0, rows, cols] = (oh * (gate * jax.nn.sigmoid(gate))).astype(o_ref.dtype)


def _gla(gq, gk, la, gv, go, norm_g):
    B, S, _ = gq.shape
    pairs = GLA_HEADS // 2
    narrow = pl.BlockSpec((1, S, LANES), lambda b, p: (b, 0, p))
    wide = pl.BlockSpec((1, S, 2 * GLA_V_DIM), lambda b, p: (b, 0, p))
    n_chunks = S // CHUNK
    return pl.pallas_call(
        _gla_kernel,
        grid=(B, pairs),
        in_specs=[narrow, narrow, narrow, wide, wide,
                  pl.BlockSpec((1, GLA_V_DIM), lambda b, p: (0, 0))],
        out_specs=wide,
        out_shape=jax.ShapeDtypeStruct((B, S, GLA_V), BF16),
        scratch_shapes=[pltpu.VMEM((S, LANES), BF16),
                        pltpu.VMEM((S, 2 * GLA_V_DIM), F32),
                        pltpu.VMEM((n_chunks, 2 * GLA_V_DIM, LANES), F32),
                        pltpu.VMEM((n_chunks, LANES), F32)],
        compiler_params=pltpu.CompilerParams(dimension_semantics=("arbitrary",) * 2,
                                             vmem_limit_bytes=VMEM_LIMIT),
        name="gla",
    )(gq, gk, la, gv, go, norm_g)


def _split3(a):
    hi = a.astype(BF16)
    lo = (a - hi.astype(F32)).astype(BF16)
    return hi, lo


def _mix_out_kernel(alpha, x_ref, da_ref, gl_ref, lng_ref, lnb_ref, wo_ref, g1_ref, b1_ref,
                    wr_hi_ref, wr_lo_ref, br_ref, lower_ref, h_ref, hp_ref, route_ref, cnt_ref):
    tm = x_ref.shape[0]
    i = pl.program_id(0)

    @pl.when(i == 0)
    def _():
        cnt_ref[...] = jnp.zeros_like(cnt_ref)

    logit_blocks = []
    for r0 in range(0, tm, ROW_BLOCK):
        rows = slice(r0, r0 + ROW_BLOCK)
        xn = _layer_norm(x_ref[rows, :], lng_ref[...], lnb_ref[...])
        mix = _dot(da_ref[rows, :], wo_ref[0:DA_V, :]) + _dot(gl_ref[rows, :], wo_ref[DA_V:, :])
        h = _layer_norm(alpha * xn + mix, g1_ref[...], b1_ref[...])
        h_ref[rows, :] = h
        _words_to_tiles(hp_ref, _pack_pairs(h), r0)
        h_hi, h_lo = _split3(h)
        logit_blocks.append(_dot(h_hi, wr_hi_ref[...]) + _dot(h_hi, wr_lo_ref[...]) + _dot(h_lo, wr_hi_ref[...])
                            + br_ref[...])
    logits = jnp.concatenate(logit_blocks, axis=0)
    lane = lax.broadcasted_iota(jnp.int32, (tm, LANES), 1)
    neg = -jnp.inf
    big = jnp.int32(LANES)

    def first_argmax(vals, valid):
        v = jnp.where(valid, vals, neg)
        mx = jnp.max(v, axis=-1, keepdims=True)
        idx = jnp.min(jnp.where(valid & (v == mx), lane, big), axis=-1, keepdims=True)
        return mx, idx

    is_group = lane < N_GROUPS
    g_max, g_top = first_argmax(logits, is_group)
    p_g = 1.0 / jnp.sum(jnp.where(is_group, jnp.exp(logits - g_max), 0.0), axis=-1, keepdims=True)

    e_lo = EXPERT_LANE0 + g_top * EXPERTS_PER_GROUP
    in_group = (lane >= e_lo) & (lane < e_lo + EXPERTS_PER_GROUP)
    v0, i0 = first_argmax(logits, in_group)
    v1, i1 = first_argmax(logits, in_group & (lane != i0))
    w1 = jnp.exp(v1 - v0)
    gate0 = p_g / (1.0 + w1)
    gate1 = p_g * w1 / (1.0 + w1)
    e0 = i0 - EXPERT_LANE0
    e1 = i1 - EXPERT_LANE0

    oh0 = jnp.where(lane == e0, 1.0, 0.0)
    oh1 = jnp.where(lane == e1, 1.0, 0.0)
    oh = oh0 + oh1
    before = _dot(lower_ref[...], oh.astype(BF16)) + cnt_ref[0:1, :]
    rank0 = jnp.sum(oh0 * before, axis=-1, keepdims=True)
    rank1 = jnp.sum(oh1 * before, axis=-1, keepdims=True)
    cnt_ref[...] = cnt_ref[...] + jnp.sum(oh, axis=0, keepdims=True)

    rec = jnp.zeros((tm, LANES), F32)
    for ln, val in ((R_E0, e0.astype(F32)), (R_E1, e1.astype(F32)), (R_G0, gate0), (R_G1, gate1),
                    (R_RANK0, rank0), (R_RANK1, rank1)):
        rec = jnp.where(lane == ln, val, rec)
    route_ref[...] = rec


def _mix_out(x2, da2, gl2, ln_g, ln_b, w_o, ln1_g, ln1_b, wr_hi, wr_lo, b_r, alpha, tm):
    T = x2.shape[0]
    row = lambda n: pl.BlockSpec((tm, n), lambda i: (i, 0))
    full = lambda a: pl.BlockSpec(a.shape, lambda i: (0,) * a.ndim)
    lower = jnp.tril(jnp.ones((tm, tm), BF16), -1)
    return pl.pallas_call(
        functools.partial(_mix_out_kernel, alpha),
        grid=(T // tm,),
        in_specs=[row(D_MODEL), row(DA_V), row(GLA_V), full(ln_g), full(ln_b), full(w_o),
                  full(ln1_g), full(ln1_b), full(wr_hi), full(wr_lo), full(b_r), full(lower)],
        out_specs=[row(D_MODEL), pl.BlockSpec((tm * ROW_SUB, LANES), lambda i: (i, 0)), row(LANES),
                   pl.BlockSpec((8, LANES), lambda i: (0, 0))],
        out_shape=[jax.ShapeDtypeStruct((T, D_MODEL), F32), jax.ShapeDtypeStruct((T * ROW_SUB, LANES), jnp.uint32),
                   jax.ShapeDtypeStruct((T, LANES), F32),
                   jax.ShapeDtypeStruct((8, LANES), F32)],
        compiler_params=pltpu.CompilerParams(dimension_semantics=("arbitrary",),
                                             vmem_limit_bytes=VMEM_LIMIT),
        name="mix_out",
    )(x2, da2, gl2, ln_g, ln_b, w_o, ln1_g, ln1_b, wr_hi, wr_lo, b_r, lower)


HIGH_HALF = 0xFFFF0000


def _pack_pairs(val):
    bits = lambda a: lax.bitcast_convert_type(a.astype(BF16).astype(F32), jnp.uint32)
    half = val.shape[1] // 2
    return (bits(val[:, :half]) >> 16) | (bits(val[:, half:]) & jnp.uint32(HIGH_HALF))


def _unpack_pairs(words):
    lo = lax.bitcast_convert_type(words << 16, F32)
    hi = lax.bitcast_convert_type(words & jnp.uint32(HIGH_HALF), F32)
    return jnp.concatenate([lo, hi], axis=1)


def _words_to_tiles(dst_ref, words, r0=0):
    n = words.shape[0]
    for s in range(ROW_SUB):
        dst_ref[pl.ds(r0 * ROW_SUB + s, n, stride=ROW_SUB), :] = words[:, s * LANES:(s + 1) * LANES]


def _tiles_to_words(src_ref, r0, n):
    return jnp.concatenate([src_ref[pl.ds(r0 * ROW_SUB + s, n, stride=ROW_SUB), :] for s in range(ROW_SUB)],
                           axis=1)


def _rows_to_tiles(dst_ref, val):
    _words_to_tiles(dst_ref, _pack_pairs(val))


def _tiles_to_rows(src_ref, r0, n):
    return _unpack_pairs(_tiles_to_words(src_ref, r0, n))


def _row_tile(ref, r):
    return ref.at[pl.ds(pl.multiple_of(r * ROW_SUB, ROW_SUB), ROW_SUB), :]


def _invert_kernel(dest_ref, inv_ref):
    n = dest_ref.shape[0]

    def pad(i, c):
        inv_ref[n + i] = 0
        return c

    lax.fori_loop(0, inv_ref.shape[0] - n, pad, 0)

    def body(i, c):
        for u in range(DMA_UNROLL):
            j = i * DMA_UNROLL + u
            inv_ref[dest_ref[j]] = j
        return c

    lax.fori_loop(0, n // DMA_UNROLL, body, 0)


def _invert(dest_flat):
    n = dest_flat.shape[0]
    return pl.pallas_call(
        _invert_kernel,
        grid_spec=pltpu.PrefetchScalarGridSpec(
            num_scalar_prefetch=1, grid=(1,), in_specs=[],
            out_specs=pl.BlockSpec(memory_space=pltpu.SMEM)),
        out_shape=jax.ShapeDtypeStruct((n + EXPERT_CHUNK,), jnp.int32),
        compiler_params=pltpu.CompilerParams(dimension_semantics=("arbitrary",)),
        name="invert",
    )(dest_flat)


def _experts_kernel(row0_ref, first_ref, inv_ref, hp_ref, wg_ref, wu_ref, wd_ref, ys_ref,
                    wgu_b, wd_b, xbuf, ybuf, xsem, ysem, pend_ref, *, n_rows):
    e = pl.program_id(0)
    ch = EXPERT_CHUNK
    depth = EXPERT_XBUFS - 1
    g_lo = first_ref[e]
    g_hi = first_ref[e + 1]
    total = first_ref[N_EXPERTS]

    def slab(ref, row0):
        return ref.at[pl.ds(pl.multiple_of(row0 * ROW_SUB, ROW_SUB), ch * ROW_SUB), :]

    def x_row(g, r, tok):
        slot = g & (EXPERT_XBUFS - 1)
        return pltpu.make_async_copy(_row_tile(hp_ref, tok), _row_tile(xbuf.at[slot], r), xsem.at[slot])

    def gather(g):
        row0 = row0_ref[g]
        for r in range(ch):
            tok = lax.shift_right_logical(inv_ref[row0 + r], TOP_K - 1)
            x_row(g, r, tok).start(priority=r % 2)

    def gather_wait(g):
        for r in range(ch):
            x_row(g, r, 0).wait()

    def y_copy(row0, half):
        return pltpu.make_async_copy(ybuf.at[half], slab(ys_ref, row0), ysem.at[half])

    def drain_y(half):
        @pl.when(pend_ref[half] == 1)
        def _():
            y_copy(0, half).wait()
            pend_ref[half] = 0

    @pl.when(e == 0)
    def _():
        for b in range(EXPERT_YBUFS):
            pend_ref[b] = 0
        ybuf[0] = jnp.zeros(ybuf.shape[1:], ybuf.dtype)
        y_copy(n_rows, 0).start()
        y_copy(n_rows, 0).wait()

        def first_chunks(d, c):
            gather(d)
            return c

        lax.fori_loop(0, jnp.minimum(depth, total), first_chunks, 0)

    @pl.when(g_hi > g_lo)
    def _():
        wgu_b[:, 0:D_EXPERT] = wg_ref[0].astype(BF16)
        wgu_b[:, D_EXPERT:] = wu_ref[0].astype(BF16)
        wd_b[...] = wd_ref[0].astype(BF16)

        def chunk(g, c):
            half = g & (EXPERT_YBUFS - 1)
            gather_wait(g)
            gather(g + depth)

            xb = _tiles_to_rows(xbuf.at[g & (EXPERT_XBUFS - 1)], 0, ch).astype(BF16)
            gu = _dot(xb, wgu_b[...])
            gate = gu[:, 0:D_EXPERT]
            mid = (gate * jax.nn.sigmoid(gate) * gu[:, D_EXPERT:]).astype(BF16)
            words = _pack_pairs(_dot(mid, wd_b[...]))

            drain_y(half)

            @pl.when(g == g_lo)
            def _():
                drain_y((g - 1) & (EXPERT_YBUFS - 1))

            _words_to_tiles(ybuf.at[half], words)
            y_copy(row0_ref[g], half).start(priority=ROW_STREAM_PRIORITY)
            pend_ref[half] = 1
            return c

        lax.fori_loop(g_lo, g_hi, chunk, 0)

    @pl.when(e == pl.num_programs(0) - 1)
    def _():
        for b in range(EXPERT_YBUFS):
            drain_y(b)

        def drain_x(d, c):
            gather_wait(jnp.maximum(total, depth) + d)
            return c

        lax.fori_loop(0, jnp.minimum(depth, total), drain_x, 0)


def _chunk_metadata(seg_start, counts, n_rows):
    ch = EXPERT_CHUNK
    max_chunks = n_rows // ch + N_EXPERTS + EXPERT_XBUFS
    n_ch = (counts + (ch - 1)) // ch
    first = jnp.concatenate([jnp.zeros((1,), jnp.int32), jnp.cumsum(n_ch).astype(jnp.int32)])
    g = jnp.arange(max_chunks, dtype=jnp.int32)
    owner = jnp.minimum(jnp.sum((first[None, 1:] <= g[:, None]).astype(jnp.int32), axis=1), N_EXPERTS - 1)
    onehot = owner[:, None] == jnp.arange(N_EXPERTS, dtype=jnp.int32)
    pick = lambda tab: jnp.sum(jnp.where(onehot, tab[None, :], 0), axis=1)
    row0 = pick(seg_start) + (g - pick(first[:-1])) * ch
    row0 = jnp.where(g < first[-1], row0, 0)
    return row0.astype(jnp.int32), first


def _experts(seg_start, counts, inv, h_packed, w_gate, w_up, w_down):
    n_rows = inv.shape[0] - EXPERT_CHUNK
    row0, first = _chunk_metadata(seg_start, counts, n_rows)
    per_expert = lambda shape: pl.BlockSpec((1,) + shape, lambda e, r, f, i: (e, 0, 0))
    slab = (EXPERT_CHUNK * ROW_SUB, LANES)
    return pl.pallas_call(
        functools.partial(_experts_kernel, n_rows=n_rows),
        grid_spec=pltpu.PrefetchScalarGridSpec(
            num_scalar_prefetch=3,
            grid=(N_EXPERTS,),
            in_specs=[pl.BlockSpec(memory_space=pl.ANY),
                      per_expert((D_MODEL, D_EXPERT)), per_expert((D_MODEL, D_EXPERT)),
                      per_expert((D_EXPERT, D_MODEL))],
            out_specs=pl.BlockSpec(memory_space=pl.ANY),
            scratch_shapes=[pltpu.VMEM((D_MODEL, 2 * D_EXPERT), BF16), pltpu.VMEM((D_EXPERT, D_MODEL), BF16),
                            pltpu.VMEM((EXPERT_XBUFS,) + slab, jnp.uint32), pltpu.VMEM((EXPERT_YBUFS,) + slab, jnp.uint32),
                            pltpu.SemaphoreType.DMA((EXPERT_XBUFS,)), pltpu.SemaphoreType.DMA((EXPERT_YBUFS,)),
                            pltpu.SMEM((EXPERT_YBUFS,), jnp.int32)]),
        out_shape=jax.ShapeDtypeStruct(((n_rows + EXPERT_CHUNK) * ROW_SUB, LANES), jnp.uint32),
        compiler_params=pltpu.CompilerParams(dimension_semantics=("arbitrary",),
                                             vmem_limit_bytes=VMEM_LIMIT),
        name="experts",
    )(row0, first, inv, h_packed, w_gate, w_up, w_down)


def _combine_kernel(alpha, dest_ref, h_ref, route_ref, g_ref, b_ref, y_ref, o_ref, buf, sems):
    tm = h_ref.shape[0]
    step = pl.program_id(0)
    half = step & 1
    n_iter = tm * TOP_K // DMA_UNROLL

    def row_copy(hf, src, slot):
        return pltpu.make_async_copy(_row_tile(y_ref, src), _row_tile(buf.at[hf], slot), sems.at[hf])

    def gather(st, hf):
        base = st * (tm * TOP_K)

        def start(i, c):
            for u in range(DMA_UNROLL):
                slot = (u % TOP_K) * tm + i * (DMA_UNROLL // TOP_K) + u // TOP_K
                row_copy(hf, dest_ref[base + i * DMA_UNROLL + u], slot).start(priority=u % 2)
            return c

        lax.fori_loop(0, n_iter, start, 0)

    @pl.when(step == 0)
    def _():
        gather(0, 0)

    @pl.when(step + 1 < pl.num_programs(0))
    def _():
        gather(step + 1, 1 - half)

    def wait(i, c):
        for u in range(DMA_UNROLL):
            row_copy(half, 0, 0).wait()
        return c

    lax.fori_loop(0, n_iter, wait, 0)

    rec = route_ref[...]
    cur = buf.at[half]
    ffn = (rec[:, R_G0:R_G0 + 1] * _tiles_to_rows(cur, 0, tm)
           + rec[:, R_G1:R_G1 + 1] * _tiles_to_rows(cur, tm, tm))
    o_ref[...] = _layer_norm(alpha * h_ref[...] + ffn, g_ref[...], b_ref[...])


def _combine(dest_flat, h, route, ln2_g, ln2_b, y_sorted, alpha, tm):
    T = h.shape[0]
    return pl.pallas_call(
        functools.partial(_combine_kernel, alpha),
        grid_spec=pltpu.PrefetchScalarGridSpec(
            num_scalar_prefetch=1,
            grid=(T // tm,),
            in_specs=[pl.BlockSpec((tm, D_MODEL), lambda i, d: (i, 0)),
                      pl.BlockSpec((tm, LANES), lambda i, d: (i, 0)),
                      pl.BlockSpec((1, D_MODEL), lambda i, d: (0, 0)),
                      pl.BlockSpec((1, D_MODEL), lambda i, d: (0, 0)),
                      pl.BlockSpec(memory_space=pl.ANY)],
            out_specs=pl.BlockSpec((tm, D_MODEL), lambda i, d: (i, 0)),
            scratch_shapes=[pltpu.VMEM((2, TOP_K * tm * ROW_SUB, LANES), jnp.uint32), pltpu.SemaphoreType.DMA((2,))]),
        out_shape=jax.ShapeDtypeStruct((T, D_MODEL), F32),
        compiler_params=pltpu.CompilerParams(dimension_semantics=("arbitrary",),
                                             vmem_limit_bytes=VMEM_LIMIT),
        name="combine",
    )(dest_flat, h, route, ln2_g, ln2_b, y_sorted)


def kernel(x, positions, ln_in_g, ln_in_b, w_in, lam_q1, lam_k1, lam_q2, lam_k2, da_subln_g, gla_w_gate2, gla_b_gate2, gla_norm_g, w_o, ln1_g, ln1_b, router_w_group, router_b_group, router_w_expert, router_b_expert, w_gate, w_up, w_down, ln2_g, ln2_b):
    B, S, D = x.shape
    T = B * S
    depth = w_in.shape[0]
    assert depth == 1, "only a single layer is supported"
    alpha = (2 * depth) ** 0.25
    row2 = lambda a: a.reshape(1, -1)

    inv_freq = ROPE_THETA ** (-jnp.arange(0, DA_HEAD_DIM, 2, dtype=F32) / DA_HEAD_DIM)
    inv_freq = jnp.tile(jnp.repeat(inv_freq, 2), LANES // DA_HEAD_DIM).reshape(1, LANES)
    pos2 = positions.reshape(T, 1)

    cur = x.reshape(T, D)
    cur_g, cur_b = row2(ln_in_g), row2(ln_in_b)
    for l in range(depth):
        w = w_in[l]
        w_main = w[:, :D_MAIN].astype(BF16)
        w_glow = jnp.pad(w[:, D_MAIN:], ((0, 0), (0, LANES - GLA_GATE_RANK))).astype(BF16)
        w_gate2 = jnp.pad(gla_w_gate2[l], ((0, LANES - GLA_GATE_RANK), (0, 0))).astype(BF16)

        q, k, v, gq, gk, gv, go, la = _in_proj(cur, pos2, cur_g, cur_b, inv_freq, w_main, w_glow,
                                               w_gate2, row2(gla_b_gate2[l]), tm=512)
        lam_init = 0.8 - 0.6 * math.exp(-0.3 * l)
        sh = lambda a: a.reshape(B, S, a.shape[-1])
        da = _diff_attn(sh(q), sh(k), sh(v), row2(lam_q1[l]), row2(lam_k1[l]), row2(lam_q2[l]),
                        row2(lam_k2[l]), row2(da_subln_g[l]), lam_init)
        gl = _gla(sh(gq), sh(gk), sh(la), sh(gv), sh(go), row2(gla_norm_g[l]))

        w_r = jnp.zeros((D, LANES), F32)
        w_r = w_r.at[:, :N_GROUPS].set(router_w_group[l])
        w_r = w_r.at[:, EXPERT_LANE0:EXPERT_LANE0 + N_EXPERTS].set(router_w_expert[l])
        b_r = jnp.zeros((1, LANES), F32)
        b_r = b_r.at[0, :N_GROUPS].set(router_b_group[l])
        b_r = b_r.at[0, EXPERT_LANE0:EXPERT_LANE0 + N_EXPERTS].set(router_b_expert[l])
        wr_hi = w_r.astype(BF16)
        wr_lo = (w_r - wr_hi.astype(F32)).astype(BF16)

        h, h_packed, route, cnt = _mix_out(cur, da.reshape(T, DA_V), gl.reshape(T, GLA_V), cur_g, cur_b,
                                 w_o[l].astype(BF16), row2(ln1_g[l]), row2(ln1_b[l]), wr_hi, wr_lo, b_r,
                                 alpha, tm=512)

        counts = cnt[0, :N_EXPERTS].astype(jnp.int32)
        seg_start = jnp.cumsum(counts) - counts
        eid = route[:, R_E0:R_E1 + 1].astype(jnp.int32)
        rank = route[:, R_RANK0:R_RANK1 + 1].astype(jnp.int32)
        onehot = eid[..., None] == jnp.arange(N_EXPERTS, dtype=jnp.int32)
        dest = jnp.sum(jnp.where(onehot, seg_start, 0), axis=-1) + rank
        dest_flat = dest.reshape(T * TOP_K)

        inv = _invert(dest_flat)
        ys = _experts(seg_start, counts, inv, h_packed, w_gate[l], w_up[l], w_down[l])
        cur = _combine(dest_flat, h, route, row2(ln2_g[l]), row2(ln2_b[l]), ys, alpha, tm=256)
    return cur.reshape(B, S, D)
```

```python
import functools
import math

import jax
import jax.numpy as jnp
from jax import lax
from jax.experimental import pallas as pl
from jax.experimental.pallas import tpu as pltpu

F32 = jnp.float32
BF16 = jnp.bfloat16

D_MODEL = 1024
CHUNK = 64
ROPE_THETA = 10000.0
LN_EPS = 1e-5
LOG2_E = math.log2(math.e)

DA_HEADS = 4
DA_V_DIM = D_MODEL // (2 * DA_HEADS)
DA_HEAD_DIM = DA_V_DIM // 2
GLA_HEADS = 4
GLA_V_DIM = D_MODEL // (2 * GLA_HEADS)
GLA_KEY_DIM = GLA_V_DIM // 2
GLA_GATE_RANK = 16
GLA_GATE_NORMALIZER = 16.0

DA_Q = DA_HEADS * 2 * DA_HEAD_DIM
DA_K = DA_Q
DA_V = DA_HEADS * DA_V_DIM
GLA_Q = GLA_HEADS * GLA_KEY_DIM
GLA_K = GLA_Q
GLA_V = GLA_HEADS * GLA_V_DIM
GLA_OG = GLA_V
D_MAIN = DA_Q + DA_K + DA_V + GLA_Q + GLA_K + GLA_V + GLA_OG

N_GROUPS = 4
EXPERTS_PER_GROUP = 8
N_EXPERTS = N_GROUPS * EXPERTS_PER_GROUP
TOP_K = 2
D_EXPERT = D_MODEL // 2

LANES = 128
ROW_SUB = D_MODEL // (2 * LANES)
ROW_BLOCK = 256
EXPERT_CHUNK = 128
EXPERT_XBUFS = 8
EXPERT_YBUFS = 4
ROW_STREAM_PRIORITY = 1
GLA_BLOCK = 256
ATTN_BLOCK = 256
ATTN_PV_KEYS = 512
DMA_UNROLL = 8
VMEM_LIMIT = 48 * 1024 * 1024

R_E0, R_E1, R_G0, R_G1, R_RANK0, R_RANK1 = 0, 1, 2, 3, 4, 5
EXPERT_LANE0 = 32


def _layer_norm(x, g, b):
    mu = jnp.mean(x, axis=-1, keepdims=True)
    xc = x - mu
    var = jnp.mean(xc * xc, axis=-1, keepdims=True)
    return xc * lax.rsqrt(var + LN_EPS) * g + b


def _dot(a, b):
    return jnp.dot(a, b, preferred_element_type=F32)


def _dot_nt(a, b):
    return lax.dot_general(a, b, (((1,), (1,)), ((), ())), preferred_element_type=F32)


def _dot_tn(a, b):
    return lax.dot_general(a, b, (((0,), (0,)), ((), ())), preferred_element_type=F32)


def _in_proj_kernel(x_ref, pos_ref, g_ref, b_ref, invf_ref, w_ref, wgl_ref, wg2_ref, bg2_ref,
                    q_ref, k_ref, v_ref, gq_ref, gk_ref, gv_ref, go_ref, la_ref):
    tm = x_ref.shape[0]
    lane = lax.broadcasted_iota(jnp.int32, (ROW_BLOCK, LANES), 1)
    first = (lane & 1) == 0

    for r0 in range(0, tm, ROW_BLOCK):
        rows = slice(r0, r0 + ROW_BLOCK)
        xn = _layer_norm(x_ref[rows, :], g_ref[...], b_ref[...])
        xb = xn.astype(BF16)
        proj = _dot(xb, w_ref[...])

        ang = pos_ref[rows, :].astype(F32) * invf_ref[...]
        c = jnp.cos(ang)
        s = jnp.sin(ang)
        s_lo = jnp.where(first, -s, 0.0)
        s_hi = jnp.where(first, 0.0, s)

        def rope(t):
            out = []
            for j in range(t.shape[1] // LANES):
                tj = t[:, j * LANES:(j + 1) * LANES]
                up = pltpu.roll(tj, LANES - 1, 1)
                dn = pltpu.roll(tj, 1, 1)
                out.append(tj * c + up * s_lo + dn * s_hi)
            return jnp.concatenate(out, axis=1)

        o = 0
        q = rope(proj[:, o:o + DA_Q]) * (DA_HEAD_DIM ** -0.5 * LOG2_E)
        o += DA_Q
        k = rope(proj[:, o:o + DA_K])
        o += DA_K
        q_ref[rows, :] = q.astype(BF16)
        k_ref[rows, :] = k.astype(BF16)
        v_ref[rows, :] = proj[:, o:o + DA_V].astype(BF16)
        o += DA_V
        gq_ref[rows, :] = proj[:, o:o + GLA_Q].astype(BF16)
        o += GLA_Q
        gk_ref[rows, :] = proj[:, o:o + GLA_K].astype(BF16)
        o += GLA_K
        gv_ref[rows, :] = proj[:, o:o + GLA_V].astype(BF16)
        o += GLA_V
        go_ref[rows, :] = proj[:, o:o + GLA_OG].astype(BF16)

        g_low = _dot(xb, wgl_ref[...])
        z = _dot(g_low.astype(BF16), wg2_ref[...]) + bg2_ref[...]
        log_sig = jnp.minimum(z, 0.0) - jnp.log1p(jnp.exp(-jnp.abs(z)))
        la_ref[rows, :] = log_sig / GLA_GATE_NORMALIZER


def _in_proj(x2, pos2, ln_g, ln_b, inv_freq, w_main, w_glow, w_gate2, b_gate2, tm):
    T = x2.shape[0]
    row = lambda n: pl.BlockSpec((tm, n), lambda i: (i, 0))
    full = lambda a: pl.BlockSpec(a.shape, lambda i: (0,) * a.ndim)
    out_shape = [jax.ShapeDtypeStruct((T, n), dt) for n, dt in (
        (DA_Q, BF16), (DA_K, BF16), (DA_V, BF16), (GLA_Q, BF16), (GLA_K, BF16),
        (GLA_V, BF16), (GLA_OG, BF16), (GLA_K, F32))]
    return pl.pallas_call(
        _in_proj_kernel,
        grid=(T // tm,),
        in_specs=[row(D_MODEL), row(1), full(ln_g), full(ln_b), full(inv_freq), full(w_main),
                  full(w_glow), full(w_gate2), full(b_gate2)],
        out_specs=[row(s.shape[1]) for s in out_shape],
        out_shape=out_shape,
        compiler_params=pltpu.CompilerParams(dimension_semantics=("arbitrary",),
                                             vmem_limit_bytes=VMEM_LIMIT),
        name="in_proj",
    )(x2, pos2, ln_g, ln_b, inv_freq, w_main, w_glow, w_gate2, b_gate2)


def _diff_attn_kernel(lam_init, lq1_ref, lk1_ref, lq2_ref, lk2_ref, g_ref, q_ref, k_ref, v_ref, o_ref,
                      s_scr, p_scr, v_ext):
    S = q_ref.shape[1]
    tq = ATTN_BLOCK
    lam = (jnp.exp(jnp.sum(lq1_ref[...] * lk1_ref[...], axis=-1, keepdims=True))
           - jnp.exp(jnp.sum(lq2_ref[...] * lk2_ref[...], axis=-1, keepdims=True)) + lam_init)
    lane = lax.broadcasted_iota(jnp.int32, (tq, LANES), 1)
    rq = lax.broadcasted_iota(jnp.int32, (2 * tq, tq), 0) % tq // CHUNK
    ck = lax.broadcasted_iota(jnp.int32, (2 * tq, tq), 1) // CHUNK
    diag_mask = ck <= rq

    n_blk = S // tq
    st = [dict() for _ in range(n_blk)]
    v_ext[:, 0:DA_V_DIM] = v_ref[0]
    ext_lane = lax.broadcasted_iota(jnp.int32, (S, DA_V_DIM), 1)
    v_ext[:, DA_V_DIM:] = jnp.where(ext_lane == 0, 1.0, 0.0).astype(BF16)

    def stage_a(qi):
        s_buf = s_scr.at[qi % 2]

        def begin():
            q = q_ref[0, qi * tq:(qi + 1) * tq, :]
            zero = jnp.zeros_like(q)
            st[qi]["qq"] = jnp.concatenate([jnp.where(lane < DA_HEAD_DIM, q, zero),
                                            jnp.where(lane >= DA_HEAD_DIM, q, zero)], axis=0)
            st[qi]["m_acc"] = None

        def tile(j):
            s = _dot_nt(st[qi]["qq"], k_ref[0, j * tq:(j + 1) * tq, :])
            if j == qi:
                s = jnp.where(diag_mask, s, -jnp.inf)
            s_buf[:, j * tq:(j + 1) * tq] = s
            m_acc = st[qi]["m_acc"]
            for c0 in range(0, tq, LANES):
                sc = s[:, c0:c0 + LANES]
                m_acc = sc if m_acc is None else jnp.maximum(m_acc, sc)
            st[qi]["m_acc"] = m_acc

        def end():
            st[qi]["m"] = jnp.broadcast_to(jnp.max(st[qi]["m_acc"], axis=-1, keepdims=True), (2 * tq, LANES))

        return [begin] + [functools.partial(tile, j) for j in range(qi + 1)] + [end]

    def stage_b(qi):
        s_buf = s_scr.at[qi % 2]
        p_buf = p_scr.at[qi % 2]

        def cols(c0):
            p_buf[:, c0:c0 + LANES] = jnp.exp2((s_buf[:, c0:c0 + LANES] - st[qi]["m"]).astype(BF16))

        return [functools.partial(cols, c0) for c0 in range(0, (qi + 1) * tq, LANES)]

    def stage_c(qi):
        p_buf = p_scr.at[qi % 2]

        nk = (qi + 1) * tq
        st[qi]["a"] = None

        def part(k0):
            k1 = min(k0 + ATTN_PV_KEYS, nk)
            a = _dot(p_buf[:, k0:k1], v_ext[k0:k1, :])
            st[qi]["a"] = a if st[qi]["a"] is None else st[qi]["a"] + a

        def finish():
            a = st[qi]["a"][:, 0:DA_V_DIM] / st[qi]["a"][:, DA_V_DIM:DA_V_DIM + 1]
            o = a[0:tq] - lam * a[tq:2 * tq]
            o = o * lax.rsqrt(jnp.mean(o * o, axis=-1, keepdims=True) + LN_EPS) * g_ref[...]
            o_ref[0, qi * tq:(qi + 1) * tq, :] = (o * (1.0 - lam_init)).astype(o_ref.dtype)

        return [functools.partial(part, k0) for k0 in range(0, nk, ATTN_PV_KEYS)] + [finish]

    for t in range(n_blk + 2):
        stages = []
        if t < n_blk:
            stages.append(stage_a(t))
        if 0 <= t - 1 < n_blk:
            stages.append(stage_b(t - 1))
        if 0 <= t - 2 < n_blk:
            stages.append(stage_c(t - 2))
        merged = sorted(((i + 0.5) / len(ops), k, i, op) for k, ops in enumerate(stages) for i, op in enumerate(ops))
        for _, _, _, op in merged:
            op()


def _diff_attn(q, k, v, lam_q1, lam_k1, lam_q2, lam_k2, subln_g, lam_init):
    B, S, _ = q.shape
    vec = pl.BlockSpec((1, DA_HEAD_DIM), lambda b, h: (0, 0))
    seq = pl.BlockSpec((1, S, LANES), lambda b, h: (b, 0, h))
    return pl.pallas_call(
        functools.partial(_diff_attn_kernel, lam_init),
        grid=(B, DA_HEADS),
        in_specs=[vec, vec, vec, vec, pl.BlockSpec((1, DA_V_DIM), lambda b, h: (0, 0)), seq, seq, seq],
        out_specs=seq,
        out_shape=jax.ShapeDtypeStruct((B, S, DA_V), BF16),
        scratch_shapes=[pltpu.VMEM((2, 2 * ATTN_BLOCK, S), F32), pltpu.VMEM((2, 2 * ATTN_BLOCK, S), BF16),
                        pltpu.VMEM((S, 2 * DA_V_DIM), BF16)],
        compiler_params=pltpu.CompilerParams(dimension_semantics=("arbitrary",) * 2,
                                             vmem_limit_bytes=VMEM_LIMIT),
        name="diff_attn",
    )(lam_q1, lam_k1, lam_q2, lam_k2, subln_g, q, k, v)


def _gla_kernel(q_ref, k_ref, la_ref, v_ref, go_ref, ng_ref, o_ref, qt_s, oi_s, ds_s, dec_s):
    S = q_ref.shape[1]
    C = CHUNK
    BLK = GLA_BLOCK
    per_blk = BLK // C
    r = lax.broadcasted_iota(jnp.int32, (BLK, BLK), 0)
    c = lax.broadcasted_iota(jnp.int32, (BLK, BLK), 1)
    chunk_causal = (r // C == c // C) & (c <= r)
    tri = jnp.where(chunk_causal, 1.0, 0.0).astype(BF16)
    lane = lax.broadcasted_iota(jnp.int32, (BLK, LANES), 1)
    head_lanes = (lane < GLA_KEY_DIM, lane >= GLA_KEY_DIM)
    st_row = lax.broadcasted_iota(jnp.int32, (2 * GLA_V_DIM, LANES), 0)
    st_lane = lax.broadcasted_iota(jnp.int32, (2 * GLA_V_DIM, LANES), 1)
    own_keys = (st_row < GLA_V_DIM) == (st_lane < GLA_KEY_DIM)

    for b in range(S // BLK):
        r0 = b * BLK
        g = la_ref[0, r0:r0 + BLK, :]
        g1 = g.astype(BF16)
        e1 = g - g1.astype(F32)
        g2 = e1.astype(BF16)
        g3 = (e1 - g2.astype(F32)).astype(BF16)
        bcum = _dot(tri, g1) + _dot(tri, g2) + _dot(tri, g3)
        b_last = jnp.concatenate(
            [jnp.broadcast_to(bcum[i * C + C - 1:i * C + C, :], (C, LANES)) for i in range(per_blk)], axis=0)
        qf = q_ref[0, r0:r0 + BLK, :].astype(F32) * (GLA_KEY_DIM ** -0.5)
        kf = k_ref[0, r0:r0 + BLK, :].astype(F32)
        q_t = (qf * jnp.exp(bcum)).astype(BF16)
        k_t = (kf * jnp.exp(-bcum)).astype(BF16)
        k_end = (kf * jnp.exp(b_last - bcum)).astype(BF16)
        decay = jnp.exp(b_last)
        qt_s[r0:r0 + BLK, :] = q_t
        zero = jnp.zeros_like(q_t)
        for hh in range(2):
            att = jnp.where(chunk_causal, _dot_nt(jnp.where(head_lanes[hh], q_t, zero), k_t), 0.0).astype(BF16)
            oi_s[r0:r0 + BLK, hh * GLA_V_DIM:(hh + 1) * GLA_V_DIM] = _dot(
                att, v_ref[0, r0:r0 + BLK, hh * GLA_V_DIM:(hh + 1) * GLA_V_DIM])
        for i in range(per_blk):
            n = b * per_blk + i
            rows = slice(r0 + i * C, r0 + (i + 1) * C)
            inc = _dot_tn(v_ref[0, rows, :], k_end[i * C:(i + 1) * C, :])
            ds_s[n] = jnp.where(own_keys, inc, 0.0)
            dec_s[n:n + 1, :] = decay[i * C:i * C + 1, :]

    state = jnp.zeros((2 * GLA_V_DIM, LANES), F32)
    for n in range(S // C):
        rows = slice(n * C, (n + 1) * C)
        o = oi_s[rows, :] + _dot_nt(qt_s[rows, :], state.astype(BF16))
        state = state * dec_s[n:n + 1, :] + ds_s[n]
        for hh in range(2):
            cols = slice(hh * GLA_V_DIM, (hh + 1) * GLA_V_DIM)
            oh = o[:, cols]
            oh = oh * lax.rsqrt(jnp.mean(oh * oh, axis=-1, keepdims=True) + LN_EPS) * ng_ref[...]
            gate = go_ref[0, rows, cols].astype(F32)
            o_ref[0, rows, cols] = (oh * (gate * jax.nn.sigmoid(gate))).astype(o_ref.dtype)


def _gla(gq, gk, la, gv, go, norm_g):
    B, S, _ = gq.shape
    pairs = GLA_HEADS // 2
    narrow = pl.BlockSpec((1, S, LANES), lambda b, p: (b, 0, p))
    wide = pl.BlockSpec((1, S, 2 * GLA_V_DIM), lambda b, p: (b, 0, p))
    n_chunks = S // CHUNK
    return pl.pallas_call(
        _gla_kernel,
        grid=(B, pairs),
        in_specs=[narrow, narrow, narrow, wide, wide,
                  pl.BlockSpec((1, GLA_V_DIM), lambda b, p: (0, 0))],
        out_specs=wide,
        out_shape=jax.ShapeDtypeStruct((B, S, GLA_V), BF16),
        scratch_shapes=[pltpu.VMEM((S, LANES), BF16),
                        pltpu.VMEM((S, 2 * GLA_V_DIM), F32),
                        pltpu.VMEM((n_chunks, 2 * GLA_V_DIM, LANES), F32),
                        pltpu.VMEM((n_chunks, LANES), F32)],
        compiler_params=pltpu.CompilerParams(dimension_semantics=("arbitrary",) * 2,
                                             vmem_limit_bytes=VMEM_LIMIT),
        name="gla",
    )(gq, gk, la, gv, go, norm_g)


def _split3(a):
    hi = a.astype(BF16)
    lo = (a - hi.astype(F32)).astype(BF16)
    return hi, lo


def _mix_out_kernel(alpha, x_ref, da_ref, gl_ref, lng_ref, lnb_ref, wo_ref, g1_ref, b1_ref,
                    wr_hi_ref, wr_lo_ref, br_ref, lower_ref, h_ref, route_ref, cnt_ref):
    tm = x_ref.shape[0]
    i = pl.program_id(0)

    @pl.when(i == 0)
    def _():
        cnt_ref[...] = jnp.zeros_like(cnt_ref)

    logit_blocks = []
    for r0 in range(0, tm, ROW_BLOCK):
        rows = slice(r0, r0 + ROW_BLOCK)
        xn = _layer_norm(x_ref[rows, :], lng_ref[...], lnb_ref[...])
        mix = _dot(da_ref[rows, :], wo_ref[0:DA_V, :]) + _dot(gl_ref[rows, :], wo_ref[DA_V:, :])
        h = _layer_norm(alpha * xn + mix, g1_ref[...], b1_ref[...])
        h_ref[rows, :] = h
        h_hi, h_lo = _split3(h)
        logit_blocks.append(_dot(h_hi, wr_hi_ref[...]) + _dot(h_hi, wr_lo_ref[...]) + _dot(h_lo, wr_hi_ref[...])
                            + br_ref[...])
    logits = jnp.concatenate(logit_blocks, axis=0)
    lane = lax.broadcasted_iota(jnp.int32, (tm, LANES), 1)
    neg = -jnp.inf
    big = jnp.int32(LANES)

    def first_argmax(vals, valid):
        v = jnp.where(valid, vals, neg)
        mx = jnp.max(v, axis=-1, keepdims=True)
        idx = jnp.min(jnp.where(valid & (v == mx), lane, big), axis=-1, keepdims=True)
        return mx, idx

    is_group = lane < N_GROUPS
    g_max, g_top = first_argmax(logits, is_group)
    p_g = 1.0 / jnp.sum(jnp.where(is_group, jnp.exp(logits - g_max), 0.0), axis=-1, keepdims=True)

    e_lo = EXPERT_LANE0 + g_top * EXPERTS_PER_GROUP
    in_group = (lane >= e_lo) & (lane < e_lo + EXPERTS_PER_GROUP)
    v0, i0 = first_argmax(logits, in_group)
    v1, i1 = first_argmax(logits, in_group & (lane != i0))
    w1 = jnp.exp(v1 - v0)
    gate0 = p_g / (1.0 + w1)
    gate1 = p_g * w1 / (1.0 + w1)
    e0 = i0 - EXPERT_LANE0
    e1 = i1 - EXPERT_LANE0

    oh0 = jnp.where(lane == e0, 1.0, 0.0)
    oh1 = jnp.where(lane == e1, 1.0, 0.0)
    oh = oh0 + oh1
    before = _dot(lower_ref[...], oh.astype(BF16)) + cnt_ref[0:1, :]
    rank0 = jnp.sum(oh0 * before, axis=-1, keepdims=True)
    rank1 = jnp.sum(oh1 * before, axis=-1, keepdims=True)
    cnt_ref[...] = cnt_ref[...] + jnp.sum(oh, axis=0, keepdims=True)

    rec = jnp.zeros((tm, LANES), F32)
    for ln, val in ((R_E0, e0.astype(F32)), (R_E1, e1.astype(F32)), (R_G0, gate0), (R_G1, gate1),
                    (R_RANK0, rank0), (R_RANK1, rank1)):
        rec = jnp.where(lane == ln, val, rec)
    route_ref[...] = rec


def _mix_out(x2, da2, gl2, ln_g, ln_b, w_o, ln1_g, ln1_b, wr_hi, wr_lo, b_r, alpha, tm):
    T = x2.shape[0]
    row = lambda n: pl.BlockSpec((tm, n), lambda i: (i, 0))
    full = lambda a: pl.BlockSpec(a.shape, lambda i: (0,) * a.ndim)
    lower = jnp.tril(jnp.ones((tm, tm), BF16), -1)
    return pl.pallas_call(
        functools.partial(_mix_out_kernel, alpha),
        grid=(T // tm,),
        in_specs=[row(D_MODEL), row(DA_V), row(GLA_V), full(ln_g), full(ln_b), full(w_o),
                  full(ln1_g), full(ln1_b), full(wr_hi), full(wr_lo), full(b_r), full(lower)],
        out_specs=[row(D_MODEL), row(LANES), pl.BlockSpec((8, LANES), lambda i: (0, 0))],
        out_shape=[jax.ShapeDtypeStruct((T, D_MODEL), F32), jax.ShapeDtypeStruct((T, LANES), F32),
                   jax.ShapeDtypeStruct((8, LANES), F32)],
        compiler_params=pltpu.CompilerParams(dimension_semantics=("arbitrary",),
                                             vmem_limit_bytes=VMEM_LIMIT),
        name="mix_out",
    )(x2, da2, gl2, ln_g, ln_b, w_o, ln1_g, ln1_b, wr_hi, wr_lo, b_r, lower)


HIGH_HALF = 0xFFFF0000


def _pack_pairs(val):
    bits = lambda a: lax.bitcast_convert_type(a.astype(BF16).astype(F32), jnp.uint32)
    half = val.shape[1] // 2
    return (bits(val[:, :half]) >> 16) | (bits(val[:, half:]) & jnp.uint32(HIGH_HALF))


def _unpack_pairs(words):
    lo = lax.bitcast_convert_type(words << 16, F32)
    hi = lax.bitcast_convert_type(words & jnp.uint32(HIGH_HALF), F32)
    return jnp.concatenate([lo, hi], axis=1)


def _words_to_tiles(dst_ref, words):
    n = words.shape[0]
    for s in range(ROW_SUB):
        dst_ref[pl.ds(s, n, stride=ROW_SUB), :] = words[:, s * LANES:(s + 1) * LANES]


def _tiles_to_words(src_ref, r0, n):
    return jnp.concatenate([src_ref[pl.ds(r0 * ROW_SUB + s, n, stride=ROW_SUB), :] for s in range(ROW_SUB)],
                           axis=1)


def _rows_to_tiles(dst_ref, val):
    _words_to_tiles(dst_ref, _pack_pairs(val))


def _tiles_to_rows(src_ref, r0, n):
    return _unpack_pairs(_tiles_to_words(src_ref, r0, n))


def _row_tile(ref, r):
    return ref.at[pl.ds(pl.multiple_of(r * ROW_SUB, ROW_SUB), ROW_SUB), :]


def _dispatch_kernel(dest_ref, h_ref, xs_ref, stage, sems):
    tm = h_ref.shape[0]
    step = pl.program_id(0)
    half = step & 1
    n_tok = pl.num_programs(0) * tm
    n_iter = tm * TOP_K // DMA_UNROLL
    _rows_to_tiles(stage.at[half], h_ref[...])

    def row_copy(hf, t, slot):
        return pltpu.make_async_copy(_row_tile(stage.at[hf], t), _row_tile(xs_ref, slot), sems.at[hf])

    def start(i, c):
        for u in range(DMA_UNROLL):
            t = i * (DMA_UNROLL // TOP_K) + u // TOP_K
            row_copy(half, t, dest_ref[(u % TOP_K) * n_tok + step * tm + t]).start(priority=u % 2)
        return c

    lax.fori_loop(0, n_iter, start, 0)

    def drain(hf):
        def wait(i, c):
            for u in range(DMA_UNROLL):
                row_copy(hf, 0, 0).wait()
            return c
        lax.fori_loop(0, n_iter, wait, 0)

    @pl.when(step > 0)
    def _():
        drain(1 - half)

    @pl.when(step == pl.num_programs(0) - 1)
    def _():
        drain(half)
        n_pad = EXPERT_CHUNK * ROW_SUB
        stage[0, 0:n_pad, :] = jnp.zeros((n_pad, LANES), stage.dtype)
        pad = pltpu.make_async_copy(stage.at[0, 0:n_pad, :],
                                    xs_ref.at[pl.ds(pl.num_programs(0) * tm * TOP_K * ROW_SUB, n_pad), :], sems.at[0])
        pad.start()
        pad.wait()


def _dispatch(dest_flat, h, tm):
    T = h.shape[0]
    return pl.pallas_call(
        _dispatch_kernel,
        grid_spec=pltpu.PrefetchScalarGridSpec(
            num_scalar_prefetch=1,
            grid=(T // tm,),
            in_specs=[pl.BlockSpec((tm, D_MODEL), lambda i, d: (i, 0))],
            out_specs=pl.BlockSpec(memory_space=pl.ANY),
            scratch_shapes=[pltpu.VMEM((2, tm * ROW_SUB, LANES), jnp.uint32), pltpu.SemaphoreType.DMA((2,))]),
        out_shape=jax.ShapeDtypeStruct(((T * TOP_K + EXPERT_CHUNK) * ROW_SUB, LANES), jnp.uint32),
        compiler_params=pltpu.CompilerParams(dimension_semantics=("arbitrary",),
                                             vmem_limit_bytes=VMEM_LIMIT),
        name="dispatch",
    )(dest_flat, h)


def _experts_kernel(row0_ref, first_ref, xs_ref, wg_ref, wu_ref, wd_ref, ys_ref,
                    wgu_b, wd_b, xbuf, ybuf, xsem, ysem, pend_ref, *, n_rows):
    e = pl.program_id(0)
    ch = EXPERT_CHUNK
    depth = EXPERT_XBUFS - 1
    g_lo = first_ref[e]
    g_hi = first_ref[e + 1]
    total = first_ref[N_EXPERTS]

    def slab(ref, row0):
        return ref.at[pl.ds(pl.multiple_of(row0 * ROW_SUB, ROW_SUB), ch * ROW_SUB), :]

    def x_copy(g):
        slot = g & (EXPERT_XBUFS - 1)
        return pltpu.make_async_copy(slab(xs_ref, row0_ref[g]), xbuf.at[slot], xsem.at[slot])

    def y_copy(row0, half):
        return pltpu.make_async_copy(ybuf.at[half], slab(ys_ref, row0), ysem.at[half])

    def drain_y(half):
        @pl.when(pend_ref[half] == 1)
        def _():
            y_copy(0, half).wait()
            pend_ref[half] = 0

    @pl.when(e == 0)
    def _():
        for b in range(EXPERT_YBUFS):
            pend_ref[b] = 0
        ybuf[0] = jnp.zeros(ybuf.shape[1:], ybuf.dtype)
        y_copy(n_rows, 0).start()
        y_copy(n_rows, 0).wait()
        for d in range(depth):
            @pl.when(d < total)
            def _():
                x_copy(d).start(priority=ROW_STREAM_PRIORITY)

    @pl.when(g_hi > g_lo)
    def _():
        wgu_b[:, 0:D_EXPERT] = wg_ref[0].astype(BF16)
        wgu_b[:, D_EXPERT:] = wu_ref[0].astype(BF16)
        wd_b[...] = wd_ref[0].astype(BF16)

        def chunk(g, c):
            half = g & (EXPERT_YBUFS - 1)
            x_copy(g).wait()

            @pl.when(g + depth < total)
            def _():
                x_copy(g + depth).start(priority=ROW_STREAM_PRIORITY)

            xb = _tiles_to_rows(xbuf.at[g & (EXPERT_XBUFS - 1)], 0, ch).astype(BF16)
            gu = _dot(xb, wgu_b[...])
            gate = gu[:, 0:D_EXPERT]
            mid = (gate * jax.nn.sigmoid(gate) * gu[:, D_EXPERT:]).astype(BF16)
            words = _pack_pairs(_dot(mid, wd_b[...]))

            drain_y(half)

            @pl.when(g == g_lo)
            def _():
                drain_y((g - 1) & (EXPERT_YBUFS - 1))

            _words_to_tiles(ybuf.at[half], words)
            y_copy(row0_ref[g], half).start(priority=ROW_STREAM_PRIORITY)
            pend_ref[half] = 1
            return c

        lax.fori_loop(g_lo, g_hi, chunk, 0)

    @pl.when(e == pl.num_programs(0) - 1)
    def _():
        for b in range(EXPERT_YBUFS):
            drain_y(b)


def _chunk_metadata(seg_start, counts, n_rows):
    ch = EXPERT_CHUNK
    max_chunks = n_rows // ch + N_EXPERTS
    n_ch = (counts + (ch - 1)) // ch
    first = jnp.concatenate([jnp.zeros((1,), jnp.int32), jnp.cumsum(n_ch).astype(jnp.int32)])
    g = jnp.arange(max_chunks, dtype=jnp.int32)
    owner = jnp.minimum(jnp.sum((first[None, 1:] <= g[:, None]).astype(jnp.int32), axis=1), N_EXPERTS - 1)
    onehot = owner[:, None] == jnp.arange(N_EXPERTS, dtype=jnp.int32)
    pick = lambda tab: jnp.sum(jnp.where(onehot, tab[None, :], 0), axis=1)
    row0 = pick(seg_start) + (g - pick(first[:-1])) * ch
    row0 = jnp.where(g < first[-1], row0, 0)
    return row0.astype(jnp.int32), first


def _experts(seg_start, counts, xs, w_gate, w_up, w_down):
    n_rows = xs.shape[0] // ROW_SUB - EXPERT_CHUNK
    row0, first = _chunk_metadata(seg_start, counts, n_rows)
    per_expert = lambda shape: pl.BlockSpec((1,) + shape, lambda e, r, f: (e, 0, 0))
    slab = (EXPERT_CHUNK * ROW_SUB, LANES)
    return pl.pallas_call(
        functools.partial(_experts_kernel, n_rows=n_rows),
        grid_spec=pltpu.PrefetchScalarGridSpec(
            num_scalar_prefetch=2,
            grid=(N_EXPERTS,),
            in_specs=[pl.BlockSpec(memory_space=pl.ANY),
                      per_expert((D_MODEL, D_EXPERT)), per_expert((D_MODEL, D_EXPERT)),
                      per_expert((D_EXPERT, D_MODEL))],
            out_specs=pl.BlockSpec(memory_space=pl.ANY),
            scratch_shapes=[pltpu.VMEM((D_MODEL, 2 * D_EXPERT), BF16), pltpu.VMEM((D_EXPERT, D_MODEL), BF16),
                            pltpu.VMEM((EXPERT_XBUFS,) + slab, jnp.uint32), pltpu.VMEM((EXPERT_YBUFS,) + slab, jnp.uint32),
                            pltpu.SemaphoreType.DMA((EXPERT_XBUFS,)), pltpu.SemaphoreType.DMA((EXPERT_YBUFS,)),
                            pltpu.SMEM((EXPERT_YBUFS,), jnp.int32)]),
        out_shape=jax.ShapeDtypeStruct(xs.shape, jnp.uint32),
        compiler_params=pltpu.CompilerParams(dimension_semantics=("arbitrary",),
                                             vmem_limit_bytes=VMEM_LIMIT),
        name="experts",
    )(row0, first, xs, w_gate, w_up, w_down)


def _combine_kernel(alpha, dest_ref, h_ref, route_ref, g_ref, b_ref, y_ref, o_ref, buf, sems):
    tm = h_ref.shape[0]
    step = pl.program_id(0)
    half = step & 1
    n_iter = tm * TOP_K // DMA_UNROLL

    def row_copy(hf, src, slot):
        return pltpu.make_async_copy(_row_tile(y_ref, src), _row_tile(buf.at[hf], slot), sems.at[hf])

    def gather(st, hf):
        n_tok = pl.num_programs(0) * tm

        def start(i, c):
            for u in range(DMA_UNROLL):
                t = i * (DMA_UNROLL // TOP_K) + u // TOP_K
                slot = (u % TOP_K) * tm + t
                row_copy(hf, dest_ref[(u % TOP_K) * n_tok + st * tm + t], slot).start(priority=u % 2)
            return c

        lax.fori_loop(0, n_iter, start, 0)

    @pl.when(step == 0)
    def _():
        gather(0, 0)

    @pl.when(step + 1 < pl.num_programs(0))
    def _():
        gather(step + 1, 1 - half)

    def wait(i, c):
        for u in range(DMA_UNROLL):
            row_copy(half, 0, 0).wait()
        return c

    lax.fori_loop(0, n_iter, wait, 0)

    rec = route_ref[...]
    cur = buf.at[half]
    ffn = (rec[:, R_G0:R_G0 + 1] * _tiles_to_rows(cur, 0, tm)
           + rec[:, R_G1:R_G1 + 1] * _tiles_to_rows(cur, tm, tm))
    o_ref[...] = _layer_norm(alpha * h_ref[...] + ffn, g_ref[...], b_ref[...])


def _combine(dest_flat, h, route, ln2_g, ln2_b, y_sorted, alpha, tm):
    T = h.shape[0]
    return pl.pallas_call(
        functools.partial(_combine_kernel, alpha),
        grid_spec=pltpu.PrefetchScalarGridSpec(
            num_scalar_prefetch=1,
            grid=(T // tm,),
            in_specs=[pl.BlockSpec((tm, D_MODEL), lambda i, d: (i, 0)),
                      pl.BlockSpec((tm, LANES), lambda i, d: (i, 0)),
                      pl.BlockSpec((1, D_MODEL), lambda i, d: (0, 0)),
                      pl.BlockSpec((1, D_MODEL), lambda i, d: (0, 0)),
                      pl.BlockSpec(memory_space=pl.ANY)],
            out_specs=pl.BlockSpec((tm, D_MODEL), lambda i, d: (i, 0)),
            scratch_shapes=[pltpu.VMEM((2, TOP_K * tm * ROW_SUB, LANES), jnp.uint32), pltpu.SemaphoreType.DMA((2,))]),
        out_shape=jax.ShapeDtypeStruct((T, D_MODEL), F32),
        compiler_params=pltpu.CompilerParams(dimension_semantics=("arbitrary",),
                                             vmem_limit_bytes=VMEM_LIMIT),
        name="combine",
    )(dest_flat, h, route, ln2_g, ln2_b, y_sorted)


def kernel(x, positions, ln_in_g, ln_in_b, w_in, lam_q1, lam_k1, lam_q2, lam_k2, da_subln_g, gla_w_gate2, gla_b_gate2, gla_norm_g, w_o, ln1_g, ln1_b, router_w_group, router_b_group, router_w_expert, router_b_expert, w_gate, w_up, w_down, ln2_g, ln2_b):
    B, S, D = x.shape
    T = B * S
    depth = w_in.shape[0]
    assert depth == 1, "only a single layer is supported"
    alpha = (2 * depth) ** 0.25
    row2 = lambda a: a.reshape(1, -1)

    inv_freq = ROPE_THETA ** (-jnp.arange(0, DA_HEAD_DIM, 2, dtype=F32) / DA_HEAD_DIM)
    inv_freq = jnp.tile(jnp.repeat(inv_freq, 2), LANES // DA_HEAD_DIM).reshape(1, LANES)
    pos2 = positions.reshape(T, 1)

    cur = x.reshape(T, D)
    cur_g, cur_b = row2(ln_in_g), row2(ln_in_b)
    for l in range(depth):
        w = w_in[l]
        w_main = w[:, :D_MAIN].astype(BF16)
        w_glow = jnp.pad(w[:, D_MAIN:], ((0, 0), (0, LANES - GLA_GATE_RANK))).astype(BF16)
        w_gate2 = jnp.pad(gla_w_gate2[l], ((0, LANES - GLA_GATE_RANK), (0, 0))).astype(BF16)

        q, k, v, gq, gk, gv, go, la = _in_proj(cur, pos2, cur_g, cur_b, inv_freq, w_main, w_glow,
                                               w_gate2, row2(gla_b_gate2[l]), tm=512)
        lam_init = 0.8 - 0.6 * math.exp(-0.3 * l)
        sh = lambda a: a.reshape(B, S, a.shape[-1])
        da = _diff_attn(sh(q), sh(k), sh(v), row2(lam_q1[l]), row2(lam_k1[l]), row2(lam_q2[l]),
                        row2(lam_k2[l]), row2(da_subln_g[l]), lam_init)
        gl = _gla(sh(gq), sh(gk), sh(la), sh(gv), sh(go), row2(gla_norm_g[l]))

        w_r = jnp.zeros((D, LANES), F32)
        w_r = w_r.at[:, :N_GROUPS].set(router_w_group[l])
        w_r = w_r.at[:, EXPERT_LANE0:EXPERT_LANE0 + N_EXPERTS].set(router_w_expert[l])
        b_r = jnp.zeros((1, LANES), F32)
        b_r = b_r.at[0, :N_GROUPS].set(router_b_group[l])
        b_r = b_r.at[0, EXPERT_LANE0:EXPERT_LANE0 + N_EXPERTS].set(router_b_expert[l])
        wr_hi = w_r.astype(BF16)
        wr_lo = (w_r - wr_hi.astype(F32)).astype(BF16)

        h, route, cnt = _mix_out(cur, da.reshape(T, DA_V), gl.reshape(T, GLA_V), cur_g, cur_b,
                                 w_o[l].astype(BF16), row2(ln1_g[l]), row2(ln1_b[l]), wr_hi, wr_lo, b_r,
                                 alpha, tm=512)

        counts = cnt[0, :N_EXPERTS].astype(jnp.int32)
        seg_start = jnp.cumsum(counts) - counts
        rec_t = route[:, :8].T
        eid = rec_t[R_E0:R_E1 + 1].astype(jnp.int32)
        rank = rec_t[R_RANK0:R_RANK1 + 1].astype(jnp.int32)
        onehot = eid[None] == jnp.arange(N_EXPERTS, dtype=jnp.int32)[:, None, None]
        dest = jnp.sum(jnp.where(onehot, seg_start[:, None, None], 0), axis=0) + rank
        dest_flat = dest.reshape(TOP_K * T)

        xs = _dispatch(dest_flat, h, tm=256)
        ys = _experts(seg_start, counts, xs, w_gate[l], w_up[l], w_down[l])
        cur = _combine(dest_flat, h, route, row2(ln2_g[l]), row2(ln2_b[l]), ys, alpha, tm=256)
    return cur.reshape(B, S, D)
```

```python
import functools
import math

import jax
import jax.numpy as jnp
from jax import lax
from jax.experimental import pallas as pl
from jax.experimental.pallas import tpu as pltpu

F32 = jnp.float32
BF16 = jnp.bfloat16

D_MODEL = 1024
CHUNK = 64
ROPE_THETA = 10000.0
LN_EPS = 1e-5
LOG2_E = math.log2(math.e)

DA_HEADS = 4
DA_V_DIM = D_MODEL // (2 * DA_HEADS)
DA_HEAD_DIM = DA_V_DIM // 2
GLA_HEADS = 4
GLA_V_DIM = D_MODEL // (2 * GLA_HEADS)
GLA_KEY_DIM = GLA_V_DIM // 2
GLA_GATE_RANK = 16
GLA_GATE_NORMALIZER = 16.0

DA_Q = DA_HEADS * 2 * DA_HEAD_DIM
DA_K = DA_Q
DA_V = DA_HEADS * DA_V_DIM
GLA_Q = GLA_HEADS * GLA_KEY_DIM
GLA_K = GLA_Q
GLA_V = GLA_HEADS * GLA_V_DIM
GLA_OG = GLA_V
D_MAIN = DA_Q + DA_K + DA_V + GLA_Q + GLA_K + GLA_V + GLA_OG

N_GROUPS = 4
EXPERTS_PER_GROUP = 8
N_EXPERTS = N_GROUPS * EXPERTS_PER_GROUP
TOP_K = 2
D_EXPERT = D_MODEL // 2

LANES = 128
ROW_SUB = D_MODEL // (2 * LANES)
ROW_BLOCK = 256
EXPERT_CHUNK = 256
EXPERT_XBUFS = 8
EXPERT_YBUFS = 4
ROW_STREAM_PRIORITY = 1
GLA_BLOCK = 256
ATTN_BLOCK = 256
ATTN_PV_KEYS = 512
DMA_UNROLL = 8
VMEM_LIMIT = 48 * 1024 * 1024

R_E0, R_E1, R_G0, R_G1, R_RANK0, R_RANK1 = 0, 1, 2, 3, 4, 5
EXPERT_LANE0 = 32


def _layer_norm(x, g, b):
    mu = jnp.mean(x, axis=-1, keepdims=True)
    xc = x - mu
    var = jnp.mean(xc * xc, axis=-1, keepdims=True)
    return xc * lax.rsqrt(var + LN_EPS) * g + b


def _dot(a, b):
    return jnp.dot(a, b, preferred_element_type=F32)


def _dot_nt(a, b):
    return lax.dot_general(a, b, (((1,), (1,)), ((), ())), preferred_element_type=F32)


def _dot_tn(a, b):
    return lax.dot_general(a, b, (((0,), (0,)), ((), ())), preferred_element_type=F32)


def _in_proj_kernel(x_ref, pos_ref, g_ref, b_ref, invf_ref, w_ref, wgl_ref, wg2_ref, bg2_ref,
                    q_ref, k_ref, v_ref, gq_ref, gk_ref, gv_ref, go_ref, la_ref):
    tm = x_ref.shape[0]
    lane = lax.broadcasted_iota(jnp.int32, (ROW_BLOCK, LANES), 1)
    first = (lane & 1) == 0

    for r0 in range(0, tm, ROW_BLOCK):
        rows = slice(r0, r0 + ROW_BLOCK)
        xn = _layer_norm(x_ref[rows, :], g_ref[...], b_ref[...])
        xb = xn.astype(BF16)
        proj = _dot(xb, w_ref[...])

        ang = pos_ref[rows, :].astype(F32) * invf_ref[...]
        c = jnp.cos(ang)
        s = jnp.sin(ang)
        s_lo = jnp.where(first, -s, 0.0)
        s_hi = jnp.where(first, 0.0, s)

        def rope(t):
            out = []
            for j in range(t.shape[1] // LANES):
                tj = t[:, j * LANES:(j + 1) * LANES]
                up = pltpu.roll(tj, LANES - 1, 1)
                dn = pltpu.roll(tj, 1, 1)
                out.append(tj * c + up * s_lo + dn * s_hi)
            return jnp.concatenate(out, axis=1)

        o = 0
        q = rope(proj[:, o:o + DA_Q]) * (DA_HEAD_DIM ** -0.5 * LOG2_E)
        o += DA_Q
        k = rope(proj[:, o:o + DA_K])
        o += DA_K
        q_ref[rows, :] = q.astype(BF16)
        k_ref[rows, :] = k.astype(BF16)
        v_ref[rows, :] = proj[:, o:o + DA_V].astype(BF16)
        o += DA_V
        gq_ref[rows, :] = proj[:, o:o + GLA_Q].astype(BF16)
        o += GLA_Q
        gk_ref[rows, :] = proj[:, o:o + GLA_K].astype(BF16)
        o += GLA_K
        gv_ref[rows, :] = proj[:, o:o + GLA_V].astype(BF16)
        o += GLA_V
        go_ref[rows, :] = proj[:, o:o + GLA_OG].astype(BF16)

        g_low = _dot(xb, wgl_ref[...])
        z = _dot(g_low.astype(BF16), wg2_ref[...]) + bg2_ref[...]
        log_sig = jnp.minimum(z, 0.0) - jnp.log1p(jnp.exp(-jnp.abs(z)))
        la_ref[rows, :] = log_sig / GLA_GATE_NORMALIZER


def _in_proj(x2, pos2, ln_g, ln_b, inv_freq, w_main, w_glow, w_gate2, b_gate2, tm):
    T = x2.shape[0]
    row = lambda n: pl.BlockSpec((tm, n), lambda i: (i, 0))
    full = lambda a: pl.BlockSpec(a.shape, lambda i: (0,) * a.ndim)
    out_shape = [jax.ShapeDtypeStruct((T, n), dt) for n, dt in (
        (DA_Q, BF16), (DA_K, BF16), (DA_V, BF16), (GLA_Q, BF16), (GLA_K, BF16),
        (GLA_V, BF16), (GLA_OG, BF16), (GLA_K, F32))]
    return pl.pallas_call(
        _in_proj_kernel,
        grid=(T // tm,),
        in_specs=[row(D_MODEL), row(1), full(ln_g), full(ln_b), full(inv_freq), full(w_main),
                  full(w_glow), full(w_gate2), full(b_gate2)],
        out_specs=[row(s.shape[1]) for s in out_shape],
        out_shape=out_shape,
        compiler_params=pltpu.CompilerParams(dimension_semantics=("arbitrary",),
                                             vmem_limit_bytes=VMEM_LIMIT),
        name="in_proj",
    )(x2, pos2, ln_g, ln_b, inv_freq, w_main, w_glow, w_gate2, b_gate2)


def _diff_attn_kernel(lam_init, lq1_ref, lk1_ref, lq2_ref, lk2_ref, g_ref, q_ref, k_ref, v_ref, o_ref,
                      s_scr, p_scr, v_ext):
    S = q_ref.shape[1]
    tq = ATTN_BLOCK
    lam = (jnp.exp(jnp.sum(lq1_ref[...] * lk1_ref[...], axis=-1, keepdims=True))
           - jnp.exp(jnp.sum(lq2_ref[...] * lk2_ref[...], axis=-1, keepdims=True)) + lam_init)
    lane = lax.broadcasted_iota(jnp.int32, (tq, LANES), 1)
    rq = lax.broadcasted_iota(jnp.int32, (2 * tq, tq), 0) % tq // CHUNK
    ck = lax.broadcasted_iota(jnp.int32, (2 * tq, tq), 1) // CHUNK
    diag_mask = ck <= rq

    n_blk = S // tq
    st = [dict() for _ in range(n_blk)]
    v_ext[:, 0:DA_V_DIM] = v_ref[0]
    ext_lane = lax.broadcasted_iota(jnp.int32, (S, DA_V_DIM), 1)
    v_ext[:, DA_V_DIM:] = jnp.where(ext_lane == 0, 1.0, 0.0).astype(BF16)

    def stage_a(qi):
        s_buf = s_scr.at[qi % 2]

        def begin():
            q = q_ref[0, qi * tq:(qi + 1) * tq, :]
            zero = jnp.zeros_like(q)
            st[qi]["qq"] = jnp.concatenate([jnp.where(lane < DA_HEAD_DIM, q, zero),
                                            jnp.where(lane >= DA_HEAD_DIM, q, zero)], axis=0)
            st[qi]["m_acc"] = None

        def tile(j):
            s = _dot_nt(st[qi]["qq"], k_ref[0, j * tq:(j + 1) * tq, :])
            if j == qi:
                s = jnp.where(diag_mask, s, -jnp.inf)
            s_buf[:, j * tq:(j + 1) * tq] = s
            m_acc = st[qi]["m_acc"]
            for c0 in range(0, tq, LANES):
                sc = s[:, c0:c0 + LANES]
                m_acc = sc if m_acc is None else jnp.maximum(m_acc, sc)
            st[qi]["m_acc"] = m_acc

        def end():
            st[qi]["m"] = jnp.broadcast_to(jnp.max(st[qi]["m_acc"], axis=-1, keepdims=True), (2 * tq, LANES))

        return [begin] + [functools.partial(tile, j) for j in range(qi + 1)] + [end]

    def stage_b(qi):
        s_buf = s_scr.at[qi % 2]
        p_buf = p_scr.at[qi % 2]

        def cols(c0):
            p_buf[:, c0:c0 + LANES] = jnp.exp2((s_buf[:, c0:c0 + LANES] - st[qi]["m"]).astype(BF16))

        return [functools.partial(cols, c0) for c0 in range(0, (qi + 1) * tq, LANES)]

    def stage_c(qi):
        p_buf = p_scr.at[qi % 2]

        nk = (qi + 1) * tq
        st[qi]["a"] = None

        def part(k0):
            k1 = min(k0 + ATTN_PV_KEYS, nk)
            a = _dot(p_buf[:, k0:k1], v_ext[k0:k1, :])
            st[qi]["a"] = a if st[qi]["a"] is None else st[qi]["a"] + a

        def finish():
            a = st[qi]["a"][:, 0:DA_V_DIM] / st[qi]["a"][:, DA_V_DIM:DA_V_DIM + 1]
            o = a[0:tq] - lam * a[tq:2 * tq]
            o = o * lax.rsqrt(jnp.mean(o * o, axis=-1, keepdims=True) + LN_EPS) * g_ref[...]
            o_ref[0, qi * tq:(qi + 1) * tq, :] = (o * (1.0 - lam_init)).astype(o_ref.dtype)

        return [functools.partial(part, k0) for k0 in range(0, nk, ATTN_PV_KEYS)] + [finish]

    for t in range(n_blk + 2):
        stages = []
        if t < n_blk:
            stages.append(stage_a(t))
        if 0 <= t - 1 < n_blk:
            stages.append(stage_b(t - 1))
        if 0 <= t - 2 < n_blk:
            stages.append(stage_c(t - 2))
        merged = sorted(((i + 0.5) / len(ops), k, i, op) for k, ops in enumerate(stages) for i, op in enumerate(ops))
        for _, _, _, op in merged:
            op()


def _diff_attn(q, k, v, lam_q1, lam_k1, lam_q2, lam_k2, subln_g, lam_init):
    B, S, _ = q.shape
    vec = pl.BlockSpec((1, DA_HEAD_DIM), lambda b, h: (0, 0))
    seq = pl.BlockSpec((1, S, LANES), lambda b, h: (b, 0, h))
    return pl.pallas_call(
        functools.partial(_diff_attn_kernel, lam_init),
        grid=(B, DA_HEADS),
        in_specs=[vec, vec, vec, vec, pl.BlockSpec((1, DA_V_DIM), lambda b, h: (0, 0)), seq, seq, seq],
        out_specs=seq,
        out_shape=jax.ShapeDtypeStruct((B, S, DA_V), BF16),
        scratch_shapes=[pltpu.VMEM((2, 2 * ATTN_BLOCK, S), F32), pltpu.VMEM((2, 2 * ATTN_BLOCK, S), BF16),
                        pltpu.VMEM((S, 2 * DA_V_DIM), BF16)],
        compiler_params=pltpu.CompilerParams(dimension_semantics=("arbitrary",) * 2,
                                             vmem_limit_bytes=VMEM_LIMIT),
        name="diff_attn",
    )(lam_q1, lam_k1, lam_q2, lam_k2, subln_g, q, k, v)


def _gla_kernel(q_ref, k_ref, la_ref, v_ref, go_ref, ng_ref, o_ref, qt_s, oi_s, ds_s, dec_s):
    S = q_ref.shape[1]
    C = CHUNK
    BLK = GLA_BLOCK
    per_blk = BLK // C
    r = lax.broadcasted_iota(jnp.int32, (BLK, BLK), 0)
    c = lax.broadcasted_iota(jnp.int32, (BLK, BLK), 1)
    chunk_causal = (r // C == c // C) & (c <= r)
    tri = jnp.where(chunk_causal, 1.0, 0.0).astype(BF16)
    lane = lax.broadcasted_iota(jnp.int32, (BLK, LANES), 1)
    head_lanes = (lane < GLA_KEY_DIM, lane >= GLA_KEY_DIM)
    st_row = lax.broadcasted_iota(jnp.int32, (2 * GLA_V_DIM, LANES), 0)
    st_lane = lax.broadcasted_iota(jnp.int32, (2 * GLA_V_DIM, LANES), 1)
    own_keys = (st_row < GLA_V_DIM) == (st_lane < GLA_KEY_DIM)

    for b in range(S // BLK):
        r0 = b * BLK
        g = la_ref[0, r0:r0 + BLK, :]
        g1 = g.astype(BF16)
        e1 = g - g1.astype(F32)
        g2 = e1.astype(BF16)
        g3 = (e1 - g2.astype(F32)).astype(BF16)
        bcum = _dot(tri, g1) + _dot(tri, g2) + _dot(tri, g3)
        b_last = jnp.concatenate(
            [jnp.broadcast_to(bcum[i * C + C - 1:i * C + C, :], (C, LANES)) for i in range(per_blk)], axis=0)
        qf = q_ref[0, r0:r0 + BLK, :].astype(F32) * (GLA_KEY_DIM ** -0.5)
        kf = k_ref[0, r0:r0 + BLK, :].astype(F32)
        q_t = (qf * jnp.exp(bcum)).astype(BF16)
        k_t = (kf * jnp.exp(-bcum)).astype(BF16)
        k_end = (kf * jnp.exp(b_last - bcum)).astype(BF16)
        decay = jnp.exp(b_last)
        qt_s[r0:r0 + BLK, :] = q_t
        zero = jnp.zeros_like(q_t)
        for hh in range(2):
            att = jnp.where(chunk_causal, _dot_nt(jnp.where(head_lanes[hh], q_t, zero), k_t), 0.0).astype(BF16)
            oi_s[r0:r0 + BLK, hh * GLA_V_DIM:(hh + 1) * GLA_V_DIM] = _dot(
                att, v_ref[0, r0:r0 + BLK, hh * GLA_V_DIM:(hh + 1) * GLA_V_DIM])
        for i in range(per_blk):
            n = b * per_blk + i
            rows = slice(r0 + i * C, r0 + (i + 1) * C)
            inc = _dot_tn(v_ref[0, rows, :], k_end[i * C:(i + 1) * C, :])
            ds_s[n] = jnp.where(own_keys, inc, 0.0)
            dec_s[n:n + 1, :] = decay[i * C:i * C + 1, :]

    state = jnp.zeros((2 * GLA_V_DIM, LANES), F32)
    for n in range(S // C):
        rows = slice(n * C, (n + 1) * C)
        o = oi_s[rows, :] + _dot_nt(qt_s[rows, :], state.astype(BF16))
        state = state * dec_s[n:n + 1, :] + ds_s[n]
        for hh in range(2):
            cols = slice(hh * GLA_V_DIM, (hh + 1) * GLA_V_DIM)
            oh = o[:, cols]
            oh = oh * lax.rsqrt(jnp.mean(oh * oh, axis=-1, keepdims=True) + LN_EPS) * ng_ref[...]
            gate = go_ref[0, rows, cols].astype(F32)
            o_ref[0, rows, cols] = (oh * (gate * jax.nn.sigmoid(gate))).astype(o_ref.dtype)


def _gla(gq, gk, la, gv, go, norm_g):
    B, S, _ = gq.shape
    pairs = GLA_HEADS // 2
    narrow = pl.BlockSpec((1, S, LANES), lambda b, p: (b, 0, p))
    wide = pl.BlockSpec((1, S, 2 * GLA_V_DIM), lambda b, p: (b, 0, p))
    n_chunks = S // CHUNK
    return pl.pallas_call(
        _gla_kernel,
        grid=(B, pairs),
        in_specs=[narrow, narrow, narrow, wide, wide,
                  pl.BlockSpec((1, GLA_V_DIM), lambda b, p: (0, 0))],
        out_specs=wide,
        out_shape=jax.ShapeDtypeStruct((B, S, GLA_V), BF16),
        scratch_shapes=[pltpu.VMEM((S, LANES), BF16),
                        pltpu.VMEM((S, 2 * GLA_V_DIM), F32),
                        pltpu.VMEM((n_chunks, 2 * GLA_V_DIM, LANES), F32),
                        pltpu.VMEM((n_chunks, LANES), F32)],
        compiler_params=pltpu.CompilerParams(dimension_semantics=("arbitrary",) * 2,
                                             vmem_limit_bytes=VMEM_LIMIT),
        name="gla",
    )(gq, gk, la, gv, go, norm_g)


def _split3(a):
    hi = a.astype(BF16)
    lo = (a - hi.astype(F32)).astype(BF16)
    return hi, lo


def _mix_out_kernel(alpha, x_ref, da_ref, gl_ref, lng_ref, lnb_ref, wo_ref, g1_ref, b1_ref,
                    wr_hi_ref, wr_lo_ref, br_ref, lower_ref, h_ref, route_ref, cnt_ref):
    tm = x_ref.shape[0]
    i = pl.program_id(0)

    @pl.when(i == 0)
    def _():
        cnt_ref[...] = jnp.zeros_like(cnt_ref)

    logit_blocks = []
    for r0 in range(0, tm, ROW_BLOCK):
        rows = slice(r0, r0 + ROW_BLOCK)
        xn = _layer_norm(x_ref[rows, :], lng_ref[...], lnb_ref[...])
        mix = _dot(da_ref[rows, :], wo_ref[0:DA_V, :]) + _dot(gl_ref[rows, :], wo_ref[DA_V:, :])
        h = _layer_norm(alpha * xn + mix, g1_ref[...], b1_ref[...])
        h_ref[rows, :] = h
        h_hi, h_lo = _split3(h)
        logit_blocks.append(_dot(h_hi, wr_hi_ref[...]) + _dot(h_hi, wr_lo_ref[...]) + _dot(h_lo, wr_hi_ref[...])
                            + br_ref[...])
    logits = jnp.concatenate(logit_blocks, axis=0)
    lane = lax.broadcasted_iota(jnp.int32, (tm, LANES), 1)
    neg = -jnp.inf
    big = jnp.int32(LANES)

    def first_argmax(vals, valid):
        v = jnp.where(valid, vals, neg)
        mx = jnp.max(v, axis=-1, keepdims=True)
        idx = jnp.min(jnp.where(valid & (v == mx), lane, big), axis=-1, keepdims=True)
        return mx, idx

    is_group = lane < N_GROUPS
    g_max, g_top = first_argmax(logits, is_group)
    p_g = 1.0 / jnp.sum(jnp.where(is_group, jnp.exp(logits - g_max), 0.0), axis=-1, keepdims=True)

    e_lo = EXPERT_LANE0 + g_top * EXPERTS_PER_GROUP
    in_group = (lane >= e_lo) & (lane < e_lo + EXPERTS_PER_GROUP)
    v0, i0 = first_argmax(logits, in_group)
    v1, i1 = first_argmax(logits, in_group & (lane != i0))
    w1 = jnp.exp(v1 - v0)
    gate0 = p_g / (1.0 + w1)
    gate1 = p_g * w1 / (1.0 + w1)
    e0 = i0 - EXPERT_LANE0
    e1 = i1 - EXPERT_LANE0

    oh0 = jnp.where(lane == e0, 1.0, 0.0)
    oh1 = jnp.where(lane == e1, 1.0, 0.0)
    oh = oh0 + oh1
    before = _dot(lower_ref[...], oh.astype(BF16)) + cnt_ref[0:1, :]
    rank0 = jnp.sum(oh0 * before, axis=-1, keepdims=True)
    rank1 = jnp.sum(oh1 * before, axis=-1, keepdims=True)
    cnt_ref[...] = cnt_ref[...] + jnp.sum(oh, axis=0, keepdims=True)

    rec = jnp.zeros((tm, LANES), F32)
    for ln, val in ((R_E0, e0.astype(F32)), (R_E1, e1.astype(F32)), (R_G0, gate0), (R_G1, gate1),
                    (R_RANK0, rank0), (R_RANK1, rank1)):
        rec = jnp.where(lane == ln, val, rec)
    route_ref[...] = rec


def _mix_out(x2, da2, gl2, ln_g, ln_b, w_o, ln1_g, ln1_b, wr_hi, wr_lo, b_r, alpha, tm):
    T = x2.shape[0]
    row = lambda n: pl.BlockSpec((tm, n), lambda i: (i, 0))
    full = lambda a: pl.BlockSpec(a.shape, lambda i: (0,) * a.ndim)
    lower = jnp.tril(jnp.ones((tm, tm), BF16), -1)
    return pl.pallas_call(
        functools.partial(_mix_out_kernel, alpha),
        grid=(T // tm,),
        in_specs=[row(D_MODEL), row(DA_V), row(GLA_V), full(ln_g), full(ln_b), full(w_o),
                  full(ln1_g), full(ln1_b), full(wr_hi), full(wr_lo), full(b_r), full(lower)],
        out_specs=[row(D_MODEL), row(LANES), pl.BlockSpec((8, LANES), lambda i: (0, 0))],
        out_shape=[jax.ShapeDtypeStruct((T, D_MODEL), F32), jax.ShapeDtypeStruct((T, LANES), F32),
                   jax.ShapeDtypeStruct((8, LANES), F32)],
        compiler_params=pltpu.CompilerParams(dimension_semantics=("arbitrary",),
                                             vmem_limit_bytes=VMEM_LIMIT),
        name="mix_out",
    )(x2, da2, gl2, ln_g, ln_b, w_o, ln1_g, ln1_b, wr_hi, wr_lo, b_r, lower)


HIGH_HALF = 0xFFFF0000


def _pack_pairs(val):
    bits = lambda a: lax.bitcast_convert_type(a.astype(BF16).astype(F32), jnp.uint32)
    half = val.shape[1] // 2
    return (bits(val[:, :half]) >> 16) | (bits(val[:, half:]) & jnp.uint32(HIGH_HALF))


def _unpack_pairs(words):
    lo = lax.bitcast_convert_type(words << 16, F32)
    hi = lax.bitcast_convert_type(words & jnp.uint32(HIGH_HALF), F32)
    return jnp.concatenate([lo, hi], axis=1)


def _words_to_tiles(dst_ref, words):
    n = words.shape[0]
    for s in range(ROW_SUB):
        dst_ref[pl.ds(s, n, stride=ROW_SUB), :] = words[:, s * LANES:(s + 1) * LANES]


def _tiles_to_words(src_ref, r0, n):
    return jnp.concatenate([src_ref[pl.ds(r0 * ROW_SUB + s, n, stride=ROW_SUB), :] for s in range(ROW_SUB)],
                           axis=1)


def _rows_to_tiles(dst_ref, val):
    _words_to_tiles(dst_ref, _pack_pairs(val))


def _tiles_to_rows(src_ref, r0, n):
    return _unpack_pairs(_tiles_to_words(src_ref, r0, n))


def _row_tile(ref, r):
    return ref.at[pl.ds(pl.multiple_of(r * ROW_SUB, ROW_SUB), ROW_SUB), :]


def _dispatch_kernel(dest_ref, h_ref, xs_ref, stage, sems):
    tm = h_ref.shape[0]
    step = pl.program_id(0)
    half = step & 1
    base = step * (tm * TOP_K)
    n_iter = tm * TOP_K // DMA_UNROLL
    _rows_to_tiles(stage.at[half], h_ref[...])

    def row_copy(hf, t, slot):
        return pltpu.make_async_copy(_row_tile(stage.at[hf], t), _row_tile(xs_ref, slot), sems.at[hf])

    def start(i, c):
        for u in range(DMA_UNROLL):
            t = i * (DMA_UNROLL // TOP_K) + u // TOP_K
            row_copy(half, t, dest_ref[base + i * DMA_UNROLL + u]).start(priority=u % 2)
        return c

    lax.fori_loop(0, n_iter, start, 0)

    def drain(hf):
        def wait(i, c):
            for u in range(DMA_UNROLL):
                row_copy(hf, 0, 0).wait()
            return c
        lax.fori_loop(0, n_iter, wait, 0)

    @pl.when(step > 0)
    def _():
        drain(1 - half)

    @pl.when(step == pl.num_programs(0) - 1)
    def _():
        drain(half)
        n_pad = EXPERT_CHUNK * ROW_SUB
        stage[0, 0:n_pad, :] = jnp.zeros((n_pad, LANES), stage.dtype)
        pad = pltpu.make_async_copy(stage.at[0, 0:n_pad, :],
                                    xs_ref.at[pl.ds(pl.num_programs(0) * tm * TOP_K * ROW_SUB, n_pad), :], sems.at[0])
        pad.start()
        pad.wait()


def _dispatch(dest_flat, h, tm):
    T = h.shape[0]
    return pl.pallas_call(
        _dispatch_kernel,
        grid_spec=pltpu.PrefetchScalarGridSpec(
            num_scalar_prefetch=1,
            grid=(T // tm,),
            in_specs=[pl.BlockSpec((tm, D_MODEL), lambda i, d: (i, 0))],
            out_specs=pl.BlockSpec(memory_space=pl.ANY),
            scratch_shapes=[pltpu.VMEM((2, tm * ROW_SUB, LANES), jnp.uint32), pltpu.SemaphoreType.DMA((2,))]),
        out_shape=jax.ShapeDtypeStruct(((T * TOP_K + EXPERT_CHUNK) * ROW_SUB, LANES), jnp.uint32),
        compiler_params=pltpu.CompilerParams(dimension_semantics=("arbitrary",),
                                             vmem_limit_bytes=VMEM_LIMIT),
        name="dispatch",
    )(dest_flat, h)


def _experts_kernel(row0_ref, first_ref, xs_ref, wg_ref, wu_ref, wd_ref, ys_ref,
                    wgu_b, wd_b, xbuf, ybuf, xsem, ysem, pend_ref, *, n_rows):
    e = pl.program_id(0)
    ch = EXPERT_CHUNK
    depth = EXPERT_XBUFS - 1
    g_lo = first_ref[e]
    g_hi = first_ref[e + 1]
    total = first_ref[N_EXPERTS]

    def slab(ref, row0):
        return ref.at[pl.ds(pl.multiple_of(row0 * ROW_SUB, ROW_SUB), ch * ROW_SUB), :]

    def x_copy(g):
        slot = g & (EXPERT_XBUFS - 1)
        return pltpu.make_async_copy(slab(xs_ref, row0_ref[g]), xbuf.at[slot], xsem.at[slot])

    def y_copy(row0, half):
        return pltpu.make_async_copy(ybuf.at[half], slab(ys_ref, row0), ysem.at[half])

    def drain_y(half):
        @pl.when(pend_ref[half] == 1)
        def _():
            y_copy(0, half).wait()
            pend_ref[half] = 0

    @pl.when(e == 0)
    def _():
        for b in range(EXPERT_YBUFS):
            pend_ref[b] = 0
        ybuf[0] = jnp.zeros(ybuf.shape[1:], ybuf.dtype)
        y_copy(n_rows, 0).start()
        y_copy(n_rows, 0).wait()
        for d in range(depth):
            @pl.when(d < total)
            def _():
                x_copy(d).start(priority=ROW_STREAM_PRIORITY)

    @pl.when(g_hi > g_lo)
    def _():
        wgu_b[:, 0:D_EXPERT] = wg_ref[0].astype(BF16)
        wgu_b[:, D_EXPERT:] = wu_ref[0].astype(BF16)
        wd_b[...] = wd_ref[0].astype(BF16)

        def chunk(g, c):
            half = g & (EXPERT_YBUFS - 1)
            x_copy(g).wait()

            @pl.when(g + depth < total)
            def _():
                x_copy(g + depth).start(priority=ROW_STREAM_PRIORITY)

            xb = _tiles_to_rows(xbuf.at[g & (EXPERT_XBUFS - 1)], 0, ch).astype(BF16)
            gu = _dot(xb, wgu_b[...])
            gate = gu[:, 0:D_EXPERT]
            mid = (gate * jax.nn.sigmoid(gate) * gu[:, D_EXPERT:]).astype(BF16)
            words = _pack_pairs(_dot(mid, wd_b[...]))

            drain_y(half)

            @pl.when(g == g_lo)
            def _():
                drain_y((g - 1) & (EXPERT_YBUFS - 1))

            _words_to_tiles(ybuf.at[half], words)
            y_copy(row0_ref[g], half).start(priority=ROW_STREAM_PRIORITY)
            pend_ref[half] = 1
            return c

        lax.fori_loop(g_lo, g_hi, chunk, 0)

    @pl.when(e == pl.num_programs(0) - 1)
    def _():
        for b in range(EXPERT_YBUFS):
            drain_y(b)


def _chunk_metadata(seg_start, counts, n_rows):
    ch = EXPERT_CHUNK
    max_chunks = n_rows // ch + N_EXPERTS
    n_ch = (counts + (ch - 1)) // ch
    first = jnp.concatenate([jnp.zeros((1,), jnp.int32), jnp.cumsum(n_ch).astype(jnp.int32)])
    g = jnp.arange(max_chunks, dtype=jnp.int32)
    owner = jnp.minimum(jnp.sum((first[None, 1:] <= g[:, None]).astype(jnp.int32), axis=1), N_EXPERTS - 1)
    onehot = owner[:, None] == jnp.arange(N_EXPERTS, dtype=jnp.int32)
    pick = lambda tab: jnp.sum(jnp.where(onehot, tab[None, :], 0), axis=1)
    row0 = pick(seg_start) + (g - pick(first[:-1])) * ch
    row0 = jnp.where(g < first[-1], row0, 0)
    return row0.astype(jnp.int32), first


def _experts(seg_start, counts, xs, w_gate, w_up, w_down):
    n_rows = xs.shape[0] // ROW_SUB - EXPERT_CHUNK
    row0, first = _chunk_metadata(seg_start, counts, n_rows)
    per_expert = lambda shape: pl.BlockSpec((1,) + shape, lambda e, r, f: (e, 0, 0))
    slab = (EXPERT_CHUNK * ROW_SUB, LANES)
    return pl.pallas_call(
        functools.partial(_experts_kernel, n_rows=n_rows),
        grid_spec=pltpu.PrefetchScalarGridSpec(
            num_scalar_prefetch=2,
            grid=(N_EXPERTS,),
            in_specs=[pl.BlockSpec(memory_space=pl.ANY),
                      per_expert((D_MODEL, D_EXPERT)), per_expert((D_MODEL, D_EXPERT)),
                      per_expert((D_EXPERT, D_MODEL))],
            out_specs=pl.BlockSpec(memory_space=pl.ANY),
            scratch_shapes=[pltpu.VMEM((D_MODEL, 2 * D_EXPERT), BF16), pltpu.VMEM((D_EXPERT, D_MODEL), BF16),
                            pltpu.VMEM((EXPERT_XBUFS,) + slab, jnp.uint32), pltpu.VMEM((EXPERT_YBUFS,) + slab, jnp.uint32),
                            pltpu.SemaphoreType.DMA((EXPERT_XBUFS,)), pltpu.SemaphoreType.DMA((EXPERT_YBUFS,)),
                            pltpu.SMEM((EXPERT_YBUFS,), jnp.int32)]),
        out_shape=jax.ShapeDtypeStruct(xs.shape, jnp.uint32),
        compiler_params=pltpu.CompilerParams(dimension_semantics=("arbitrary",),
                                             vmem_limit_bytes=VMEM_LIMIT),
        name="experts",
    )(row0, first, xs, w_gate, w_up, w_down)


def _combine_kernel(alpha, dest_ref, h_ref, route_ref, g_ref, b_ref, y_ref, o_ref, buf, sems):
    tm = h_ref.shape[0]
    step = pl.program_id(0)
    half = step & 1
    n_iter = tm * TOP_K // DMA_UNROLL

    def row_copy(hf, src, slot):
        return pltpu.make_async_copy(_row_tile(y_ref, src), _row_tile(buf.at[hf], slot), sems.at[hf])

    def gather(st, hf):
        base = st * (tm * TOP_K)

        def start(i, c):
            for u in range(DMA_UNROLL):
                slot = (u % TOP_K) * tm + i * (DMA_UNROLL // TOP_K) + u // TOP_K
                row_copy(hf, dest_ref[base + i * DMA_UNROLL + u], slot).start(priority=u % 2)
            return c

        lax.fori_loop(0, n_iter, start, 0)

    @pl.when(step == 0)
    def _():
        gather(0, 0)

    @pl.when(step + 1 < pl.num_programs(0))
    def _():
        gather(step + 1, 1 - half)

    def wait(i, c):
        for u in range(DMA_UNROLL):
            row_copy(half, 0, 0).wait()
        return c

    lax.fori_loop(0, n_iter, wait, 0)

    rec = route_ref[...]
    cur = buf.at[half]
    ffn = (rec[:, R_G0:R_G0 + 1] * _tiles_to_rows(cur, 0, tm)
           + rec[:, R_G1:R_G1 + 1] * _tiles_to_rows(cur, tm, tm))
    o_ref[...] = _layer_norm(alpha * h_ref[...] + ffn, g_ref[...], b_ref[...])


def _combine(dest_flat, h, route, ln2_g, ln2_b, y_sorted, alpha, tm):
    T = h.shape[0]
    return pl.pallas_call(
        functools.partial(_combine_kernel, alpha),
        grid_spec=pltpu.PrefetchScalarGridSpec(
            num_scalar_prefetch=1,
            grid=(T // tm,),
            in_specs=[pl.BlockSpec((tm, D_MODEL), lambda i, d: (i, 0)),
                      pl.BlockSpec((tm, LANES), lambda i, d: (i, 0)),
                      pl.BlockSpec((1, D_MODEL), lambda i, d: (0, 0)),
                      pl.BlockSpec((1, D_MODEL), lambda i, d: (0, 0)),
                      pl.BlockSpec(memory_space=pl.ANY)],
            out_specs=pl.BlockSpec((tm, D_MODEL), lambda i, d: (i, 0)),
            scratch_shapes=[pltpu.VMEM((2, TOP_K * tm * ROW_SUB, LANES), jnp.uint32), pltpu.SemaphoreType.DMA((2,))]),
        out_shape=jax.ShapeDtypeStruct((T, D_MODEL), F32),
        compiler_params=pltpu.CompilerParams(dimension_semantics=("arbitrary",),
                                             vmem_limit_bytes=VMEM_LIMIT),
        name="combine",
    )(dest_flat, h, route, ln2_g, ln2_b, y_sorted)


def kernel(x, positions, ln_in_g, ln_in_b, w_in, lam_q1, lam_k1, lam_q2, lam_k2, da_subln_g, gla_w_gate2, gla_b_gate2, gla_norm_g, w_o, ln1_g, ln1_b, router_w_group, router_b_group, router_w_expert, router_b_expert, w_gate, w_up, w_down, ln2_g, ln2_b):
    B, S, D = x.shape
    T = B * S
    depth = w_in.shape[0]
    assert depth == 1, "only a single layer is supported"
    alpha = (2 * depth) ** 0.25
    row2 = lambda a: a.reshape(1, -1)

    inv_freq = ROPE_THETA ** (-jnp.arange(0, DA_HEAD_DIM, 2, dtype=F32) / DA_HEAD_DIM)
    inv_freq = jnp.tile(jnp.repeat(inv_freq, 2), LANES // DA_HEAD_DIM).reshape(1, LANES)
    pos2 = positions.reshape(T, 1)

    cur = x.reshape(T, D)
    cur_g, cur_b = row2(ln_in_g), row2(ln_in_b)
    for l in range(depth):
        w = w_in[l]
        w_main = w[:, :D_MAIN].astype(BF16)
        w_glow = jnp.pad(w[:, D_MAIN:], ((0, 0), (0, LANES - GLA_GATE_RANK))).astype(BF16)
        w_gate2 = jnp.pad(gla_w_gate2[l], ((0, LANES - GLA_GATE_RANK), (0, 0))).astype(BF16)

        q, k, v, gq, gk, gv, go, la = _in_proj(cur, pos2, cur_g, cur_b, inv_freq, w_main, w_glow,
                                               w_gate2, row2(gla_b_gate2[l]), tm=512)
        lam_init = 0.8 - 0.6 * math.exp(-0.3 * l)
        sh = lambda a: a.reshape(B, S, a.shape[-1])
        da = _diff_attn(sh(q), sh(k), sh(v), row2(lam_q1[l]), row2(lam_k1[l]), row2(lam_q2[l]),
                        row2(lam_k2[l]), row2(da_subln_g[l]), lam_init)
        gl = _gla(sh(gq), sh(gk), sh(la), sh(gv), sh(go), row2(gla_norm_g[l]))

        w_r = jnp.zeros((D, LANES), F32)
        w_r = w_r.at[:, :N_GROUPS].set(router_w_group[l])
        w_r = w_r.at[:, EXPERT_LANE0:EXPERT_LANE0 + N_EXPERTS].set(router_w_expert[l])
        b_r = jnp.zeros((1, LANES), F32)
        b_r = b_r.at[0, :N_GROUPS].set(router_b_group[l])
        b_r = b_r.at[0, EXPERT_LANE0:EXPERT_LANE0 + N_EXPERTS].set(router_b_expert[l])
        wr_hi = w_r.astype(BF16)
        wr_lo = (w_r - wr_hi.astype(F32)).astype(BF16)

        h, route, cnt = _mix_out(cur, da.reshape(T, DA_V), gl.reshape(T, GLA_V), cur_g, cur_b,
                                 w_o[l].astype(BF16), row2(ln1_g[l]), row2(ln1_b[l]), wr_hi, wr_lo, b_r,
                                 alpha, tm=512)

        counts = cnt[0, :N_EXPERTS].astype(jnp.int32)
        seg_start = jnp.cumsum(counts) - counts
        eid = route[:, R_E0:R_E1 + 1].astype(jnp.int32)
        rank = route[:, R_RANK0:R_RANK1 + 1].astype(jnp.int32)
        onehot = eid[..., None] == jnp.arange(N_EXPERTS, dtype=jnp.int32)
        dest = jnp.sum(jnp.where(onehot, seg_start, 0), axis=-1) + rank
        dest_flat = dest.reshape(T * TOP_K)

        xs = _dispatch(dest_flat, h, tm=512)
        ys = _experts(seg_start, counts, xs, w_gate[l], w_up[l], w_down[l])
        cur = _combine(dest_flat, h, route, row2(ln2_g[l]), row2(ln2_b[l]), ys, alpha, tm=512)
    return cur.reshape(B, S, D)
```

```python
import functools
import math

import jax
import jax.numpy as jnp
from jax import lax
from jax.experimental import pallas as pl
from jax.experimental.pallas import tpu as pltpu

F32 = jnp.float32
BF16 = jnp.bfloat16

D_MODEL = 1024
CHUNK = 64
ROPE_THETA = 10000.0
LN_EPS = 1e-5
LOG2_E = math.log2(math.e)

DA_HEADS = 4
DA_V_DIM = D_MODEL // (2 * DA_HEADS)
DA_HEAD_DIM = DA_V_DIM // 2
GLA_HEADS = 4
GLA_V_DIM = D_MODEL // (2 * GLA_HEADS)
GLA_KEY_DIM = GLA_V_DIM // 2
GLA_GATE_RANK = 16
GLA_GATE_NORMALIZER = 16.0

DA_Q = DA_HEADS * 2 * DA_HEAD_DIM
DA_K = DA_Q
DA_V = DA_HEADS * DA_V_DIM
GLA_Q = GLA_HEADS * GLA_KEY_DIM
GLA_K = GLA_Q
GLA_V = GLA_HEADS * GLA_V_DIM
GLA_OG = GLA_V
D_MAIN = DA_Q + DA_K + DA_V + GLA_Q + GLA_K + GLA_V + GLA_OG

N_GROUPS = 4
EXPERTS_PER_GROUP = 8
N_EXPERTS = N_GROUPS * EXPERTS_PER_GROUP
TOP_K = 2
D_EXPERT = D_MODEL // 2

LANES = 128
ROW_SUB = D_MODEL // (2 * LANES)
ROW_BLOCK = 256
EXPERT_CHUNK = 512
EXPERT_XBUFS = 8
EXPERT_YBUFS = 4
ROW_STREAM_PRIORITY = 1
GLA_BLOCK = 256
ATTN_BLOCK = 256
ATTN_PV_KEYS = 512
DMA_UNROLL = 8
VMEM_LIMIT = 48 * 1024 * 1024

R_E0, R_E1, R_G0, R_G1, R_RANK0, R_RANK1 = 0, 1, 2, 3, 4, 5
EXPERT_LANE0 = 32


def _layer_norm(x, g, b):
    mu = jnp.mean(x, axis=-1, keepdims=True)
    xc = x - mu
    var = jnp.mean(xc * xc, axis=-1, keepdims=True)
    return xc * lax.rsqrt(var + LN_EPS) * g + b


def _dot(a, b):
    return jnp.dot(a, b, preferred_element_type=F32)


def _dot_nt(a, b):
    return lax.dot_general(a, b, (((1,), (1,)), ((), ())), preferred_element_type=F32)


def _dot_tn(a, b):
    return lax.dot_general(a, b, (((0,), (0,)), ((), ())), preferred_element_type=F32)


def _in_proj_kernel(x_ref, pos_ref, g_ref, b_ref, invf_ref, w_ref, wgl_ref, wg2_ref, bg2_ref,
                    q_ref, k_ref, v_ref, gq_ref, gk_ref, gv_ref, go_ref, la_ref):
    tm = x_ref.shape[0]
    lane = lax.broadcasted_iota(jnp.int32, (ROW_BLOCK, LANES), 1)
    first = (lane & 1) == 0

    for r0 in range(0, tm, ROW_BLOCK):
        rows = slice(r0, r0 + ROW_BLOCK)
        xn = _layer_norm(x_ref[rows, :], g_ref[...], b_ref[...])
        xb = xn.astype(BF16)
        proj = _dot(xb, w_ref[...])

        ang = pos_ref[rows, :].astype(F32) * invf_ref[...]
        c = jnp.cos(ang)
        s = jnp.sin(ang)
        s_lo = jnp.where(first, -s, 0.0)
        s_hi = jnp.where(first, 0.0, s)

        def rope(t):
            out = []
            for j in range(t.shape[1] // LANES):
                tj = t[:, j * LANES:(j + 1) * LANES]
                up = pltpu.roll(tj, LANES - 1, 1)
                dn = pltpu.roll(tj, 1, 1)
                out.append(tj * c + up * s_lo + dn * s_hi)
            return jnp.concatenate(out, axis=1)

        o = 0
        q = rope(proj[:, o:o + DA_Q]) * (DA_HEAD_DIM ** -0.5 * LOG2_E)
        o += DA_Q
        k = rope(proj[:, o:o + DA_K])
        o += DA_K
        q_ref[rows, :] = q.astype(BF16)
        k_ref[rows, :] = k.astype(BF16)
        v_ref[rows, :] = proj[:, o:o + DA_V].astype(BF16)
        o += DA_V
        gq_ref[rows, :] = proj[:, o:o + GLA_Q].astype(BF16)
        o += GLA_Q
        gk_ref[rows, :] = proj[:, o:o + GLA_K].astype(BF16)
        o += GLA_K
        gv_ref[rows, :] = proj[:, o:o + GLA_V].astype(BF16)
        o += GLA_V
        go_ref[rows, :] = proj[:, o:o + GLA_OG].astype(BF16)

        g_low = _dot(xb, wgl_ref[...])
        z = _dot(g_low.astype(BF16), wg2_ref[...]) + bg2_ref[...]
        log_sig = jnp.minimum(z, 0.0) - jnp.log1p(jnp.exp(-jnp.abs(z)))
        la_ref[rows, :] = log_sig / GLA_GATE_NORMALIZER


def _in_proj(x2, pos2, ln_g, ln_b, inv_freq, w_main, w_glow, w_gate2, b_gate2, tm):
    T = x2.shape[0]
    row = lambda n: pl.BlockSpec((tm, n), lambda i: (i, 0))
    full = lambda a: pl.BlockSpec(a.shape, lambda i: (0,) * a.ndim)
    out_shape = [jax.ShapeDtypeStruct((T, n), dt) for n, dt in (
        (DA_Q, BF16), (DA_K, BF16), (DA_V, BF16), (GLA_Q, BF16), (GLA_K, BF16),
        (GLA_V, BF16), (GLA_OG, BF16), (GLA_K, F32))]
    return pl.pallas_call(
        _in_proj_kernel,
        grid=(T // tm,),
        in_specs=[row(D_MODEL), row(1), full(ln_g), full(ln_b), full(inv_freq), full(w_main),
                  full(w_glow), full(w_gate2), full(b_gate2)],
        out_specs=[row(s.shape[1]) for s in out_shape],
        out_shape=out_shape,
        compiler_params=pltpu.CompilerParams(dimension_semantics=("arbitrary",),
                                             vmem_limit_bytes=VMEM_LIMIT),
        name="in_proj",
    )(x2, pos2, ln_g, ln_b, inv_freq, w_main, w_glow, w_gate2, b_gate2)


def _diff_attn_kernel(lam_init, lq1_ref, lk1_ref, lq2_ref, lk2_ref, g_ref, q_ref, k_ref, v_ref, o_ref,
                      s_scr, p_scr, v_ext):
    S = q_ref.shape[1]
    tq = ATTN_BLOCK
    lam = (jnp.exp(jnp.sum(lq1_ref[...] * lk1_ref[...], axis=-1, keepdims=True))
           - jnp.exp(jnp.sum(lq2_ref[...] * lk2_ref[...], axis=-1, keepdims=True)) + lam_init)
    lane = lax.broadcasted_iota(jnp.int32, (tq, LANES), 1)
    rq = lax.broadcasted_iota(jnp.int32, (2 * tq, tq), 0) % tq // CHUNK
    ck = lax.broadcasted_iota(jnp.int32, (2 * tq, tq), 1) // CHUNK
    diag_mask = ck <= rq

    n_blk = S // tq
    st = [dict() for _ in range(n_blk)]
    v_ext[:, 0:DA_V_DIM] = v_ref[0]
    ext_lane = lax.broadcasted_iota(jnp.int32, (S, DA_V_DIM), 1)
    v_ext[:, DA_V_DIM:] = jnp.where(ext_lane == 0, 1.0, 0.0).astype(BF16)

    def stage_a(qi):
        s_buf = s_scr.at[qi % 2]

        def begin():
            q = q_ref[0, qi * tq:(qi + 1) * tq, :]
            zero = jnp.zeros_like(q)
            st[qi]["qq"] = jnp.concatenate([jnp.where(lane < DA_HEAD_DIM, q, zero),
                                            jnp.where(lane >= DA_HEAD_DIM, q, zero)], axis=0)
            st[qi]["m_acc"] = None

        def tile(j):
            s = _dot_nt(st[qi]["qq"], k_ref[0, j * tq:(j + 1) * tq, :])
            if j == qi:
                s = jnp.where(diag_mask, s, -jnp.inf)
            s_buf[:, j * tq:(j + 1) * tq] = s
            m_acc = st[qi]["m_acc"]
            for c0 in range(0, tq, LANES):
                sc = s[:, c0:c0 + LANES]
                m_acc = sc if m_acc is None else jnp.maximum(m_acc, sc)
            st[qi]["m_acc"] = m_acc

        def end():
            st[qi]["m"] = jnp.broadcast_to(jnp.max(st[qi]["m_acc"], axis=-1, keepdims=True), (2 * tq, LANES))

        return [begin] + [functools.partial(tile, j) for j in range(qi + 1)] + [end]

    def stage_b(qi):
        s_buf = s_scr.at[qi % 2]
        p_buf = p_scr.at[qi % 2]

        def cols(c0):
            p_buf[:, c0:c0 + LANES] = jnp.exp2((s_buf[:, c0:c0 + LANES] - st[qi]["m"]).astype(BF16))

        return [functools.partial(cols, c0) for c0 in range(0, (qi + 1) * tq, LANES)]

    def stage_c(qi):
        p_buf = p_scr.at[qi % 2]

        nk = (qi + 1) * tq
        st[qi]["a"] = None

        def part(k0):
            k1 = min(k0 + ATTN_PV_KEYS, nk)
            a = _dot(p_buf[:, k0:k1], v_ext[k0:k1, :])
            st[qi]["a"] = a if st[qi]["a"] is None else st[qi]["a"] + a

        def finish():
            a = st[qi]["a"][:, 0:DA_V_DIM] / st[qi]["a"][:, DA_V_DIM:DA_V_DIM + 1]
            o = a[0:tq] - lam * a[tq:2 * tq]
            o = o * lax.rsqrt(jnp.mean(o * o, axis=-1, keepdims=True) + LN_EPS) * g_ref[...]
            o_ref[0, qi * tq:(qi + 1) * tq, :] = (o * (1.0 - lam_init)).astype(o_ref.dtype)

        return [functools.partial(part, k0) for k0 in range(0, nk, ATTN_PV_KEYS)] + [finish]

    for t in range(n_blk + 2):
        stages = []
        if t < n_blk:
            stages.append(stage_a(t))
        if 0 <= t - 1 < n_blk:
            stages.append(stage_b(t - 1))
        if 0 <= t - 2 < n_blk:
            stages.append(stage_c(t - 2))
        merged = sorted(((i + 0.5) / len(ops), k, i, op) for k, ops in enumerate(stages) for i, op in enumerate(ops))
        for _, _, _, op in merged:
            op()


def _diff_attn(q, k, v, lam_q1, lam_k1, lam_q2, lam_k2, subln_g, lam_init):
    B, S, _ = q.shape
    vec = pl.BlockSpec((1, DA_HEAD_DIM), lambda b, h: (0, 0))
    seq = pl.BlockSpec((1, S, LANES), lambda b, h: (b, 0, h))
    return pl.pallas_call(
        functools.partial(_diff_attn_kernel, lam_init),
        grid=(B, DA_HEADS),
        in_specs=[vec, vec, vec, vec, pl.BlockSpec((1, DA_V_DIM), lambda b, h: (0, 0)), seq, seq, seq],
        out_specs=seq,
        out_shape=jax.ShapeDtypeStruct((B, S, DA_V), BF16),
        scratch_shapes=[pltpu.VMEM((2, 2 * ATTN_BLOCK, S), F32), pltpu.VMEM((2, 2 * ATTN_BLOCK, S), BF16),
                        pltpu.VMEM((S, 2 * DA_V_DIM), BF16)],
        compiler_params=pltpu.CompilerParams(dimension_semantics=("arbitrary",) * 2,
                                             vmem_limit_bytes=VMEM_LIMIT),
        name="diff_attn",
    )(lam_q1, lam_k1, lam_q2, lam_k2, subln_g, q, k, v)


def _gla_kernel(q_ref, k_ref, la_ref, v_ref, go_ref, ng_ref, o_ref, qt_s, oi_s, ds_s, dec_s):
    S = q_ref.shape[1]
    C = CHUNK
    BLK = GLA_BLOCK
    per_blk = BLK // C
    r = lax.broadcasted_iota(jnp.int32, (BLK, BLK), 0)
    c = lax.broadcasted_iota(jnp.int32, (BLK, BLK), 1)
    chunk_causal = (r // C == c // C) & (c <= r)
    tri = jnp.where(chunk_causal, 1.0, 0.0).astype(BF16)
    lane = lax.broadcasted_iota(jnp.int32, (BLK, LANES), 1)
    head_lanes = (lane < GLA_KEY_DIM, lane >= GLA_KEY_DIM)
    st_row = lax.broadcasted_iota(jnp.int32, (2 * GLA_V_DIM, LANES), 0)
    st_lane = lax.broadcasted_iota(jnp.int32, (2 * GLA_V_DIM, LANES), 1)
    own_keys = (st_row < GLA_V_DIM) == (st_lane < GLA_KEY_DIM)

    for b in range(S // BLK):
        r0 = b * BLK
        g = la_ref[0, r0:r0 + BLK, :]
        g1 = g.astype(BF16)
        e1 = g - g1.astype(F32)
        g2 = e1.astype(BF16)
        g3 = (e1 - g2.astype(F32)).astype(BF16)
        bcum = _dot(tri, g1) + _dot(tri, g2) + _dot(tri, g3)
        b_last = jnp.concatenate(
            [jnp.broadcast_to(bcum[i * C + C - 1:i * C + C, :], (C, LANES)) for i in range(per_blk)], axis=0)
        qf = q_ref[0, r0:r0 + BLK, :].astype(F32) * (GLA_KEY_DIM ** -0.5)
        kf = k_ref[0, r0:r0 + BLK, :].astype(F32)
        q_t = (qf * jnp.exp(bcum)).astype(BF16)
        k_t = (kf * jnp.exp(-bcum)).astype(BF16)
        k_end = (kf * jnp.exp(b_last - bcum)).astype(BF16)
        decay = jnp.exp(b_last)
        qt_s[r0:r0 + BLK, :] = q_t
        zero = jnp.zeros_like(q_t)
        for hh in range(2):
            att = jnp.where(chunk_causal, _dot_nt(jnp.where(head_lanes[hh], q_t, zero), k_t), 0.0).astype(BF16)
            oi_s[r0:r0 + BLK, hh * GLA_V_DIM:(hh + 1) * GLA_V_DIM] = _dot(
                att, v_ref[0, r0:r0 + BLK, hh * GLA_V_DIM:(hh + 1) * GLA_V_DIM])
        for i in range(per_blk):
            n = b * per_blk + i
            rows = slice(r0 + i * C, r0 + (i + 1) * C)
            inc = _dot_tn(v_ref[0, rows, :], k_end[i * C:(i + 1) * C, :])
            ds_s[n] = jnp.where(own_keys, inc, 0.0)
            dec_s[n:n + 1, :] = decay[i * C:i * C + 1, :]

    state = jnp.zeros((2 * GLA_V_DIM, LANES), F32)
    for n in range(S // C):
        rows = slice(n * C, (n + 1) * C)
        o = oi_s[rows, :] + _dot_nt(qt_s[rows, :], state.astype(BF16))
        state = state * dec_s[n:n + 1, :] + ds_s[n]
        for hh in range(2):
            cols = slice(hh * GLA_V_DIM, (hh + 1) * GLA_V_DIM)
            oh = o[:, cols]
            oh = oh * lax.rsqrt(jnp.mean(oh * oh, axis=-1, keepdims=True) + LN_EPS) * ng_ref[...]
            gate = go_ref[0, rows, cols].astype(F32)
            o_ref[0, rows, cols] = (oh * (gate * jax.nn.sigmoid(gate))).astype(o_ref.dtype)


def _gla(gq, gk, la, gv, go, norm_g):
    B, S, _ = gq.shape
    pairs = GLA_HEADS // 2
    narrow = pl.BlockSpec((1, S, LANES), lambda b, p: (b, 0, p))
    wide = pl.BlockSpec((1, S, 2 * GLA_V_DIM), lambda b, p: (b, 0, p))
    n_chunks = S // CHUNK
    return pl.pallas_call(
        _gla_kernel,
        grid=(B, pairs),
        in_specs=[narrow, narrow, narrow, wide, wide,
                  pl.BlockSpec((1, GLA_V_DIM), lambda b, p: (0, 0))],
        out_specs=wide,
        out_shape=jax.ShapeDtypeStruct((B, S, GLA_V), BF16),
        scratch_shapes=[pltpu.VMEM((S, LANES), BF16),
                        pltpu.VMEM((S, 2 * GLA_V_DIM), F32),
                        pltpu.VMEM((n_chunks, 2 * GLA_V_DIM, LANES), F32),
                        pltpu.VMEM((n_chunks, LANES), F32)],
        compiler_params=pltpu.CompilerParams(dimension_semantics=("arbitrary",) * 2,
                                             vmem_limit_bytes=VMEM_LIMIT),
        name="gla",
    )(gq, gk, la, gv, go, norm_g)


def _split3(a):
    hi = a.astype(BF16)
    lo = (a - hi.astype(F32)).astype(BF16)
    return hi, lo


def _mix_out_kernel(alpha, x_ref, da_ref, gl_ref, lng_ref, lnb_ref, wo_ref, g1_ref, b1_ref,
                    wr_hi_ref, wr_lo_ref, br_ref, lower_ref, h_ref, route_ref, cnt_ref):
    tm = x_ref.shape[0]
    i = pl.program_id(0)

    @pl.when(i == 0)
    def _():
        cnt_ref[...] = jnp.zeros_like(cnt_ref)

    logit_blocks = []
    for r0 in range(0, tm, ROW_BLOCK):
        rows = slice(r0, r0 + ROW_BLOCK)
        xn = _layer_norm(x_ref[rows, :], lng_ref[...], lnb_ref[...])
        mix = _dot(da_ref[rows, :], wo_ref[0:DA_V, :]) + _dot(gl_ref[rows, :], wo_ref[DA_V:, :])
        h = _layer_norm(alpha * xn + mix, g1_ref[...], b1_ref[...])
        h_ref[rows, :] = h
        h_hi, h_lo = _split3(h)
        logit_blocks.append(_dot(h_hi, wr_hi_ref[...]) + _dot(h_hi, wr_lo_ref[...]) + _dot(h_lo, wr_hi_ref[...])
                            + br_ref[...])
    logits = jnp.concatenate(logit_blocks, axis=0)
    lane = lax.broadcasted_iota(jnp.int32, (tm, LANES), 1)
    neg = -jnp.inf
    big = jnp.int32(LANES)

    def first_argmax(vals, valid):
        v = jnp.where(valid, vals, neg)
        mx = jnp.max(v, axis=-1, keepdims=True)
        idx = jnp.min(jnp.where(valid & (v == mx), lane, big), axis=-1, keepdims=True)
        return mx, idx

    is_group = lane < N_GROUPS
    g_max, g_top = first_argmax(logits, is_group)
    p_g = 1.0 / jnp.sum(jnp.where(is_group, jnp.exp(logits - g_max), 0.0), axis=-1, keepdims=True)

    e_lo = EXPERT_LANE0 + g_top * EXPERTS_PER_GROUP
    in_group = (lane >= e_lo) & (lane < e_lo + EXPERTS_PER_GROUP)
    v0, i0 = first_argmax(logits, in_group)
    v1, i1 = first_argmax(logits, in_group & (lane != i0))
    w1 = jnp.exp(v1 - v0)
    gate0 = p_g / (1.0 + w1)
    gate1 = p_g * w1 / (1.0 + w1)
    e0 = i0 - EXPERT_LANE0
    e1 = i1 - EXPERT_LANE0

    oh0 = jnp.where(lane == e0, 1.0, 0.0)
    oh1 = jnp.where(lane == e1, 1.0, 0.0)
    oh = oh0 + oh1
    before = _dot(lower_ref[...], oh.astype(BF16)) + cnt_ref[0:1, :]
    rank0 = jnp.sum(oh0 * before, axis=-1, keepdims=True)
    rank1 = jnp.sum(oh1 * before, axis=-1, keepdims=True)
    cnt_ref[...] = cnt_ref[...] + jnp.sum(oh, axis=0, keepdims=True)

    rec = jnp.zeros((tm, LANES), F32)
    for ln, val in ((R_E0, e0.astype(F32)), (R_E1, e1.astype(F32)), (R_G0, gate0), (R_G1, gate1),
                    (R_RANK0, rank0), (R_RANK1, rank1)):
        rec = jnp.where(lane == ln, val, rec)
    route_ref[...] = rec


def _mix_out(x2, da2, gl2, ln_g, ln_b, w_o, ln1_g, ln1_b, wr_hi, wr_lo, b_r, alpha, tm):
    T = x2.shape[0]
    row = lambda n: pl.BlockSpec((tm, n), lambda i: (i, 0))
    full = lambda a: pl.BlockSpec(a.shape, lambda i: (0,) * a.ndim)
    lower = jnp.tril(jnp.ones((tm, tm), BF16), -1)
    return pl.pallas_call(
        functools.partial(_mix_out_kernel, alpha),
        grid=(T // tm,),
        in_specs=[row(D_MODEL), row(DA_V), row(GLA_V), full(ln_g), full(ln_b), full(w_o),
                  full(ln1_g), full(ln1_b), full(wr_hi), full(wr_lo), full(b_r), full(lower)],
        out_specs=[row(D_MODEL), row(LANES), pl.BlockSpec((8, LANES), lambda i: (0, 0))],
        out_shape=[jax.ShapeDtypeStruct((T, D_MODEL), F32), jax.ShapeDtypeStruct((T, LANES), F32),
                   jax.ShapeDtypeStruct((8, LANES), F32)],
        compiler_params=pltpu.CompilerParams(dimension_semantics=("arbitrary",),
                                             vmem_limit_bytes=VMEM_LIMIT),
        name="mix_out",
    )(x2, da2, gl2, ln_g, ln_b, w_o, ln1_g, ln1_b, wr_hi, wr_lo, b_r, lower)


HIGH_HALF = 0xFFFF0000


def _pack_pairs(val):
    bits = lambda a: lax.bitcast_convert_type(a.astype(BF16).astype(F32), jnp.uint32)
    half = val.shape[1] // 2
    return (bits(val[:, :half]) >> 16) | (bits(val[:, half:]) & jnp.uint32(HIGH_HALF))


def _unpack_pairs(words):
    lo = lax.bitcast_convert_type(words << 16, F32)
    hi = lax.bitcast_convert_type(words & jnp.uint32(HIGH_HALF), F32)
    return jnp.concatenate([lo, hi], axis=1)


def _words_to_tiles(dst_ref, words):
    n = words.shape[0]
    for s in range(ROW_SUB):
        dst_ref[pl.ds(s, n, stride=ROW_SUB), :] = words[:, s * LANES:(s + 1) * LANES]


def _tiles_to_words(src_ref, r0, n):
    return jnp.concatenate([src_ref[pl.ds(r0 * ROW_SUB + s, n, stride=ROW_SUB), :] for s in range(ROW_SUB)],
                           axis=1)


def _rows_to_tiles(dst_ref, val):
    _words_to_tiles(dst_ref, _pack_pairs(val))


def _tiles_to_rows(src_ref, r0, n):
    return _unpack_pairs(_tiles_to_words(src_ref, r0, n))


def _row_tile(ref, r):
    return ref.at[pl.ds(pl.multiple_of(r * ROW_SUB, ROW_SUB), ROW_SUB), :]


def _dispatch_kernel(dest_ref, h_ref, xs_ref, stage, sems):
    tm = h_ref.shape[0]
    step = pl.program_id(0)
    half = step & 1
    base = step * (tm * TOP_K)
    n_iter = tm * TOP_K // DMA_UNROLL
    _rows_to_tiles(stage.at[half], h_ref[...])

    def row_copy(hf, t, slot):
        return pltpu.make_async_copy(_row_tile(stage.at[hf], t), _row_tile(xs_ref, slot), sems.at[hf])

    def start(i, c):
        for u in range(DMA_UNROLL):
            t = i * (DMA_UNROLL // TOP_K) + u // TOP_K
            row_copy(half, t, dest_ref[base + i * DMA_UNROLL + u]).start(priority=u % 2)
        return c

    lax.fori_loop(0, n_iter, start, 0)

    def drain(hf):
        def wait(i, c):
            for u in range(DMA_UNROLL):
                row_copy(hf, 0, 0).wait()
            return c
        lax.fori_loop(0, n_iter, wait, 0)

    @pl.when(step > 0)
    def _():
        drain(1 - half)

    @pl.when(step == pl.num_programs(0) - 1)
    def _():
        drain(half)
        n_pad = EXPERT_CHUNK * ROW_SUB
        stage[0, 0:n_pad, :] = jnp.zeros((n_pad, LANES), stage.dtype)
        pad = pltpu.make_async_copy(stage.at[0, 0:n_pad, :],
                                    xs_ref.at[pl.ds(pl.num_programs(0) * tm * TOP_K * ROW_SUB, n_pad), :], sems.at[0])
        pad.start()
        pad.wait()


def _dispatch(dest_flat, h, tm):
    T = h.shape[0]
    return pl.pallas_call(
        _dispatch_kernel,
        grid_spec=pltpu.PrefetchScalarGridSpec(
            num_scalar_prefetch=1,
            grid=(T // tm,),
            in_specs=[pl.BlockSpec((tm, D_MODEL), lambda i, d: (i, 0))],
            out_specs=pl.BlockSpec(memory_space=pl.ANY),
            scratch_shapes=[pltpu.VMEM((2, tm * ROW_SUB, LANES), jnp.uint32), pltpu.SemaphoreType.DMA((2,))]),
        out_shape=jax.ShapeDtypeStruct(((T * TOP_K + EXPERT_CHUNK) * ROW_SUB, LANES), jnp.uint32),
        compiler_params=pltpu.CompilerParams(dimension_semantics=("arbitrary",),
                                             vmem_limit_bytes=VMEM_LIMIT),
        name="dispatch",
    )(dest_flat, h)


def _experts_kernel(row0_ref, first_ref, xs_ref, wg_ref, wu_ref, wd_ref, ys_ref,
                    wgu_b, wd_b, xbuf, ybuf, xsem, ysem, pend_ref, *, n_rows):
    e = pl.program_id(0)
    ch = EXPERT_CHUNK
    depth = EXPERT_XBUFS - 1
    g_lo = first_ref[e]
    g_hi = first_ref[e + 1]
    total = first_ref[N_EXPERTS]

    def slab(ref, row0):
        return ref.at[pl.ds(pl.multiple_of(row0 * ROW_SUB, ROW_SUB), ch * ROW_SUB), :]

    def x_copy(g):
        slot = g & (EXPERT_XBUFS - 1)
        return pltpu.make_async_copy(slab(xs_ref, row0_ref[g]), xbuf.at[slot], xsem.at[slot])

    def y_copy(row0, half):
        return pltpu.make_async_copy(ybuf.at[half], slab(ys_ref, row0), ysem.at[half])

    def drain_y(half):
        @pl.when(pend_ref[half] == 1)
        def _():
            y_copy(0, half).wait()
            pend_ref[half] = 0

    @pl.when(e == 0)
    def _():
        for b in range(EXPERT_YBUFS):
            pend_ref[b] = 0
        ybuf[0] = jnp.zeros(ybuf.shape[1:], ybuf.dtype)
        y_copy(n_rows, 0).start()
        y_copy(n_rows, 0).wait()
        for d in range(depth):
            @pl.when(d < total)
            def _():
                x_copy(d).start(priority=ROW_STREAM_PRIORITY)

    @pl.when(g_hi > g_lo)
    def _():
        wgu_b[:, 0:D_EXPERT] = wg_ref[0].astype(BF16)
        wgu_b[:, D_EXPERT:] = wu_ref[0].astype(BF16)
        wd_b[...] = wd_ref[0].astype(BF16)

        def chunk(g, c):
            half = g & (EXPERT_YBUFS - 1)
            x_copy(g).wait()

            @pl.when(g + depth < total)
            def _():
                x_copy(g + depth).start(priority=ROW_STREAM_PRIORITY)

            xb = _tiles_to_rows(xbuf.at[g & (EXPERT_XBUFS - 1)], 0, ch).astype(BF16)
            gu = _dot(xb, wgu_b[...])
            gate = gu[:, 0:D_EXPERT]
            mid = (gate * jax.nn.sigmoid(gate) * gu[:, D_EXPERT:]).astype(BF16)
            words = _pack_pairs(_dot(mid, wd_b[...]))

            drain_y(half)

            @pl.when(g == g_lo)
            def _():
                drain_y((g - 1) & (EXPERT_YBUFS - 1))

            _words_to_tiles(ybuf.at[half], words)
            y_copy(row0_ref[g], half).start(priority=ROW_STREAM_PRIORITY)
            pend_ref[half] = 1
            return c

        lax.fori_loop(g_lo, g_hi, chunk, 0)

    @pl.when(e == pl.num_programs(0) - 1)
    def _():
        for b in range(EXPERT_YBUFS):
            drain_y(b)


def _chunk_metadata(seg_start, counts, n_rows):
    ch = EXPERT_CHUNK
    max_chunks = n_rows // ch + N_EXPERTS
    n_ch = (counts + (ch - 1)) // ch
    first = jnp.concatenate([jnp.zeros((1,), jnp.int32), jnp.cumsum(n_ch).astype(jnp.int32)])
    g = jnp.arange(max_chunks, dtype=jnp.int32)
    owner = jnp.minimum(jnp.sum((first[None, 1:] <= g[:, None]).astype(jnp.int32), axis=1), N_EXPERTS - 1)
    onehot = owner[:, None] == jnp.arange(N_EXPERTS, dtype=jnp.int32)
    pick = lambda tab: jnp.sum(jnp.where(onehot, tab[None, :], 0), axis=1)
    row0 = pick(seg_start) + (g - pick(first[:-1])) * ch
    row0 = jnp.where(g < first[-1], row0, 0)
    return row0.astype(jnp.int32), first


def _experts(seg_start, counts, xs, w_gate, w_up, w_down):
    n_rows = xs.shape[0] // ROW_SUB - EXPERT_CHUNK
    row0, first = _chunk_metadata(seg_start, counts, n_rows)
    per_expert = lambda shape: pl.BlockSpec((1,) + shape, lambda e, r, f: (e, 0, 0))
    slab = (EXPERT_CHUNK * ROW_SUB, LANES)
    return pl.pallas_call(
        functools.partial(_experts_kernel, n_rows=n_rows),
        grid_spec=pltpu.PrefetchScalarGridSpec(
            num_scalar_prefetch=2,
            grid=(N_EXPERTS,),
            in_specs=[pl.BlockSpec(memory_space=pl.ANY),
                      per_expert((D_MODEL, D_EXPERT)), per_expert((D_MODEL, D_EXPERT)),
                      per_expert((D_EXPERT, D_MODEL))],
            out_specs=pl.BlockSpec(memory_space=pl.ANY),
            scratch_shapes=[pltpu.VMEM((D_MODEL, 2 * D_EXPERT), BF16), pltpu.VMEM((D_EXPERT, D_MODEL), BF16),
                            pltpu.VMEM((EXPERT_XBUFS,) + slab, jnp.uint32), pltpu.VMEM((EXPERT_YBUFS,) + slab, jnp.uint32),
                            pltpu.SemaphoreType.DMA((EXPERT_XBUFS,)), pltpu.SemaphoreType.DMA((EXPERT_YBUFS,)),
                            pltpu.SMEM((EXPERT_YBUFS,), jnp.int32)]),
        out_shape=jax.ShapeDtypeStruct(xs.shape, jnp.uint32),
        compiler_params=pltpu.CompilerParams(dimension_semantics=("arbitrary",),
                                             vmem_limit_bytes=VMEM_LIMIT),
        name="experts",
    )(row0, first, xs, w_gate, w_up, w_down)


def _combine_kernel(alpha, dest_ref, h_ref, route_ref, g_ref, b_ref, y_ref, o_ref, buf, sems):
    tm = h_ref.shape[0]
    step = pl.program_id(0)
    half = step & 1
    n_iter = tm * TOP_K // DMA_UNROLL

    def row_copy(hf, src, slot):
        return pltpu.make_async_copy(_row_tile(y_ref, src), _row_tile(buf.at[hf], slot), sems.at[hf])

    def gather(st, hf):
        base = st * (tm * TOP_K)

        def start(i, c):
            for u in range(DMA_UNROLL):
                slot = (u % TOP_K) * tm + i * (DMA_UNROLL // TOP_K) + u // TOP_K
                row_copy(hf, dest_ref[base + i * DMA_UNROLL + u], slot).start(priority=u % 2)
            return c

        lax.fori_loop(0, n_iter, start, 0)

    @pl.when(step == 0)
    def _():
        gather(0, 0)

    @pl.when(step + 1 < pl.num_programs(0))
    def _():
        gather(step + 1, 1 - half)

    def wait(i, c):
        for u in range(DMA_UNROLL):
            row_copy(half, 0, 0).wait()
        return c

    lax.fori_loop(0, n_iter, wait, 0)

    rec = route_ref[...]
    cur = buf.at[half]
    ffn = (rec[:, R_G0:R_G0 + 1] * _tiles_to_rows(cur, 0, tm)
           + rec[:, R_G1:R_G1 + 1] * _tiles_to_rows(cur, tm, tm))
    o_ref[...] = _layer_norm(alpha * h_ref[...] + ffn, g_ref[...], b_ref[...])


def _combine(dest_flat, h, route, ln2_g, ln2_b, y_sorted, alpha, tm):
    T = h.shape[0]
    return pl.pallas_call(
        functools.partial(_combine_kernel, alpha),
        grid_spec=pltpu.PrefetchScalarGridSpec(
            num_scalar_prefetch=1,
            grid=(T // tm,),
            in_specs=[pl.BlockSpec((tm, D_MODEL), lambda i, d: (i, 0)),
                      pl.BlockSpec((tm, LANES), lambda i, d: (i, 0)),
                      pl.BlockSpec((1, D_MODEL), lambda i, d: (0, 0)),
                      pl.BlockSpec((1, D_MODEL), lambda i, d: (0, 0)),
                      pl.BlockSpec(memory_space=pl.ANY)],
            out_specs=pl.BlockSpec((tm, D_MODEL), lambda i, d: (i, 0)),
            scratch_shapes=[pltpu.VMEM((2, TOP_K * tm * ROW_SUB, LANES), jnp.uint32), pltpu.SemaphoreType.DMA((2,))]),
        out_shape=jax.ShapeDtypeStruct((T, D_MODEL), F32),
        compiler_params=pltpu.CompilerParams(dimension_semantics=("arbitrary",),
                                             vmem_limit_bytes=VMEM_LIMIT),
        name="combine",
    )(dest_flat, h, route, ln2_g, ln2_b, y_sorted)


def kernel(x, positions, ln_in_g, ln_in_b, w_in, lam_q1, lam_k1, lam_q2, lam_k2, da_subln_g, gla_w_gate2, gla_b_gate2, gla_norm_g, w_o, ln1_g, ln1_b, router_w_group, router_b_group, router_w_expert, router_b_expert, w_gate, w_up, w_down, ln2_g, ln2_b):
    B, S, D = x.shape
    T = B * S
    depth = w_in.shape[0]
    assert depth == 1, "only a single layer is supported"
    alpha = (2 * depth) ** 0.25
    row2 = lambda a: a.reshape(1, -1)

    inv_freq = ROPE_THETA ** (-jnp.arange(0, DA_HEAD_DIM, 2, dtype=F32) / DA_HEAD_DIM)
    inv_freq = jnp.tile(jnp.repeat(inv_freq, 2), LANES // DA_HEAD_DIM).reshape(1, LANES)
    pos2 = positions.reshape(T, 1)

    cur = x.reshape(T, D)
    cur_g, cur_b = row2(ln_in_g), row2(ln_in_b)
    for l in range(depth):
        w = w_in[l]
        w_main = w[:, :D_MAIN].astype(BF16)
        w_glow = jnp.pad(w[:, D_MAIN:], ((0, 0), (0, LANES - GLA_GATE_RANK))).astype(BF16)
        w_gate2 = jnp.pad(gla_w_gate2[l], ((0, LANES - GLA_GATE_RANK), (0, 0))).astype(BF16)

        q, k, v, gq, gk, gv, go, la = _in_proj(cur, pos2, cur_g, cur_b, inv_freq, w_main, w_glow,
                                               w_gate2, row2(gla_b_gate2[l]), tm=512)
        lam_init = 0.8 - 0.6 * math.exp(-0.3 * l)
        sh = lambda a: a.reshape(B, S, a.shape[-1])
        da = _diff_attn(sh(q), sh(k), sh(v), row2(lam_q1[l]), row2(lam_k1[l]), row2(lam_q2[l]),
                        row2(lam_k2[l]), row2(da_subln_g[l]), lam_init)
        gl = _gla(sh(gq), sh(gk), sh(la), sh(gv), sh(go), row2(gla_norm_g[l]))

        w_r = jnp.zeros((D, LANES), F32)
        w_r = w_r.at[:, :N_GROUPS].set(router_w_group[l])
        w_r = w_r.at[:, EXPERT_LANE0:EXPERT_LANE0 + N_EXPERTS].set(router_w_expert[l])
        b_r = jnp.zeros((1, LANES), F32)
        b_r = b_r.at[0, :N_GROUPS].set(router_b_group[l])
        b_r = b_r.at[0, EXPERT_LANE0:EXPERT_LANE0 + N_EXPERTS].set(router_b_expert[l])
        wr_hi = w_r.astype(BF16)
        wr_lo = (w_r - wr_hi.astype(F32)).astype(BF16)

        h, route, cnt = _mix_out(cur, da.reshape(T, DA_V), gl.reshape(T, GLA_V), cur_g, cur_b,
                                 w_o[l].astype(BF16), row2(ln1_g[l]), row2(ln1_b[l]), wr_hi, wr_lo, b_r,
                                 alpha, tm=512)

        counts = cnt[0, :N_EXPERTS].astype(jnp.int32)
        seg_start = jnp.cumsum(counts) - counts
        eid = route[:, R_E0:R_E1 + 1].astype(jnp.int32)
        rank = route[:, R_RANK0:R_RANK1 + 1].astype(jnp.int32)
        onehot = eid[..., None] == jnp.arange(N_EXPERTS, dtype=jnp.int32)
        dest = jnp.sum(jnp.where(onehot, seg_start, 0), axis=-1) + rank
        dest_flat = dest.reshape(T * TOP_K)

        xs = _dispatch(dest_flat, h, tm=512)
        ys = _experts(seg_start, counts, xs, w_gate[l], w_up[l], w_down[l])
        cur = _combine(dest_flat, h, route, row2(ln2_g[l]), row2(ln2_b[l]), ys, alpha, tm=512)
    return cur.reshape(B, S, D)
```

```python
import functools
import math

import jax
import jax.numpy as jnp
from jax import lax
from jax.experimental import pallas as pl
from jax.experimental.pallas import tpu as pltpu

F32 = jnp.float32
BF16 = jnp.bfloat16

D_MODEL = 1024
CHUNK = 64
ROPE_THETA = 10000.0
LN_EPS = 1e-5
LOG2_E = math.log2(math.e)

DA_HEADS = 4
DA_V_DIM = D_MODEL // (2 * DA_HEADS)
DA_HEAD_DIM = DA_V_DIM // 2
GLA_HEADS = 4
GLA_V_DIM = D_MODEL // (2 * GLA_HEADS)
GLA_KEY_DIM = GLA_V_DIM // 2
GLA_GATE_RANK = 16
GLA_GATE_NORMALIZER = 16.0

DA_Q = DA_HEADS * 2 * DA_HEAD_DIM
DA_K = DA_Q
DA_V = DA_HEADS * DA_V_DIM
GLA_Q = GLA_HEADS * GLA_KEY_DIM
GLA_K = GLA_Q
GLA_V = GLA_HEADS * GLA_V_DIM
GLA_OG = GLA_V
D_MAIN = DA_Q + DA_K + DA_V + GLA_Q + GLA_K + GLA_V + GLA_OG

N_GROUPS = 4
EXPERTS_PER_GROUP = 8
N_EXPERTS = N_GROUPS * EXPERTS_PER_GROUP
TOP_K = 2
D_EXPERT = D_MODEL // 2

LANES = 128
ROW_SUB = D_MODEL // (2 * LANES)
ROW_BLOCK = 256
EXPERT_CHUNK = 512
EXPERT_GRAIN = 128
EXPERT_XBUFS = 8
EXPERT_YBUFS = 4
ROW_STREAM_PRIORITY = 1
GLA_BLOCK = 256
ATTN_BLOCK = 256
ATTN_PV_KEYS = 512
DMA_UNROLL = 8
VMEM_LIMIT = 48 * 1024 * 1024

R_E0, R_E1, R_G0, R_G1, R_RANK0, R_RANK1 = 0, 1, 2, 3, 4, 5
EXPERT_LANE0 = 32


def _layer_norm(x, g, b):
    mu = jnp.mean(x, axis=-1, keepdims=True)
    xc = x - mu
    var = jnp.mean(xc * xc, axis=-1, keepdims=True)
    return xc * lax.rsqrt(var + LN_EPS) * g + b


def _dot(a, b):
    return jnp.dot(a, b, preferred_element_type=F32)


def _dot_nt(a, b):
    return lax.dot_general(a, b, (((1,), (1,)), ((), ())), preferred_element_type=F32)


def _dot_tn(a, b):
    return lax.dot_general(a, b, (((0,), (0,)), ((), ())), preferred_element_type=F32)


def _in_proj_kernel(x_ref, pos_ref, g_ref, b_ref, invf_ref, w_ref, wgl_ref, wg2_ref, bg2_ref,
                    q_ref, k_ref, v_ref, gq_ref, gk_ref, gv_ref, go_ref, la_ref):
    tm = x_ref.shape[0]
    lane = lax.broadcasted_iota(jnp.int32, (ROW_BLOCK, LANES), 1)
    first = (lane & 1) == 0

    for r0 in range(0, tm, ROW_BLOCK):
        rows = slice(r0, r0 + ROW_BLOCK)
        xn = _layer_norm(x_ref[rows, :], g_ref[...], b_ref[...])
        xb = xn.astype(BF16)
        proj = _dot(xb, w_ref[...])

        ang = pos_ref[rows, :].astype(F32) * invf_ref[...]
        c = jnp.cos(ang)
        s = jnp.sin(ang)
        s_lo = jnp.where(first, -s, 0.0)
        s_hi = jnp.where(first, 0.0, s)

        def rope(t):
            out = []
            for j in range(t.shape[1] // LANES):
                tj = t[:, j * LANES:(j + 1) * LANES]
                up = pltpu.roll(tj, LANES - 1, 1)
                dn = pltpu.roll(tj, 1, 1)
                out.append(tj * c + up * s_lo + dn * s_hi)
            return jnp.concatenate(out, axis=1)

        o = 0
        q = rope(proj[:, o:o + DA_Q]) * (DA_HEAD_DIM ** -0.5 * LOG2_E)
        o += DA_Q
        k = rope(proj[:, o:o + DA_K])
        o += DA_K
        q_ref[rows, :] = q.astype(BF16)
        k_ref[rows, :] = k.astype(BF16)
        v_ref[rows, :] = proj[:, o:o + DA_V].astype(BF16)
        o += DA_V
        gq_ref[rows, :] = proj[:, o:o + GLA_Q].astype(BF16)
        o += GLA_Q
        gk_ref[rows, :] = proj[:, o:o + GLA_K].astype(BF16)
        o += GLA_K
        gv_ref[rows, :] = proj[:, o:o + GLA_V].astype(BF16)
        o += GLA_V
        go_ref[rows, :] = proj[:, o:o + GLA_OG].astype(BF16)

        g_low = _dot(xb, wgl_ref[...])
        z = _dot(g_low.astype(BF16), wg2_ref[...]) + bg2_ref[...]
        log_sig = jnp.minimum(z, 0.0) - jnp.log1p(jnp.exp(-jnp.abs(z)))
        la_ref[rows, :] = log_sig / GLA_GATE_NORMALIZER


def _in_proj(x2, pos2, ln_g, ln_b, inv_freq, w_main, w_glow, w_gate2, b_gate2, tm):
    T = x2.shape[0]
    row = lambda n: pl.BlockSpec((tm, n), lambda i: (i, 0))
    full = lambda a: pl.BlockSpec(a.shape, lambda i: (0,) * a.ndim)
    out_shape = [jax.ShapeDtypeStruct((T, n), dt) for n, dt in (
        (DA_Q, BF16), (DA_K, BF16), (DA_V, BF16), (GLA_Q, BF16), (GLA_K, BF16),
        (GLA_V, BF16), (GLA_OG, BF16), (GLA_K, F32))]
    return pl.pallas_call(
        _in_proj_kernel,
        grid=(T // tm,),
        in_specs=[row(D_MODEL), row(1), full(ln_g), full(ln_b), full(inv_freq), full(w_main),
                  full(w_glow), full(w_gate2), full(b_gate2)],
        out_specs=[row(s.shape[1]) for s in out_shape],
        out_shape=out_shape,
        compiler_params=pltpu.CompilerParams(dimension_semantics=("arbitrary",),
                                             vmem_limit_bytes=VMEM_LIMIT),
        name="in_proj",
    )(x2, pos2, ln_g, ln_b, inv_freq, w_main, w_glow, w_gate2, b_gate2)


def _diff_attn_kernel(lam_init, lq1_ref, lk1_ref, lq2_ref, lk2_ref, g_ref, q_ref, k_ref, v_ref, o_ref,
                      s_scr, p_scr, v_ext):
    S = q_ref.shape[1]
    tq = ATTN_BLOCK
    lam = (jnp.exp(jnp.sum(lq1_ref[...] * lk1_ref[...], axis=-1, keepdims=True))
           - jnp.exp(jnp.sum(lq2_ref[...] * lk2_ref[...], axis=-1, keepdims=True)) + lam_init)
    lane = lax.broadcasted_iota(jnp.int32, (tq, LANES), 1)
    rq = lax.broadcasted_iota(jnp.int32, (2 * tq, tq), 0) % tq // CHUNK
    ck = lax.broadcasted_iota(jnp.int32, (2 * tq, tq), 1) // CHUNK
    diag_mask = ck <= rq

    n_blk = S // tq
    st = [dict() for _ in range(n_blk)]
    v_ext[:, 0:DA_V_DIM] = v_ref[0]
    ext_lane = lax.broadcasted_iota(jnp.int32, (S, DA_V_DIM), 1)
    v_ext[:, DA_V_DIM:] = jnp.where(ext_lane == 0, 1.0, 0.0).astype(BF16)

    def stage_a(qi):
        s_buf = s_scr.at[qi % 2]

        def begin():
            q = q_ref[0, qi * tq:(qi + 1) * tq, :]
            zero = jnp.zeros_like(q)
            st[qi]["qq"] = jnp.concatenate([jnp.where(lane < DA_HEAD_DIM, q, zero),
                                            jnp.where(lane >= DA_HEAD_DIM, q, zero)], axis=0)
            st[qi]["m_acc"] = None

        def tile(j):
            s = _dot_nt(st[qi]["qq"], k_ref[0, j * tq:(j + 1) * tq, :])
            if j == qi:
                s = jnp.where(diag_mask, s, -jnp.inf)
            s_buf[:, j * tq:(j + 1) * tq] = s
            m_acc = st[qi]["m_acc"]
            for c0 in range(0, tq, LANES):
                sc = s[:, c0:c0 + LANES]
                m_acc = sc if m_acc is None else jnp.maximum(m_acc, sc)
            st[qi]["m_acc"] = m_acc

        def end():
            st[qi]["m"] = jnp.broadcast_to(jnp.max(st[qi]["m_acc"], axis=-1, keepdims=True), (2 * tq, LANES))

        return [begin] + [functools.partial(tile, j) for j in range(qi + 1)] + [end]

    def stage_b(qi):
        s_buf = s_scr.at[qi % 2]
        p_buf = p_scr.at[qi % 2]

        def cols(c0):
            p_buf[:, c0:c0 + LANES] = jnp.exp2((s_buf[:, c0:c0 + LANES] - st[qi]["m"]).astype(BF16))

        return [functools.partial(cols, c0) for c0 in range(0, (qi + 1) * tq, LANES)]

    def stage_c(qi):
        p_buf = p_scr.at[qi % 2]

        nk = (qi + 1) * tq
        st[qi]["a"] = None

        def part(k0):
            k1 = min(k0 + ATTN_PV_KEYS, nk)
            a = _dot(p_buf[:, k0:k1], v_ext[k0:k1, :])
            st[qi]["a"] = a if st[qi]["a"] is None else st[qi]["a"] + a

        def finish():
            a = st[qi]["a"][:, 0:DA_V_DIM] / st[qi]["a"][:, DA_V_DIM:DA_V_DIM + 1]
            o = a[0:tq] - lam * a[tq:2 * tq]
            o = o * lax.rsqrt(jnp.mean(o * o, axis=-1, keepdims=True) + LN_EPS) * g_ref[...]
            o_ref[0, qi * tq:(qi + 1) * tq, :] = (o * (1.0 - lam_init)).astype(o_ref.dtype)

        return [functools.partial(part, k0) for k0 in range(0, nk, ATTN_PV_KEYS)] + [finish]

    for t in range(n_blk + 2):
        stages = []
        if t < n_blk:
            stages.append(stage_a(t))
        if 0 <= t - 1 < n_blk:
            stages.append(stage_b(t - 1))
        if 0 <= t - 2 < n_blk:
            stages.append(stage_c(t - 2))
        merged = sorted(((i + 0.5) / len(ops), k, i, op) for k, ops in enumerate(stages) for i, op in enumerate(ops))
        for _, _, _, op in merged:
            op()


def _diff_attn(q, k, v, lam_q1, lam_k1, lam_q2, lam_k2, subln_g, lam_init):
    B, S, _ = q.shape
    vec = pl.BlockSpec((1, DA_HEAD_DIM), lambda b, h: (0, 0))
    seq = pl.BlockSpec((1, S, LANES), lambda b, h: (b, 0, h))
    return pl.pallas_call(
        functools.partial(_diff_attn_kernel, lam_init),
        grid=(B, DA_HEADS),
        in_specs=[vec, vec, vec, vec, pl.BlockSpec((1, DA_V_DIM), lambda b, h: (0, 0)), seq, seq, seq],
        out_specs=seq,
        out_shape=jax.ShapeDtypeStruct((B, S, DA_V), BF16),
        scratch_shapes=[pltpu.VMEM((2, 2 * ATTN_BLOCK, S), F32), pltpu.VMEM((2, 2 * ATTN_BLOCK, S), BF16),
                        pltpu.VMEM((S, 2 * DA_V_DIM), BF16)],
        compiler_params=pltpu.CompilerParams(dimension_semantics=("arbitrary",) * 2,
                                             vmem_limit_bytes=VMEM_LIMIT),
        name="diff_attn",
    )(lam_q1, lam_k1, lam_q2, lam_k2, subln_g, q, k, v)


def _gla_kernel(q_ref, k_ref, la_ref, v_ref, go_ref, ng_ref, o_ref, qt_s, oi_s, ds_s, dec_s):
    S = q_ref.shape[1]
    C = CHUNK
    BLK = GLA_BLOCK
    per_blk = BLK // C
    r = lax.broadcasted_iota(jnp.int32, (BLK, BLK), 0)
    c = lax.broadcasted_iota(jnp.int32, (BLK, BLK), 1)
    chunk_causal = (r // C == c // C) & (c <= r)
    tri = jnp.where(chunk_causal, 1.0, 0.0).astype(BF16)
    lane = lax.broadcasted_iota(jnp.int32, (BLK, LANES), 1)
    head_lanes = (lane < GLA_KEY_DIM, lane >= GLA_KEY_DIM)
    st_row = lax.broadcasted_iota(jnp.int32, (2 * GLA_V_DIM, LANES), 0)
    st_lane = lax.broadcasted_iota(jnp.int32, (2 * GLA_V_DIM, LANES), 1)
    own_keys = (st_row < GLA_V_DIM) == (st_lane < GLA_KEY_DIM)

    for b in range(S // BLK):
        r0 = b * BLK
        g = la_ref[0, r0:r0 + BLK, :]
        g1 = g.astype(BF16)
        e1 = g - g1.astype(F32)
        g2 = e1.astype(BF16)
        g3 = (e1 - g2.astype(F32)).astype(BF16)
        bcum = _dot(tri, g1) + _dot(tri, g2) + _dot(tri, g3)
        b_last = jnp.concatenate(
            [jnp.broadcast_to(bcum[i * C + C - 1:i * C + C, :], (C, LANES)) for i in range(per_blk)], axis=0)
        qf = q_ref[0, r0:r0 + BLK, :].astype(F32) * (GLA_KEY_DIM ** -0.5)
        kf = k_ref[0, r0:r0 + BLK, :].astype(F32)
        q_t = (qf * jnp.exp(bcum)).astype(BF16)
        k_t = (kf * jnp.exp(-bcum)).astype(BF16)
        k_end = (kf * jnp.exp(b_last - bcum)).astype(BF16)
        decay = jnp.exp(b_last)
        qt_s[r0:r0 + BLK, :] = q_t
        zero = jnp.zeros_like(q_t)
        for hh in range(2):
            att = jnp.where(chunk_causal, _dot_nt(jnp.where(head_lanes[hh], q_t, zero), k_t), 0.0).astype(BF16)
            oi_s[r0:r0 + BLK, hh * GLA_V_DIM:(hh + 1) * GLA_V_DIM] = _dot(
                att, v_ref[0, r0:r0 + BLK, hh * GLA_V_DIM:(hh + 1) * GLA_V_DIM])
        for i in range(per_blk):
            n = b * per_blk + i
            rows = slice(r0 + i * C, r0 + (i + 1) * C)
            inc = _dot_tn(v_ref[0, rows, :], k_end[i * C:(i + 1) * C, :])
            ds_s[n] = jnp.where(own_keys, inc, 0.0)
            dec_s[n:n + 1, :] = decay[i * C:i * C + 1, :]

    state = jnp.zeros((2 * GLA_V_DIM, LANES), F32)
    for n in range(S // C):
        rows = slice(n * C, (n + 1) * C)
        o = oi_s[rows, :] + _dot_nt(qt_s[rows, :], state.astype(BF16))
        state = state * dec_s[n:n + 1, :] + ds_s[n]
        for hh in range(2):
            cols = slice(hh * GLA_V_DIM, (hh + 1) * GLA_V_DIM)
            oh = o[:, cols]
            oh = oh * lax.rsqrt(jnp.mean(oh * oh, axis=-1, keepdims=True) + LN_EPS) * ng_ref[...]
            gate = go_ref[0, rows, cols].astype(F32)
            o_ref[0, rows, cols] = (oh * (gate * jax.nn.sigmoid(gate))).astype(o_ref.dtype)


def _gla(gq, gk, la, gv, go, norm_g):
    B, S, _ = gq.shape
    pairs = GLA_HEADS // 2
    narrow = pl.BlockSpec((1, S, LANES), lambda b, p: (b, 0, p))
    wide = pl.BlockSpec((1, S, 2 * GLA_V_DIM), lambda b, p: (b, 0, p))
    n_chunks = S // CHUNK
    return pl.pallas_call(
        _gla_kernel,
        grid=(B, pairs),
        in_specs=[narrow, narrow, narrow, wide, wide,
                  pl.BlockSpec((1, GLA_V_DIM), lambda b, p: (0, 0))],
        out_specs=wide,
        out_shape=jax.ShapeDtypeStruct((B, S, GLA_V), BF16),
        scratch_shapes=[pltpu.VMEM((S, LANES), BF16),
                        pltpu.VMEM((S, 2 * GLA_V_DIM), F32),
                        pltpu.VMEM((n_chunks, 2 * GLA_V_DIM, LANES), F32),
                        pltpu.VMEM((n_chunks, LANES), F32)],
        compiler_params=pltpu.CompilerParams(dimension_semantics=("arbitrary",) * 2,
                                             vmem_limit_bytes=VMEM_LIMIT),
        name="gla",
    )(gq, gk, la, gv, go, norm_g)


def _split3(a):
    hi = a.astype(BF16)
    lo = (a - hi.astype(F32)).astype(BF16)
    return hi, lo


def _mix_out_kernel(alpha, x_ref, da_ref, gl_ref, lng_ref, lnb_ref, wo_ref, g1_ref, b1_ref,
                    wr_hi_ref, wr_lo_ref, br_ref, lower_ref, h_ref, route_ref, cnt_ref):
    tm = x_ref.shape[0]
    i = pl.program_id(0)

    @pl.when(i == 0)
    def _():
        cnt_ref[...] = jnp.zeros_like(cnt_ref)

    logit_blocks = []
    for r0 in range(0, tm, ROW_BLOCK):
        rows = slice(r0, r0 + ROW_BLOCK)
        xn = _layer_norm(x_ref[rows, :], lng_ref[...], lnb_ref[...])
        mix = _dot(da_ref[rows, :], wo_ref[0:DA_V, :]) + _dot(gl_ref[rows, :], wo_ref[DA_V:, :])
        h = _layer_norm(alpha * xn + mix, g1_ref[...], b1_ref[...])
        h_ref[rows, :] = h
        h_hi, h_lo = _split3(h)
        logit_blocks.append(_dot(h_hi, wr_hi_ref[...]) + _dot(h_hi, wr_lo_ref[...]) + _dot(h_lo, wr_hi_ref[...])
                            + br_ref[...])
    logits = jnp.concatenate(logit_blocks, axis=0)
    lane = lax.broadcasted_iota(jnp.int32, (tm, LANES), 1)
    neg = -jnp.inf
    big = jnp.int32(LANES)

    def first_argmax(vals, valid):
        v = jnp.where(valid, vals, neg)
        mx = jnp.max(v, axis=-1, keepdims=True)
        idx = jnp.min(jnp.where(valid & (v == mx), lane, big), axis=-1, keepdims=True)
        return mx, idx

    is_group = lane < N_GROUPS
    g_max, g_top = first_argmax(logits, is_group)
    p_g = 1.0 / jnp.sum(jnp.where(is_group, jnp.exp(logits - g_max), 0.0), axis=-1, keepdims=True)

    e_lo = EXPERT_LANE0 + g_top * EXPERTS_PER_GROUP
    in_group = (lane >= e_lo) & (lane < e_lo + EXPERTS_PER_GROUP)
    v0, i0 = first_argmax(logits, in_group)
    v1, i1 = first_argmax(logits, in_group & (lane != i0))
    w1 = jnp.exp(v1 - v0)
    gate0 = p_g / (1.0 + w1)
    gate1 = p_g * w1 / (1.0 + w1)
    e0 = i0 - EXPERT_LANE0
    e1 = i1 - EXPERT_LANE0

    oh0 = jnp.where(lane == e0, 1.0, 0.0)
    oh1 = jnp.where(lane == e1, 1.0, 0.0)
    oh = oh0 + oh1
    before = _dot(lower_ref[...], oh.astype(BF16)) + cnt_ref[0:1, :]
    rank0 = jnp.sum(oh0 * before, axis=-1, keepdims=True)
    rank1 = jnp.sum(oh1 * before, axis=-1, keepdims=True)
    cnt_ref[...] = cnt_ref[...] + jnp.sum(oh, axis=0, keepdims=True)

    rec = jnp.zeros((tm, LANES), F32)
    for ln, val in ((R_E0, e0.astype(F32)), (R_E1, e1.astype(F32)), (R_G0, gate0), (R_G1, gate1),
                    (R_RANK0, rank0), (R_RANK1, rank1)):
        rec = jnp.where(lane == ln, val, rec)
    route_ref[...] = rec


def _mix_out(x2, da2, gl2, ln_g, ln_b, w_o, ln1_g, ln1_b, wr_hi, wr_lo, b_r, alpha, tm):
    T = x2.shape[0]
    row = lambda n: pl.BlockSpec((tm, n), lambda i: (i, 0))
    full = lambda a: pl.BlockSpec(a.shape, lambda i: (0,) * a.ndim)
    lower = jnp.tril(jnp.ones((tm, tm), BF16), -1)
    return pl.pallas_call(
        functools.partial(_mix_out_kernel, alpha),
        grid=(T // tm,),
        in_specs=[row(D_MODEL), row(DA_V), row(GLA_V), full(ln_g), full(ln_b), full(w_o),
                  full(ln1_g), full(ln1_b), full(wr_hi), full(wr_lo), full(b_r), full(lower)],
        out_specs=[row(D_MODEL), row(LANES), pl.BlockSpec((8, LANES), lambda i: (0, 0))],
        out_shape=[jax.ShapeDtypeStruct((T, D_MODEL), F32), jax.ShapeDtypeStruct((T, LANES), F32),
                   jax.ShapeDtypeStruct((8, LANES), F32)],
        compiler_params=pltpu.CompilerParams(dimension_semantics=("arbitrary",),
                                             vmem_limit_bytes=VMEM_LIMIT),
        name="mix_out",
    )(x2, da2, gl2, ln_g, ln_b, w_o, ln1_g, ln1_b, wr_hi, wr_lo, b_r, lower)


HIGH_HALF = 0xFFFF0000


def _pack_pairs(val):
    bits = lambda a: lax.bitcast_convert_type(a.astype(BF16).astype(F32), jnp.uint32)
    half = val.shape[1] // 2
    return (bits(val[:, :half]) >> 16) | (bits(val[:, half:]) & jnp.uint32(HIGH_HALF))


def _unpack_pairs(words):
    lo = lax.bitcast_convert_type(words << 16, F32)
    hi = lax.bitcast_convert_type(words & jnp.uint32(HIGH_HALF), F32)
    return jnp.concatenate([lo, hi], axis=1)


def _words_to_tiles(dst_ref, words):
    n = words.shape[0]
    for s in range(ROW_SUB):
        dst_ref[pl.ds(s, n, stride=ROW_SUB), :] = words[:, s * LANES:(s + 1) * LANES]


def _tiles_to_words(src_ref, r0, n):
    return jnp.concatenate([src_ref[pl.ds(r0 * ROW_SUB + s, n, stride=ROW_SUB), :] for s in range(ROW_SUB)],
                           axis=1)


def _rows_to_tiles(dst_ref, val):
    _words_to_tiles(dst_ref, _pack_pairs(val))


def _tiles_to_rows(src_ref, r0, n):
    return _unpack_pairs(_tiles_to_words(src_ref, r0, n))


def _row_tile(ref, r):
    return ref.at[pl.ds(pl.multiple_of(r * ROW_SUB, ROW_SUB), ROW_SUB), :]


def _dispatch_kernel(dest_ref, h_ref, xs_ref, stage, sems):
    tm = h_ref.shape[0]
    step = pl.program_id(0)
    half = step & 1
    base = step * (tm * TOP_K)
    n_iter = tm * TOP_K // DMA_UNROLL
    _rows_to_tiles(stage.at[half], h_ref[...])

    def row_copy(hf, t, slot):
        return pltpu.make_async_copy(_row_tile(stage.at[hf], t), _row_tile(xs_ref, slot), sems.at[hf])

    def start(i, c):
        for u in range(DMA_UNROLL):
            t = i * (DMA_UNROLL // TOP_K) + u // TOP_K
            row_copy(half, t, dest_ref[base + i * DMA_UNROLL + u]).start(priority=u % 2)
        return c

    lax.fori_loop(0, n_iter, start, 0)

    def drain(hf):
        def wait(i, c):
            for u in range(DMA_UNROLL):
                row_copy(hf, 0, 0).wait()
            return c
        lax.fori_loop(0, n_iter, wait, 0)

    @pl.when(step > 0)
    def _():
        drain(1 - half)

    @pl.when(step == pl.num_programs(0) - 1)
    def _():
        drain(half)
        n_pad = EXPERT_CHUNK * ROW_SUB
        stage[0, 0:n_pad, :] = jnp.zeros((n_pad, LANES), stage.dtype)
        pad = pltpu.make_async_copy(stage.at[0, 0:n_pad, :],
                                    xs_ref.at[pl.ds(pl.num_programs(0) * tm * TOP_K * ROW_SUB, n_pad), :], sems.at[0])
        pad.start()
        pad.wait()


def _dispatch(dest_flat, h, tm):
    T = h.shape[0]
    return pl.pallas_call(
        _dispatch_kernel,
        grid_spec=pltpu.PrefetchScalarGridSpec(
            num_scalar_prefetch=1,
            grid=(T // tm,),
            in_specs=[pl.BlockSpec((tm, D_MODEL), lambda i, d: (i, 0))],
            out_specs=pl.BlockSpec(memory_space=pl.ANY),
            scratch_shapes=[pltpu.VMEM((2, tm * ROW_SUB, LANES), jnp.uint32), pltpu.SemaphoreType.DMA((2,))]),
        out_shape=jax.ShapeDtypeStruct(((T * TOP_K + EXPERT_CHUNK) * ROW_SUB, LANES), jnp.uint32),
        compiler_params=pltpu.CompilerParams(dimension_semantics=("arbitrary",),
                                             vmem_limit_bytes=VMEM_LIMIT),
        name="dispatch",
    )(dest_flat, h)


def _experts_kernel(row0_ref, first_ref, end_ref, xs_ref, wg_ref, wu_ref, wd_ref, ys_ref,
                    wgu_b, wd_b, xbuf, ybuf, xsem, ysem, pend_ref, *, n_rows):
    e = pl.program_id(0)
    ch = EXPERT_CHUNK
    depth = EXPERT_XBUFS - 1
    g_lo = first_ref[e]
    g_hi = first_ref[e + 1]
    total = first_ref[N_EXPERTS]

    def slab(ref, row0):
        return ref.at[pl.ds(pl.multiple_of(row0 * ROW_SUB, ROW_SUB), ch * ROW_SUB), :]

    def x_copy(g):
        slot = g & (EXPERT_XBUFS - 1)
        return pltpu.make_async_copy(slab(xs_ref, row0_ref[g]), xbuf.at[slot], xsem.at[slot])

    def y_copy(row0, half):
        return pltpu.make_async_copy(ybuf.at[half], slab(ys_ref, row0), ysem.at[half])

    def drain_y(half):
        @pl.when(pend_ref[half] == 1)
        def _():
            y_copy(0, half).wait()
            pend_ref[half] = 0

    @pl.when(e == 0)
    def _():
        for b in range(EXPERT_YBUFS):
            pend_ref[b] = 0
        for b in range(EXPERT_YBUFS):
            ybuf[b] = jnp.zeros(ybuf.shape[1:], ybuf.dtype)
        y_copy(n_rows, 0).start()
        y_copy(n_rows, 0).wait()
        for d in range(depth):
            @pl.when(d < total)
            def _():
                x_copy(d).start(priority=ROW_STREAM_PRIORITY)

    @pl.when(g_hi > g_lo)
    def _():
        wgu_b[:, 0:D_EXPERT] = wg_ref[0].astype(BF16)
        wgu_b[:, D_EXPERT:] = wu_ref[0].astype(BF16)
        wd_b[...] = wd_ref[0].astype(BF16)

        def chunk(g, c):
            half = g & (EXPERT_YBUFS - 1)
            x_copy(g).wait()

            @pl.when(g + depth < total)
            def _():
                x_copy(g + depth).start(priority=ROW_STREAM_PRIORITY)

            drain_y(half)

            @pl.when(g == g_lo)
            def _():
                for b in range(EXPERT_YBUFS):
                    drain_y(b)

            def compute(m):
                xb = _tiles_to_rows(xbuf.at[g & (EXPERT_XBUFS - 1)], 0, m).astype(BF16)
                gu = _dot(xb, wgu_b[...])
                gate = gu[:, 0:D_EXPERT]
                mid = (gate * jax.nn.sigmoid(gate) * gu[:, D_EXPERT:]).astype(BF16)
                _words_to_tiles(ybuf.at[half], _pack_pairs(_dot(mid, wd_b[...])))

            owned = jnp.minimum(end_ref[e] - row0_ref[g], ch)
            grains = lax.shift_right_logical(owned + (EXPERT_GRAIN - 1), EXPERT_GRAIN.bit_length() - 1)
            for n in range(1, ch // EXPERT_GRAIN + 1):
                @pl.when(grains == n)
                def _():
                    compute(n * EXPERT_GRAIN)

            y_copy(row0_ref[g], half).start(priority=ROW_STREAM_PRIORITY)
            pend_ref[half] = 1
            return c

        lax.fori_loop(g_lo, g_hi, chunk, 0)

    @pl.when(e == pl.num_programs(0) - 1)
    def _():
        for b in range(EXPERT_YBUFS):
            drain_y(b)


def _chunk_metadata(seg_start, counts, n_rows):
    ch = EXPERT_CHUNK
    max_chunks = n_rows // ch + N_EXPERTS
    n_ch = (counts + (ch - 1)) // ch
    first = jnp.concatenate([jnp.zeros((1,), jnp.int32), jnp.cumsum(n_ch).astype(jnp.int32)])
    g = jnp.arange(max_chunks, dtype=jnp.int32)
    owner = jnp.minimum(jnp.sum((first[None, 1:] <= g[:, None]).astype(jnp.int32), axis=1), N_EXPERTS - 1)
    onehot = owner[:, None] == jnp.arange(N_EXPERTS, dtype=jnp.int32)
    pick = lambda tab: jnp.sum(jnp.where(onehot, tab[None, :], 0), axis=1)
    row0 = pick(seg_start) + (g - pick(first[:-1])) * ch
    row0 = jnp.where(g < first[-1], row0, 0)
    return row0.astype(jnp.int32), first


def _experts(seg_start, counts, xs, w_gate, w_up, w_down):
    n_rows = xs.shape[0] // ROW_SUB - EXPERT_CHUNK
    row0, first = _chunk_metadata(seg_start, counts, n_rows)
    per_expert = lambda shape: pl.BlockSpec((1,) + shape, lambda e, r, f, n: (e, 0, 0))
    slab = (EXPERT_CHUNK * ROW_SUB, LANES)
    return pl.pallas_call(
        functools.partial(_experts_kernel, n_rows=n_rows),
        grid_spec=pltpu.PrefetchScalarGridSpec(
            num_scalar_prefetch=3,
            grid=(N_EXPERTS,),
            in_specs=[pl.BlockSpec(memory_space=pl.ANY),
                      per_expert((D_MODEL, D_EXPERT)), per_expert((D_MODEL, D_EXPERT)),
                      per_expert((D_EXPERT, D_MODEL))],
            out_specs=pl.BlockSpec(memory_space=pl.ANY),
            scratch_shapes=[pltpu.VMEM((D_MODEL, 2 * D_EXPERT), BF16), pltpu.VMEM((D_EXPERT, D_MODEL), BF16),
                            pltpu.VMEM((EXPERT_XBUFS,) + slab, jnp.uint32), pltpu.VMEM((EXPERT_YBUFS,) + slab, jnp.uint32),
                            pltpu.SemaphoreType.DMA((EXPERT_XBUFS,)), pltpu.SemaphoreType.DMA((EXPERT_YBUFS,)),
                            pltpu.SMEM((EXPERT_YBUFS,), jnp.int32)]),
        out_shape=jax.ShapeDtypeStruct(xs.shape, jnp.uint32),
        compiler_params=pltpu.CompilerParams(dimension_semantics=("arbitrary",),
                                             vmem_limit_bytes=VMEM_LIMIT),
        name="experts",
    )(row0, first, (seg_start + counts).astype(jnp.int32), xs, w_gate, w_up, w_down)


def _combine_kernel(alpha, dest_ref, h_ref, route_ref, g_ref, b_ref, y_ref, o_ref, buf, sems):
    tm = h_ref.shape[0]
    step = pl.program_id(0)
    half = step & 1
    n_iter = tm * TOP_K // DMA_UNROLL

    def row_copy(hf, src, slot):
        return pltpu.make_async_copy(_row_tile(y_ref, src), _row_tile(buf.at[hf], slot), sems.at[hf])

    def gather(st, hf):
        base = st * (tm * TOP_K)

        def start(i, c):
            for u in range(DMA_UNROLL):
                slot = (u % TOP_K) * tm + i * (DMA_UNROLL // TOP_K) + u // TOP_K
                row_copy(hf, dest_ref[base + i * DMA_UNROLL + u], slot).start(priority=u % 2)
            return c

        lax.fori_loop(0, n_iter, start, 0)

    @pl.when(step == 0)
    def _():
        gather(0, 0)

    @pl.when(step + 1 < pl.num_programs(0))
    def _():
        gather(step + 1, 1 - half)

    def wait(i, c):
        for u in range(DMA_UNROLL):
            row_copy(half, 0, 0).wait()
        return c

    lax.fori_loop(0, n_iter, wait, 0)

    rec = route_ref[...]
    cur = buf.at[half]
    ffn = (rec[:, R_G0:R_G0 + 1] * _tiles_to_rows(cur, 0, tm)
           + rec[:, R_G1:R_G1 + 1] * _tiles_to_rows(cur, tm, tm))
    o_ref[...] = _layer_norm(alpha * h_ref[...] + ffn, g_ref[...], b_ref[...])


def _combine(dest_flat, h, route, ln2_g, ln2_b, y_sorted, alpha, tm):
    T = h.shape[0]
    return pl.pallas_call(
        functools.partial(_combine_kernel, alpha),
        grid_spec=pltpu.PrefetchScalarGridSpec(
            num_scalar_prefetch=1,
            grid=(T // tm,),
            in_specs=[pl.BlockSpec((tm, D_MODEL), lambda i, d: (i, 0)),
                      pl.BlockSpec((tm, LANES), lambda i, d: (i, 0)),
                      pl.BlockSpec((1, D_MODEL), lambda i, d: (0, 0)),
                      pl.BlockSpec((1, D_MODEL), lambda i, d: (0, 0)),
                      pl.BlockSpec(memory_space=pl.ANY)],
            out_specs=pl.BlockSpec((tm, D_MODEL), lambda i, d: (i, 0)),
            scratch_shapes=[pltpu.VMEM((2, TOP_K * tm * ROW_SUB, LANES), jnp.uint32), pltpu.SemaphoreType.DMA((2,))]),
        out_shape=jax.ShapeDtypeStruct((T, D_MODEL), F32),
        compiler_params=pltpu.CompilerParams(dimension_semantics=("arbitrary",),
                                             vmem_limit_bytes=VMEM_LIMIT),
        name="combine",
    )(dest_flat, h, route, ln2_g, ln2_b, y_sorted)


def kernel(x, positions, ln_in_g, ln_in_b, w_in, lam_q1, lam_k1, lam_q2, lam_k2, da_subln_g, gla_w_gate2, gla_b_gate2, gla_norm_g, w_o, ln1_g, ln1_b, router_w_group, router_b_group, router_w_expert, router_b_expert, w_gate, w_up, w_down, ln2_g, ln2_b):
    B, S, D = x.shape
    T = B * S
    depth = w_in.shape[0]
    assert depth == 1, "only a single layer is supported"
    alpha = (2 * depth) ** 0.25
    row2 = lambda a: a.reshape(1, -1)

    inv_freq = ROPE_THETA ** (-jnp.arange(0, DA_HEAD_DIM, 2, dtype=F32) / DA_HEAD_DIM)
    inv_freq = jnp.tile(jnp.repeat(inv_freq, 2), LANES // DA_HEAD_DIM).reshape(1, LANES)
    pos2 = positions.reshape(T, 1)

    cur = x.reshape(T, D)
    cur_g, cur_b = row2(ln_in_g), row2(ln_in_b)
    for l in range(depth):
        w = w_in[l]
        w_main = w[:, :D_MAIN].astype(BF16)
        w_glow = jnp.pad(w[:, D_MAIN:], ((0, 0), (0, LANES - GLA_GATE_RANK))).astype(BF16)
        w_gate2 = jnp.pad(gla_w_gate2[l], ((0, LANES - GLA_GATE_RANK), (0, 0))).astype(BF16)

        q, k, v, gq, gk, gv, go, la = _in_proj(cur, pos2, cur_g, cur_b, inv_freq, w_main, w_glow,
                                               w_gate2, row2(gla_b_gate2[l]), tm=512)
        lam_init = 0.8 - 0.6 * math.exp(-0.3 * l)
        sh = lambda a: a.reshape(B, S, a.shape[-1])
        da = _diff_attn(sh(q), sh(k), sh(v), row2(lam_q1[l]), row2(lam_k1[l]), row2(lam_q2[l]),
                        row2(lam_k2[l]), row2(da_subln_g[l]), lam_init)
        gl = _gla(sh(gq), sh(gk), sh(la), sh(gv), sh(go), row2(gla_norm_g[l]))

        w_r = jnp.zeros((D, LANES), F32)
        w_r = w_r.at[:, :N_GROUPS].set(router_w_group[l])
        w_r = w_r.at[:, EXPERT_LANE0:EXPERT_LANE0 + N_EXPERTS].set(router_w_expert[l])
        b_r = jnp.zeros((1, LANES), F32)
        b_r = b_r.at[0, :N_GROUPS].set(router_b_group[l])
        b_r = b_r.at[0, EXPERT_LANE0:EXPERT_LANE0 + N_EXPERTS].set(router_b_expert[l])
        wr_hi = w_r.astype(BF16)
        wr_lo = (w_r - wr_hi.astype(F32)).astype(BF16)

        h, route, cnt = _mix_out(cur, da.reshape(T, DA_V), gl.reshape(T, GLA_V), cur_g, cur_b,
                                 w_o[l].astype(BF16), row2(ln1_g[l]), row2(ln1_b[l]), wr_hi, wr_lo, b_r,
                                 alpha, tm=512)

        counts = cnt[0, :N_EXPERTS].astype(jnp.int32)
        seg_start = jnp.cumsum(counts) - counts
        eid = route[:, R_E0:R_E1 + 1].astype(jnp.int32)
        rank = route[:, R_RANK0:R_RANK1 + 1].astype(jnp.int32)
        onehot = eid[..., None] == jnp.arange(N_EXPERTS, dtype=jnp.int32)
        dest = jnp.sum(jnp.where(onehot, seg_start, 0), axis=-1) + rank
        dest_flat = dest.reshape(T * TOP_K)

        xs = _dispatch(dest_flat, h, tm=512)
        ys = _experts(seg_start, counts, xs, w_gate[l], w_up[l], w_down[l])
        cur = _combine(dest_flat, h, route, row2(ln2_g[l]), row2(ln2_b[l]), ys, alpha, tm=512)
    return cur.reshape(B, S, D)
```

```python
import functools
import math

import jax
import jax.numpy as jnp
from jax import lax
from jax.experimental import pallas as pl
from jax.experimental.pallas import tpu as pltpu

F32 = jnp.float32
BF16 = jnp.bfloat16

D_MODEL = 1024
CHUNK = 64
ROPE_THETA = 10000.0
LN_EPS = 1e-5
LOG2_E = math.log2(math.e)

DA_HEADS = 4
DA_V_DIM = D_MODEL // (2 * DA_HEADS)
DA_HEAD_DIM = DA_V_DIM // 2
GLA_HEADS = 4
GLA_V_DIM = D_MODEL // (2 * GLA_HEADS)
GLA_KEY_DIM = GLA_V_DIM // 2
GLA_GATE_RANK = 16
GLA_GATE_NORMALIZER = 16.0

DA_Q = DA_HEADS * 2 * DA_HEAD_DIM
DA_K = DA_Q
DA_V = DA_HEADS * DA_V_DIM
GLA_Q = GLA_HEADS * GLA_KEY_DIM
GLA_K = GLA_Q
GLA_V = GLA_HEADS * GLA_V_DIM
GLA_OG = GLA_V
D_MAIN = DA_Q + DA_K + DA_V + GLA_Q + GLA_K + GLA_V + GLA_OG

N_GROUPS = 4
EXPERTS_PER_GROUP = 8
N_EXPERTS = N_GROUPS * EXPERTS_PER_GROUP
TOP_K = 2
D_EXPERT = D_MODEL // 2

LANES = 128
ROW_SUB = D_MODEL // (2 * LANES)
ROW_BLOCK = 256
EXPERT_CHUNK = 512
EXPERT_GRAIN = 128
EXPERT_XBUFS = 8
EXPERT_YBUFS = 4
ROW_STREAM_PRIORITY = 1
GLA_BLOCK = 256
ATTN_BLOCK = 256
ATTN_PV_KEYS = 512
DMA_UNROLL = 8
VMEM_LIMIT = 48 * 1024 * 1024

R_E0, R_E1, R_G0, R_G1, R_RANK0, R_RANK1 = 0, 1, 2, 3, 4, 5
EXPERT_LANE0 = 32


def _layer_norm(x, g, b):
    mu = jnp.mean(x, axis=-1, keepdims=True)
    xc = x - mu
    var = jnp.mean(xc * xc, axis=-1, keepdims=True)
    return xc * lax.rsqrt(var + LN_EPS) * g + b


def _dot(a, b):
    return jnp.dot(a, b, preferred_element_type=F32)


def _dot_nt(a, b):
    return lax.dot_general(a, b, (((1,), (1,)), ((), ())), preferred_element_type=F32)


def _dot_tn(a, b):
    return lax.dot_general(a, b, (((0,), (0,)), ((), ())), preferred_element_type=F32)


def _in_proj_kernel(x_ref, pos_ref, g_ref, b_ref, invf_ref, w_ref, wgl_ref, wg2_ref, bg2_ref,
                    q_ref, k_ref, v_ref, gq_ref, gk_ref, gv_ref, go_ref, la_ref):
    tm = x_ref.shape[0]
    lane = lax.broadcasted_iota(jnp.int32, (ROW_BLOCK, LANES), 1)
    first = (lane & 1) == 0

    for r0 in range(0, tm, ROW_BLOCK):
        rows = slice(r0, r0 + ROW_BLOCK)
        xn = _layer_norm(x_ref[rows, :], g_ref[...], b_ref[...])
        xb = xn.astype(BF16)
        proj = _dot(xb, w_ref[...])

        ang = pos_ref[rows, :].astype(F32) * invf_ref[...]
        c = jnp.cos(ang)
        s = jnp.sin(ang)
        s_lo = jnp.where(first, -s, 0.0)
        s_hi = jnp.where(first, 0.0, s)

        def rope(t):
            out = []
            for j in range(t.shape[1] // LANES):
                tj = t[:, j * LANES:(j + 1) * LANES]
                up = pltpu.roll(tj, LANES - 1, 1)
                dn = pltpu.roll(tj, 1, 1)
                out.append(tj * c + up * s_lo + dn * s_hi)
            return jnp.concatenate(out, axis=1)

        o = 0
        q = rope(proj[:, o:o + DA_Q]) * (DA_HEAD_DIM ** -0.5 * LOG2_E)
        o += DA_Q
        k = rope(proj[:, o:o + DA_K])
        o += DA_K
        q_ref[rows, :] = q.astype(BF16)
        k_ref[rows, :] = k.astype(BF16)
        v_ref[rows, :] = proj[:, o:o + DA_V].astype(BF16)
        o += DA_V
        gq_ref[rows, :] = proj[:, o:o + GLA_Q].astype(BF16)
        o += GLA_Q
        gk_ref[rows, :] = proj[:, o:o + GLA_K].astype(BF16)
        o += GLA_K
        gv_ref[rows, :] = proj[:, o:o + GLA_V].astype(BF16)
        o += GLA_V
        go_ref[rows, :] = proj[:, o:o + GLA_OG].astype(BF16)

        g_low = _dot(xb, wgl_ref[...])
        z = _dot(g_low.astype(BF16), wg2_ref[...]) + bg2_ref[...]
        log_sig = jnp.minimum(z, 0.0) - jnp.log1p(jnp.exp(-jnp.abs(z)))
        la_ref[rows, :] = log_sig / GLA_GATE_NORMALIZER


def _in_proj(x2, pos2, ln_g, ln_b, inv_freq, w_main, w_glow, w_gate2, b_gate2, tm):
    T = x2.shape[0]
    row = lambda n: pl.BlockSpec((tm, n), lambda i: (i, 0))
    full = lambda a: pl.BlockSpec(a.shape, lambda i: (0,) * a.ndim)
    out_shape = [jax.ShapeDtypeStruct((T, n), dt) for n, dt in (
        (DA_Q, BF16), (DA_K, BF16), (DA_V, BF16), (GLA_Q, BF16), (GLA_K, BF16),
        (GLA_V, BF16), (GLA_OG, BF16), (GLA_K, F32))]
    return pl.pallas_call(
        _in_proj_kernel,
        grid=(T // tm,),
        in_specs=[row(D_MODEL), row(1), full(ln_g), full(ln_b), full(inv_freq), full(w_main),
                  full(w_glow), full(w_gate2), full(b_gate2)],
        out_specs=[row(s.shape[1]) for s in out_shape],
        out_shape=out_shape,
        compiler_params=pltpu.CompilerParams(dimension_semantics=("arbitrary",),
                                             vmem_limit_bytes=VMEM_LIMIT),
        name="in_proj",
    )(x2, pos2, ln_g, ln_b, inv_freq, w_main, w_glow, w_gate2, b_gate2)


def _diff_attn_kernel(lam_init, lq1_ref, lk1_ref, lq2_ref, lk2_ref, g_ref, q_ref, k_ref, v_ref, o_ref,
                      s_scr, p_scr, v_ext):
    S = q_ref.shape[1]
    tq = ATTN_BLOCK
    lam = (jnp.exp(jnp.sum(lq1_ref[...] * lk1_ref[...], axis=-1, keepdims=True))
           - jnp.exp(jnp.sum(lq2_ref[...] * lk2_ref[...], axis=-1, keepdims=True)) + lam_init)
    lane = lax.broadcasted_iota(jnp.int32, (tq, LANES), 1)
    rq = lax.broadcasted_iota(jnp.int32, (2 * tq, tq), 0) % tq // CHUNK
    ck = lax.broadcasted_iota(jnp.int32, (2 * tq, tq), 1) // CHUNK
    diag_mask = ck <= rq

    n_blk = S // tq
    st = [dict() for _ in range(n_blk)]
    v_ext[:, 0:DA_V_DIM] = v_ref[0]
    ext_lane = lax.broadcasted_iota(jnp.int32, (S, DA_V_DIM), 1)
    v_ext[:, DA_V_DIM:] = jnp.where(ext_lane == 0, 1.0, 0.0).astype(BF16)

    def stage_a(qi):
        s_buf = s_scr.at[qi % 2]

        def begin():
            q = q_ref[0, qi * tq:(qi + 1) * tq, :]
            zero = jnp.zeros_like(q)
            st[qi]["qq"] = jnp.concatenate([jnp.where(lane < DA_HEAD_DIM, q, zero),
                                            jnp.where(lane >= DA_HEAD_DIM, q, zero)], axis=0)
            st[qi]["m_acc"] = None

        def tile(j):
            s = _dot_nt(st[qi]["qq"], k_ref[0, j * tq:(j + 1) * tq, :])
            if j == qi:
                s = jnp.where(diag_mask, s, -jnp.inf)
            s_buf[j] = s
            m_acc = st[qi]["m_acc"]
            for c0 in range(0, tq, LANES):
                sc = s[:, c0:c0 + LANES]
                m_acc = sc if m_acc is None else jnp.maximum(m_acc, sc)
            st[qi]["m_acc"] = m_acc

        def end():
            st[qi]["m"] = jnp.broadcast_to(jnp.max(st[qi]["m_acc"], axis=-1, keepdims=True), (2 * tq, LANES))

        return [begin] + [functools.partial(tile, j) for j in range(qi + 1)] + [end]

    def stage_b(qi):
        s_buf = s_scr.at[qi % 2]
        p_buf = p_scr.at[qi % 2]

        def cols(c0):
            sc = s_buf[c0 // tq, :, c0 % tq:c0 % tq + LANES]
            p_buf[c0 // LANES] = jnp.exp2((sc - st[qi]["m"]).astype(BF16))

        return [functools.partial(cols, c0) for c0 in range(0, (qi + 1) * tq, LANES)]

    def stage_c(qi):
        p_buf = p_scr.at[qi % 2]

        nk = (qi + 1) * tq
        st[qi]["a"] = None

        def part(k0):
            k1 = min(k0 + ATTN_PV_KEYS, nk)
            p = jnp.concatenate([p_buf[c] for c in range(k0 // LANES, k1 // LANES)], axis=1)
            a = _dot(p, v_ext[k0:k1, :])
            st[qi]["a"] = a if st[qi]["a"] is None else st[qi]["a"] + a

        def finish():
            a = st[qi]["a"][:, 0:DA_V_DIM] / st[qi]["a"][:, DA_V_DIM:DA_V_DIM + 1]
            o = a[0:tq] - lam * a[tq:2 * tq]
            o = o * lax.rsqrt(jnp.mean(o * o, axis=-1, keepdims=True) + LN_EPS) * g_ref[...]
            o_ref[0, qi * tq:(qi + 1) * tq, :] = (o * (1.0 - lam_init)).astype(o_ref.dtype)

        return [functools.partial(part, k0) for k0 in range(0, nk, ATTN_PV_KEYS)] + [finish]

    for t in range(n_blk + 2):
        stages = []
        if t < n_blk:
            stages.append(stage_a(t))
        if 0 <= t - 1 < n_blk:
            stages.append(stage_b(t - 1))
        if 0 <= t - 2 < n_blk:
            stages.append(stage_c(t - 2))
        merged = sorted(((i + 0.5) / len(ops), k, i, op) for k, ops in enumerate(stages) for i, op in enumerate(ops))
        for _, _, _, op in merged:
            op()


def _diff_attn(q, k, v, lam_q1, lam_k1, lam_q2, lam_k2, subln_g, lam_init):
    B, S, _ = q.shape
    vec = pl.BlockSpec((1, DA_HEAD_DIM), lambda b, h: (0, 0))
    seq = pl.BlockSpec((1, S, LANES), lambda b, h: (b, 0, h))
    return pl.pallas_call(
        functools.partial(_diff_attn_kernel, lam_init),
        grid=(B, DA_HEADS),
        in_specs=[vec, vec, vec, vec, pl.BlockSpec((1, DA_V_DIM), lambda b, h: (0, 0)), seq, seq, seq],
        out_specs=seq,
        out_shape=jax.ShapeDtypeStruct((B, S, DA_V), BF16),
        scratch_shapes=[pltpu.VMEM((2, S // ATTN_BLOCK, 2 * ATTN_BLOCK, ATTN_BLOCK), F32),
                        pltpu.VMEM((2, S // LANES, 2 * ATTN_BLOCK, LANES), BF16),
                        pltpu.VMEM((S, 2 * DA_V_DIM), BF16)],
        compiler_params=pltpu.CompilerParams(dimension_semantics=("arbitrary",) * 2,
                                             vmem_limit_bytes=VMEM_LIMIT),
        name="diff_attn",
    )(lam_q1, lam_k1, lam_q2, lam_k2, subln_g, q, k, v)


def _gla_kernel(q_ref, k_ref, la_ref, v_ref, go_ref, ng_ref, o_ref, qt_s, oi_s, ds_s, dec_s):
    S = q_ref.shape[1]
    C = CHUNK
    BLK = GLA_BLOCK
    per_blk = BLK // C
    r = lax.broadcasted_iota(jnp.int32, (BLK, BLK), 0)
    c = lax.broadcasted_iota(jnp.int32, (BLK, BLK), 1)
    chunk_causal = (r // C == c // C) & (c <= r)
    tri = jnp.where(chunk_causal, 1.0, 0.0).astype(BF16)
    lane = lax.broadcasted_iota(jnp.int32, (BLK, LANES), 1)
    head_lanes = (lane < GLA_KEY_DIM, lane >= GLA_KEY_DIM)
    st_row = lax.broadcasted_iota(jnp.int32, (2 * GLA_V_DIM, LANES), 0)
    st_lane = lax.broadcasted_iota(jnp.int32, (2 * GLA_V_DIM, LANES), 1)
    own_keys = (st_row < GLA_V_DIM) == (st_lane < GLA_KEY_DIM)

    for b in range(S // BLK):
        r0 = b * BLK
        g = la_ref[0, r0:r0 + BLK, :]
        g1 = g.astype(BF16)
        e1 = g - g1.astype(F32)
        g2 = e1.astype(BF16)
        g3 = (e1 - g2.astype(F32)).astype(BF16)
        bcum = _dot(tri, g1) + _dot(tri, g2) + _dot(tri, g3)
        b_last = jnp.concatenate(
            [jnp.broadcast_to(bcum[i * C + C - 1:i * C + C, :], (C, LANES)) for i in range(per_blk)], axis=0)
        qf = q_ref[0, r0:r0 + BLK, :].astype(F32) * (GLA_KEY_DIM ** -0.5)
        kf = k_ref[0, r0:r0 + BLK, :].astype(F32)
        q_t = (qf * jnp.exp(bcum)).astype(BF16)
        k_t = (kf * jnp.exp(-bcum)).astype(BF16)
        k_end = (kf * jnp.exp(b_last - bcum)).astype(BF16)
        decay = jnp.exp(b_last)
        qt_s[r0:r0 + BLK, :] = q_t
        zero = jnp.zeros_like(q_t)
        for hh in range(2):
            att = jnp.where(chunk_causal, _dot_nt(jnp.where(head_lanes[hh], q_t, zero), k_t), 0.0).astype(BF16)
            oi_s[r0:r0 + BLK, hh * GLA_V_DIM:(hh + 1) * GLA_V_DIM] = _dot(
                att, v_ref[0, r0:r0 + BLK, hh * GLA_V_DIM:(hh + 1) * GLA_V_DIM])
        for i in range(per_blk):
            n = b * per_blk + i
            rows = slice(r0 + i * C, r0 + (i + 1) * C)
            inc = _dot_tn(v_ref[0, rows, :], k_end[i * C:(i + 1) * C, :])
            ds_s[n] = jnp.where(own_keys, inc, 0.0)
            dec_s[n:n + 1, :] = decay[i * C:i * C + 1, :]

    state = jnp.zeros((2 * GLA_V_DIM, LANES), F32)
    for n in range(S // C):
        rows = slice(n * C, (n + 1) * C)
        o = oi_s[rows, :] + _dot_nt(qt_s[rows, :], state.astype(BF16))
        state = state * dec_s[n:n + 1, :] + ds_s[n]
        for hh in range(2):
            cols = slice(hh * GLA_V_DIM, (hh + 1) * GLA_V_DIM)
            oh = o[:, cols]
            oh = oh * lax.rsqrt(jnp.mean(oh * oh, axis=-1, keepdims=True) + LN_EPS) * ng_ref[...]
            gate = go_ref[0, rows, cols].astype(F32)
            o_ref[0, rows, cols] = (oh * (gate * jax.nn.sigmoid(gate))).astype(o_ref.dtype)


def _gla(gq, gk, la, gv, go, norm_g):
    B, S, _ = gq.shape
    pairs = GLA_HEADS // 2
    narrow = pl.BlockSpec((1, S, LANES), lambda b, p: (b, 0, p))
    wide = pl.BlockSpec((1, S, 2 * GLA_V_DIM), lambda b, p: (b, 0, p))
    n_chunks = S // CHUNK
    return pl.pallas_call(
        _gla_kernel,
        grid=(B, pairs),
        in_specs=[narrow, narrow, narrow, wide, wide,
                  pl.BlockSpec((1, GLA_V_DIM), lambda b, p: (0, 0))],
        out_specs=wide,
        out_shape=jax.ShapeDtypeStruct((B, S, GLA_V), BF16),
        scratch_shapes=[pltpu.VMEM((S, LANES), BF16),
                        pltpu.VMEM((S, 2 * GLA_V_DIM), F32),
                        pltpu.VMEM((n_chunks, 2 * GLA_V_DIM, LANES), F32),
                        pltpu.VMEM((n_chunks, LANES), F32)],
        compiler_params=pltpu.CompilerParams(dimension_semantics=("arbitrary",) * 2,
                                             vmem_limit_bytes=VMEM_LIMIT),
        name="gla",
    )(gq, gk, la, gv, go, norm_g)


def _split3(a):
    hi = a.astype(BF16)
    lo = (a - hi.astype(F32)).astype(BF16)
    return hi, lo


def _mix_out_kernel(alpha, x_ref, da_ref, gl_ref, lng_ref, lnb_ref, wo_ref, g1_ref, b1_ref,
                    wr_hi_ref, wr_lo_ref, br_ref, lower_ref, h_ref, route_ref, cnt_ref):
    tm = x_ref.shape[0]
    i = pl.program_id(0)

    @pl.when(i == 0)
    def _():
        cnt_ref[...] = jnp.zeros_like(cnt_ref)

    logit_blocks = []
    for r0 in range(0, tm, ROW_BLOCK):
        rows = slice(r0, r0 + ROW_BLOCK)
        xn = _layer_norm(x_ref[rows, :], lng_ref[...], lnb_ref[...])
        mix = _dot(da_ref[rows, :], wo_ref[0:DA_V, :]) + _dot(gl_ref[rows, :], wo_ref[DA_V:, :])
        h = _layer_norm(alpha * xn + mix, g1_ref[...], b1_ref[...])
        h_ref[rows, :] = h
        h_hi, h_lo = _split3(h)
        logit_blocks.append(_dot(h_hi, wr_hi_ref[...]) + _dot(h_hi, wr_lo_ref[...]) + _dot(h_lo, wr_hi_ref[...])
                            + br_ref[...])
    logits = jnp.concatenate(logit_blocks, axis=0)
    lane = lax.broadcasted_iota(jnp.int32, (tm, LANES), 1)
    neg = -jnp.inf
    big = jnp.int32(LANES)

    def first_argmax(vals, valid):
        v = jnp.where(valid, vals, neg)
        mx = jnp.max(v, axis=-1, keepdims=True)
        idx = jnp.min(jnp.where(valid & (v == mx), lane, big), axis=-1, keepdims=True)
        return mx, idx

    is_group = lane < N_GROUPS
    g_max, g_top = first_argmax(logits, is_group)
    p_g = 1.0 / jnp.sum(jnp.where(is_group, jnp.exp(logits - g_max), 0.0), axis=-1, keepdims=True)

    e_lo = EXPERT_LANE0 + g_top * EXPERTS_PER_GROUP
    in_group = (lane >= e_lo) & (lane < e_lo + EXPERTS_PER_GROUP)
    v0, i0 = first_argmax(logits, in_group)
    v1, i1 = first_argmax(logits, in_group & (lane != i0))
    w1 = jnp.exp(v1 - v0)
    gate0 = p_g / (1.0 + w1)
    gate1 = p_g * w1 / (1.0 + w1)
    e0 = i0 - EXPERT_LANE0
    e1 = i1 - EXPERT_LANE0

    oh0 = jnp.where(lane == e0, 1.0, 0.0)
    oh1 = jnp.where(lane == e1, 1.0, 0.0)
    oh = oh0 + oh1
    before = _dot(lower_ref[...], oh.astype(BF16)) + cnt_ref[0:1, :]
    rank0 = jnp.sum(oh0 * before, axis=-1, keepdims=True)
    rank1 = jnp.sum(oh1 * before, axis=-1, keepdims=True)
    cnt_ref[...] = cnt_ref[...] + jnp.sum(oh, axis=0, keepdims=True)

    rec = jnp.zeros((tm, LANES), F32)
    for ln, val in ((R_E0, e0.astype(F32)), (R_E1, e1.astype(F32)), (R_G0, gate0), (R_G1, gate1),
                    (R_RANK0, rank0), (R_RANK1, rank1)):
        rec = jnp.where(lane == ln, val, rec)
    route_ref[...] = rec


def _mix_out(x2, da2, gl2, ln_g, ln_b, w_o, ln1_g, ln1_b, wr_hi, wr_lo, b_r, alpha, tm):
    T = x2.shape[0]
    row = lambda n: pl.BlockSpec((tm, n), lambda i: (i, 0))
    full = lambda a: pl.BlockSpec(a.shape, lambda i: (0,) * a.ndim)
    lower = jnp.tril(jnp.ones((tm, tm), BF16), -1)
    return pl.pallas_call(
        functools.partial(_mix_out_kernel, alpha),
        grid=(T // tm,),
        in_specs=[row(D_MODEL), row(DA_V), row(GLA_V), full(ln_g), full(ln_b), full(w_o),
                  full(ln1_g), full(ln1_b), full(wr_hi), full(wr_lo), full(b_r), full(lower)],
        out_specs=[row(D_MODEL), row(LANES), pl.BlockSpec((8, LANES), lambda i: (0, 0))],
        out_shape=[jax.ShapeDtypeStruct((T, D_MODEL), F32), jax.ShapeDtypeStruct((T, LANES), F32),
                   jax.ShapeDtypeStruct((8, LANES), F32)],
        compiler_params=pltpu.CompilerParams(dimension_semantics=("arbitrary",),
                                             vmem_limit_bytes=VMEM_LIMIT),
        name="mix_out",
    )(x2, da2, gl2, ln_g, ln_b, w_o, ln1_g, ln1_b, wr_hi, wr_lo, b_r, lower)


HIGH_HALF = 0xFFFF0000


def _pack_pairs(val):
    bits = lambda a: lax.bitcast_convert_type(a.astype(BF16).astype(F32), jnp.uint32)
    half = val.shape[1] // 2
    return (bits(val[:, :half]) >> 16) | (bits(val[:, half:]) & jnp.uint32(HIGH_HALF))


def _unpack_pairs(words):
    lo = lax.bitcast_convert_type(words << 16, F32)
    hi = lax.bitcast_convert_type(words & jnp.uint32(HIGH_HALF), F32)
    return jnp.concatenate([lo, hi], axis=1)


def _words_to_tiles(dst_ref, words):
    n = words.shape[0]
    for s in range(ROW_SUB):
        dst_ref[pl.ds(s, n, stride=ROW_SUB), :] = words[:, s * LANES:(s + 1) * LANES]


def _tiles_to_words(src_ref, r0, n):
    return jnp.concatenate([src_ref[pl.ds(r0 * ROW_SUB + s, n, stride=ROW_SUB), :] for s in range(ROW_SUB)],
                           axis=1)


def _rows_to_tiles(dst_ref, val):
    _words_to_tiles(dst_ref, _pack_pairs(val))


def _tiles_to_rows(src_ref, r0, n):
    return _unpack_pairs(_tiles_to_words(src_ref, r0, n))


def _row_tile(ref, r):
    return ref.at[pl.ds(pl.multiple_of(r * ROW_SUB, ROW_SUB), ROW_SUB), :]


def _dispatch_kernel(dest_ref, h_ref, xs_ref, stage, sems):
    tm = h_ref.shape[0]
    step = pl.program_id(0)
    half = step & 1
    base = step * (tm * TOP_K)
    n_iter = tm * TOP_K // DMA_UNROLL
    _rows_to_tiles(stage.at[half], h_ref[...])

    def row_copy(hf, t, slot):
        return pltpu.make_async_copy(_row_tile(stage.at[hf], t), _row_tile(xs_ref, slot), sems.at[hf])

    def start(i, c):
        for u in range(DMA_UNROLL):
            t = i * (DMA_UNROLL // TOP_K) + u // TOP_K
            row_copy(half, t, dest_ref[base + i * DMA_UNROLL + u]).start(priority=u % 2)
        return c

    lax.fori_loop(0, n_iter, start, 0)

    def drain(hf):
        def wait(i, c):
            for u in range(DMA_UNROLL):
                row_copy(hf, 0, 0).wait()
            return c
        lax.fori_loop(0, n_iter, wait, 0)

    @pl.when(step > 0)
    def _():
        drain(1 - half)

    @pl.when(step == pl.num_programs(0) - 1)
    def _():
        drain(half)
        n_pad = EXPERT_CHUNK * ROW_SUB
        stage[0, 0:n_pad, :] = jnp.zeros((n_pad, LANES), stage.dtype)
        pad = pltpu.make_async_copy(stage.at[0, 0:n_pad, :],
                                    xs_ref.at[pl.ds(pl.num_programs(0) * tm * TOP_K * ROW_SUB, n_pad), :], sems.at[0])
        pad.start()
        pad.wait()


def _dispatch(dest_flat, h, tm):
    T = h.shape[0]
    return pl.pallas_call(
        _dispatch_kernel,
        grid_spec=pltpu.PrefetchScalarGridSpec(
            num_scalar_prefetch=1,
            grid=(T // tm,),
            in_specs=[pl.BlockSpec((tm, D_MODEL), lambda i, d: (i, 0))],
            out_specs=pl.BlockSpec(memory_space=pl.ANY),
            scratch_shapes=[pltpu.VMEM((2, tm * ROW_SUB, LANES), jnp.uint32), pltpu.SemaphoreType.DMA((2,))]),
        out_shape=jax.ShapeDtypeStruct(((T * TOP_K + EXPERT_CHUNK) * ROW_SUB, LANES), jnp.uint32),
        compiler_params=pltpu.CompilerParams(dimension_semantics=("arbitrary",),
                                             vmem_limit_bytes=VMEM_LIMIT),
        name="dispatch",
    )(dest_flat, h)


def _experts_kernel(row0_ref, first_ref, end_ref, xs_ref, wg_ref, wu_ref, wd_ref, ys_ref,
                    wgu_b, wd_b, xbuf, ybuf, xsem, ysem, pend_ref, *, n_rows):
    e = pl.program_id(0)
    ch = EXPERT_CHUNK
    depth = EXPERT_XBUFS - 1
    g_lo = first_ref[e]
    g_hi = first_ref[e + 1]
    total = first_ref[N_EXPERTS]

    def slab(ref, row0):
        return ref.at[pl.ds(pl.multiple_of(row0 * ROW_SUB, ROW_SUB), ch * ROW_SUB), :]

    def x_copy(g):
        slot = g & (EXPERT_XBUFS - 1)
        return pltpu.make_async_copy(slab(xs_ref, row0_ref[g]), xbuf.at[slot], xsem.at[slot])

    def y_copy(row0, half):
        return pltpu.make_async_copy(ybuf.at[half], slab(ys_ref, row0), ysem.at[half])

    def drain_y(half):
        @pl.when(pend_ref[half] == 1)
        def _():
            y_copy(0, half).wait()
            pend_ref[half] = 0

    @pl.when(e == 0)
    def _():
        for b in range(EXPERT_YBUFS):
            pend_ref[b] = 0
        for b in range(EXPERT_YBUFS):
            ybuf[b] = jnp.zeros(ybuf.shape[1:], ybuf.dtype)
        y_copy(n_rows, 0).start()
        y_copy(n_rows, 0).wait()
        for d in range(depth):
            @pl.when(d < total)
            def _():
                x_copy(d).start(priority=ROW_STREAM_PRIORITY)

    @pl.when(g_hi > g_lo)
    def _():
        wgu_b[:, 0:D_EXPERT] = wg_ref[0].astype(BF16)
        wgu_b[:, D_EXPERT:] = wu_ref[0].astype(BF16)
        wd_b[...] = wd_ref[0].astype(BF16)

        def chunk(g, c):
            half = g & (EXPERT_YBUFS - 1)
            x_copy(g).wait()

            @pl.when(g + depth < total)
            def _():
                x_copy(g + depth).start(priority=ROW_STREAM_PRIORITY)

            drain_y(half)

            @pl.when(g == g_lo)
            def _():
                for b in range(EXPERT_YBUFS):
                    drain_y(b)

            def compute(m):
                xb = _tiles_to_rows(xbuf.at[g & (EXPERT_XBUFS - 1)], 0, m).astype(BF16)
                gu = _dot(xb, wgu_b[...])
                gate = gu[:, 0:D_EXPERT]
                mid = (gate * jax.nn.sigmoid(gate) * gu[:, D_EXPERT:]).astype(BF16)
                _words_to_tiles(ybuf.at[half], _pack_pairs(_dot(mid, wd_b[...])))

            owned = jnp.minimum(end_ref[e] - row0_ref[g], ch)
            grains = lax.shift_right_logical(owned + (EXPERT_GRAIN - 1), EXPERT_GRAIN.bit_length() - 1)
            for n in range(1, ch // EXPERT_GRAIN + 1):
                @pl.when(grains == n)
                def _():
                    compute(n * EXPERT_GRAIN)

            y_copy(row0_ref[g], half).start(priority=ROW_STREAM_PRIORITY)
            pend_ref[half] = 1
            return c

        lax.fori_loop(g_lo, g_hi, chunk, 0)

    @pl.when(e == pl.num_programs(0) - 1)
    def _():
        for b in range(EXPERT_YBUFS):
            drain_y(b)


def _chunk_metadata(seg_start, counts, n_rows):
    ch = EXPERT_CHUNK
    max_chunks = n_rows // ch + N_EXPERTS
    n_ch = (counts + (ch - 1)) // ch
    first = jnp.concatenate([jnp.zeros((1,), jnp.int32), jnp.cumsum(n_ch).astype(jnp.int32)])
    g = jnp.arange(max_chunks, dtype=jnp.int32)
    owner = jnp.minimum(jnp.sum((first[None, 1:] <= g[:, None]).astype(jnp.int32), axis=1), N_EXPERTS - 1)
    onehot = owner[:, None] == jnp.arange(N_EXPERTS, dtype=jnp.int32)
    pick = lambda tab: jnp.sum(jnp.where(onehot, tab[None, :], 0), axis=1)
    row0 = pick(seg_start) + (g - pick(first[:-1])) * ch
    row0 = jnp.where(g < first[-1], row0, 0)
    return row0.astype(jnp.int32), first


def _experts(seg_start, counts, xs, w_gate, w_up, w_down):
    n_rows = xs.shape[0] // ROW_SUB - EXPERT_CHUNK
    row0, first = _chunk_metadata(seg_start, counts, n_rows)
    per_expert = lambda shape: pl.BlockSpec((1,) + shape, lambda e, r, f, n: (e, 0, 0))
    slab = (EXPERT_CHUNK * ROW_SUB, LANES)
    return pl.pallas_call(
        functools.partial(_experts_kernel, n_rows=n_rows),
        grid_spec=pltpu.PrefetchScalarGridSpec(
            num_scalar_prefetch=3,
            grid=(N_EXPERTS,),
            in_specs=[pl.BlockSpec(memory_space=pl.ANY),
                      per_expert((D_MODEL, D_EXPERT)), per_expert((D_MODEL, D_EXPERT)),
                      per_expert((D_EXPERT, D_MODEL))],
            out_specs=pl.BlockSpec(memory_space=pl.ANY),
            scratch_shapes=[pltpu.VMEM((D_MODEL, 2 * D_EXPERT), BF16), pltpu.VMEM((D_EXPERT, D_MODEL), BF16),
                            pltpu.VMEM((EXPERT_XBUFS,) + slab, jnp.uint32), pltpu.VMEM((EXPERT_YBUFS,) + slab, jnp.uint32),
                            pltpu.SemaphoreType.DMA((EXPERT_XBUFS,)), pltpu.SemaphoreType.DMA((EXPERT_YBUFS,)),
                            pltpu.SMEM((EXPERT_YBUFS,), jnp.int32)]),
        out_shape=jax.ShapeDtypeStruct(xs.shape, jnp.uint32),
        compiler_params=pltpu.CompilerParams(dimension_semantics=("arbitrary",),
                                             vmem_limit_bytes=VMEM_LIMIT),
        name="experts",
    )(row0, first, (seg_start + counts).astype(jnp.int32), xs, w_gate, w_up, w_down)


def _combine_kernel(alpha, dest_ref, h_ref, route_ref, g_ref, b_ref, y_ref, o_ref, buf, sems):
    tm = h_ref.shape[0]
    step = pl.program_id(0)
    half = step & 1
    n_iter = tm * TOP_K // DMA_UNROLL

    def row_copy(hf, src, slot):
        return pltpu.make_async_copy(_row_tile(y_ref, src), _row_tile(buf.at[hf], slot), sems.at[hf])

    def gather(st, hf):
        base = st * (tm * TOP_K)

        def start(i, c):
            for u in range(DMA_UNROLL):
                slot = (u % TOP_K) * tm + i * (DMA_UNROLL // TOP_K) + u // TOP_K
                row_copy(hf, dest_ref[base + i * DMA_UNROLL + u], slot).start(priority=u % 2)
            return c

        lax.fori_loop(0, n_iter, start, 0)

    @pl.when(step == 0)
    def _():
        gather(0, 0)

    @pl.when(step + 1 < pl.num_programs(0))
    def _():
        gather(step + 1, 1 - half)

    def wait(i, c):
        for u in range(DMA_UNROLL):
            row_copy(half, 0, 0).wait()
        return c

    lax.fori_loop(0, n_iter, wait, 0)

    rec = route_ref[...]
    cur = buf.at[half]
    ffn = (rec[:, R_G0:R_G0 + 1] * _tiles_to_rows(cur, 0, tm)
           + rec[:, R_G1:R_G1 + 1] * _tiles_to_rows(cur, tm, tm))
    o_ref[...] = _layer_norm(alpha * h_ref[...] + ffn, g_ref[...], b_ref[...])


def _combine(dest_flat, h, route, ln2_g, ln2_b, y_sorted, alpha, tm):
    T = h.shape[0]
    return pl.pallas_call(
        functools.partial(_combine_kernel, alpha),
        grid_spec=pltpu.PrefetchScalarGridSpec(
            num_scalar_prefetch=1,
            grid=(T // tm,),
            in_specs=[pl.BlockSpec((tm, D_MODEL), lambda i, d: (i, 0)),
                      pl.BlockSpec((tm, LANES), lambda i, d: (i, 0)),
                      pl.BlockSpec((1, D_MODEL), lambda i, d: (0, 0)),
                      pl.BlockSpec((1, D_MODEL), lambda i, d: (0, 0)),
                      pl.BlockSpec(memory_space=pl.ANY)],
            out_specs=pl.BlockSpec((tm, D_MODEL), lambda i, d: (i, 0)),
            scratch_shapes=[pltpu.VMEM((2, TOP_K * tm * ROW_SUB, LANES), jnp.uint32), pltpu.SemaphoreType.DMA((2,))]),
        out_shape=jax.ShapeDtypeStruct((T, D_MODEL), F32),
        compiler_params=pltpu.CompilerParams(dimension_semantics=("arbitrary",),
                                             vmem_limit_bytes=VMEM_LIMIT),
        name="combine",
    )(dest_flat, h, route, ln2_g, ln2_b, y_sorted)


def kernel(x, positions, ln_in_g, ln_in_b, w_in, lam_q1, lam_k1, lam_q2, lam_k2, da_subln_g, gla_w_gate2, gla_b_gate2, gla_norm_g, w_o, ln1_g, ln1_b, router_w_group, router_b_group, router_w_expert, router_b_expert, w_gate, w_up, w_down, ln2_g, ln2_b):
    B, S, D = x.shape
    T = B * S
    depth = w_in.shape[0]
    assert depth == 1, "only a single layer is supported"
    alpha = (2 * depth) ** 0.25
    row2 = lambda a: a.reshape(1, -1)

    inv_freq = ROPE_THETA ** (-jnp.arange(0, DA_HEAD_DIM, 2, dtype=F32) / DA_HEAD_DIM)
    inv_freq = jnp.tile(jnp.repeat(inv_freq, 2), LANES // DA_HEAD_DIM).reshape(1, LANES)
    pos2 = positions.reshape(T, 1)

    cur = x.reshape(T, D)
    cur_g, cur_b = row2(ln_in_g), row2(ln_in_b)
    for l in range(depth):
        w = w_in[l]
        w_main = w[:, :D_MAIN].astype(BF16)
        w_glow = jnp.pad(w[:, D_MAIN:], ((0, 0), (0, LANES - GLA_GATE_RANK))).astype(BF16)
        w_gate2 = jnp.pad(gla_w_gate2[l], ((0, LANES - GLA_GATE_RANK), (0, 0))).astype(BF16)

        q, k, v, gq, gk, gv, go, la = _in_proj(cur, pos2, cur_g, cur_b, inv_freq, w_main, w_glow,
                                               w_gate2, row2(gla_b_gate2[l]), tm=512)
        lam_init = 0.8 - 0.6 * math.exp(-0.3 * l)
        sh = lambda a: a.reshape(B, S, a.shape[-1])
        da = _diff_attn(sh(q), sh(k), sh(v), row2(lam_q1[l]), row2(lam_k1[l]), row2(lam_q2[l]),
                        row2(lam_k2[l]), row2(da_subln_g[l]), lam_init)
        gl = _gla(sh(gq), sh(gk), sh(la), sh(gv), sh(go), row2(gla_norm_g[l]))

        w_r = jnp.zeros((D, LANES), F32)
        w_r = w_r.at[:, :N_GROUPS].set(router_w_group[l])
        w_r = w_r.at[:, EXPERT_LANE0:EXPERT_LANE0 + N_EXPERTS].set(router_w_expert[l])
        b_r = jnp.zeros((1, LANES), F32)
        b_r = b_r.at[0, :N_GROUPS].set(router_b_group[l])
        b_r = b_r.at[0, EXPERT_LANE0:EXPERT_LANE0 + N_EXPERTS].set(router_b_expert[l])
        wr_hi = w_r.astype(BF16)
        wr_lo = (w_r - wr_hi.astype(F32)).astype(BF16)

        h, route, cnt = _mix_out(cur, da.reshape(T, DA_V), gl.reshape(T, GLA_V), cur_g, cur_b,
                                 w_o[l].astype(BF16), row2(ln1_g[l]), row2(ln1_b[l]), wr_hi, wr_lo, b_r,
                                 alpha, tm=512)

        counts = cnt[0, :N_EXPERTS].astype(jnp.int32)
        seg_start = jnp.cumsum(counts) - counts
        eid = route[:, R_E0:R_E1 + 1].astype(jnp.int32)
        rank = route[:, R_RANK0:R_RANK1 + 1].astype(jnp.int32)
        onehot = eid[..., None] == jnp.arange(N_EXPERTS, dtype=jnp.int32)
        dest = jnp.sum(jnp.where(onehot, seg_start, 0), axis=-1) + rank
        dest_flat = dest.reshape(T * TOP_K)

        xs = _dispatch(dest_flat, h, tm=512)
        ys = _experts(seg_start, counts, xs, w_gate[l], w_up[l], w_down[l])
        cur = _combine(dest_flat, h, route, row2(ln2_g[l]), row2(ln2_b[l]), ys, alpha, tm=512)
    return cur.reshape(B, S, D)
```

```python
import functools
import math

import jax
import jax.numpy as jnp
from jax import lax
from jax.experimental import pallas as pl
from jax.experimental.pallas import tpu as pltpu

F32 = jnp.float32
BF16 = jnp.bfloat16

D_MODEL = 1024
CHUNK = 64
ROPE_THETA = 10000.0
LN_EPS = 1e-5
LOG2_E = math.log2(math.e)

DA_HEADS = 4
DA_V_DIM = D_MODEL // (2 * DA_HEADS)
DA_HEAD_DIM = DA_V_DIM // 2
GLA_HEADS = 4
GLA_V_DIM = D_MODEL // (2 * GLA_HEADS)
GLA_KEY_DIM = GLA_V_DIM // 2
GLA_GATE_RANK = 16
GLA_GATE_NORMALIZER = 16.0

DA_Q = DA_HEADS * 2 * DA_HEAD_DIM
DA_K = DA_Q
DA_V = DA_HEADS * DA_V_DIM
GLA_Q = GLA_HEADS * GLA_KEY_DIM
GLA_K = GLA_Q
GLA_V = GLA_HEADS * GLA_V_DIM
GLA_OG = GLA_V
D_MAIN = DA_Q + DA_K + DA_V + GLA_Q + GLA_K + GLA_V + GLA_OG

N_GROUPS = 4
EXPERTS_PER_GROUP = 8
N_EXPERTS = N_GROUPS * EXPERTS_PER_GROUP
TOP_K = 2
D_EXPERT = D_MODEL // 2

LANES = 128
ROW_SUB = D_MODEL // (2 * LANES)
ROW_BLOCK = 256
EXPERT_CHUNK = 512
EXPERT_GRAIN = 128
EXPERT_XBUFS = 8
EXPERT_YBUFS = 4
ROW_STREAM_PRIORITY = 1
GLA_BLOCK = 256
ATTN_BLOCK = 256
ATTN_PV_KEYS = 512
DMA_UNROLL = 8
VMEM_LIMIT = 48 * 1024 * 1024

R_E0, R_E1, R_G0, R_G1, R_RANK0, R_RANK1 = 0, 1, 2, 3, 4, 5
EXPERT_LANE0 = 32


def _layer_norm(x, g, b):
    mu = jnp.mean(x, axis=-1, keepdims=True)
    xc = x - mu
    var = jnp.mean(xc * xc, axis=-1, keepdims=True)
    return xc * lax.rsqrt(var + LN_EPS) * g + b


def _dot(a, b):
    return jnp.dot(a, b, preferred_element_type=F32)


def _dot_nt(a, b):
    return lax.dot_general(a, b, (((1,), (1,)), ((), ())), preferred_element_type=F32)


def _dot_tn(a, b):
    return lax.dot_general(a, b, (((0,), (0,)), ((), ())), preferred_element_type=F32)


def _in_proj_kernel(x_ref, pos_ref, g_ref, b_ref, invf_ref, w_ref, wgl_ref, wg2_ref, bg2_ref,
                    q_ref, k_ref, v_ref, gq_ref, gk_ref, gv_ref, go_ref, la_ref):
    tm = x_ref.shape[0]
    lane = lax.broadcasted_iota(jnp.int32, (ROW_BLOCK, LANES), 1)
    first = (lane & 1) == 0

    for r0 in range(0, tm, ROW_BLOCK):
        rows = slice(r0, r0 + ROW_BLOCK)
        xn = _layer_norm(x_ref[rows, :], g_ref[...], b_ref[...])
        xb = xn.astype(BF16)
        proj = _dot(xb, w_ref[...])

        ang = pos_ref[rows, :].astype(F32) * invf_ref[...]
        c = jnp.cos(ang)
        s = jnp.sin(ang)
        s_lo = jnp.where(first, -s, 0.0)
        s_hi = jnp.where(first, 0.0, s)

        def rope(t):
            out = []
            for j in range(t.shape[1] // LANES):
                tj = t[:, j * LANES:(j + 1) * LANES]
                up = pltpu.roll(tj, LANES - 1, 1)
                dn = pltpu.roll(tj, 1, 1)
                out.append(tj * c + up * s_lo + dn * s_hi)
            return jnp.concatenate(out, axis=1)

        o = 0
        q = rope(proj[:, o:o + DA_Q]) * (DA_HEAD_DIM ** -0.5 * LOG2_E)
        o += DA_Q
        k = rope(proj[:, o:o + DA_K])
        o += DA_K
        q_ref[rows, :] = q.astype(BF16)
        k_ref[rows, :] = k.astype(BF16)
        v_ref[rows, :] = proj[:, o:o + DA_V].astype(BF16)
        o += DA_V
        gq_ref[rows, :] = proj[:, o:o + GLA_Q].astype(BF16)
        o += GLA_Q
        gk_ref[rows, :] = proj[:, o:o + GLA_K].astype(BF16)
        o += GLA_K
        gv_ref[rows, :] = proj[:, o:o + GLA_V].astype(BF16)
        o += GLA_V
        go_ref[rows, :] = proj[:, o:o + GLA_OG].astype(BF16)

        g_low = _dot(xb, wgl_ref[...])
        z = _dot(g_low.astype(BF16), wg2_ref[...]) + bg2_ref[...]
        log_sig = jnp.minimum(z, 0.0) - jnp.log1p(jnp.exp(-jnp.abs(z)))
        la_ref[rows, :] = log_sig / GLA_GATE_NORMALIZER


def _in_proj(x2, pos2, ln_g, ln_b, inv_freq, w_main, w_glow, w_gate2, b_gate2, tm):
    T = x2.shape[0]
    row = lambda n: pl.BlockSpec((tm, n), lambda i: (i, 0))
    full = lambda a: pl.BlockSpec(a.shape, lambda i: (0,) * a.ndim)
    out_shape = [jax.ShapeDtypeStruct((T, n), dt) for n, dt in (
        (DA_Q, BF16), (DA_K, BF16), (DA_V, BF16), (GLA_Q, BF16), (GLA_K, BF16),
        (GLA_V, BF16), (GLA_OG, BF16), (GLA_K, F32))]
    return pl.pallas_call(
        _in_proj_kernel,
        grid=(T // tm,),
        in_specs=[row(D_MODEL), row(1), full(ln_g), full(ln_b), full(inv_freq), full(w_main),
                  full(w_glow), full(w_gate2), full(b_gate2)],
        out_specs=[row(s.shape[1]) for s in out_shape],
        out_shape=out_shape,
        compiler_params=pltpu.CompilerParams(dimension_semantics=("arbitrary",),
                                             vmem_limit_bytes=VMEM_LIMIT),
        name="in_proj",
    )(x2, pos2, ln_g, ln_b, inv_freq, w_main, w_glow, w_gate2, b_gate2)


def _diff_attn_kernel(lam_init, lq1_ref, lk1_ref, lq2_ref, lk2_ref, g_ref, q_ref, k_ref, v_ref, o_ref,
                      s_scr, p_scr, v_ext):
    S = q_ref.shape[1]
    tq = ATTN_BLOCK
    lam = (jnp.exp(jnp.sum(lq1_ref[...] * lk1_ref[...], axis=-1, keepdims=True))
           - jnp.exp(jnp.sum(lq2_ref[...] * lk2_ref[...], axis=-1, keepdims=True)) + lam_init)
    lane = lax.broadcasted_iota(jnp.int32, (tq, LANES), 1)
    rq = lax.broadcasted_iota(jnp.int32, (2 * tq, tq), 0) % tq // CHUNK
    ck = lax.broadcasted_iota(jnp.int32, (2 * tq, tq), 1) // CHUNK
    diag_mask = ck <= rq

    n_blk = S // tq
    st = [dict() for _ in range(n_blk)]
    v_ext[:, 0:DA_V_DIM] = v_ref[0]
    ext_lane = lax.broadcasted_iota(jnp.int32, (S, DA_V_DIM), 1)
    v_ext[:, DA_V_DIM:] = jnp.where(ext_lane == 0, 1.0, 0.0).astype(BF16)

    def stage_a(qi):
        s_buf = s_scr.at[qi % 2]

        def begin():
            q = q_ref[0, qi * tq:(qi + 1) * tq, :]
            zero = jnp.zeros_like(q)
            st[qi]["qq"] = jnp.concatenate([jnp.where(lane < DA_HEAD_DIM, q, zero),
                                            jnp.where(lane >= DA_HEAD_DIM, q, zero)], axis=0)
            st[qi]["m_acc"] = None

        def tile(j):
            s = _dot_nt(st[qi]["qq"], k_ref[0, j * tq:(j + 1) * tq, :])
            if j == qi:
                s = jnp.where(diag_mask, s, -jnp.inf)
            for c0 in range(0, tq, LANES):
                s_buf[(j * tq + c0) // LANES] = s[:, c0:c0 + LANES]
            m_acc = st[qi]["m_acc"]
            for c0 in range(0, tq, LANES):
                sc = s[:, c0:c0 + LANES]
                m_acc = sc if m_acc is None else jnp.maximum(m_acc, sc)
            st[qi]["m_acc"] = m_acc

        def end():
            st[qi]["m"] = jnp.broadcast_to(jnp.max(st[qi]["m_acc"], axis=-1, keepdims=True), (2 * tq, LANES))

        return [begin] + [functools.partial(tile, j) for j in range(qi + 1)] + [end]

    def stage_b(qi):
        s_buf = s_scr.at[qi % 2]
        p_buf = p_scr.at[qi % 2]

        def cols(c0):
            p_buf[c0 // LANES] = jnp.exp2((s_buf[c0 // LANES] - st[qi]["m"]).astype(BF16))

        return [functools.partial(cols, c0) for c0 in range(0, (qi + 1) * tq, LANES)]

    def stage_c(qi):
        p_buf = p_scr.at[qi % 2]

        nk = (qi + 1) * tq
        st[qi]["a"] = None

        def part(k0):
            k1 = min(k0 + ATTN_PV_KEYS, nk)
            p = jnp.concatenate([p_buf[c] for c in range(k0 // LANES, k1 // LANES)], axis=1)
            a = _dot(p, v_ext[k0:k1, :])
            st[qi]["a"] = a if st[qi]["a"] is None else st[qi]["a"] + a

        def finish():
            a = st[qi]["a"][:, 0:DA_V_DIM] / st[qi]["a"][:, DA_V_DIM:DA_V_DIM + 1]
            o = a[0:tq] - lam * a[tq:2 * tq]
            o = o * lax.rsqrt(jnp.mean(o * o, axis=-1, keepdims=True) + LN_EPS) * g_ref[...]
            o_ref[0, qi * tq:(qi + 1) * tq, :] = (o * (1.0 - lam_init)).astype(o_ref.dtype)

        return [functools.partial(part, k0) for k0 in range(0, nk, ATTN_PV_KEYS)] + [finish]

    for t in range(n_blk + 2):
        stages = []
        if t < n_blk:
            stages.append(stage_a(t))
        if 0 <= t - 1 < n_blk:
            stages.append(stage_b(t - 1))
        if 0 <= t - 2 < n_blk:
            stages.append(stage_c(t - 2))
        merged = sorted(((i + 0.5) / len(ops), k, i, op) for k, ops in enumerate(stages) for i, op in enumerate(ops))
        for _, _, _, op in merged:
            op()


def _diff_attn(q, k, v, lam_q1, lam_k1, lam_q2, lam_k2, subln_g, lam_init):
    B, S, _ = q.shape
    vec = pl.BlockSpec((1, DA_HEAD_DIM), lambda b, h: (0, 0))
    seq = pl.BlockSpec((1, S, LANES), lambda b, h: (b, 0, h))
    return pl.pallas_call(
        functools.partial(_diff_attn_kernel, lam_init),
        grid=(B, DA_HEADS),
        in_specs=[vec, vec, vec, vec, pl.BlockSpec((1, DA_V_DIM), lambda b, h: (0, 0)), seq, seq, seq],
        out_specs=seq,
        out_shape=jax.ShapeDtypeStruct((B, S, DA_V), BF16),
        scratch_shapes=[pltpu.VMEM((2, S // LANES, 2 * ATTN_BLOCK, LANES), F32),
                        pltpu.VMEM((2, S // LANES, 2 * ATTN_BLOCK, LANES), BF16),
                        pltpu.VMEM((S, 2 * DA_V_DIM), BF16)],
        compiler_params=pltpu.CompilerParams(dimension_semantics=("arbitrary",) * 2,
                                             vmem_limit_bytes=VMEM_LIMIT),
        name="diff_attn",
    )(lam_q1, lam_k1, lam_q2, lam_k2, subln_g, q, k, v)


def _gla_kernel(q_ref, k_ref, la_ref, v_ref, go_ref, ng_ref, o_ref, qt_s, oi_s, ds_s, dec_s):
    S = q_ref.shape[1]
    C = CHUNK
    BLK = GLA_BLOCK
    per_blk = BLK // C
    r = lax.broadcasted_iota(jnp.int32, (BLK, BLK), 0)
    c = lax.broadcasted_iota(jnp.int32, (BLK, BLK), 1)
    chunk_causal = (r // C == c // C) & (c <= r)
    tri = jnp.where(chunk_causal, 1.0, 0.0).astype(BF16)
    lane = lax.broadcasted_iota(jnp.int32, (BLK, LANES), 1)
    head_lanes = (lane < GLA_KEY_DIM, lane >= GLA_KEY_DIM)
    st_row = lax.broadcasted_iota(jnp.int32, (2 * GLA_V_DIM, LANES), 0)
    st_lane = lax.broadcasted_iota(jnp.int32, (2 * GLA_V_DIM, LANES), 1)
    own_keys = (st_row < GLA_V_DIM) == (st_lane < GLA_KEY_DIM)

    for b in range(S // BLK):
        r0 = b * BLK
        g = la_ref[0, r0:r0 + BLK, :]
        g1 = g.astype(BF16)
        e1 = g - g1.astype(F32)
        g2 = e1.astype(BF16)
        g3 = (e1 - g2.astype(F32)).astype(BF16)
        bcum = _dot(tri, g1) + _dot(tri, g2) + _dot(tri, g3)
        b_last = jnp.concatenate(
            [jnp.broadcast_to(bcum[i * C + C - 1:i * C + C, :], (C, LANES)) for i in range(per_blk)], axis=0)
        qf = q_ref[0, r0:r0 + BLK, :].astype(F32) * (GLA_KEY_DIM ** -0.5)
        kf = k_ref[0, r0:r0 + BLK, :].astype(F32)
        q_t = (qf * jnp.exp(bcum)).astype(BF16)
        k_t = (kf * jnp.exp(-bcum)).astype(BF16)
        k_end = (kf * jnp.exp(b_last - bcum)).astype(BF16)
        decay = jnp.exp(b_last)
        qt_s[r0:r0 + BLK, :] = q_t
        zero = jnp.zeros_like(q_t)
        for hh in range(2):
            att = jnp.where(chunk_causal, _dot_nt(jnp.where(head_lanes[hh], q_t, zero), k_t), 0.0).astype(BF16)
            oi_s[r0:r0 + BLK, hh * GLA_V_DIM:(hh + 1) * GLA_V_DIM] = _dot(
                att, v_ref[0, r0:r0 + BLK, hh * GLA_V_DIM:(hh + 1) * GLA_V_DIM])
        for i in range(per_blk):
            n = b * per_blk + i
            rows = slice(r0 + i * C, r0 + (i + 1) * C)
            inc = _dot_tn(v_ref[0, rows, :], k_end[i * C:(i + 1) * C, :])
            ds_s[n] = jnp.where(own_keys, inc, 0.0)
            dec_s[n:n + 1, :] = decay[i * C:i * C + 1, :]

    state = jnp.zeros((2 * GLA_V_DIM, LANES), F32)
    for n in range(S // C):
        rows = slice(n * C, (n + 1) * C)
        o = oi_s[rows, :] + _dot_nt(qt_s[rows, :], state.astype(BF16))
        state = state * dec_s[n:n + 1, :] + ds_s[n]
        for hh in range(2):
            cols = slice(hh * GLA_V_DIM, (hh + 1) * GLA_V_DIM)
            oh = o[:, cols]
            oh = oh * lax.rsqrt(jnp.mean(oh * oh, axis=-1, keepdims=True) + LN_EPS) * ng_ref[...]
            gate = go_ref[0, rows, cols].astype(F32)
            o_ref[0, rows, cols] = (oh * (gate * jax.nn.sigmoid(gate))).astype(o_ref.dtype)


def _gla(gq, gk, la, gv, go, norm_g):
    B, S, _ = gq.shape
    pairs = GLA_HEADS // 2
    narrow = pl.BlockSpec((1, S, LANES), lambda b, p: (b, 0, p))
    wide = pl.BlockSpec((1, S, 2 * GLA_V_DIM), lambda b, p: (b, 0, p))
    n_chunks = S // CHUNK
    return pl.pallas_call(
        _gla_kernel,
        grid=(B, pairs),
        in_specs=[narrow, narrow, narrow, wide, wide,
                  pl.BlockSpec((1, GLA_V_DIM), lambda b, p: (0, 0))],
        out_specs=wide,
        out_shape=jax.ShapeDtypeStruct((B, S, GLA_V), BF16),
        scratch_shapes=[pltpu.VMEM((S, LANES), BF16),
                        pltpu.VMEM((S, 2 * GLA_V_DIM), F32),
                        pltpu.VMEM((n_chunks, 2 * GLA_V_DIM, LANES), F32),
                        pltpu.VMEM((n_chunks, LANES), F32)],
        compiler_params=pltpu.CompilerParams(dimension_semantics=("arbitrary",) * 2,
                                             vmem_limit_bytes=VMEM_LIMIT),
        name="gla",
    )(gq, gk, la, gv, go, norm_g)


def _split3(a):
    hi = a.astype(BF16)
    lo = (a - hi.astype(F32)).astype(BF16)
    return hi, lo


def _mix_out_kernel(alpha, x_ref, da_ref, gl_ref, lng_ref, lnb_ref, wo_ref, g1_ref, b1_ref,
                    wr_hi_ref, wr_lo_ref, br_ref, lower_ref, h_ref, route_ref, cnt_ref):
    tm = x_ref.shape[0]
    i = pl.program_id(0)

    @pl.when(i == 0)
    def _():
        cnt_ref[...] = jnp.zeros_like(cnt_ref)

    logit_blocks = []
    for r0 in range(0, tm, ROW_BLOCK):
        rows = slice(r0, r0 + ROW_BLOCK)
        xn = _layer_norm(x_ref[rows, :], lng_ref[...], lnb_ref[...])
        mix = _dot(da_ref[rows, :], wo_ref[0:DA_V, :]) + _dot(gl_ref[rows, :], wo_ref[DA_V:, :])
        h = _layer_norm(alpha * xn + mix, g1_ref[...], b1_ref[...])
        h_ref[rows, :] = h
        h_hi, h_lo = _split3(h)
        logit_blocks.append(_dot(h_hi, wr_hi_ref[...]) + _dot(h_hi, wr_lo_ref[...]) + _dot(h_lo, wr_hi_ref[...])
                            + br_ref[...])
    logits = jnp.concatenate(logit_blocks, axis=0)
    lane = lax.broadcasted_iota(jnp.int32, (tm, LANES), 1)
    neg = -jnp.inf
    big = jnp.int32(LANES)

    def first_argmax(vals, valid):
        v = jnp.where(valid, vals, neg)
        mx = jnp.max(v, axis=-1, keepdims=True)
        idx = jnp.min(jnp.where(valid & (v == mx), lane, big), axis=-1, keepdims=True)
        return mx, idx

    is_group = lane < N_GROUPS
    g_max, g_top = first_argmax(logits, is_group)
    p_g = 1.0 / jnp.sum(jnp.where(is_group, jnp.exp(logits - g_max), 0.0), axis=-1, keepdims=True)

    e_lo = EXPERT_LANE0 + g_top * EXPERTS_PER_GROUP
    in_group = (lane >= e_lo) & (lane < e_lo + EXPERTS_PER_GROUP)
    v0, i0 = first_argmax(logits, in_group)
    v1, i1 = first_argmax(logits, in_group & (lane != i0))
    w1 = jnp.exp(v1 - v0)
    gate0 = p_g / (1.0 + w1)
    gate1 = p_g * w1 / (1.0 + w1)
    e0 = i0 - EXPERT_LANE0
    e1 = i1 - EXPERT_LANE0

    oh0 = jnp.where(lane == e0, 1.0, 0.0)
    oh1 = jnp.where(lane == e1, 1.0, 0.0)
    oh = oh0 + oh1
    before = _dot(lower_ref[...], oh.astype(BF16)) + cnt_ref[0:1, :]
    rank0 = jnp.sum(oh0 * before, axis=-1, keepdims=True)
    rank1 = jnp.sum(oh1 * before, axis=-1, keepdims=True)
    cnt_ref[...] = cnt_ref[...] + jnp.sum(oh, axis=0, keepdims=True)

    rec = jnp.zeros((tm, LANES), F32)
    for ln, val in ((R_E0, e0.astype(F32)), (R_E1, e1.astype(F32)), (R_G0, gate0), (R_G1, gate1),
                    (R_RANK0, rank0), (R_RANK1, rank1)):
        rec = jnp.where(lane == ln, val, rec)
    route_ref[...] = rec


def _mix_out(x2, da2, gl2, ln_g, ln_b, w_o, ln1_g, ln1_b, wr_hi, wr_lo, b_r, alpha, tm):
    T = x2.shape[0]
    row = lambda n: pl.BlockSpec((tm, n), lambda i: (i, 0))
    full = lambda a: pl.BlockSpec(a.shape, lambda i: (0,) * a.ndim)
    lower = jnp.tril(jnp.ones((tm, tm), BF16), -1)
    return pl.pallas_call(
        functools.partial(_mix_out_kernel, alpha),
        grid=(T // tm,),
        in_specs=[row(D_MODEL), row(DA_V), row(GLA_V), full(ln_g), full(ln_b), full(w_o),
                  full(ln1_g), full(ln1_b), full(wr_hi), full(wr_lo), full(b_r), full(lower)],
        out_specs=[row(D_MODEL), row(LANES), pl.BlockSpec((8, LANES), lambda i: (0, 0))],
        out_shape=[jax.ShapeDtypeStruct((T, D_MODEL), F32), jax.ShapeDtypeStruct((T, LANES), F32),
                   jax.ShapeDtypeStruct((8, LANES), F32)],
        compiler_params=pltpu.CompilerParams(dimension_semantics=("arbitrary",),
                                             vmem_limit_bytes=VMEM_LIMIT),
        name="mix_out",
    )(x2, da2, gl2, ln_g, ln_b, w_o, ln1_g, ln1_b, wr_hi, wr_lo, b_r, lower)


HIGH_HALF = 0xFFFF0000


def _pack_pairs(val):
    bits = lambda a: lax.bitcast_convert_type(a.astype(BF16).astype(F32), jnp.uint32)
    half = val.shape[1] // 2
    return (bits(val[:, :half]) >> 16) | (bits(val[:, half:]) & jnp.uint32(HIGH_HALF))


def _unpack_pairs(words):
    lo = lax.bitcast_convert_type(words << 16, F32)
    hi = lax.bitcast_convert_type(words & jnp.uint32(HIGH_HALF), F32)
    return jnp.concatenate([lo, hi], axis=1)


def _words_to_tiles(dst_ref, words):
    n = words.shape[0]
    for s in range(ROW_SUB):
        dst_ref[pl.ds(s, n, stride=ROW_SUB), :] = words[:, s * LANES:(s + 1) * LANES]


def _tiles_to_words(src_ref, r0, n):
    return jnp.concatenate([src_ref[pl.ds(r0 * ROW_SUB + s, n, stride=ROW_SUB), :] for s in range(ROW_SUB)],
                           axis=1)


def _rows_to_tiles(dst_ref, val):
    _words_to_tiles(dst_ref, _pack_pairs(val))


def _tiles_to_rows(src_ref, r0, n):
    return _unpack_pairs(_tiles_to_words(src_ref, r0, n))


def _row_tile(ref, r):
    return ref.at[pl.ds(pl.multiple_of(r * ROW_SUB, ROW_SUB), ROW_SUB), :]


def _dispatch_kernel(dest_ref, h_ref, xs_ref, stage, sems):
    tm = h_ref.shape[0]
    step = pl.program_id(0)
    half = step & 1
    base = step * (tm * TOP_K)
    n_iter = tm * TOP_K // DMA_UNROLL
    _rows_to_tiles(stage.at[half], h_ref[...])

    def row_copy(hf, t, slot):
        return pltpu.make_async_copy(_row_tile(stage.at[hf], t), _row_tile(xs_ref, slot), sems.at[hf])

    def start(i, c):
        for u in range(DMA_UNROLL):
            t = i * (DMA_UNROLL // TOP_K) + u // TOP_K
            row_copy(half, t, dest_ref[base + i * DMA_UNROLL + u]).start(priority=u % 2)
        return c

    lax.fori_loop(0, n_iter, start, 0)

    def drain(hf):
        def wait(i, c):
            for u in range(DMA_UNROLL):
                row_copy(hf, 0, 0).wait()
            return c
        lax.fori_loop(0, n_iter, wait, 0)

    @pl.when(step > 0)
    def _():
        drain(1 - half)

    @pl.when(step == pl.num_programs(0) - 1)
    def _():
        drain(half)
        n_pad = EXPERT_CHUNK * ROW_SUB
        stage[0, 0:n_pad, :] = jnp.zeros((n_pad, LANES), stage.dtype)
        pad = pltpu.make_async_copy(stage.at[0, 0:n_pad, :],
                                    xs_ref.at[pl.ds(pl.num_programs(0) * tm * TOP_K * ROW_SUB, n_pad), :], sems.at[0])
        pad.start()
        pad.wait()


def _dispatch(dest_flat, h, tm):
    T = h.shape[0]
    return pl.pallas_call(
        _dispatch_kernel,
        grid_spec=pltpu.PrefetchScalarGridSpec(
            num_scalar_prefetch=1,
            grid=(T // tm,),
            in_specs=[pl.BlockSpec((tm, D_MODEL), lambda i, d: (i, 0))],
            out_specs=pl.BlockSpec(memory_space=pl.ANY),
            scratch_shapes=[pltpu.VMEM((2, tm * ROW_SUB, LANES), jnp.uint32), pltpu.SemaphoreType.DMA((2,))]),
        out_shape=jax.ShapeDtypeStruct(((T * TOP_K + EXPERT_CHUNK) * ROW_SUB, LANES), jnp.uint32),
        compiler_params=pltpu.CompilerParams(dimension_semantics=("arbitrary",),
                                             vmem_limit_bytes=VMEM_LIMIT),
        name="dispatch",
    )(dest_flat, h)


def _experts_kernel(row0_ref, first_ref, end_ref, xs_ref, wg_ref, wu_ref, wd_ref, ys_ref,
                    wgu_b, wd_b, xbuf, ybuf, xsem, ysem, pend_ref, *, n_rows):
    e = pl.program_id(0)
    ch = EXPERT_CHUNK
    depth = EXPERT_XBUFS - 1
    g_lo = first_ref[e]
    g_hi = first_ref[e + 1]
    total = first_ref[N_EXPERTS]

    def slab(ref, row0):
        return ref.at[pl.ds(pl.multiple_of(row0 * ROW_SUB, ROW_SUB), ch * ROW_SUB), :]

    def x_copy(g):
        slot = g & (EXPERT_XBUFS - 1)
        return pltpu.make_async_copy(slab(xs_ref, row0_ref[g]), xbuf.at[slot], xsem.at[slot])

    def y_copy(row0, half):
        return pltpu.make_async_copy(ybuf.at[half], slab(ys_ref, row0), ysem.at[half])

    def drain_y(half):
        @pl.when(pend_ref[half] == 1)
        def _():
            y_copy(0, half).wait()
            pend_ref[half] = 0

    @pl.when(e == 0)
    def _():
        for b in range(EXPERT_YBUFS):
            pend_ref[b] = 0
        for b in range(EXPERT_YBUFS):
            ybuf[b] = jnp.zeros(ybuf.shape[1:], ybuf.dtype)
        y_copy(n_rows, 0).start()
        y_copy(n_rows, 0).wait()
        for d in range(depth):
            @pl.when(d < total)
            def _():
                x_copy(d).start(priority=ROW_STREAM_PRIORITY)

    @pl.when(g_hi > g_lo)
    def _():
        wgu_b[:, 0:D_EXPERT] = wg_ref[0].astype(BF16)
        wgu_b[:, D_EXPERT:] = wu_ref[0].astype(BF16)
        wd_b[...] = wd_ref[0].astype(BF16)

        def chunk(g, c):
            half = g & (EXPERT_YBUFS - 1)
            x_copy(g).wait()

            @pl.when(g + depth < total)
            def _():
                x_copy(g + depth).start(priority=ROW_STREAM_PRIORITY)

            drain_y(half)

            @pl.when(g == g_lo)
            def _():
                for b in range(EXPERT_YBUFS):
                    drain_y(b)

            def compute(m):
                xb = _tiles_to_rows(xbuf.at[g & (EXPERT_XBUFS - 1)], 0, m).astype(BF16)
                gu = _dot(xb, wgu_b[...])
                gate = gu[:, 0:D_EXPERT]
                mid = (gate * jax.nn.sigmoid(gate) * gu[:, D_EXPERT:]).astype(BF16)
                _words_to_tiles(ybuf.at[half], _pack_pairs(_dot(mid, wd_b[...])))

            owned = jnp.minimum(end_ref[e] - row0_ref[g], ch)
            grains = lax.shift_right_logical(owned + (EXPERT_GRAIN - 1), EXPERT_GRAIN.bit_length() - 1)
            for n in range(1, ch // EXPERT_GRAIN + 1):
                @pl.when(grains == n)
                def _():
                    compute(n * EXPERT_GRAIN)

            y_copy(row0_ref[g], half).start(priority=ROW_STREAM_PRIORITY)
            pend_ref[half] = 1
            return c

        lax.fori_loop(g_lo, g_hi, chunk, 0)

    @pl.when(e == pl.num_programs(0) - 1)
    def _():
        for b in range(EXPERT_YBUFS):
            drain_y(b)


def _chunk_metadata(seg_start, counts, n_rows):
    ch = EXPERT_CHUNK
    max_chunks = n_rows // ch + N_EXPERTS
    n_ch = (counts + (ch - 1)) // ch
    first = jnp.concatenate([jnp.zeros((1,), jnp.int32), jnp.cumsum(n_ch).astype(jnp.int32)])
    g = jnp.arange(max_chunks, dtype=jnp.int32)
    owner = jnp.minimum(jnp.sum((first[None, 1:] <= g[:, None]).astype(jnp.int32), axis=1), N_EXPERTS - 1)
    onehot = owner[:, None] == jnp.arange(N_EXPERTS, dtype=jnp.int32)
    pick = lambda tab: jnp.sum(jnp.where(onehot, tab[None, :], 0), axis=1)
    row0 = pick(seg_start) + (g - pick(first[:-1])) * ch
    row0 = jnp.where(g < first[-1], row0, 0)
    return row0.astype(jnp.int32), first


def _experts(seg_start, counts, xs, w_gate, w_up, w_down):
    n_rows = xs.shape[0] // ROW_SUB - EXPERT_CHUNK
    row0, first = _chunk_metadata(seg_start, counts, n_rows)
    per_expert = lambda shape: pl.BlockSpec((1,) + shape, lambda e, r, f, n: (e, 0, 0))
    slab = (EXPERT_CHUNK * ROW_SUB, LANES)
    return pl.pallas_call(
        functools.partial(_experts_kernel, n_rows=n_rows),
        grid_spec=pltpu.PrefetchScalarGridSpec(
            num_scalar_prefetch=3,
            grid=(N_EXPERTS,),
            in_specs=[pl.BlockSpec(memory_space=pl.ANY),
                      per_expert((D_MODEL, D_EXPERT)), per_expert((D_MODEL, D_EXPERT)),
                      per_expert((D_EXPERT, D_MODEL))],
            out_specs=pl.BlockSpec(memory_space=pl.ANY),
            scratch_shapes=[pltpu.VMEM((D_MODEL, 2 * D_EXPERT), BF16), pltpu.VMEM((D_EXPERT, D_MODEL), BF16),
                            pltpu.VMEM((EXPERT_XBUFS,) + slab, jnp.uint32), pltpu.VMEM((EXPERT_YBUFS,) + slab, jnp.uint32),
                            pltpu.SemaphoreType.DMA((EXPERT_XBUFS,)), pltpu.SemaphoreType.DMA((EXPERT_YBUFS,)),
                            pltpu.SMEM((EXPERT_YBUFS,), jnp.int32)]),
        out_shape=jax.ShapeDtypeStruct(xs.shape, jnp.uint32),
        compiler_params=pltpu.CompilerParams(dimension_semantics=("arbitrary",),
                                             vmem_limit_bytes=VMEM_LIMIT),
        name="experts",
    )(row0, first, (seg_start + counts).astype(jnp.int32), xs, w_gate, w_up, w_down)


def _combine_kernel(alpha, dest_ref, h_ref, route_ref, g_ref, b_ref, y_ref, o_ref, buf, sems):
    tm = h_ref.shape[0]
    step = pl.program_id(0)
    half = step & 1
    n_iter = tm * TOP_K // DMA_UNROLL

    def row_copy(hf, src, slot):
        return pltpu.make_async_copy(_row_tile(y_ref, src), _row_tile(buf.at[hf], slot), sems.at[hf])

    def gather(st, hf):
        base = st * (tm * TOP_K)

        def start(i, c):
            for u in range(DMA_UNROLL):
                slot = (u % TOP_K) * tm + i * (DMA_UNROLL // TOP_K) + u // TOP_K
                row_copy(hf, dest_ref[base + i * DMA_UNROLL + u], slot).start(priority=u % 2)
            return c

        lax.fori_loop(0, n_iter, start, 0)

    @pl.when(step == 0)
    def _():
        gather(0, 0)

    @pl.when(step + 1 < pl.num_programs(0))
    def _():
        gather(step + 1, 1 - half)

    def wait(i, c):
        for u in range(DMA_UNROLL):
            row_copy(half, 0, 0).wait()
        return c

    lax.fori_loop(0, n_iter, wait, 0)

    rec = route_ref[...]
    cur = buf.at[half]
    ffn = (rec[:, R_G0:R_G0 + 1] * _tiles_to_rows(cur, 0, tm)
           + rec[:, R_G1:R_G1 + 1] * _tiles_to_rows(cur, tm, tm))
    o_ref[...] = _layer_norm(alpha * h_ref[...] + ffn, g_ref[...], b_ref[...])


def _combine(dest_flat, h, route, ln2_g, ln2_b, y_sorted, alpha, tm):
    T = h.shape[0]
    return pl.pallas_call(
        functools.partial(_combine_kernel, alpha),
        grid_spec=pltpu.PrefetchScalarGridSpec(
            num_scalar_prefetch=1,
            grid=(T // tm,),
            in_specs=[pl.BlockSpec((tm, D_MODEL), lambda i, d: (i, 0)),
                      pl.BlockSpec((tm, LANES), lambda i, d: (i, 0)),
                      pl.BlockSpec((1, D_MODEL), lambda i, d: (0, 0)),
                      pl.BlockSpec((1, D_MODEL), lambda i, d: (0, 0)),
                      pl.BlockSpec(memory_space=pl.ANY)],
            out_specs=pl.BlockSpec((tm, D_MODEL), lambda i, d: (i, 0)),
            scratch_shapes=[pltpu.VMEM((2, TOP_K * tm * ROW_SUB, LANES), jnp.uint32), pltpu.SemaphoreType.DMA((2,))]),
        out_shape=jax.ShapeDtypeStruct((T, D_MODEL), F32),
        compiler_params=pltpu.CompilerParams(dimension_semantics=("arbitrary",),
                                             vmem_limit_bytes=VMEM_LIMIT),
        name="combine",
    )(dest_flat, h, route, ln2_g, ln2_b, y_sorted)


def kernel(x, positions, ln_in_g, ln_in_b, w_in, lam_q1, lam_k1, lam_q2, lam_k2, da_subln_g, gla_w_gate2, gla_b_gate2, gla_norm_g, w_o, ln1_g, ln1_b, router_w_group, router_b_group, router_w_expert, router_b_expert, w_gate, w_up, w_down, ln2_g, ln2_b):
    B, S, D = x.shape
    T = B * S
    depth = w_in.shape[0]
    assert depth == 1, "only a single layer is supported"
    alpha = (2 * depth) ** 0.25
    row2 = lambda a: a.reshape(1, -1)

    inv_freq = ROPE_THETA ** (-jnp.arange(0, DA_HEAD_DIM, 2, dtype=F32) / DA_HEAD_DIM)
    inv_freq = jnp.tile(jnp.repeat(inv_freq, 2), LANES // DA_HEAD_DIM).reshape(1, LANES)
    pos2 = positions.reshape(T, 1)

    cur = x.reshape(T, D)
    cur_g, cur_b = row2(ln_in_g), row2(ln_in_b)
    for l in range(depth):
        w = w_in[l]
        w_main = w[:, :D_MAIN].astype(BF16)
        w_glow = jnp.pad(w[:, D_MAIN:], ((0, 0), (0, LANES - GLA_GATE_RANK))).astype(BF16)
        w_gate2 = jnp.pad(gla_w_gate2[l], ((0, LANES - GLA_GATE_RANK), (0, 0))).astype(BF16)

        q, k, v, gq, gk, gv, go, la = _in_proj(cur, pos2, cur_g, cur_b, inv_freq, w_main, w_glow,
                                               w_gate2, row2(gla_b_gate2[l]), tm=512)
        lam_init = 0.8 - 0.6 * math.exp(-0.3 * l)
        sh = lambda a: a.reshape(B, S, a.shape[-1])
        da = _diff_attn(sh(q), sh(k), sh(v), row2(lam_q1[l]), row2(lam_k1[l]), row2(lam_q2[l]),
                        row2(lam_k2[l]), row2(da_subln_g[l]), lam_init)
        gl = _gla(sh(gq), sh(gk), sh(la), sh(gv), sh(go), row2(gla_norm_g[l]))

        w_r = jnp.zeros((D, LANES), F32)
        w_r = w_r.at[:, :N_GROUPS].set(router_w_group[l])
        w_r = w_r.at[:, EXPERT_LANE0:EXPERT_LANE0 + N_EXPERTS].set(router_w_expert[l])
        b_r = jnp.zeros((1, LANES), F32)
        b_r = b_r.at[0, :N_GROUPS].set(router_b_group[l])
        b_r = b_r.at[0, EXPERT_LANE0:EXPERT_LANE0 + N_EXPERTS].set(router_b_expert[l])
        wr_hi = w_r.astype(BF16)
        wr_lo = (w_r - wr_hi.astype(F32)).astype(BF16)

        h, route, cnt = _mix_out(cur, da.reshape(T, DA_V), gl.reshape(T, GLA_V), cur_g, cur_b,
                                 w_o[l].astype(BF16), row2(ln1_g[l]), row2(ln1_b[l]), wr_hi, wr_lo, b_r,
                                 alpha, tm=512)

        counts = cnt[0, :N_EXPERTS].astype(jnp.int32)
        seg_start = jnp.cumsum(counts) - counts
        eid = route[:, R_E0:R_E1 + 1].astype(jnp.int32)
        rank = route[:, R_RANK0:R_RANK1 + 1].astype(jnp.int32)
        onehot = eid[..., None] == jnp.arange(N_EXPERTS, dtype=jnp.int32)
        dest = jnp.sum(jnp.where(onehot, seg_start, 0), axis=-1) + rank
        dest_flat = dest.reshape(T * TOP_K)

        xs = _dispatch(dest_flat, h, tm=512)
        ys = _experts(seg_start, counts, xs, w_gate[l], w_up[l], w_down[l])
        cur = _combine(dest_flat, h, route, row2(ln2_g[l]), row2(ln2_b[l]), ys, alpha, tm=512)
    return cur.reshape(B, S, D)
```

```python
import functools
import math

import jax
import jax.numpy as jnp
from jax import lax
from jax.experimental import pallas as pl
from jax.experimental.pallas import tpu as pltpu

F32 = jnp.float32
BF16 = jnp.bfloat16

D_MODEL = 1024
CHUNK = 64
ROPE_THETA = 10000.0
LN_EPS = 1e-5
LOG2_E = math.log2(math.e)

DA_HEADS = 4
DA_V_DIM = D_MODEL // (2 * DA_HEADS)
DA_HEAD_DIM = DA_V_DIM // 2
GLA_HEADS = 4
GLA_V_DIM = D_MODEL // (2 * GLA_HEADS)
GLA_KEY_DIM = GLA_V_DIM // 2
GLA_GATE_RANK = 16
GLA_GATE_NORMALIZER = 16.0

DA_Q = DA_HEADS * 2 * DA_HEAD_DIM
DA_K = DA_Q
DA_V = DA_HEADS * DA_V_DIM
GLA_Q = GLA_HEADS * GLA_KEY_DIM
GLA_K = GLA_Q
GLA_V = GLA_HEADS * GLA_V_DIM
GLA_OG = GLA_V
D_MAIN = DA_Q + DA_K + DA_V + GLA_Q + GLA_K + GLA_V + GLA_OG

N_GROUPS = 4
EXPERTS_PER_GROUP = 8
N_EXPERTS = N_GROUPS * EXPERTS_PER_GROUP
TOP_K = 2
D_EXPERT = D_MODEL // 2

LANES = 128
ROW_SUB = D_MODEL // (2 * LANES)
ROW_BLOCK = 256
EXPERT_CHUNK = 512
EXPERT_GRAIN = 128
EXPERT_XBUFS = 8
EXPERT_YBUFS = 4
ROW_STREAM_PRIORITY = 1
GLA_BLOCK = 256
ATTN_BLOCK = 256
ATTN_PV_KEYS = 512
DMA_UNROLL = 8
VMEM_LIMIT = 48 * 1024 * 1024

R_E0, R_E1, R_G0, R_G1, R_RANK0, R_RANK1 = 0, 1, 2, 3, 4, 5
EXPERT_LANE0 = 32


def _layer_norm(x, g, b):
    mu = jnp.mean(x, axis=-1, keepdims=True)
    xc = x - mu
    var = jnp.mean(xc * xc, axis=-1, keepdims=True)
    return xc * lax.rsqrt(var + LN_EPS) * g + b


def _dot(a, b):
    return jnp.dot(a, b, preferred_element_type=F32)


def _dot_nt(a, b):
    return lax.dot_general(a, b, (((1,), (1,)), ((), ())), preferred_element_type=F32)


def _dot_tn(a, b):
    return lax.dot_general(a, b, (((0,), (0,)), ((), ())), preferred_element_type=F32)


def _in_proj_kernel(x_ref, pos_ref, g_ref, b_ref, invf_ref, w_ref, wgl_ref, wg2_ref, bg2_ref,
                    q_ref, k_ref, v_ref, gq_ref, gk_ref, gv_ref, go_ref, la_ref):
    tm = x_ref.shape[0]
    lane = lax.broadcasted_iota(jnp.int32, (ROW_BLOCK, LANES), 1)
    first = (lane & 1) == 0

    for r0 in range(0, tm, ROW_BLOCK):
        rows = slice(r0, r0 + ROW_BLOCK)
        xn = _layer_norm(x_ref[rows, :], g_ref[...], b_ref[...])
        xb = xn.astype(BF16)
        proj = _dot(xb, w_ref[...])

        ang = pos_ref[rows, :].astype(F32) * invf_ref[...]
        c = jnp.cos(ang)
        s = jnp.sin(ang)
        s_lo = jnp.where(first, -s, 0.0)
        s_hi = jnp.where(first, 0.0, s)

        def rope(t):
            out = []
            for j in range(t.shape[1] // LANES):
                tj = t[:, j * LANES:(j + 1) * LANES]
                up = pltpu.roll(tj, LANES - 1, 1)
                dn = pltpu.roll(tj, 1, 1)
                out.append(tj * c + up * s_lo + dn * s_hi)
            return jnp.concatenate(out, axis=1)

        o = 0
        q = rope(proj[:, o:o + DA_Q]) * (DA_HEAD_DIM ** -0.5 * LOG2_E)
        o += DA_Q
        k = rope(proj[:, o:o + DA_K])
        o += DA_K
        q_ref[rows, :] = q.astype(BF16)
        k_ref[rows, :] = k.astype(BF16)
        v_ref[rows, :] = proj[:, o:o + DA_V].astype(BF16)
        o += DA_V
        gq_ref[rows, :] = proj[:, o:o + GLA_Q].astype(BF16)
        o += GLA_Q
        gk_ref[rows, :] = proj[:, o:o + GLA_K].astype(BF16)
        o += GLA_K
        gv_ref[rows, :] = proj[:, o:o + GLA_V].astype(BF16)
        o += GLA_V
        go_ref[rows, :] = proj[:, o:o + GLA_OG].astype(BF16)

        g_low = _dot(xb, wgl_ref[...])
        z = _dot(g_low.astype(BF16), wg2_ref[...]) + bg2_ref[...]
        log_sig = jnp.minimum(z, 0.0) - jnp.log1p(jnp.exp(-jnp.abs(z)))
        la_ref[rows, :] = log_sig / GLA_GATE_NORMALIZER


def _in_proj(x2, pos2, ln_g, ln_b, inv_freq, w_main, w_glow, w_gate2, b_gate2, tm):
    T = x2.shape[0]
    row = lambda n: pl.BlockSpec((tm, n), lambda i: (i, 0))
    full = lambda a: pl.BlockSpec(a.shape, lambda i: (0,) * a.ndim)
    out_shape = [jax.ShapeDtypeStruct((T, n), dt) for n, dt in (
        (DA_Q, BF16), (DA_K, BF16), (DA_V, BF16), (GLA_Q, BF16), (GLA_K, BF16),
        (GLA_V, BF16), (GLA_OG, BF16), (GLA_K, F32))]
    return pl.pallas_call(
        _in_proj_kernel,
        grid=(T // tm,),
        in_specs=[row(D_MODEL), row(1), full(ln_g), full(ln_b), full(inv_freq), full(w_main),
                  full(w_glow), full(w_gate2), full(b_gate2)],
        out_specs=[row(s.shape[1]) for s in out_shape],
        out_shape=out_shape,
        compiler_params=pltpu.CompilerParams(dimension_semantics=("arbitrary",),
                                             vmem_limit_bytes=VMEM_LIMIT),
        name="in_proj",
    )(x2, pos2, ln_g, ln_b, inv_freq, w_main, w_glow, w_gate2, b_gate2)


def _diff_attn_kernel(lam_init, lq1_ref, lk1_ref, lq2_ref, lk2_ref, g_ref, q_ref, k_ref, v_ref, o_ref,
                      s_scr, p_scr, v_ext):
    S = q_ref.shape[1]
    tq = ATTN_BLOCK
    lam = (jnp.exp(jnp.sum(lq1_ref[...] * lk1_ref[...], axis=-1, keepdims=True))
           - jnp.exp(jnp.sum(lq2_ref[...] * lk2_ref[...], axis=-1, keepdims=True)) + lam_init)
    lane = lax.broadcasted_iota(jnp.int32, (tq, LANES), 1)
    rq = lax.broadcasted_iota(jnp.int32, (2 * tq, tq), 0) % tq // CHUNK
    ck = lax.broadcasted_iota(jnp.int32, (2 * tq, tq), 1) // CHUNK
    diag_mask = ck <= rq

    n_blk = S // tq
    st = [dict() for _ in range(n_blk)]
    v_ext[:, 0:DA_V_DIM] = v_ref[0]
    ext_lane = lax.broadcasted_iota(jnp.int32, (S, DA_V_DIM), 1)
    v_ext[:, DA_V_DIM:] = jnp.where(ext_lane == 0, 1.0, 0.0).astype(BF16)

    def stage_a(qi):
        s_buf = s_scr.at[qi % 2]

        def begin():
            q = q_ref[0, qi * tq:(qi + 1) * tq, :]
            zero = jnp.zeros_like(q)
            st[qi]["qq"] = jnp.concatenate([jnp.where(lane < DA_HEAD_DIM, q, zero),
                                            jnp.where(lane >= DA_HEAD_DIM, q, zero)], axis=0)
            st[qi]["m_acc"] = None

        def tile(j):
            s = _dot_nt(st[qi]["qq"], k_ref[0, j * tq:(j + 1) * tq, :])
            if j == qi:
                s = jnp.where(diag_mask, s, -jnp.inf)
            s_buf[j] = s
            m_acc = st[qi]["m_acc"]
            for c0 in range(0, tq, LANES):
                sc = s[:, c0:c0 + LANES]
                m_acc = sc if m_acc is None else jnp.maximum(m_acc, sc)
            st[qi]["m_acc"] = m_acc

        def end():
            st[qi]["m"] = jnp.broadcast_to(jnp.max(st[qi]["m_acc"], axis=-1, keepdims=True), (2 * tq, LANES))

        return [begin] + [functools.partial(tile, j) for j in range(qi + 1)] + [end]

    def stage_b(qi):
        s_buf = s_scr.at[qi % 2]
        p_buf = p_scr.at[qi % 2]

        def cols(c0):
            sc = s_buf[c0 // tq, :, c0 % tq:c0 % tq + LANES]
            p_buf[c0 // LANES] = jnp.exp2((sc - st[qi]["m"]).astype(BF16))

        return [functools.partial(cols, c0) for c0 in range(0, (qi + 1) * tq, LANES)]

    def stage_c(qi):
        p_buf = p_scr.at[qi % 2]

        nk = (qi + 1) * tq
        st[qi]["a"] = None

        def part(k0):
            k1 = min(k0 + ATTN_PV_KEYS, nk)
            p = jnp.concatenate([p_buf[c] for c in range(k0 // LANES, k1 // LANES)], axis=1)
            a = _dot(p, v_ext[k0:k1, :])
            st[qi]["a"] = a if st[qi]["a"] is None else st[qi]["a"] + a

        def finish():
            a = st[qi]["a"][:, 0:DA_V_DIM] / st[qi]["a"][:, DA_V_DIM:DA_V_DIM + 1]
            o = a[0:tq] - lam * a[tq:2 * tq]
            o = o * lax.rsqrt(jnp.mean(o * o, axis=-1, keepdims=True) + LN_EPS) * g_ref[...]
            o_ref[0, qi * tq:(qi + 1) * tq, :] = (o * (1.0 - lam_init)).astype(o_ref.dtype)

        return [functools.partial(part, k0) for k0 in range(0, nk, ATTN_PV_KEYS)] + [finish]

    for t in range(n_blk + 2):
        stages = []
        if t < n_blk:
            stages.append(stage_a(t))
        if 0 <= t - 1 < n_blk:
            stages.append(stage_b(t - 1))
        if 0 <= t - 2 < n_blk:
            stages.append(stage_c(t - 2))
        merged = sorted(((i + 0.5) / len(ops), k, i, op) for k, ops in enumerate(stages) for i, op in enumerate(ops))
        for _, _, _, op in merged:
            op()


def _diff_attn(q, k, v, lam_q1, lam_k1, lam_q2, lam_k2, subln_g, lam_init):
    B, S, _ = q.shape
    vec = pl.BlockSpec((1, DA_HEAD_DIM), lambda b, h: (0, 0))
    seq = pl.BlockSpec((1, S, LANES), lambda b, h: (b, 0, h))
    return pl.pallas_call(
        functools.partial(_diff_attn_kernel, lam_init),
        grid=(B, DA_HEADS),
        in_specs=[vec, vec, vec, vec, pl.BlockSpec((1, DA_V_DIM), lambda b, h: (0, 0)), seq, seq, seq],
        out_specs=seq,
        out_shape=jax.ShapeDtypeStruct((B, S, DA_V), BF16),
        scratch_shapes=[pltpu.VMEM((2, S // ATTN_BLOCK, 2 * ATTN_BLOCK, ATTN_BLOCK), F32),
                        pltpu.VMEM((2, S // LANES, 2 * ATTN_BLOCK, LANES), BF16),
                        pltpu.VMEM((S, 2 * DA_V_DIM), BF16)],
        compiler_params=pltpu.CompilerParams(dimension_semantics=("arbitrary",) * 2,
                                             vmem_limit_bytes=VMEM_LIMIT),
        name="diff_attn",
    )(lam_q1, lam_k1, lam_q2, lam_k2, subln_g, q, k, v)


def _gla_kernel(q_ref, k_ref, la_ref, v_ref, go_ref, ng_ref, o_ref, qt_s, oi_s, ds_s, dec_s):
    S = q_ref.shape[1]
    C = CHUNK
    BLK = GLA_BLOCK
    per_blk = BLK // C
    r = lax.broadcasted_iota(jnp.int32, (BLK, BLK), 0)
    c = lax.broadcasted_iota(jnp.int32, (BLK, BLK), 1)
    chunk_causal = (r // C == c // C) & (c <= r)
    tri = jnp.where(chunk_causal, 1.0, 0.0).astype(BF16)
    lane = lax.broadcasted_iota(jnp.int32, (BLK, LANES), 1)
    head_lanes = (lane < GLA_KEY_DIM, lane >= GLA_KEY_DIM)
    st_row = lax.broadcasted_iota(jnp.int32, (2 * GLA_V_DIM, LANES), 0)
    st_lane = lax.broadcasted_iota(jnp.int32, (2 * GLA_V_DIM, LANES), 1)
    own_keys = (st_row < GLA_V_DIM) == (st_lane < GLA_KEY_DIM)

    for b in range(S // BLK):
        r0 = b * BLK
        g = la_ref[0, r0:r0 + BLK, :]
        g1 = g.astype(BF16)
        e1 = g - g1.astype(F32)
        g2 = e1.astype(BF16)
        g3 = (e1 - g2.astype(F32)).astype(BF16)
        bcum = _dot(tri, g1) + _dot(tri, g2) + _dot(tri, g3)
        b_last = jnp.concatenate(
            [jnp.broadcast_to(bcum[i * C + C - 1:i * C + C, :], (C, LANES)) for i in range(per_blk)], axis=0)
        qf = q_ref[0, r0:r0 + BLK, :].astype(F32) * (GLA_KEY_DIM ** -0.5)
        kf = k_ref[0, r0:r0 + BLK, :].astype(F32)
        q_t = (qf * jnp.exp(bcum)).astype(BF16)
        k_t = (kf * jnp.exp(-bcum)).astype(BF16)
        k_end = (kf * jnp.exp(b_last - bcum)).astype(BF16)
        decay = jnp.exp(b_last)
        qt_s[r0:r0 + BLK, :] = q_t
        zero = jnp.zeros_like(q_t)
        for hh in range(2):
            att = jnp.where(chunk_causal, _dot_nt(jnp.where(head_lanes[hh], q_t, zero), k_t), 0.0).astype(BF16)
            oi_s[r0:r0 + BLK, hh * GLA_V_DIM:(hh + 1) * GLA_V_DIM] = _dot(
                att, v_ref[0, r0:r0 + BLK, hh * GLA_V_DIM:(hh + 1) * GLA_V_DIM])
        for i in range(per_blk):
            n = b * per_blk + i
            rows = slice(r0 + i * C, r0 + (i + 1) * C)
            inc = _dot_tn(v_ref[0, rows, :], k_end[i * C:(i + 1) * C, :])
            ds_s[n] = jnp.where(own_keys, inc, 0.0)
            dec_s[n:n + 1, :] = decay[i * C:i * C + 1, :]

    state = jnp.zeros((2 * GLA_V_DIM, LANES), F32)
    for n in range(S // C):
        rows = slice(n * C, (n + 1) * C)
        o = oi_s[rows, :] + _dot_nt(qt_s[rows, :], state.astype(BF16))
        state = state * dec_s[n:n + 1, :] + ds_s[n]
        for hh in range(2):
            cols = slice(hh * GLA_V_DIM, (hh + 1) * GLA_V_DIM)
            oh = o[:, cols]
            oh = oh * lax.rsqrt(jnp.mean(oh * oh, axis=-1, keepdims=True) + LN_EPS) * ng_ref[...]
            gate = go_ref[0, rows, cols].astype(F32)
            o_ref[0, rows, cols] = (oh * (gate * jax.nn.sigmoid(gate))).astype(o_ref.dtype)


def _gla(gq, gk, la, gv, go, norm_g):
    B, S, _ = gq.shape
    pairs = GLA_HEADS // 2
    narrow = pl.BlockSpec((1, S, LANES), lambda b, p: (b, 0, p))
    wide = pl.BlockSpec((1, S, 2 * GLA_V_DIM), lambda b, p: (b, 0, p))
    n_chunks = S // CHUNK
    return pl.pallas_call(
        _gla_kernel,
        grid=(B, pairs),
        in_specs=[narrow, narrow, narrow, wide, wide,
                  pl.BlockSpec((1, GLA_V_DIM), lambda b, p: (0, 0))],
        out_specs=wide,
        out_shape=jax.ShapeDtypeStruct((B, S, GLA_V), BF16),
        scratch_shapes=[pltpu.VMEM((S, LANES), BF16),
                        pltpu.VMEM((S, 2 * GLA_V_DIM), F32),
                        pltpu.VMEM((n_chunks, 2 * GLA_V_DIM, LANES), F32),
                        pltpu.VMEM((n_chunks, LANES), F32)],
        compiler_params=pltpu.CompilerParams(dimension_semantics=("arbitrary",) * 2,
                                             vmem_limit_bytes=VMEM_LIMIT),
        name="gla",
    )(gq, gk, la, gv, go, norm_g)


def _split3(a):
    hi = a.astype(BF16)
    lo = (a - hi.astype(F32)).astype(BF16)
    return hi, lo


def _mix_out_kernel(alpha, x_ref, da_ref, gl_ref, lng_ref, lnb_ref, wo_ref, g1_ref, b1_ref,
                    wr_hi_ref, wr_lo_ref, br_ref, lower_ref, h_ref, route_ref, cnt_ref):
    tm = x_ref.shape[0]
    i = pl.program_id(0)

    @pl.when(i == 0)
    def _():
        cnt_ref[...] = jnp.zeros_like(cnt_ref)

    logit_blocks = []
    for r0 in range(0, tm, ROW_BLOCK):
        rows = slice(r0, r0 + ROW_BLOCK)
        xn = _layer_norm(x_ref[rows, :], lng_ref[...], lnb_ref[...])
        mix = _dot(da_ref[rows, :], wo_ref[0:DA_V, :]) + _dot(gl_ref[rows, :], wo_ref[DA_V:, :])
        h = _layer_norm(alpha * xn + mix, g1_ref[...], b1_ref[...])
        h_ref[rows, :] = h
        h_hi, h_lo = _split3(h)
        logit_blocks.append(_dot(h_hi, wr_hi_ref[...]) + _dot(h_hi, wr_lo_ref[...]) + _dot(h_lo, wr_hi_ref[...])
                            + br_ref[...])
    logits = jnp.concatenate(logit_blocks, axis=0)
    lane = lax.broadcasted_iota(jnp.int32, (tm, LANES), 1)
    neg = -jnp.inf
    big = jnp.int32(LANES)

    def first_argmax(vals, valid):
        v = jnp.where(valid, vals, neg)
        mx = jnp.max(v, axis=-1, keepdims=True)
        idx = jnp.min(jnp.where(valid & (v == mx), lane, big), axis=-1, keepdims=True)
        return mx, idx

    is_group = lane < N_GROUPS
    g_max, g_top = first_argmax(logits, is_group)
    p_g = 1.0 / jnp.sum(jnp.where(is_group, jnp.exp(logits - g_max), 0.0), axis=-1, keepdims=True)

    e_lo = EXPERT_LANE0 + g_top * EXPERTS_PER_GROUP
    in_group = (lane >= e_lo) & (lane < e_lo + EXPERTS_PER_GROUP)
    v0, i0 = first_argmax(logits, in_group)
    v1, i1 = first_argmax(logits, in_group & (lane != i0))
    w1 = jnp.exp(v1 - v0)
    gate0 = p_g / (1.0 + w1)
    gate1 = p_g * w1 / (1.0 + w1)
    e0 = i0 - EXPERT_LANE0
    e1 = i1 - EXPERT_LANE0

    oh0 = jnp.where(lane == e0, 1.0, 0.0)
    oh1 = jnp.where(lane == e1, 1.0, 0.0)
    oh = oh0 + oh1
    before = _dot(lower_ref[...], oh.astype(BF16)) + cnt_ref[0:1, :]
    rank0 = jnp.sum(oh0 * before, axis=-1, keepdims=True)
    rank1 = jnp.sum(oh1 * before, axis=-1, keepdims=True)
    cnt_ref[...] = cnt_ref[...] + jnp.sum(oh, axis=0, keepdims=True)

    rec = jnp.zeros((tm, LANES), F32)
    for ln, val in ((R_E0, e0.astype(F32)), (R_E1, e1.astype(F32)), (R_G0, gate0), (R_G1, gate1),
                    (R_RANK0, rank0), (R_RANK1, rank1)):
        rec = jnp.where(lane == ln, val, rec)
    route_ref[...] = rec


def _mix_out(x2, da2, gl2, ln_g, ln_b, w_o, ln1_g, ln1_b, wr_hi, wr_lo, b_r, alpha, tm):
    T = x2.shape[0]
    row = lambda n: pl.BlockSpec((tm, n), lambda i: (i, 0))
    full = lambda a: pl.BlockSpec(a.shape, lambda i: (0,) * a.ndim)
    lower = jnp.tril(jnp.ones((tm, tm), BF16), -1)
    return pl.pallas_call(
        functools.partial(_mix_out_kernel, alpha),
        grid=(T // tm,),
        in_specs=[row(D_MODEL), row(DA_V), row(GLA_V), full(ln_g), full(ln_b), full(w_o),
                  full(ln1_g), full(ln1_b), full(wr_hi), full(wr_lo), full(b_r), full(lower)],
        out_specs=[row(D_MODEL), row(LANES), pl.BlockSpec((8, LANES), lambda i: (0, 0))],
        out_shape=[jax.ShapeDtypeStruct((T, D_MODEL), F32), jax.ShapeDtypeStruct((T, LANES), F32),
                   jax.ShapeDtypeStruct((8, LANES), F32)],
        compiler_params=pltpu.CompilerParams(dimension_semantics=("arbitrary",),
                                             vmem_limit_bytes=VMEM_LIMIT),
        name="mix_out",
    )(x2, da2, gl2, ln_g, ln_b, w_o, ln1_g, ln1_b, wr_hi, wr_lo, b_r, lower)


HIGH_HALF = 0xFFFF0000


def _pack_pairs(val):
    bits = lambda a: lax.bitcast_convert_type(a.astype(BF16).astype(F32), jnp.uint32)
    half = val.shape[1] // 2
    return (bits(val[:, :half]) >> 16) | (bits(val[:, half:]) & jnp.uint32(HIGH_HALF))


def _unpack_pairs(words):
    lo = lax.bitcast_convert_type(words << 16, F32)
    hi = lax.bitcast_convert_type(words & jnp.uint32(HIGH_HALF), F32)
    return jnp.concatenate([lo, hi], axis=1)


def _words_to_tiles(dst_ref, words):
    n = words.shape[0]
    for s in range(ROW_SUB):
        dst_ref[pl.ds(s, n, stride=ROW_SUB), :] = words[:, s * LANES:(s + 1) * LANES]


def _tiles_to_words(src_ref, r0, n):
    return jnp.concatenate([src_ref[pl.ds(r0 * ROW_SUB + s, n, stride=ROW_SUB), :] for s in range(ROW_SUB)],
                           axis=1)


def _rows_to_tiles(dst_ref, val):
    _words_to_tiles(dst_ref, _pack_pairs(val))


def _tiles_to_rows(src_ref, r0, n):
    return _unpack_pairs(_tiles_to_words(src_ref, r0, n))


def _row_tile(ref, r):
    return ref.at[pl.ds(pl.multiple_of(r * ROW_SUB, ROW_SUB), ROW_SUB), :]


def _dispatch_kernel(dest_ref, h_ref, xs_ref, stage, sems):
    tm = h_ref.shape[0]
    step = pl.program_id(0)
    half = step & 1
    base = step * (tm * TOP_K)
    n_iter = tm * TOP_K // DMA_UNROLL
    _rows_to_tiles(stage.at[half], h_ref[...])

    def row_copy(hf, t, slot):
        return pltpu.make_async_copy(_row_tile(stage.at[hf], t), _row_tile(xs_ref, slot), sems.at[hf])

    def start(i, c):
        for u in range(DMA_UNROLL):
            t = i * (DMA_UNROLL // TOP_K) + u // TOP_K
            row_copy(half, t, dest_ref[base + i * DMA_UNROLL + u]).start(priority=u % 2)
        return c

    lax.fori_loop(0, n_iter, start, 0)

    def drain(hf):
        def wait(i, c):
            for u in range(DMA_UNROLL):
                row_copy(hf, 0, 0).wait()
            return c
        lax.fori_loop(0, n_iter, wait, 0)

    @pl.when(step > 0)
    def _():
        drain(1 - half)

    @pl.when(step == pl.num_programs(0) - 1)
    def _():
        drain(half)
        n_pad = EXPERT_CHUNK * ROW_SUB
        stage[0, 0:n_pad, :] = jnp.zeros((n_pad, LANES), stage.dtype)
        pad = pltpu.make_async_copy(stage.at[0, 0:n_pad, :],
                                    xs_ref.at[pl.ds(pl.num_programs(0) * tm * TOP_K * ROW_SUB, n_pad), :], sems.at[0])
        pad.start()
        pad.wait()


def _dispatch(dest_flat, h, tm):
    T = h.shape[0]
    return pl.pallas_call(
        _dispatch_kernel,
        grid_spec=pltpu.PrefetchScalarGridSpec(
            num_scalar_prefetch=1,
            grid=(T // tm,),
            in_specs=[pl.BlockSpec((tm, D_MODEL), lambda i, d: (i, 0))],
            out_specs=pl.BlockSpec(memory_space=pl.ANY),
            scratch_shapes=[pltpu.VMEM((2, tm * ROW_SUB, LANES), jnp.uint32), pltpu.SemaphoreType.DMA((2,))]),
        out_shape=jax.ShapeDtypeStruct(((T * TOP_K + EXPERT_CHUNK) * ROW_SUB, LANES), jnp.uint32),
        compiler_params=pltpu.CompilerParams(dimension_semantics=("arbitrary",),
                                             vmem_limit_bytes=VMEM_LIMIT),
        name="dispatch",
    )(dest_flat, h)


def _experts_kernel(row0_ref, first_ref, end_ref, xs_ref, wg_ref, wu_ref, wd_ref, ys_ref,
                    wgu_b, wd_b, xbuf, ybuf, xsem, ysem, pend_ref, *, n_rows):
    e = pl.program_id(0)
    ch = EXPERT_CHUNK
    depth = EXPERT_XBUFS - 1
    g_lo = first_ref[e]
    g_hi = first_ref[e + 1]
    total = first_ref[N_EXPERTS]

    def slab(ref, row0):
        return ref.at[pl.ds(pl.multiple_of(row0 * ROW_SUB, ROW_SUB), ch * ROW_SUB), :]

    def x_copy(g):
        slot = g & (EXPERT_XBUFS - 1)
        return pltpu.make_async_copy(slab(xs_ref, row0_ref[g]), xbuf.at[slot], xsem.at[slot])

    def y_copy(row0, half):
        return pltpu.make_async_copy(ybuf.at[half], slab(ys_ref, row0), ysem.at[half])

    def drain_y(half):
        @pl.when(pend_ref[half] == 1)
        def _():
            y_copy(0, half).wait()
            pend_ref[half] = 0

    @pl.when(e == 0)
    def _():
        for b in range(EXPERT_YBUFS):
            pend_ref[b] = 0
        for b in range(EXPERT_YBUFS):
            ybuf[b] = jnp.zeros(ybuf.shape[1:], ybuf.dtype)
        y_copy(n_rows, 0).start()
        y_copy(n_rows, 0).wait()
        for d in range(depth):
            @pl.when(d < total)
            def _():
                x_copy(d).start(priority=ROW_STREAM_PRIORITY)

    @pl.when(g_hi > g_lo)
    def _():
        wgu_b[:, 0:D_EXPERT] = wg_ref[0].astype(BF16)
        wgu_b[:, D_EXPERT:] = wu_ref[0].astype(BF16)
        wd_b[...] = wd_ref[0].astype(BF16)

        def chunk(g, c):
            half = g & (EXPERT_YBUFS - 1)
            x_copy(g).wait()

            @pl.when(g + depth < total)
            def _():
                x_copy(g + depth).start(priority=ROW_STREAM_PRIORITY)

            drain_y(half)

            @pl.when(g == g_lo)
            def _():
                for b in range(EXPERT_YBUFS):
                    drain_y(b)

            def compute(m):
                xb = _tiles_to_rows(xbuf.at[g & (EXPERT_XBUFS - 1)], 0, m).astype(BF16)
                gu = _dot(xb, wgu_b[...])
                gate = gu[:, 0:D_EXPERT]
                mid = (gate * jax.nn.sigmoid(gate) * gu[:, D_EXPERT:]).astype(BF16)
                _words_to_tiles(ybuf.at[half], _pack_pairs(_dot(mid, wd_b[...])))

            owned = jnp.minimum(end_ref[e] - row0_ref[g], ch)
            grains = lax.shift_right_logical(owned + (EXPERT_GRAIN - 1), EXPERT_GRAIN.bit_length() - 1)
            for n in range(1, ch // EXPERT_GRAIN + 1):
                @pl.when(grains == n)
                def _():
                    compute(n * EXPERT_GRAIN)

            y_copy(row0_ref[g], half).start(priority=ROW_STREAM_PRIORITY)
            pend_ref[half] = 1
            return c

        lax.fori_loop(g_lo, g_hi, chunk, 0)

    @pl.when(e == pl.num_programs(0) - 1)
    def _():
        for b in range(EXPERT_YBUFS):
            drain_y(b)


def _chunk_metadata(seg_start, counts, n_rows):
    ch = EXPERT_CHUNK
    max_chunks = n_rows // ch + N_EXPERTS
    n_ch = (counts + (ch - 1)) // ch
    first = jnp.concatenate([jnp.zeros((1,), jnp.int32), jnp.cumsum(n_ch).astype(jnp.int32)])
    g = jnp.arange(max_chunks, dtype=jnp.int32)
    owner = jnp.minimum(jnp.sum((first[None, 1:] <= g[:, None]).astype(jnp.int32), axis=1), N_EXPERTS - 1)
    onehot = owner[:, None] == jnp.arange(N_EXPERTS, dtype=jnp.int32)
    pick = lambda tab: jnp.sum(jnp.where(onehot, tab[None, :], 0), axis=1)
    row0 = pick(seg_start) + (g - pick(first[:-1])) * ch
    row0 = jnp.where(g < first[-1], row0, 0)
    return row0.astype(jnp.int32), first


def _experts(seg_start, counts, xs, w_gate, w_up, w_down):
    n_rows = xs.shape[0] // ROW_SUB - EXPERT_CHUNK
    row0, first = _chunk_metadata(seg_start, counts, n_rows)
    per_expert = lambda shape: pl.BlockSpec((1,) + shape, lambda e, r, f, n: (e, 0, 0))
    slab = (EXPERT_CHUNK * ROW_SUB, LANES)
    return pl.pallas_call(
        functools.partial(_experts_kernel, n_rows=n_rows),
        grid_spec=pltpu.PrefetchScalarGridSpec(
            num_scalar_prefetch=3,
            grid=(N_EXPERTS,),
            in_specs=[pl.BlockSpec(memory_space=pl.ANY),
                      per_expert((D_MODEL, D_EXPERT)), per_expert((D_MODEL, D_EXPERT)),
                      per_expert((D_EXPERT, D_MODEL))],
            out_specs=pl.BlockSpec(memory_space=pl.ANY),
            scratch_shapes=[pltpu.VMEM((D_MODEL, 2 * D_EXPERT), BF16), pltpu.VMEM((D_EXPERT, D_MODEL), BF16),
                            pltpu.VMEM((EXPERT_XBUFS,) + slab, jnp.uint32), pltpu.VMEM((EXPERT_YBUFS,) + slab, jnp.uint32),
                            pltpu.SemaphoreType.DMA((EXPERT_XBUFS,)), pltpu.SemaphoreType.DMA((EXPERT_YBUFS,)),
                            pltpu.SMEM((EXPERT_YBUFS,), jnp.int32)]),
        out_shape=jax.ShapeDtypeStruct(xs.shape, jnp.uint32),
        compiler_params=pltpu.CompilerParams(dimension_semantics=("arbitrary",),
                                             vmem_limit_bytes=VMEM_LIMIT),
        name="experts",
    )(row0, first, (seg_start + counts).astype(jnp.int32), xs, w_gate, w_up, w_down)


def _combine_kernel(alpha, dest_ref, h_ref, route_ref, g_ref, b_ref, y_ref, o_ref, buf, sems):
    tm = h_ref.shape[0]
    step = pl.program_id(0)
    half = step & 1
    n_iter = tm * TOP_K // DMA_UNROLL

    def row_copy(hf, src, slot):
        return pltpu.make_async_copy(_row_tile(y_ref, src), _row_tile(buf.at[hf], slot), sems.at[hf])

    def gather(st, hf):
        base = st * (tm * TOP_K)

        def start(i, c):
            for u in range(DMA_UNROLL):
                slot = (u % TOP_K) * tm + i * (DMA_UNROLL // TOP_K) + u // TOP_K
                row_copy(hf, dest_ref[base + i * DMA_UNROLL + u], slot).start(priority=u % 2)
            return c

        lax.fori_loop(0, n_iter, start, 0)

    @pl.when(step == 0)
    def _():
        gather(0, 0)

    @pl.when(step + 1 < pl.num_programs(0))
    def _():
        gather(step + 1, 1 - half)

    def wait(i, c):
        for u in range(DMA_UNROLL):
            row_copy(half, 0, 0).wait()
        return c

    lax.fori_loop(0, n_iter, wait, 0)

    rec = route_ref[...]
    cur = buf.at[half]
    ffn = (rec[:, R_G0:R_G0 + 1] * _tiles_to_rows(cur, 0, tm)
           + rec[:, R_G1:R_G1 + 1] * _tiles_to_rows(cur, tm, tm))
    o_ref[...] = _layer_norm(alpha * h_ref[...] + ffn, g_ref[...], b_ref[...])


def _combine(dest_flat, h, route, ln2_g, ln2_b, y_sorted, alpha, tm):
    T = h.shape[0]
    return pl.pallas_call(
        functools.partial(_combine_kernel, alpha),
        grid_spec=pltpu.PrefetchScalarGridSpec(
            num_scalar_prefetch=1,
            grid=(T // tm,),
            in_specs=[pl.BlockSpec((tm, D_MODEL), lambda i, d: (i, 0)),
                      pl.BlockSpec((tm, LANES), lambda i, d: (i, 0)),
                      pl.BlockSpec((1, D_MODEL), lambda i, d: (0, 0)),
                      pl.BlockSpec((1, D_MODEL), lambda i, d: (0, 0)),
                      pl.BlockSpec(memory_space=pl.ANY)],
            out_specs=pl.BlockSpec((tm, D_MODEL), lambda i, d: (i, 0)),
            scratch_shapes=[pltpu.VMEM((2, TOP_K * tm * ROW_SUB, LANES), jnp.uint32), pltpu.SemaphoreType.DMA((2,))]),
        out_shape=jax.ShapeDtypeStruct((T, D_MODEL), F32),
        compiler_params=pltpu.CompilerParams(dimension_semantics=("arbitrary",),
                                             vmem_limit_bytes=VMEM_LIMIT),
        name="combine",
    )(dest_flat, h, route, ln2_g, ln2_b, y_sorted)


def kernel(x, positions, ln_in_g, ln_in_b, w_in, lam_q1, lam_k1, lam_q2, lam_k2, da_subln_g, gla_w_gate2, gla_b_gate2, gla_norm_g, w_o, ln1_g, ln1_b, router_w_group, router_b_group, router_w_expert, router_b_expert, w_gate, w_up, w_down, ln2_g, ln2_b):
    B, S, D = x.shape
    T = B * S
    depth = w_in.shape[0]
    assert depth == 1, "only a single layer is supported"
    alpha = (2 * depth) ** 0.25
    row2 = lambda a: a.reshape(1, -1)

    inv_freq = ROPE_THETA ** (-jnp.arange(0, DA_HEAD_DIM, 2, dtype=F32) / DA_HEAD_DIM)
    inv_freq = jnp.tile(jnp.repeat(inv_freq, 2), LANES // DA_HEAD_DIM).reshape(1, LANES)
    pos2 = positions.reshape(T, 1)

    cur = x.reshape(T, D)
    cur_g, cur_b = row2(ln_in_g), row2(ln_in_b)
    for l in range(depth):
        w = w_in[l]
        w_main = w[:, :D_MAIN].astype(BF16)
        w_glow = jnp.pad(w[:, D_MAIN:], ((0, 0), (0, LANES - GLA_GATE_RANK))).astype(BF16)
        w_gate2 = jnp.pad(gla_w_gate2[l], ((0, LANES - GLA_GATE_RANK), (0, 0))).astype(BF16)

        q, k, v, gq, gk, gv, go, la = _in_proj(cur, pos2, cur_g, cur_b, inv_freq, w_main, w_glow,
                                               w_gate2, row2(gla_b_gate2[l]), tm=512)
        lam_init = 0.8 - 0.6 * math.exp(-0.3 * l)
        sh = lambda a: a.reshape(B, S, a.shape[-1])
        da = _diff_attn(sh(q), sh(k), sh(v), row2(lam_q1[l]), row2(lam_k1[l]), row2(lam_q2[l]),
                        row2(lam_k2[l]), row2(da_subln_g[l]), lam_init)
        gl = _gla(sh(gq), sh(gk), sh(la), sh(gv), sh(go), row2(gla_norm_g[l]))

        w_r = jnp.zeros((D, LANES), F32)
        w_r = w_r.at[:, :N_GROUPS].set(router_w_group[l])
        w_r = w_r.at[:, EXPERT_LANE0:EXPERT_LANE0 + N_EXPERTS].set(router_w_expert[l])
        b_r = jnp.zeros((1, LANES), F32)
        b_r = b_r.at[0, :N_GROUPS].set(router_b_group[l])
        b_r = b_r.at[0, EXPERT_LANE0:EXPERT_LANE0 + N_EXPERTS].set(router_b_expert[l])
        wr_hi = w_r.astype(BF16)
        wr_lo = (w_r - wr_hi.astype(F32)).astype(BF16)

        h, route, cnt = _mix_out(cur, da.reshape(T, DA_V), gl.reshape(T, GLA_V), cur_g, cur_b,
                                 w_o[l].astype(BF16), row2(ln1_g[l]), row2(ln1_b[l]), wr_hi, wr_lo, b_r,
                                 alpha, tm=512)

        counts = cnt[0, :N_EXPERTS].astype(jnp.int32)
        seg_start = jnp.cumsum(counts) - counts
        eid = route[:, R_E0:R_E1 + 1].astype(jnp.int32)
        rank = route[:, R_RANK0:R_RANK1 + 1].astype(jnp.int32)
        onehot = eid[..., None] == jnp.arange(N_EXPERTS, dtype=jnp.int32)
        dest = jnp.sum(jnp.where(onehot, seg_start, 0), axis=-1) + rank
        dest_flat = dest.reshape(T * TOP_K)

        xs = _dispatch(dest_flat, h, tm=512)
        ys = _experts(seg_start, counts, xs, w_gate[l], w_up[l], w_down[l])
        cur = _combine(dest_flat, h, route, row2(ln2_g[l]), row2(ln2_b[l]), ys, alpha, tm=512)
    return cur.reshape(B, S, D)
```

```python
import functools
import math

import jax
import jax.numpy as jnp
from jax import lax
from jax.experimental import pallas as pl
from jax.experimental.pallas import tpu as pltpu

F32 = jnp.float32
BF16 = jnp.bfloat16

D_MODEL = 1024
CHUNK = 64
ROPE_THETA = 10000.0
LN_EPS = 1e-5
LOG2_E = math.log2(math.e)

DA_HEADS = 4
DA_V_DIM = D_MODEL // (2 * DA_HEADS)
DA_HEAD_DIM = DA_V_DIM // 2
GLA_HEADS = 4
GLA_V_DIM = D_MODEL // (2 * GLA_HEADS)
GLA_KEY_DIM = GLA_V_DIM // 2
GLA_GATE_RANK = 16
GLA_GATE_NORMALIZER = 16.0

DA_Q = DA_HEADS * 2 * DA_HEAD_DIM
DA_K = DA_Q
DA_V = DA_HEADS * DA_V_DIM
GLA_Q = GLA_HEADS * GLA_KEY_DIM
GLA_K = GLA_Q
GLA_V = GLA_HEADS * GLA_V_DIM
GLA_OG = GLA_V
D_MAIN = DA_Q + DA_K + DA_V + GLA_Q + GLA_K + GLA_V + GLA_OG

N_GROUPS = 4
EXPERTS_PER_GROUP = 8
N_EXPERTS = N_GROUPS * EXPERTS_PER_GROUP
TOP_K = 2
D_EXPERT = D_MODEL // 2

LANES = 128
ROW_SUB = D_MODEL // (2 * LANES)
ROW_BLOCK = 256
EXPERT_CHUNK = 512
EXPERT_GRAIN = 128
EXPERT_XBUFS = 8
EXPERT_YBUFS = 4
ROW_STREAM_PRIORITY = 1
GLA_BLOCK = 256
ATTN_BLOCK = 256
ATTN_PV_KEYS = 512
DMA_UNROLL = 8
VMEM_LIMIT = 48 * 1024 * 1024

R_E0, R_E1, R_G0, R_G1, R_RANK0, R_RANK1 = 0, 1, 2, 3, 4, 5
EXPERT_LANE0 = 32


def _layer_norm(x, g, b):
    mu = jnp.mean(x, axis=-1, keepdims=True)
    xc = x - mu
    var = jnp.mean(xc * xc, axis=-1, keepdims=True)
    return xc * lax.rsqrt(var + LN_EPS) * g + b


def _dot(a, b):
    return jnp.dot(a, b, preferred_element_type=F32)


def _dot_nt(a, b):
    return lax.dot_general(a, b, (((1,), (1,)), ((), ())), preferred_element_type=F32)


def _dot_tn(a, b):
    return lax.dot_general(a, b, (((0,), (0,)), ((), ())), preferred_element_type=F32)


def _in_proj_kernel(x_ref, pos_ref, g_ref, b_ref, invf_ref, w_ref, wgl_ref, wg2_ref, bg2_ref,
                    q_ref, k_ref, v_ref, gq_ref, gk_ref, gv_ref, go_ref, la_ref):
    tm = x_ref.shape[0]
    lane = lax.broadcasted_iota(jnp.int32, (ROW_BLOCK, LANES), 1)
    first = (lane & 1) == 0

    for r0 in range(0, tm, ROW_BLOCK):
        rows = slice(r0, r0 + ROW_BLOCK)
        xn = _layer_norm(x_ref[rows, :], g_ref[...], b_ref[...])
        xb = xn.astype(BF16)
        proj = _dot(xb, w_ref[...])

        ang = pos_ref[rows, :].astype(F32) * invf_ref[...]
        c = jnp.cos(ang)
        s = jnp.sin(ang)
        s_lo = jnp.where(first, -s, 0.0)
        s_hi = jnp.where(first, 0.0, s)

        def rope(t):
            out = []
            for j in range(t.shape[1] // LANES):
                tj = t[:, j * LANES:(j + 1) * LANES]
                up = pltpu.roll(tj, LANES - 1, 1)
                dn = pltpu.roll(tj, 1, 1)
                out.append(tj * c + up * s_lo + dn * s_hi)
            return jnp.concatenate(out, axis=1)

        o = 0
        q = rope(proj[:, o:o + DA_Q]) * (DA_HEAD_DIM ** -0.5 * LOG2_E)
        o += DA_Q
        k = rope(proj[:, o:o + DA_K])
        o += DA_K
        q_ref[rows, :] = q.astype(BF16)
        k_ref[rows, :] = k.astype(BF16)
        v_ref[rows, :] = proj[:, o:o + DA_V].astype(BF16)
        o += DA_V
        gq_ref[rows, :] = proj[:, o:o + GLA_Q].astype(BF16)
        o += GLA_Q
        gk_ref[rows, :] = proj[:, o:o + GLA_K].astype(BF16)
        o += GLA_K
        gv_ref[rows, :] = proj[:, o:o + GLA_V].astype(BF16)
        o += GLA_V
        go_ref[rows, :] = proj[:, o:o + GLA_OG].astype(BF16)

        g_low = _dot(xb, wgl_ref[...])
        z = _dot(g_low.astype(BF16), wg2_ref[...]) + bg2_ref[...]
        log_sig = jnp.minimum(z, 0.0) - jnp.log1p(jnp.exp(-jnp.abs(z)))
        la_ref[rows, :] = log_sig / GLA_GATE_NORMALIZER


def _in_proj(x2, pos2, ln_g, ln_b, inv_freq, w_main, w_glow, w_gate2, b_gate2, tm):
    T = x2.shape[0]
    row = lambda n: pl.BlockSpec((tm, n), lambda i: (i, 0))
    full = lambda a: pl.BlockSpec(a.shape, lambda i: (0,) * a.ndim)
    out_shape = [jax.ShapeDtypeStruct((T, n), dt) for n, dt in (
        (DA_Q, BF16), (DA_K, BF16), (DA_V, BF16), (GLA_Q, BF16), (GLA_K, BF16),
        (GLA_V, BF16), (GLA_OG, BF16), (GLA_K, F32))]
    return pl.pallas_call(
        _in_proj_kernel,
        grid=(T // tm,),
        in_specs=[row(D_MODEL), row(1), full(ln_g), full(ln_b), full(inv_freq), full(w_main),
                  full(w_glow), full(w_gate2), full(b_gate2)],
        out_specs=[row(s.shape[1]) for s in out_shape],
        out_shape=out_shape,
        compiler_params=pltpu.CompilerParams(dimension_semantics=("arbitrary",),
                                             vmem_limit_bytes=VMEM_LIMIT),
        name="in_proj",
    )(x2, pos2, ln_g, ln_b, inv_freq, w_main, w_glow, w_gate2, b_gate2)


def _diff_attn_kernel(lam_init, lq1_ref, lk1_ref, lq2_ref, lk2_ref, g_ref, q_ref, k_ref, v_ref, o_ref,
                      s_scr, p_scr, v_ext):
    S = q_ref.shape[1]
    tq = ATTN_BLOCK
    lam = (jnp.exp(jnp.sum(lq1_ref[...] * lk1_ref[...], axis=-1, keepdims=True))
           - jnp.exp(jnp.sum(lq2_ref[...] * lk2_ref[...], axis=-1, keepdims=True)) + lam_init)
    lane = lax.broadcasted_iota(jnp.int32, (tq, LANES), 1)
    rq = lax.broadcasted_iota(jnp.int32, (2 * tq, tq), 0) % tq // CHUNK
    ck = lax.broadcasted_iota(jnp.int32, (2 * tq, tq), 1) // CHUNK
    diag_mask = ck <= rq

    n_blk = S // tq
    st = [dict() for _ in range(n_blk)]
    v_ext[:, 0:DA_V_DIM] = v_ref[0]
    ext_lane = lax.broadcasted_iota(jnp.int32, (S, DA_V_DIM), 1)
    v_ext[:, DA_V_DIM:] = jnp.where(ext_lane == 0, 1.0, 0.0).astype(BF16)

    def stage_a(qi):
        s_buf = s_scr.at[qi % 2]

        def begin():
            q = q_ref[0, qi * tq:(qi + 1) * tq, :]
            zero = jnp.zeros_like(q)
            st[qi]["qq"] = jnp.concatenate([jnp.where(lane < DA_HEAD_DIM, q, zero),
                                            jnp.where(lane >= DA_HEAD_DIM, q, zero)], axis=0)
            st[qi]["m_acc"] = None

        def tile(j):
            s = _dot_nt(st[qi]["qq"], k_ref[0, j * tq:(j + 1) * tq, :])
            if j == qi:
                s = jnp.where(diag_mask, s, -jnp.inf)
            s_buf[j] = s
            m_acc = st[qi]["m_acc"]
            for c0 in range(0, tq, LANES):
                sc = s[:, c0:c0 + LANES]
                m_acc = sc if m_acc is None else jnp.maximum(m_acc, sc)
            st[qi]["m_acc"] = m_acc

        def end():
            st[qi]["m"] = jnp.broadcast_to(jnp.max(st[qi]["m_acc"], axis=-1, keepdims=True), (2 * tq, LANES))

        return [begin] + [functools.partial(tile, j) for j in range(qi + 1)] + [end]

    def stage_b(qi):
        s_buf = s_scr.at[qi % 2]
        p_buf = p_scr.at[qi % 2]

        def cols(c0):
            sc = s_buf[c0 // tq, :, c0 % tq:c0 % tq + LANES]
            p_buf[c0 // LANES] = jnp.exp2((sc - st[qi]["m"]).astype(BF16))

        return [functools.partial(cols, c0) for c0 in range(0, (qi + 1) * tq, LANES)]

    def stage_c(qi):
        p_buf = p_scr.at[qi % 2]

        nk = (qi + 1) * tq
        st[qi]["a"] = None

        def part(k0):
            k1 = min(k0 + ATTN_PV_KEYS, nk)
            p = jnp.concatenate([p_buf[c] for c in range(k0 // LANES, k1 // LANES)], axis=1)
            a = _dot(p, v_ext[k0:k1, :])
            st[qi]["a"] = a if st[qi]["a"] is None else st[qi]["a"] + a

        def finish():
            a = st[qi]["a"][:, 0:DA_V_DIM] / st[qi]["a"][:, DA_V_DIM:DA_V_DIM + 1]
            o = a[0:tq] - lam * a[tq:2 * tq]
            o = o * lax.rsqrt(jnp.mean(o * o, axis=-1, keepdims=True) + LN_EPS) * g_ref[...]
            o_ref[0, qi * tq:(qi + 1) * tq, :] = (o * (1.0 - lam_init)).astype(o_ref.dtype)

        return [functools.partial(part, k0) for k0 in range(0, nk, ATTN_PV_KEYS)] + [finish]

    for t in range(n_blk + 2):
        stages = []
        if t < n_blk:
            stages.append(stage_a(t))
        if 0 <= t - 1 < n_blk:
            stages.append(stage_b(t - 1))
        if 0 <= t - 2 < n_blk:
            stages.append(stage_c(t - 2))
        merged = sorted(((i + 0.5) / len(ops), k, i, op) for k, ops in enumerate(stages) for i, op in enumerate(ops))
        for _, _, _, op in merged:
            op()


def _diff_attn(q, k, v, lam_q1, lam_k1, lam_q2, lam_k2, subln_g, lam_init):
    B, S, _ = q.shape
    vec = pl.BlockSpec((1, DA_HEAD_DIM), lambda b, h: (0, 0))
    seq = pl.BlockSpec((1, S, LANES), lambda b, h: (b, 0, h))
    return pl.pallas_call(
        functools.partial(_diff_attn_kernel, lam_init),
        grid=(B, DA_HEADS),
        in_specs=[vec, vec, vec, vec, pl.BlockSpec((1, DA_V_DIM), lambda b, h: (0, 0)), seq, seq, seq],
        out_specs=seq,
        out_shape=jax.ShapeDtypeStruct((B, S, DA_V), BF16),
        scratch_shapes=[pltpu.VMEM((2, S // ATTN_BLOCK, 2 * ATTN_BLOCK, ATTN_BLOCK), F32),
                        pltpu.VMEM((2, S // LANES, 2 * ATTN_BLOCK, LANES), BF16),
                        pltpu.VMEM((S, 2 * DA_V_DIM), BF16)],
        compiler_params=pltpu.CompilerParams(dimension_semantics=("arbitrary",) * 2,
                                             vmem_limit_bytes=VMEM_LIMIT),
        name="diff_attn",
    )(lam_q1, lam_k1, lam_q2, lam_k2, subln_g, q, k, v)


def _gla_kernel(q_ref, k_ref, la_ref, v_ref, go_ref, ng_ref, o_ref, qt_s, oi_s, ds_s, dec_s):
    S = q_ref.shape[1]
    C = CHUNK
    BLK = GLA_BLOCK
    per_blk = BLK // C
    r = lax.broadcasted_iota(jnp.int32, (BLK, BLK), 0)
    c = lax.broadcasted_iota(jnp.int32, (BLK, BLK), 1)
    chunk_causal = (r // C == c // C) & (c <= r)
    tri = jnp.where(chunk_causal, 1.0, 0.0).astype(BF16)
    lane = lax.broadcasted_iota(jnp.int32, (BLK, LANES), 1)
    head_lanes = (lane < GLA_KEY_DIM, lane >= GLA_KEY_DIM)
    st_row = lax.broadcasted_iota(jnp.int32, (2 * GLA_V_DIM, LANES), 0)
    st_lane = lax.broadcasted_iota(jnp.int32, (2 * GLA_V_DIM, LANES), 1)
    own_keys = (st_row < GLA_V_DIM) == (st_lane < GLA_KEY_DIM)

    for b in range(S // BLK):
        r0 = b * BLK
        g = la_ref[0, r0:r0 + BLK, :]
        g1 = g.astype(BF16)
        e1 = g - g1.astype(F32)
        g2 = e1.astype(BF16)
        g3 = (e1 - g2.astype(F32)).astype(BF16)
        bcum = _dot(tri, g1) + _dot(tri, g2) + _dot(tri, g3)
        b_last = jnp.concatenate(
            [jnp.broadcast_to(bcum[i * C + C - 1:i * C + C, :], (C, LANES)) for i in range(per_blk)], axis=0)
        qf = q_ref[0, r0:r0 + BLK, :].astype(F32) * (GLA_KEY_DIM ** -0.5)
        kf = k_ref[0, r0:r0 + BLK, :].astype(F32)
        q_t = (qf * jnp.exp(bcum)).astype(BF16)
        k_t = (kf * jnp.exp(-bcum)).astype(BF16)
        k_end = (kf * jnp.exp(b_last - bcum)).astype(BF16)
        decay = jnp.exp(b_last)
        qt_s[r0:r0 + BLK, :] = q_t
        zero = jnp.zeros_like(q_t)
        for hh in range(2):
            att = jnp.where(chunk_causal, _dot_nt(jnp.where(head_lanes[hh], q_t, zero), k_t), 0.0).astype(BF16)
            oi_s[r0:r0 + BLK, hh * GLA_V_DIM:(hh + 1) * GLA_V_DIM] = _dot(
                att, v_ref[0, r0:r0 + BLK, hh * GLA_V_DIM:(hh + 1) * GLA_V_DIM])
        for i in range(per_blk):
            n = b * per_blk + i
            rows = slice(r0 + i * C, r0 + (i + 1) * C)
            inc = _dot_tn(v_ref[0, rows, :], k_end[i * C:(i + 1) * C, :])
            ds_s[n] = jnp.where(own_keys, inc, 0.0)
            dec_s[n:n + 1, :] = decay[i * C:i * C + 1, :]

    state = jnp.zeros((2 * GLA_V_DIM, LANES), F32)
    for n in range(S // C):
        rows = slice(n * C, (n + 1) * C)
        o = oi_s[rows, :] + _dot_nt(qt_s[rows, :], state.astype(BF16))
        state = state * dec_s[n:n + 1, :] + ds_s[n]
        for hh in range(2):
            cols = slice(hh * GLA_V_DIM, (hh + 1) * GLA_V_DIM)
            oh = o[:, cols]
            oh = oh * lax.rsqrt(jnp.mean(oh * oh, axis=-1, keepdims=True) + LN_EPS) * ng_ref[...]
            gate = go_ref[0, rows, cols].astype(F32)
            o_ref[0, rows, cols] = (oh * (gate * jax.nn.sigmoid(gate))).astype(o_ref.dtype)


def _gla(gq, gk, la, gv, go, norm_g):
    B, S, _ = gq.shape
    pairs = GLA_HEADS // 2
    narrow = pl.BlockSpec((1, S, LANES), lambda b, p: (b, 0, p))
    wide = pl.BlockSpec((1, S, 2 * GLA_V_DIM), lambda b, p: (b, 0, p))
    n_chunks = S // CHUNK
    return pl.pallas_call(
        _gla_kernel,
        grid=(B, pairs),
        in_specs=[narrow, narrow, narrow, wide, wide,
                  pl.BlockSpec((1, GLA_V_DIM), lambda b, p: (0, 0))],
        out_specs=wide,
        out_shape=jax.ShapeDtypeStruct((B, S, GLA_V), BF16),
        scratch_shapes=[pltpu.VMEM((S, LANES), BF16),
                        pltpu.VMEM((S, 2 * GLA_V_DIM), F32),
                        pltpu.VMEM((n_chunks, 2 * GLA_V_DIM, LANES), F32),
                        pltpu.VMEM((n_chunks, LANES), F32)],
        compiler_params=pltpu.CompilerParams(dimension_semantics=("arbitrary",) * 2,
                                             vmem_limit_bytes=VMEM_LIMIT),
        name="gla",
    )(gq, gk, la, gv, go, norm_g)


def _split3(a):
    hi = a.astype(BF16)
    lo = (a - hi.astype(F32)).astype(BF16)
    return hi, lo


def _mix_out_kernel(alpha, x_ref, da_ref, gl_ref, lng_ref, lnb_ref, wo_ref, g1_ref, b1_ref,
                    wr_hi_ref, wr_lo_ref, br_ref, lower_ref, h_ref, route_ref, cnt_ref):
    tm = x_ref.shape[0]
    i = pl.program_id(0)

    @pl.when(i == 0)
    def _():
        cnt_ref[...] = jnp.zeros_like(cnt_ref)

    logit_blocks = []
    for r0 in range(0, tm, ROW_BLOCK):
        rows = slice(r0, r0 + ROW_BLOCK)
        xn = _layer_norm(x_ref[rows, :], lng_ref[...], lnb_ref[...])
        mix = _dot(da_ref[rows, :], wo_ref[0:DA_V, :]) + _dot(gl_ref[rows, :], wo_ref[DA_V:, :])
        h = _layer_norm(alpha * xn + mix, g1_ref[...], b1_ref[...])
        h_ref[rows, :] = h
        h_hi, h_lo = _split3(h)
        logit_blocks.append(_dot(h_hi, wr_hi_ref[...]) + _dot(h_hi, wr_lo_ref[...]) + _dot(h_lo, wr_hi_ref[...])
                            + br_ref[...])
    logits = jnp.concatenate(logit_blocks, axis=0)
    lane = lax.broadcasted_iota(jnp.int32, (tm, LANES), 1)
    neg = -jnp.inf
    big = jnp.int32(LANES)

    def first_argmax(vals, valid):
        v = jnp.where(valid, vals, neg)
        mx = jnp.max(v, axis=-1, keepdims=True)
        idx = jnp.min(jnp.where(valid & (v == mx), lane, big), axis=-1, keepdims=True)
        return mx, idx

    is_group = lane < N_GROUPS
    g_max, g_top = first_argmax(logits, is_group)
    p_g = 1.0 / jnp.sum(jnp.where(is_group, jnp.exp(logits - g_max), 0.0), axis=-1, keepdims=True)

    e_lo = EXPERT_LANE0 + g_top * EXPERTS_PER_GROUP
    in_group = (lane >= e_lo) & (lane < e_lo + EXPERTS_PER_GROUP)
    v0, i0 = first_argmax(logits, in_group)
    v1, i1 = first_argmax(logits, in_group & (lane != i0))
    w1 = jnp.exp(v1 - v0)
    gate0 = p_g / (1.0 + w1)
    gate1 = p_g * w1 / (1.0 + w1)
    e0 = i0 - EXPERT_LANE0
    e1 = i1 - EXPERT_LANE0

    oh0 = jnp.where(lane == e0, 1.0, 0.0)
    oh1 = jnp.where(lane == e1, 1.0, 0.0)
    oh = oh0 + oh1
    before = _dot(lower_ref[...], oh.astype(BF16)) + cnt_ref[0:1, :]
    rank0 = jnp.sum(oh0 * before, axis=-1, keepdims=True)
    rank1 = jnp.sum(oh1 * before, axis=-1, keepdims=True)
    cnt_ref[...] = cnt_ref[...] + jnp.sum(oh, axis=0, keepdims=True)

    rec = jnp.zeros((tm, LANES), F32)
    for ln, val in ((R_E0, e0.astype(F32)), (R_E1, e1.astype(F32)), (R_G0, gate0), (R_G1, gate1),
                    (R_RANK0, rank0), (R_RANK1, rank1)):
        rec = jnp.where(lane == ln, val, rec)
    route_ref[...] = rec


def _mix_out(x2, da2, gl2, ln_g, ln_b, w_o, ln1_g, ln1_b, wr_hi, wr_lo, b_r, alpha, tm):
    T = x2.shape[0]
    row = lambda n: pl.BlockSpec((tm, n), lambda i: (i, 0))
    full = lambda a: pl.BlockSpec(a.shape, lambda i: (0,) * a.ndim)
    lower = jnp.tril(jnp.ones((tm, tm), BF16), -1)
    return pl.pallas_call(
        functools.partial(_mix_out_kernel, alpha),
        grid=(T // tm,),
        in_specs=[row(D_MODEL), row(DA_V), row(GLA_V), full(ln_g), full(ln_b), full(w_o),
                  full(ln1_g), full(ln1_b), full(wr_hi), full(wr_lo), full(b_r), full(lower)],
        out_specs=[row(D_MODEL), row(LANES), pl.BlockSpec((8, LANES), lambda i: (0, 0))],
        out_shape=[jax.ShapeDtypeStruct((T, D_MODEL), F32), jax.ShapeDtypeStruct((T, LANES), F32),
                   jax.ShapeDtypeStruct((8, LANES), F32)],
        compiler_params=pltpu.CompilerParams(dimension_semantics=("arbitrary",),
                                             vmem_limit_bytes=VMEM_LIMIT),
        name="mix_out",
    )(x2, da2, gl2, ln_g, ln_b, w_o, ln1_g, ln1_b, wr_hi, wr_lo, b_r, lower)


HIGH_HALF = 0xFFFF0000


def _pack_pairs(val):
    bits = lambda a: lax.bitcast_convert_type(a.astype(BF16).astype(F32), jnp.uint32)
    half = val.shape[1] // 2
    return (bits(val[:, :half]) >> 16) | (bits(val[:, half:]) & jnp.uint32(HIGH_HALF))


def _unpack_pairs(words):
    lo = lax.bitcast_convert_type(words << 16, F32)
    hi = lax.bitcast_convert_type(words & jnp.uint32(HIGH_HALF), F32)
    return jnp.concatenate([lo, hi], axis=1)


def _words_to_tiles(dst_ref, words):
    n = words.shape[0]
    for s in range(ROW_SUB):
        dst_ref[pl.ds(s, n, stride=ROW_SUB), :] = words[:, s * LANES:(s + 1) * LANES]


def _tiles_to_words(src_ref, r0, n):
    return jnp.concatenate([src_ref[pl.ds(r0 * ROW_SUB + s, n, stride=ROW_SUB), :] for s in range(ROW_SUB)],
                           axis=1)


def _rows_to_tiles(dst_ref, val):
    _words_to_tiles(dst_ref, _pack_pairs(val))


def _tiles_to_rows(src_ref, r0, n):
    return _unpack_pairs(_tiles_to_words(src_ref, r0, n))


def _row_tile(ref, r):
    return ref.at[pl.ds(pl.multiple_of(r * ROW_SUB, ROW_SUB), ROW_SUB), :]


def _dispatch_kernel(dest_ref, h_ref, xs_ref, stage, sems):
    tm = h_ref.shape[0]
    step = pl.program_id(0)
    half = step & 1
    base = step * (tm * TOP_K)
    n_iter = tm * TOP_K // DMA_UNROLL
    _rows_to_tiles(stage.at[half], h_ref[...])

    def row_copy(hf, t, slot):
        return pltpu.make_async_copy(_row_tile(stage.at[hf], t), _row_tile(xs_ref, slot), sems.at[hf])

    def start(i, c):
        for u in range(DMA_UNROLL):
            t = i * (DMA_UNROLL // TOP_K) + u // TOP_K
            row_copy(half, t, dest_ref[base + i * DMA_UNROLL + u]).start(priority=u % 2)
        return c

    lax.fori_loop(0, n_iter, start, 0)

    def drain(hf):
        def wait(i, c):
            for u in range(DMA_UNROLL):
                row_copy(hf, 0, 0).wait()
            return c
        lax.fori_loop(0, n_iter, wait, 0)

    @pl.when(step > 0)
    def _():
        drain(1 - half)

    @pl.when(step == pl.num_programs(0) - 1)
    def _():
        drain(half)
        n_pad = EXPERT_CHUNK * ROW_SUB
        stage[0, 0:n_pad, :] = jnp.zeros((n_pad, LANES), stage.dtype)
        pad = pltpu.make_async_copy(stage.at[0, 0:n_pad, :],
                                    xs_ref.at[pl.ds(pl.num_programs(0) * tm * TOP_K * ROW_SUB, n_pad), :], sems.at[0])
        pad.start()
        pad.wait()


def _dispatch(dest_flat, h, tm):
    T = h.shape[0]
    return pl.pallas_call(
        _dispatch_kernel,
        grid_spec=pltpu.PrefetchScalarGridSpec(
            num_scalar_prefetch=1,
            grid=(T // tm,),
            in_specs=[pl.BlockSpec((tm, D_MODEL), lambda i, d: (i, 0))],
            out_specs=pl.BlockSpec(memory_space=pl.ANY),
            scratch_shapes=[pltpu.VMEM((2, tm * ROW_SUB, LANES), jnp.uint32), pltpu.SemaphoreType.DMA((2,))]),
        out_shape=jax.ShapeDtypeStruct(((T * TOP_K + EXPERT_CHUNK) * ROW_SUB, LANES), jnp.uint32),
        compiler_params=pltpu.CompilerParams(dimension_semantics=("arbitrary",),
                                             vmem_limit_bytes=VMEM_LIMIT),
        name="dispatch",
    )(dest_flat, h)


def _experts_kernel(row0_ref, first_ref, end_ref, xs_ref, wg_ref, wu_ref, wd_ref, ys_ref,
                    wgu_b, wd_b, xbuf, ybuf, xsem, ysem, pend_ref, *, n_rows):
    e = pl.program_id(0)
    ch = EXPERT_CHUNK
    depth = EXPERT_XBUFS - 1
    g_lo = first_ref[e]
    g_hi = first_ref[e + 1]
    total = first_ref[N_EXPERTS]

    def slab(ref, row0):
        return ref.at[pl.ds(pl.multiple_of(row0 * ROW_SUB, ROW_SUB), ch * ROW_SUB), :]

    def x_copy(g):
        slot = g & (EXPERT_XBUFS - 1)
        return pltpu.make_async_copy(slab(xs_ref, row0_ref[g]), xbuf.at[slot], xsem.at[slot])

    def y_copy(row0, half):
        return pltpu.make_async_copy(ybuf.at[half], slab(ys_ref, row0), ysem.at[half])

    def drain_y(half):
        @pl.when(pend_ref[half] == 1)
        def _():
            y_copy(0, half).wait()
            pend_ref[half] = 0

    @pl.when(e == 0)
    def _():
        for b in range(EXPERT_YBUFS):
            pend_ref[b] = 0
        for b in range(EXPERT_YBUFS):
            ybuf[b] = jnp.zeros(ybuf.shape[1:], ybuf.dtype)
        y_copy(n_rows, 0).start()
        y_copy(n_rows, 0).wait()
        for d in range(depth):
            @pl.when(d < total)
            def _():
                x_copy(d).start(priority=ROW_STREAM_PRIORITY)

    @pl.when(g_hi > g_lo)
    def _():
        wgu_b[:, 0:D_EXPERT] = wg_ref[0].astype(BF16)
        wgu_b[:, D_EXPERT:] = wu_ref[0].astype(BF16)
        wd_b[...] = wd_ref[0].astype(BF16)

        def chunk(g, c):
            half = g & (EXPERT_YBUFS - 1)
            x_copy(g).wait()

            @pl.when(g + depth < total)
            def _():
                x_copy(g + depth).start(priority=ROW_STREAM_PRIORITY)

            drain_y(half)

            @pl.when(g == g_lo)
            def _():
                for b in range(EXPERT_YBUFS):
                    drain_y(b)

            def compute(m):
                xb = _tiles_to_rows(xbuf.at[g & (EXPERT_XBUFS - 1)], 0, m).astype(BF16)
                gu = _dot(xb, wgu_b[...])
                gate = gu[:, 0:D_EXPERT]
                mid = (gate * jax.nn.sigmoid(gate) * gu[:, D_EXPERT:]).astype(BF16)
                _words_to_tiles(ybuf.at[half], _pack_pairs(_dot(mid, wd_b[...])))

            owned = jnp.minimum(end_ref[e] - row0_ref[g], ch)
            grains = lax.shift_right_logical(owned + (EXPERT_GRAIN - 1), EXPERT_GRAIN.bit_length() - 1)
            for n in range(1, ch // EXPERT_GRAIN + 1):
                @pl.when(grains == n)
                def _():
                    compute(n * EXPERT_GRAIN)

            y_copy(row0_ref[g], half).start(priority=ROW_STREAM_PRIORITY)
            pend_ref[half] = 1
            return c

        lax.fori_loop(g_lo, g_hi, chunk, 0)

    @pl.when(e == pl.num_programs(0) - 1)
    def _():
        for b in range(EXPERT_YBUFS):
            drain_y(b)


def _chunk_metadata(seg_start, counts, n_rows):
    ch = EXPERT_CHUNK
    max_chunks = n_rows // ch + N_EXPERTS
    n_ch = (counts + (ch - 1)) // ch
    first = jnp.concatenate([jnp.zeros((1,), jnp.int32), jnp.cumsum(n_ch).astype(jnp.int32)])
    g = jnp.arange(max_chunks, dtype=jnp.int32)
    owner = jnp.minimum(jnp.sum((first[None, 1:] <= g[:, None]).astype(jnp.int32), axis=1), N_EXPERTS - 1)
    onehot = owner[:, None] == jnp.arange(N_EXPERTS, dtype=jnp.int32)
    pick = lambda tab: jnp.sum(jnp.where(onehot, tab[None, :], 0), axis=1)
    row0 = pick(seg_start) + (g - pick(first[:-1])) * ch
    row0 = jnp.where(g < first[-1], row0, 0)
    return row0.astype(jnp.int32), first


def _experts(seg_start, counts, xs, w_gate, w_up, w_down):
    n_rows = xs.shape[0] // ROW_SUB - EXPERT_CHUNK
    row0, first = _chunk_metadata(seg_start, counts, n_rows)
    per_expert = lambda shape: pl.BlockSpec((1,) + shape, lambda e, r, f, n: (e, 0, 0))
    slab = (EXPERT_CHUNK * ROW_SUB, LANES)
    return pl.pallas_call(
        functools.partial(_experts_kernel, n_rows=n_rows),
        grid_spec=pltpu.PrefetchScalarGridSpec(
            num_scalar_prefetch=3,
            grid=(N_EXPERTS,),
            in_specs=[pl.BlockSpec(memory_space=pl.ANY),
                      per_expert((D_MODEL, D_EXPERT)), per_expert((D_MODEL, D_EXPERT)),
                      per_expert((D_EXPERT, D_MODEL))],
            out_specs=pl.BlockSpec(memory_space=pl.ANY),
            scratch_shapes=[pltpu.VMEM((D_MODEL, 2 * D_EXPERT), BF16), pltpu.VMEM((D_EXPERT, D_MODEL), BF16),
                            pltpu.VMEM((EXPERT_XBUFS,) + slab, jnp.uint32), pltpu.VMEM((EXPERT_YBUFS,) + slab, jnp.uint32),
                            pltpu.SemaphoreType.DMA((EXPERT_XBUFS,)), pltpu.SemaphoreType.DMA((EXPERT_YBUFS,)),
                            pltpu.SMEM((EXPERT_YBUFS,), jnp.int32)]),
        out_shape=jax.ShapeDtypeStruct(xs.shape, jnp.uint32),
        compiler_params=pltpu.CompilerParams(dimension_semantics=("arbitrary",),
                                             vmem_limit_bytes=VMEM_LIMIT),
        name="experts",
    )(row0, first, (seg_start + counts).astype(jnp.int32), xs, w_gate, w_up, w_down)


def _combine_kernel(alpha, dest_ref, h_ref, route_ref, g_ref, b_ref, y_ref, o_ref, buf, sems):
    tm = h_ref.shape[0]
    step = pl.program_id(0)
    half = step & 1
    n_iter = tm * TOP_K // DMA_UNROLL

    def row_copy(hf, src, slot):
        return pltpu.make_async_copy(_row_tile(y_ref, src), _row_tile(buf.at[hf], slot), sems.at[hf])

    def gather(st, hf):
        base = st * (tm * TOP_K)

        def start(i, c):
            for u in range(DMA_UNROLL):
                slot = (u % TOP_K) * tm + i * (DMA_UNROLL // TOP_K) + u // TOP_K
                row_copy(hf, dest_ref[base + i * DMA_UNROLL + u], slot).start(priority=u % 2)
            return c

        lax.fori_loop(0, n_iter, start, 0)

    @pl.when(step == 0)
    def _():
        gather(0, 0)

    @pl.when(step + 1 < pl.num_programs(0))
    def _():
        gather(step + 1, 1 - half)

    def wait(i, c):
        for u in range(DMA_UNROLL):
            row_copy(half, 0, 0).wait()
        return c

    lax.fori_loop(0, n_iter, wait, 0)

    rec = route_ref[...]
    cur = buf.at[half]
    ffn = (rec[:, R_G0:R_G0 + 1] * _tiles_to_rows(cur, 0, tm)
           + rec[:, R_G1:R_G1 + 1] * _tiles_to_rows(cur, tm, tm))
    o_ref[...] = _layer_norm(alpha * h_ref[...] + ffn, g_ref[...], b_ref[...])


def _combine(dest_flat, h, route, ln2_g, ln2_b, y_sorted, alpha, tm):
    T = h.shape[0]
    return pl.pallas_call(
        functools.partial(_combine_kernel, alpha),
        grid_spec=pltpu.PrefetchScalarGridSpec(
            num_scalar_prefetch=1,
            grid=(T // tm,),
            in_specs=[pl.BlockSpec((tm, D_MODEL), lambda i, d: (i, 0)),
                      pl.BlockSpec((tm, LANES), lambda i, d: (i, 0)),
                      pl.BlockSpec((1, D_MODEL), lambda i, d: (0, 0)),
                      pl.BlockSpec((1, D_MODEL), lambda i, d: (0, 0)),
                      pl.BlockSpec(memory_space=pl.ANY)],
            out_specs=pl.BlockSpec((tm, D_MODEL), lambda i, d: (i, 0)),
            scratch_shapes=[pltpu.VMEM((2, TOP_K * tm * ROW_SUB, LANES), jnp.uint32), pltpu.SemaphoreType.DMA((2,))]),
        out_shape=jax.ShapeDtypeStruct((T, D_MODEL), F32),
        compiler_params=pltpu.CompilerParams(dimension_semantics=("arbitrary",),
                                             vmem_limit_bytes=VMEM_LIMIT),
        name="combine",
    )(dest_flat, h, route, ln2_g, ln2_b, y_sorted)


def kernel(x, positions, ln_in_g, ln_in_b, w_in, lam_q1, lam_k1, lam_q2, lam_k2, da_subln_g, gla_w_gate2, gla_b_gate2, gla_norm_g, w_o, ln1_g, ln1_b, router_w_group, router_b_group, router_w_expert, router_b_expert, w_gate, w_up, w_down, ln2_g, ln2_b):
    B, S, D = x.shape
    T = B * S
    depth = w_in.shape[0]
    assert depth == 1, "only a single layer is supported"
    alpha = (2 * depth) ** 0.25
    row2 = lambda a: a.reshape(1, -1)

    inv_freq = ROPE_THETA ** (-jnp.arange(0, DA_HEAD_DIM, 2, dtype=F32) / DA_HEAD_DIM)
    inv_freq = jnp.tile(jnp.repeat(inv_freq, 2), LANES // DA_HEAD_DIM).reshape(1, LANES)
    pos2 = positions.reshape(T, 1)

    cur = x.reshape(T, D)
    cur_g, cur_b = row2(ln_in_g), row2(ln_in_b)
    for l in range(depth):
        w = w_in[l]
        w_main = w[:, :D_MAIN].astype(BF16)
        w_glow = jnp.pad(w[:, D_MAIN:], ((0, 0), (0, LANES - GLA_GATE_RANK))).astype(BF16)
        w_gate2 = jnp.pad(gla_w_gate2[l], ((0, LANES - GLA_GATE_RANK), (0, 0))).astype(BF16)

        q, k, v, gq, gk, gv, go, la = _in_proj(cur, pos2, cur_g, cur_b, inv_freq, w_main, w_glow,
                                               w_gate2, row2(gla_b_gate2[l]), tm=512)
        lam_init = 0.8 - 0.6 * math.exp(-0.3 * l)
        sh = lambda a: a.reshape(B, S, a.shape[-1])
        da = _diff_attn(sh(q), sh(k), sh(v), row2(lam_q1[l]), row2(lam_k1[l]), row2(lam_q2[l]),
                        row2(lam_k2[l]), row2(da_subln_g[l]), lam_init)
        gl = _gla(sh(gq), sh(gk), sh(la), sh(gv), sh(go), row2(gla_norm_g[l]))

        w_r = jnp.zeros((D, LANES), F32)
        w_r = w_r.at[:, :N_GROUPS].set(router_w_group[l])
        w_r = w_r.at[:, EXPERT_LANE0:EXPERT_LANE0 + N_EXPERTS].set(router_w_expert[l])
        b_r = jnp.zeros((1, LANES), F32)
        b_r = b_r.at[0, :N_GROUPS].set(router_b_group[l])
        b_r = b_r.at[0, EXPERT_LANE0:EXPERT_LANE0 + N_EXPERTS].set(router_b_expert[l])
        wr_hi = w_r.astype(BF16)
        wr_lo = (w_r - wr_hi.astype(F32)).astype(BF16)

        h, route, cnt = _mix_out(cur, da.reshape(T, DA_V), gl.reshape(T, GLA_V), cur_g, cur_b,
                                 w_o[l].astype(BF16), row2(ln1_g[l]), row2(ln1_b[l]), wr_hi, wr_lo, b_r,
                                 alpha, tm=512)

        counts = cnt[0, :N_EXPERTS].astype(jnp.int32)
        seg_start = jnp.cumsum(counts) - counts
        eid = route[:, R_E0:R_E1 + 1].astype(jnp.int32)
        rank = route[:, R_RANK0:R_RANK1 + 1].astype(jnp.int32)
        onehot = eid[..., None] == jnp.arange(N_EXPERTS, dtype=jnp.int32)
        dest = jnp.sum(jnp.where(onehot, seg_start, 0), axis=-1) + rank
        dest_flat = dest.reshape(T * TOP_K)

        xs = _dispatch(dest_flat, h, tm=1024)
        ys = _experts(seg_start, counts, xs, w_gate[l], w_up[l], w_down[l])
        cur = _combine(dest_flat, h, route, row2(ln2_g[l]), row2(ln2_b[l]), ys, alpha, tm=1024)
    return cur.reshape(B, S, D)
```
